```python
import jax, jax.numpy as jnp
from jax import lax
import numpy as np

D_MODEL = 1024
BATCH = 8
SEQ = 2048
DEPTH = 2

N_A = DEPTH // 2
N_B = DEPTH - N_A
N_DENSE = (DEPTH + 1) // 2
N_MOE = DEPTH // 2

N_MEM = 256
M_HEADS = 4
M_HEAD_DIM = 64
M_W = M_HEADS * M_HEAD_DIM

A_HEADS = 4
A_QK_DIM = 96
A_V_DIM = 192
A_CHUNK = 64
A_QK_W = A_HEADS * A_QK_DIM
A_V_W = A_HEADS * A_V_DIM
A_IN = 2 * A_QK_W + 2 * A_V_W + 2 * A_HEADS + M_W
MIX_A = A_V_W + M_W

B_HEADS = 6
Q_LORA = 384
KV_LORA = 256
QK_NOPE = 128
QK_ROPE = 64
V_HEAD = 128
B_QK_HEAD = QK_NOPE + QK_ROPE
B_IN = Q_LORA + M_W
MIX_B = B_HEADS * V_HEAD + M_W
ROPE_THETA = 10000.0
Q_BLOCK = 128

D_FF = 3584
N_EXPERTS = 8
TOP_K = 2

EPS = 1e-6

kernel_name = "yoco_mlstm_mla_memory_moe_block"


def rms_norm(x, g):
    xf = x.astype(jnp.float32)
    y = xf * lax.rsqrt(jnp.mean(xf * xf, axis=-1, keepdims=True) + EPS)
    return (y * g.astype(jnp.float32)).astype(x.dtype)


def rope_angles(positions):
    inv = 1.0 / (ROPE_THETA ** (jnp.arange(0, QK_ROPE, 2, dtype=jnp.float32) / QK_ROPE))
    ang = positions.astype(jnp.float32)[..., None] * inv
    return jnp.cos(ang), jnp.sin(ang)


def apply_rope(x, cos, sin):
    x1, x2 = jnp.split(x.astype(jnp.float32), 2, axis=-1)
    return jnp.concatenate([x1 * cos - x2 * sin, x2 * cos + x1 * sin], axis=-1).astype(x.dtype)


def memory_kv(mem, norm_g, w_kv):
    B, M, _ = mem.shape
    kv = rms_norm(mem, norm_g) @ w_kv
    k, v = jnp.split(kv, 2, axis=-1)
    return k.reshape(B, M, M_HEADS, M_HEAD_DIM), v.reshape(B, M, M_HEADS, M_HEAD_DIM)


def memory_attention(q, mem_k, mem_v, q_g, k_g):
    B, S = q.shape[:2]
    q = rms_norm(q, q_g)
    k = rms_norm(mem_k, k_g)
    s = jnp.einsum('bshd,bmhd->bhsm', q, k).astype(jnp.float32) * (M_HEAD_DIM ** -0.5)
    p = jax.nn.softmax(s, axis=-1).astype(mem_v.dtype)
    return jnp.einsum('bhsm,bmhd->bshd', p, mem_v).reshape(B, S, M_W)


def mlstm_chunkwise(q, k, v, i_pre, f_pre):
    B, S, H, dk = q.shape
    dv = v.shape[-1]
    L = A_CHUNK
    NC = S // L
    f32 = jnp.float32

    def chunk(t):
        return t.astype(f32).reshape(B, NC, L, H, -1).transpose(0, 3, 1, 2, 4)

    def chunk_gate(t):
        return t.astype(f32).reshape(B, NC, L, H).transpose(0, 3, 1, 2)

    qc = chunk(q)
    kc = chunk(k) * (dk ** -0.5)
    vc = chunk(v)
    ig = chunk_gate(i_pre)
    lf = jax.nn.log_sigmoid(chunk_gate(f_pre))
    g = jnp.cumsum(lf, axis=-1)
    g_last = g[..., -1]

    w_loc = g_last[..., None] - g + ig
    m_loc = jnp.max(w_loc, axis=-1)
    e_loc = jnp.exp(w_loc - m_loc[..., None])
    C_loc = jnp.einsum('bhcld,bhcle->bhcde', vc * e_loc[..., None], kc)
    n_loc = jnp.einsum('bhcl,bhcle->bhce', e_loc, kc)

    def step(carry, xs):
        C, n, m = carry
        Cl, nl, ml, gl = xs
        m_new = jnp.maximum(gl + m, ml)
        a = jnp.exp(gl + m - m_new)
        b = jnp.exp(ml - m_new)
        C_new = a[..., None, None] * C + b[..., None, None] * Cl
        n_new = a[..., None] * n + b[..., None] * nl
        return (C_new, n_new, m_new), (C, n, m)

    init = (jnp.zeros((B, H, dv, dk), f32), jnp.zeros((B, H, dk), f32), jnp.zeros((B, H), f32))
    xs = (jnp.moveaxis(C_loc, 2, 0), jnp.moveaxis(n_loc, 2, 0),
          jnp.moveaxis(m_loc, 2, 0), jnp.moveaxis(g_last, 2, 0))
    _, (C_prev, n_prev, m_prev) = lax.scan(step, init, xs)
    C_prev = jnp.moveaxis(C_prev, 0, 2)
    n_prev = jnp.moveaxis(n_prev, 0, 2)
    m_prev = jnp.moveaxis(m_prev, 0, 2)

    a_log = g + m_prev[..., None]
    causal = jnp.tril(jnp.ones((L, L), dtype=bool))
    d_log = jnp.where(causal, g[..., :, None] - g[..., None, :] + ig[..., None, :], -jnp.inf)
    m_t = jnp.maximum(a_log, jnp.max(d_log, axis=-1))
    inter = jnp.exp(a_log - m_t)
    P = jnp.exp(d_log - m_t[..., None]) * jnp.einsum('bhctd,bhcsd->bhcts', qc, kc)
    num = inter[..., None] * jnp.einsum('bhcde,bhcte->bhctd', C_prev, qc) \
        + jnp.einsum('bhcts,bhcsd->bhctd', P, vc)
    den = inter * jnp.einsum('bhce,bhcte->bhct', n_prev, qc) + jnp.sum(P, axis=-1)
    h = num / jnp.maximum(jnp.abs(den), jnp.exp(-m_t))[..., None]
    return h.transpose(0, 2, 3, 1, 4).reshape(B, S, H, dv)


def mlstm_layer(x, norm_g, w_in, gate_bias, head_g, w_out, mem_k, mem_v, mq_g, mk_g):
    B, S, _ = x.shape
    proj = rms_norm(x, norm_g) @ w_in
    offs = [A_QK_W, 2 * A_QK_W, 2 * A_QK_W + A_V_W, 2 * A_QK_W + 2 * A_V_W,
            2 * A_QK_W + 2 * A_V_W + 2 * A_HEADS]
    q, k, v, o, ifg, mq = jnp.split(proj, offs, axis=-1)
    ifg = ifg + gate_bias
    i_pre, f_pre = ifg[..., :A_HEADS], ifg[..., A_HEADS:]
    hm = mlstm_chunkwise(q.reshape(B, S, A_HEADS, A_QK_DIM), k.reshape(B, S, A_HEADS, A_QK_DIM),
                         v.reshape(B, S, A_HEADS, A_V_DIM), i_pre, f_pre).astype(x.dtype)
    hm = rms_norm(hm, head_g.reshape(A_HEADS, A_V_DIM)).reshape(B, S, A_V_W)
    hm = hm * jax.nn.sigmoid(o)
    mo = memory_attention(mq.reshape(B, S, M_HEADS, M_HEAD_DIM), mem_k, mem_v, mq_g, mk_g)
    return x + jnp.concatenate([hm, mo], axis=-1) @ w_out


def shared_latent_kv(h, kv_norm, w_dkv, kv_a_norm, w_ukv, k_head_g, cos, sin):
    B, S, _ = h.shape
    z = rms_norm(h, kv_norm) @ w_dkv
    c_kv, k_rope = z[..., :KV_LORA], z[..., KV_LORA:]
    kv = (rms_norm(c_kv, kv_a_norm) @ w_ukv).reshape(B, S, B_HEADS, QK_NOPE + V_HEAD)
    k_nope, v = kv[..., :QK_NOPE], kv[..., QK_NOPE:]
    k = jnp.concatenate([k_nope, jnp.broadcast_to(k_rope[:, :, None, :], (B, S, B_HEADS, QK_ROPE))], axis=-1)
    k = rms_norm(k, k_head_g)
    k = jnp.concatenate([k[..., :QK_NOPE],
                         apply_rope(k[..., QK_NOPE:], cos[:, :, None], sin[:, :, None])], axis=-1)
    return k, v


def causal_block_attention(q, k, v):
    B, S, H, dq = q.shape
    scale = dq ** -0.5
    qt = q.transpose(0, 2, 1, 3)
    kt = k.transpose(0, 2, 1, 3)
    vt = v.transpose(0, 2, 1, 3)
    local_mask = jnp.tril(jnp.ones((Q_BLOCK, Q_BLOCK), dtype=bool))
    outs = []
    for blk in range(S // Q_BLOCK):
        start = blk * Q_BLOCK
        end = start + Q_BLOCK
        s = jnp.einsum('bhqd,bhkd->bhqk', qt[:, :, start:end], kt[:, :, :end]).astype(jnp.float32) * scale
        mask = jnp.concatenate([jnp.ones((Q_BLOCK, start), dtype=bool), local_mask], axis=1)
        p = jax.nn.softmax(jnp.where(mask, s, -jnp.inf), axis=-1).astype(v.dtype)
        outs.append(jnp.einsum('bhqk,bhkd->bqhd', p, vt[:, :, :end]))
    return jnp.concatenate(outs, axis=1)


def mla_layer(x, norm_g, w_in, q_a_g, w_uq, q_head_g, w_out, k_sh, v_sh, cos, sin,
              mem_k, mem_v, mq_g, mk_g):
    B, S, _ = x.shape
    proj = rms_norm(x, norm_g) @ w_in
    cq, mq = proj[..., :Q_LORA], proj[..., Q_LORA:]
    q = (rms_norm(cq, q_a_g) @ w_uq).reshape(B, S, B_HEADS, B_QK_HEAD)
    q = rms_norm(q, q_head_g)
    q = jnp.concatenate([q[..., :QK_NOPE],
                         apply_rope(q[..., QK_NOPE:], cos[:, :, None], sin[:, :, None])], axis=-1)
    att = causal_block_attention(q, k_sh, v_sh).reshape(B, S, B_HEADS * V_HEAD)
    mo = memory_attention(mq.reshape(B, S, M_HEADS, M_HEAD_DIM), mem_k, mem_v, mq_g, mk_g)
    return x + jnp.concatenate([att, mo], axis=-1) @ w_out


def swiglu(h, w_gate_up, w_down):
    g, u = jnp.split(h @ w_gate_up, 2, axis=-1)
    return (jax.nn.silu(g) * u) @ w_down


def moe_ffn(h, router, w_gate_up, w_down):
    B, S, D = h.shape
    t = h.reshape(B * S, D)
    logits = (t @ router).astype(jnp.float32)
    top_v, top_i = lax.top_k(logits, TOP_K)
    gates = jax.nn.softmax(top_v, axis=-1)
    combine = jnp.sum(jax.nn.one_hot(top_i, N_EXPERTS, dtype=jnp.float32) * gates[..., None], axis=1)
    out = jnp.zeros_like(t)
    for e in range(N_EXPERTS):
        out = out + combine[:, e:e + 1].astype(t.dtype) * swiglu(t, w_gate_up[e], w_down[e])
    return out.reshape(B, S, D)


def setup_inputs(seed: int = 0) -> dict:
    key = jax.random.key(seed)
    ks = iter(jax.random.split(key, 40))
    f32 = jnp.float32

    def w(shape, fan_in):
        return jax.random.normal(next(ks), shape, f32) * (fan_in ** -0.5)

    def gain(shape):
        return 1.0 + 0.02 * jax.random.normal(next(ks), shape, f32)

    x = jax.random.normal(next(ks), (BATCH, SEQ, D_MODEL), f32)
    mem = jax.random.normal(next(ks), (BATCH, N_MEM, D_MODEL), f32)
    offset = jax.random.randint(next(ks), (BATCH, 1), 0, 4096, dtype=jnp.int32)
    positions = jnp.arange(SEQ, dtype=jnp.int32)[None, :] + offset

    i_bias = 0.1 * jax.random.normal(next(ks), (N_A, A_HEADS), f32)
    f_bias = 3.0 + 0.1 * jax.random.normal(next(ks), (N_A, A_HEADS), f32)

    return {
        "x": x, "mem": mem, "positions": positions,
        "a_norm": gain((N_A, D_MODEL)),
        "a_w_in": w((N_A, D_MODEL, A_IN), D_MODEL),
        "a_gate_bias": jnp.concatenate([i_bias, f_bias], axis=-1),
        "a_head_norm": gain((N_A, A_V_W)),
        "a_w_out": w((N_A, MIX_A, D_MODEL), MIX_A),
        "b_norm": gain((N_B, D_MODEL)),
        "b_w_in": w((N_B, D_MODEL, B_IN), D_MODEL),
        "b_q_a_norm": gain((N_B, Q_LORA)),
        "b_w_uq": w((N_B, Q_LORA, B_HEADS * B_QK_HEAD), Q_LORA),
        "b_q_head_norm": gain((N_B, B_QK_HEAD)),
        "b_w_out": w((N_B, MIX_B, D_MODEL), MIX_B),
        "kv_norm": gain((D_MODEL,)),
        "w_dkv": w((D_MODEL, KV_LORA + QK_ROPE), D_MODEL),
        "kv_a_norm": gain((KV_LORA,)),
        "w_ukv": w((KV_LORA, B_HEADS * (QK_NOPE + V_HEAD)), KV_LORA),
        "k_head_norm": gain((B_QK_HEAD,)),
        "mem_norm": gain((DEPTH, D_MODEL)),
        "mem_w_kv": w((DEPTH, D_MODEL, 2 * M_W), D_MODEL),
        "mem_q_norm": gain((DEPTH, M_HEAD_DIM)),
        "mem_k_norm": gain((DEPTH, M_HEAD_DIM)),
        "ffn_norm": gain((DEPTH, D_MODEL)),
        "dense_w_gate_up": w((N_DENSE, D_MODEL, 2 * D_FF), D_MODEL),
        "dense_w_down": w((N_DENSE, D_FF, D_MODEL), D_FF),
        "moe_router": w((N_MOE, D_MODEL, N_EXPERTS), D_MODEL),
        "moe_w_gate_up": w((N_MOE, N_EXPERTS, D_MODEL, 2 * D_FF), D_MODEL),
        "moe_w_down": w((N_MOE, N_EXPERTS, D_FF, D_MODEL), D_FF),
    }


def reference(x, mem, positions,
              a_norm, a_w_in, a_gate_bias, a_head_norm, a_w_out,
              b_norm, b_w_in, b_q_a_norm, b_w_uq, b_q_head_norm, b_w_out,
              kv_norm, w_dkv, kv_a_norm, w_ukv, k_head_norm,
              mem_norm, mem_w_kv, mem_q_norm, mem_k_norm,
              ffn_norm, dense_w_gate_up, dense_w_down,
              moe_router, moe_w_gate_up, moe_w_down):
    cos, sin = rope_angles(positions)
    k_sh = v_sh = None
    if N_A == 0:
        k_sh, v_sh = shared_latent_kv(x, kv_norm, w_dkv, kv_a_norm, w_ukv, k_head_norm, cos, sin)
    for l in range(DEPTH):
        mem_k, mem_v = memory_kv(mem, mem_norm[l], mem_w_kv[l])
        if l < N_A:
            x = mlstm_layer(x, a_norm[l], a_w_in[l], a_gate_bias[l], a_head_norm[l], a_w_out[l],
                            mem_k, mem_v, mem_q_norm[l], mem_k_norm[l])
        else:
            j = l - N_A
            x = mla_layer(x, b_norm[j], b_w_in[j], b_q_a_norm[j], b_w_uq[j], b_q_head_norm[j], b_w_out[j],
                          k_sh, v_sh, cos, sin, mem_k, mem_v, mem_q_norm[l], mem_k_norm[l])
        hf = rms_norm(x, ffn_norm[l])
        if l % 2 == 0:
            x = x + swiglu(hf, dense_w_gate_up[l // 2], dense_w_down[l // 2])
        else:
            x = x + moe_ffn(hf, moe_router[l // 2], moe_w_gate_up[l // 2], moe_w_down[l // 2])
        if l == N_A - 1:
            k_sh, v_sh = shared_latent_kv(x, kv_norm, w_dkv, kv_a_norm, w_ukv, k_head_norm, cos, sin)
    return x
```

```python
import functools

import jax
import jax.numpy as jnp
from jax import lax
from jax.experimental import pallas as pl
from jax.experimental.pallas import tpu as pltpu

F32 = jnp.float32
BF16 = jnp.bfloat16
I32 = jnp.int32

EPS = 1e-6
LANE = 128
VMEM_LIMIT = 48 * 1024 * 1024

D_MODEL = 1024
N_MEM = 256
M_HEADS, M_HEAD_DIM = 4, 64
M_W = M_HEADS * M_HEAD_DIM
A_HEADS, A_QK_DIM, A_V_DIM = 4, 96, 192
A_QK_PAD, A_V_PAD = 128, 256
B_HEADS, Q_LORA, KV_LORA = 6, 384, 256
QK_NOPE, QK_ROPE, V_HEAD = 128, 64, 128
B_QK_HEAD = QK_NOPE + QK_ROPE
B_QK_PAD = 256
ROPE_THETA = 10000.0
D_FF = 3584
N_EXPERTS, TOP_K = 8, 2

MLSTM_CHUNK = 256
ROW_TILE = 512
FFN_ROW_TILE = 1024
FFN_FF_TILE = 512
MOE_ROW_TILE = 512
ATT_Q_TILE = 256
ATT_K_TILE = 256
RANK_TILE = 512
DMA_CHUNK = 256

HIGHEST = lax.Precision.HIGHEST


def _cparams(sem):
    return pltpu.CompilerParams(dimension_semantics=sem, vmem_limit_bytes=VMEM_LIMIT)


def _rms(x, g):
    return x * lax.rsqrt(jnp.mean(x * x, axis=-1, keepdims=True) + EPS) * g


def _dot(a, b):
    return jnp.dot(a, b, preferred_element_type=F32)


def _dot_nt(a, b):
    return lax.dot_general(a, b, (((1,), (1,)), ((), ())), preferred_element_type=F32)


def _dot_tn(a, b):
    return lax.dot_general(a, b, (((0,), (0,)), ((), ())), preferred_element_type=F32)


def _group_mean_sq(x, gmat):
    return jnp.dot(x * x, gmat, precision=HIGHEST, preferred_element_type=F32)


def _memkv_kernel(mem_ref, g_ref, w_ref, kg_ref, gmat_ref, kbd_ref, vbd_ref):
    hn = _rms(mem_ref[0], g_ref[...]).astype(BF16)
    kv = _dot(hn, w_ref[...])
    k, v = kv[:, :M_W], kv[:, M_W:]
    kn = k * lax.rsqrt(_group_mean_sq(k, gmat_ref[...]) + EPS) * kg_ref[...]
    lane_head = lax.broadcasted_iota(I32, (1, M_W), 1) // M_HEAD_DIM
    for h in range(M_HEADS):
        keep = lane_head == h
        kbd_ref[0, h * N_MEM:(h + 1) * N_MEM, :] = jnp.where(keep, kn, 0.0).astype(BF16)
        vbd_ref[0, h * N_MEM:(h + 1) * N_MEM, :] = jnp.where(keep, v, 0.0).astype(BF16)


def _memory_kv(mem, g, w_kv, k_g, gmat):
    nb = mem.shape[0]
    out = jax.ShapeDtypeStruct((nb, M_HEADS * N_MEM, M_W), BF16)
    return pl.pallas_call(
        _memkv_kernel,
        out_shape=(out, out),
        grid=(nb,),
        in_specs=[
            pl.BlockSpec((1, N_MEM, D_MODEL), lambda b: (b, 0, 0)),
            pl.BlockSpec((1, D_MODEL), lambda b: (0, 0)),
            pl.BlockSpec((D_MODEL, 2 * M_W), lambda b: (0, 0)),
            pl.BlockSpec((1, M_W), lambda b: (0, 0)),
            pl.BlockSpec((M_W, M_W), lambda b: (0, 0)),
        ],
        out_specs=(pl.BlockSpec((1, M_HEADS * N_MEM, M_W), lambda b: (b, 0, 0)),) * 2,
        compiler_params=_cparams(("parallel",)),
        name="memory_kv",
    )(mem, g.reshape(1, -1), w_kv.astype(BF16), jnp.tile(k_g, M_HEADS).reshape(1, -1), gmat)


def _memory_attention(mq, kbd, vbd, gmat, qg):
    qn = mq * lax.rsqrt(_group_mean_sq(mq, gmat) + EPS) * (qg * (M_HEAD_DIM ** -0.5))
    s = _dot_nt(qn.astype(BF16), kbd)
    ps = []
    for h in range(M_HEADS):
        sh = s[:, h * N_MEM:(h + 1) * N_MEM]
        e = jnp.exp(sh - jnp.max(sh, axis=-1, keepdims=True))
        ps.append((e / jnp.sum(e, axis=-1, keepdims=True)).astype(BF16))
    return _dot(jnp.concatenate(ps, axis=-1), vbd)


def _a_proj_kernel(x_ref, g_ref, w_ref, wif_ref, wift_ref, bc_ref, br_ref,
                   q_ref, k_ref, v_ref, o_ref, mq_ref, gc_ref, gr_ref):
    hn = _rms(x_ref[...], g_ref[...]).astype(BF16)
    nq = A_HEADS * A_QK_PAD
    nv = A_HEADS * A_V_PAD
    q_ref[...] = _dot(hn, w_ref[:, :nq]).astype(BF16)
    k_ref[...] = (_dot(hn, w_ref[:, nq:2 * nq]) * (A_QK_DIM ** -0.5)).astype(BF16)
    v = _dot(hn, w_ref[:, 2 * nq:2 * nq + nv])
    ones_lane = lax.broadcasted_iota(I32, (1, nv), 1) % A_V_PAD == A_V_DIM
    v_ref[...] = jnp.where(ones_lane, 1.0, v).astype(BF16)
    o_ref[...] = _dot(hn, w_ref[:, 2 * nq + nv:2 * nq + 2 * nv])
    mq_ref[...] = _dot(hn, w_ref[:, 2 * nq + 2 * nv:])
    gc_ref[...] = _dot(hn, wif_ref[...])[:, :2 * A_HEADS] + bc_ref[...]
    gr_ref[...] = _dot_nt(wift_ref[...], hn) + br_ref[...]


def _a_projection(x2, g, w_main, w_if, gate_bias):
    t = x2.shape[0]
    nq, nv = A_HEADS * A_QK_PAD, A_HEADS * A_V_PAD
    ng = 2 * A_HEADS
    wif_pad = jnp.pad(w_if, ((0, 0), (0, LANE - ng))).astype(BF16)
    row = lambda n: pl.BlockSpec((ROW_TILE, n), lambda i: (i, 0))
    full = lambda a: pl.BlockSpec(a.shape, lambda i: (0,) * a.ndim)
    args = (x2, g.reshape(1, -1), w_main, wif_pad, w_if.T.astype(BF16),
            gate_bias.reshape(1, ng), gate_bias.reshape(ng, 1))
    return pl.pallas_call(
        _a_proj_kernel,
        out_shape=(jax.ShapeDtypeStruct((t, nq), BF16), jax.ShapeDtypeStruct((t, nq), BF16),
                   jax.ShapeDtypeStruct((t, nv), BF16), jax.ShapeDtypeStruct((t, nv), F32),
                   jax.ShapeDtypeStruct((t, M_W), F32), jax.ShapeDtypeStruct((t, ng), F32),
                   jax.ShapeDtypeStruct((ng, t), F32)),
        grid=(t // ROW_TILE,),
        in_specs=[row(D_MODEL)] + [full(a) for a in args[1:]],
        out_specs=(row(nq), row(nq), row(nv), row(nv), row(M_W), row(ng),
                   pl.BlockSpec((ng, ROW_TILE), lambda i: (0, i))),
        compiler_params=_cparams(("parallel",)),
        name="mlstm_in_proj",
    )(*args)


def _log_sigmoid(f):
    return jnp.minimum(f, 0.0) - jnp.log(1.0 + jnp.exp(-jnp.abs(f)))


def _gates_kernel(gc_ref, gr_ref, oc_ref, or_ref):
    L = MLSTM_CHUNK
    gc = gc_ref[...]
    gr = gr_ref[...]
    r = lax.broadcasted_iota(I32, (L, L), 0)
    c = lax.broadcasted_iota(I32, (L, L), 1)
    lower = (c <= r).astype(F32)
    is_f_col = lax.broadcasted_iota(I32, gc.shape, 1) >= A_HEADS
    is_f_row = lax.broadcasted_iota(I32, gr.shape, 0) >= A_HEADS
    lf_c = jnp.where(is_f_col, _log_sigmoid(gc), 0.0)
    lf_r = jnp.where(is_f_row, _log_sigmoid(gr), 0.0)
    lf_c = jnp.concatenate([lf_c, jnp.zeros((L, LANE - gc.shape[1]), F32)], axis=1)
    cum_c = jnp.dot(lower, lf_c, precision=HIGHEST, preferred_element_type=F32)[:, :gc.shape[1]]
    cum_r = _dot_nt_highest(lf_r, lower)
    oc_ref[...] = jnp.where(is_f_col, cum_c, gc)
    or_ref[...] = jnp.where(is_f_row, cum_r, gr)


def _dot_nt_highest(a, b):
    return lax.dot_general(a, b, (((1,), (1,)), ((), ())), precision=HIGHEST,
                           preferred_element_type=F32)


def _mlstm_gates(gc, gr):
    t, ng = gc.shape
    L = MLSTM_CHUNK
    return pl.pallas_call(
        _gates_kernel,
        out_shape=(jax.ShapeDtypeStruct((t, ng), F32), jax.ShapeDtypeStruct((ng, t), F32)),
        grid=(t // L,),
        in_specs=[pl.BlockSpec((L, ng), lambda i: (i, 0)), pl.BlockSpec((ng, L), lambda i: (0, i))],
        out_specs=(pl.BlockSpec((L, ng), lambda i: (i, 0)), pl.BlockSpec((ng, L), lambda i: (0, i))),
        compiler_params=_cparams(("parallel",)),
        name="mlstm_gates",
    )(gc, gr)


def _mlstm_kernel(q_ref, k_ref, v_ref, o_ref, gc_ref, gr_ref, hg_ref, out_ref, c_ref, m_ref):
    L = MLSTM_CHUNK

    @pl.when(pl.program_id(2) == 0)
    def _():
        c_ref[...] = jnp.zeros_like(c_ref)
        m_ref[...] = jnp.zeros_like(m_ref)

    q, k, v = q_ref[0], k_ref[0], v_ref[0]
    gcol = gc_ref[0, 0]
    grow = gr_ref[0, 0]
    i_c, g_c = gcol[:, 0:1], gcol[:, 1:2]
    i_r, g_r = grow[0:1, :], grow[1:2, :]
    g_last = g_r[:, L - 1:L]
    m_prev = m_ref[0:1, 0:1]

    a_log = g_c + m_prev
    t_idx = lax.broadcasted_iota(I32, (L, L), 0)
    s_idx = lax.broadcasted_iota(I32, (L, L), 1)
    d_log = jnp.where(s_idx <= t_idx, g_c - g_r + i_r, -jnp.inf)
    m_t = jnp.maximum(a_log, jnp.max(d_log, axis=-1, keepdims=True))
    inter = jnp.exp(a_log - m_t)
    p = (jnp.exp(d_log - m_t) * _dot_nt(q, k)).astype(BF16)
    num = inter * _dot(q, c_ref[...].astype(BF16)) + _dot(p, v)
    den = num[:, A_V_DIM:A_V_DIM + 1]
    h = num / jnp.maximum(jnp.abs(den), jnp.exp(-m_t))
    real = lax.broadcasted_iota(I32, (1, A_V_PAD), 1) < A_V_DIM
    h = jnp.where(real, h, 0.0)
    hn = h * lax.rsqrt(jnp.sum(h * h, axis=-1, keepdims=True) * (1.0 / A_V_DIM) + EPS) * hg_ref[...]
    out_ref[0] = (hn * jax.nn.sigmoid(o_ref[0])).astype(BF16)

    w_r = g_last - g_r + i_r
    m_new = jnp.maximum(g_last + m_prev, jnp.max(w_r, axis=-1, keepdims=True))
    e_c = jnp.exp(g_last - g_c + i_c - m_new)
    ev = (e_c * v.astype(F32)).astype(BF16)
    c_ref[...] = jnp.exp(g_last + m_prev - m_new) * c_ref[...] + _dot_tn(k, ev)
    m_ref[...] = jnp.broadcast_to(m_new, m_ref.shape)


def _mlstm(q, k, v, o, gcol, grow, head_g):
    nb, s, _ = q.shape
    L = MLSTM_CHUNK
    blk = lambda w: pl.BlockSpec((1, L, w), lambda b, h, c: (b, c, h))
    return pl.pallas_call(
        _mlstm_kernel,
        out_shape=jax.ShapeDtypeStruct((nb, s, A_HEADS * A_V_PAD), BF16),
        grid=(nb, A_HEADS, s // L),
        in_specs=[blk(A_QK_PAD), blk(A_QK_PAD), blk(A_V_PAD), blk(A_V_PAD),
                  pl.BlockSpec((1, 1, L, 2), lambda b, h, c: (b, h, c, 0)),
                  pl.BlockSpec((1, 1, 2, L), lambda b, h, c: (b, h, 0, c)),
                  pl.BlockSpec((1, A_V_PAD), lambda b, h, c: (0, h))],
        out_specs=blk(A_V_PAD),
        scratch_shapes=[pltpu.VMEM((A_QK_PAD, A_V_PAD), F32), pltpu.VMEM((8, LANE), F32)],
        compiler_params=_cparams(("parallel", "parallel", "arbitrary")),
        name="mlstm_chunkwise",
    )(q, k, v, o, gcol, grow, head_g)


def _mix_out_kernel(x_ref, h_ref, mq_ref, kbd_ref, vbd_ref, gmat_ref, qg_ref, w1_ref, w2_ref, out_ref):
    mo = _memory_attention(mq_ref[...], kbd_ref[0], vbd_ref[0], gmat_ref[...], qg_ref[...])
    out_ref[...] = x_ref[...] + _dot(h_ref[...], w1_ref[...]) + _dot(mo.astype(BF16), w2_ref[...])


def _mix_out(x2, h2, mq, kbd, vbd, gmat, qg, w_main, w_mem, seq):
    t = x2.shape[0]
    tm = ROW_TILE
    row = lambda n: pl.BlockSpec((tm, n), lambda i: (i, 0))
    full = lambda a: pl.BlockSpec(a.shape, lambda i: (0,) * a.ndim)
    per_batch = pl.BlockSpec((1,) + kbd.shape[1:], lambda i: ((i * tm) // seq, 0, 0))
    qg_t = jnp.tile(qg, M_HEADS).reshape(1, -1)
    return pl.pallas_call(
        _mix_out_kernel,
        out_shape=jax.ShapeDtypeStruct((t, D_MODEL), F32),
        grid=(t // tm,),
        in_specs=[row(D_MODEL), row(h2.shape[1]), row(M_W), per_batch, per_batch,
                  full(gmat), full(qg_t), full(w_main), full(w_mem)],
        out_specs=row(D_MODEL),
        compiler_params=_cparams(("parallel",)),
        name="mixer_out_proj",
    )(x2, h2, mq, kbd, vbd, gmat, qg_t, w_main, w_mem)


def _ffn_kernel(x_ref, g_ref, wg_ref, wu_ref, wd_ref, out_ref, hn_ref, acc_ref):
    j = pl.program_id(1)

    @pl.when(j == 0)
    def _():
        hn_ref[...] = _rms(x_ref[...], g_ref[...]).astype(BF16)
        acc_ref[...] = jnp.zeros_like(acc_ref)

    hn = hn_ref[...]
    gate = _dot(hn, wg_ref[...])
    up = _dot(hn, wu_ref[...])
    acc_ref[...] += _dot((jax.nn.silu(gate) * up).astype(BF16), wd_ref[...])

    @pl.when(j == pl.num_programs(1) - 1)
    def _():
        out_ref[...] = x_ref[...] + acc_ref[...]


def _dense_ffn(x2, g, w_gate, w_up, w_down):
    t = x2.shape[0]
    tm, tf = FFN_ROW_TILE, FFN_FF_TILE
    return pl.pallas_call(
        _ffn_kernel,
        out_shape=jax.ShapeDtypeStruct((t, D_MODEL), F32),
        grid=(t // tm, D_FF // tf),
        in_specs=[pl.BlockSpec((tm, D_MODEL), lambda i, j: (i, 0)),
                  pl.BlockSpec((1, D_MODEL), lambda i, j: (0, 0)),
                  pl.BlockSpec((D_MODEL, tf), lambda i, j: (0, j)),
                  pl.BlockSpec((D_MODEL, tf), lambda i, j: (0, j)),
                  pl.BlockSpec((tf, D_MODEL), lambda i, j: (j, 0))],
        out_specs=pl.BlockSpec((tm, D_MODEL), lambda i, j: (i, 0)),
        scratch_shapes=[pltpu.VMEM((tm, D_MODEL), BF16), pltpu.VMEM((tm, D_MODEL), F32)],
        compiler_params=_cparams(("parallel", "arbitrary")),
        name="dense_swiglu",
    )(x2, g.reshape(1, -1), w_gate, w_up, w_down)


def _rope_kernel(pos_ref, inv_ref, sign_ref, cs_ref, sn_ref):
    ang = pos_ref[...].astype(F32) * inv_ref[...]
    cs_ref[...] = jnp.cos(ang)
    sn_ref[...] = jnp.sin(ang) * sign_ref[...]


def _rope_tables(positions):
    t = positions.size
    half = QK_ROPE // 2
    inv = 1.0 / (ROPE_THETA ** (jnp.arange(0, QK_ROPE, 2, dtype=F32) / QK_ROPE))
    pad = jnp.zeros((LANE - QK_ROPE,), F32)
    inv_l = jnp.concatenate([inv, inv, pad]).reshape(1, LANE)
    sign = jnp.concatenate([-jnp.ones((half,), F32), jnp.ones((half,), F32), pad]).reshape(1, LANE)
    out = jax.ShapeDtypeStruct((t, LANE), F32)
    return pl.pallas_call(
        _rope_kernel,
        out_shape=(out, out),
        grid=(t // ROW_TILE,),
        in_specs=[pl.BlockSpec((ROW_TILE, 1), lambda i: (i, 0)),
                  pl.BlockSpec((1, LANE), lambda i: (0, 0)),
                  pl.BlockSpec((1, LANE), lambda i: (0, 0))],
        out_specs=(pl.BlockSpec((ROW_TILE, LANE), lambda i: (i, 0)),) * 2,
        compiler_params=_cparams(("parallel",)),
        name="rope_tables",
    )(positions.reshape(t, 1), inv_l, sign)


def _head_qk_norm_rope(nope, rope, rope_sw, g_nope, g_rope, g_rope_sw, cs, sn, scale):
    ss = jnp.sum(nope * nope, axis=-1, keepdims=True) + jnp.sum(rope * rope, axis=-1, keepdims=True)
    r = lax.rsqrt(ss * (1.0 / B_QK_HEAD) + EPS) * scale
    return nope * r * g_nope, (rope * g_rope * cs + rope_sw * g_rope_sw * sn) * r


def _latent_kv_kernel(x_ref, g_ref, wd_ref, ga_ref, wu_ref, kg_ref, cs_ref, sn_ref, k_ref, v_ref):
    hn = _rms(x_ref[...], g_ref[...]).astype(BF16)
    z = _dot(hn, wd_ref[...])
    c_kv = z[:, :KV_LORA]
    rope, rope_sw = z[:, KV_LORA:KV_LORA + LANE], z[:, KV_LORA + LANE:]
    kv = _dot(_rms(c_kv, ga_ref[...]).astype(BF16), wu_ref[...])
    nk = B_HEADS * QK_NOPE
    v_ref[...] = kv[:, nk:].astype(BF16)
    kg = kg_ref[...]
    for h in range(B_HEADS):
        kn, kr = _head_qk_norm_rope(kv[:, h * QK_NOPE:(h + 1) * QK_NOPE], rope, rope_sw,
                                    kg[:, :LANE], kg[:, LANE:2 * LANE], kg[:, 2 * LANE:],
                                    cs_ref[...], sn_ref[...], 1.0)
        k_ref[:, h * B_QK_PAD:h * B_QK_PAD + QK_NOPE] = kn.astype(BF16)
        k_ref[:, h * B_QK_PAD + QK_NOPE:(h + 1) * B_QK_PAD] = kr.astype(BF16)


def _rope_swap(w):
    half = QK_ROPE // 2
    return jnp.concatenate([w[..., half:], w[..., :half]], axis=-1)


def _pad_lanes(w, n=LANE):
    return jnp.pad(w, [(0, 0)] * (w.ndim - 1) + [(0, n - w.shape[-1])])


def _head_gain(g):
    g_rope = g[QK_NOPE:]
    return jnp.concatenate([g[:QK_NOPE], _pad_lanes(g_rope), _pad_lanes(_rope_swap(g_rope))]).reshape(1, -1)


def _latent_kv(x2, kv_norm, w_dkv, kv_a_norm, w_ukv, k_head_norm, cs, sn):
    t = x2.shape[0]
    w_rope = w_dkv[:, KV_LORA:]
    wd = jnp.concatenate([w_dkv[:, :KV_LORA], _pad_lanes(w_rope), _pad_lanes(_rope_swap(w_rope))],
                         axis=1).astype(BF16)
    wu = w_ukv.reshape(KV_LORA, B_HEADS, QK_NOPE + V_HEAD)
    wu = jnp.concatenate([wu[:, :, :QK_NOPE].reshape(KV_LORA, -1), wu[:, :, QK_NOPE:].reshape(KV_LORA, -1)],
                         axis=1).astype(BF16)
    args = (x2, kv_norm.reshape(1, -1), wd, kv_a_norm.reshape(1, -1), wu, _head_gain(k_head_norm), cs, sn)
    row = lambda n: pl.BlockSpec((ROW_TILE, n), lambda i: (i, 0))
    full = lambda a: pl.BlockSpec(a.shape, lambda i: (0,) * a.ndim)
    return pl.pallas_call(
        _latent_kv_kernel,
        out_shape=(jax.ShapeDtypeStruct((t, B_HEADS * B_QK_PAD), BF16),
                   jax.ShapeDtypeStruct((t, B_HEADS * V_HEAD), BF16)),
        grid=(t // ROW_TILE,),
        in_specs=[row(D_MODEL)] + [full(a) for a in args[1:6]] + [row(LANE), row(LANE)],
        out_specs=(row(B_HEADS * B_QK_PAD), row(B_HEADS * V_HEAD)),
        compiler_params=_cparams(("parallel",)),
        name="latent_kv",
    )(*args)


def _b_proj_kernel(x_ref, g_ref, win_ref, ga_ref, wuq_ref, qg_ref, cs_ref, sn_ref, q_ref, mq_ref):
    hn = _rms(x_ref[...], g_ref[...]).astype(BF16)
    proj = _dot(hn, win_ref[...])
    mq_ref[...] = proj[:, Q_LORA:]
    qall = _dot(_rms(proj[:, :Q_LORA], ga_ref[...]).astype(BF16), wuq_ref[...])
    qg = qg_ref[...]
    per_head = QK_NOPE + 2 * LANE
    for h in range(B_HEADS):
        base = h * per_head
        qn, qr = _head_qk_norm_rope(qall[:, base:base + QK_NOPE],
                                    qall[:, base + QK_NOPE:base + QK_NOPE + LANE],
                                    qall[:, base + QK_NOPE + LANE:base + per_head],
                                    qg[:, :LANE], qg[:, LANE:2 * LANE], qg[:, 2 * LANE:],
                                    cs_ref[...], sn_ref[...], B_QK_HEAD ** -0.5)
        q_ref[:, h * B_QK_PAD:h * B_QK_PAD + QK_NOPE] = qn.astype(BF16)
        q_ref[:, h * B_QK_PAD + QK_NOPE:(h + 1) * B_QK_PAD] = qr.astype(BF16)


def _b_projection(x2, g, w_in, q_a_g, w_uq, q_head_g, cs, sn):
    t = x2.shape[0]
    wq = w_uq.reshape(Q_LORA, B_HEADS, B_QK_HEAD)
    w_rope = wq[:, :, QK_NOPE:]
    wq = jnp.concatenate([wq[:, :, :QK_NOPE], _pad_lanes(w_rope), _pad_lanes(_rope_swap(w_rope))],
                         axis=-1).reshape(Q_LORA, -1).astype(BF16)
    args = (x2, g.reshape(1, -1), w_in.astype(BF16), q_a_g.reshape(1, -1), wq, _head_gain(q_head_g), cs, sn)
    row = lambda n: pl.BlockSpec((ROW_TILE, n), lambda i: (i, 0))
    full = lambda a: pl.BlockSpec(a.shape, lambda i: (0,) * a.ndim)
    return pl.pallas_call(
        _b_proj_kernel,
        out_shape=(jax.ShapeDtypeStruct((t, B_HEADS * B_QK_PAD), BF16),
                   jax.ShapeDtypeStruct((t, M_W), F32)),
        grid=(t // ROW_TILE,),
        in_specs=[row(D_MODEL)] + [full(a) for a in args[1:6]] + [row(LANE), row(LANE)],
        out_specs=(row(B_HEADS * B_QK_PAD), row(M_W)),
        compiler_params=_cparams(("parallel",)),
        name="mla_q_proj",
    )(*args)


def _attn_kernel(q_ref, k_ref, v_ref, out_ref):
    tq, tk = ATT_Q_TILE, ATT_K_TILE
    i = pl.program_id(2)
    q = q_ref[0]

    def block(j, carry, masked):
        m, l, acc = carry
        off = pl.multiple_of(j * tk, tk)
        s = _dot_nt(q, k_ref[0, pl.ds(off, tk), :])
        if masked:
            r = lax.broadcasted_iota(I32, (tq, tk), 0)
            c = lax.broadcasted_iota(I32, (tq, tk), 1)
            s = jnp.where(c <= r, s, -jnp.inf)
        m_new = jnp.maximum(m, jnp.max(s, axis=-1, keepdims=True))
        a = jnp.exp(m - m_new)
        p = jnp.exp(s - m_new)
        l = a * l + jnp.sum(p, axis=-1, keepdims=True)
        acc = a * acc + _dot(p.astype(BF16), v_ref[0, pl.ds(off, tk), :])
        return m_new, l, acc

    init = (jnp.full((tq, 1), -jnp.inf, F32), jnp.zeros((tq, 1), F32), jnp.zeros((tq, V_HEAD), F32))
    carry = lax.fori_loop(0, i, lambda j, c: block(j, c, False), init)
    _, l, acc = block(i, carry, True)
    out_ref[0] = (acc / l).astype(BF16)


def _causal_attention(q, k, v):
    nb, s, _ = q.shape
    assert ATT_Q_TILE == ATT_K_TILE
    return pl.pallas_call(
        _attn_kernel,
        out_shape=jax.ShapeDtypeStruct((nb, s, B_HEADS * V_HEAD), BF16),
        grid=(nb, B_HEADS, s // ATT_Q_TILE),
        in_specs=[pl.BlockSpec((1, ATT_Q_TILE, B_QK_PAD), lambda b, h, i: (b, i, h)),
                  pl.BlockSpec((1, s, B_QK_PAD), lambda b, h, i: (b, 0, h)),
                  pl.BlockSpec((1, s, V_HEAD), lambda b, h, i: (b, 0, h))],
        out_specs=pl.BlockSpec((1, ATT_Q_TILE, V_HEAD), lambda b, h, i: (b, i, h)),
        compiler_params=_cparams(("parallel", "parallel", "arbitrary")),
        name="causal_attention",
    )(q, k, v)


def _router_kernel(x_ref, g_ref, wr_ref, hn_ref, idx_ref, gate_ref):
    hn = _rms(x_ref[...], g_ref[...])
    hn_ref[...] = hn.astype(BF16)
    logits = jnp.dot(hn, wr_ref[...], precision=HIGHEST, preferred_element_type=F32)
    lane = lax.broadcasted_iota(I32, logits.shape, 1)
    logits = jnp.where(lane < N_EXPERTS, logits, -jnp.inf)
    v1 = jnp.max(logits, axis=-1, keepdims=True)
    i1 = jnp.min(jnp.where(logits == v1, lane, LANE), axis=-1, keepdims=True)
    rest = jnp.where(lane == i1, -jnp.inf, logits)
    v2 = jnp.max(rest, axis=-1, keepdims=True)
    i2 = jnp.min(jnp.where(rest == v2, lane, LANE), axis=-1, keepdims=True)
    e2 = jnp.exp(v2 - v1)
    den = 1.0 + e2
    idx_ref[...] = jnp.where(lane == 0, i1, jnp.where(lane == 1, i2, 0))
    gate_ref[...] = jnp.where(lane == 0, 1.0 / den, jnp.where(lane == 1, e2 / den, 0.0))


def _router(x2, g, w_router):
    t = x2.shape[0]
    row = lambda n: pl.BlockSpec((ROW_TILE, n), lambda i: (i, 0))
    wr = _pad_lanes(w_router)
    return pl.pallas_call(
        _router_kernel,
        out_shape=(jax.ShapeDtypeStruct((t, D_MODEL), BF16), jax.ShapeDtypeStruct((t, LANE), I32),
                   jax.ShapeDtypeStruct((t, LANE), F32)),
        grid=(t // ROW_TILE,),
        in_specs=[row(D_MODEL), pl.BlockSpec((1, D_MODEL), lambda i: (0, 0)),
                  pl.BlockSpec(wr.shape, lambda i: (0, 0))],
        out_specs=(row(D_MODEL), row(LANE), row(LANE)),
        compiler_params=_cparams(("parallel",)),
        name="moe_router",
    )(x2, g.reshape(1, -1), wr)


def _rank_kernel(idx_ref, pos_ref, tile_ref, run_ref, start_ref):
    phase, blk = pl.program_id(0), pl.program_id(1)
    tb = RANK_TILE
    lane = lax.broadcasted_iota(I32, (tb, LANE), 1)
    idx = idx_ref[...]
    oh0 = (lane == idx[:, 0:1]).astype(F32)
    oh1 = (lane == idx[:, 1:2]).astype(F32)
    both = oh0 + oh1

    @pl.when((phase == 0) & (blk == 0))
    def _():
        run_ref[...] = jnp.zeros_like(run_ref)

    @pl.when(phase == 0)
    def _():
        run_ref[...] += jnp.sum(both, axis=0, keepdims=True)
        pos_ref[...] = jnp.zeros_like(pos_ref)

    @pl.when((phase == 1) & (blk == 0))
    def _():
        counts = run_ref[...]
        tiles = jnp.ceil(counts * (1.0 / MOE_ROW_TILE))
        r = lax.broadcasted_iota(I32, (LANE, LANE), 0)
        c = lax.broadcasted_iota(I32, (LANE, LANE), 1)
        before = (r < c).astype(F32)
        tile_start = jnp.dot(tiles, before, precision=HIGHEST, preferred_element_type=F32)
        start_ref[...] = tile_start * MOE_ROW_TILE
        tile_end = tile_start + tiles
        n_col = lax.broadcasted_iota(I32, (LANE, LANE), 0).astype(F32)
        ended = ((n_col >= tile_end) & (c < N_EXPERTS)).astype(F32)
        expert = jnp.sum(ended, axis=-1, keepdims=True)
        total = jnp.max(tile_end, axis=-1, keepdims=True)
        col = lax.broadcasted_iota(I32, tile_ref.shape, 1)
        tile_ref[...] = jnp.where(col == 0, expert, total).astype(I32)
        run_ref[...] = jnp.zeros_like(run_ref)

    @pl.when(phase == 1)
    def _():
        r = lax.broadcasted_iota(I32, (tb, tb), 0)
        c = lax.broadcasted_iota(I32, (tb, tb), 1)
        strict = (c < r).astype(BF16)
        base = _dot(strict, both.astype(BF16)) + run_ref[...] + start_ref[...]
        p0 = jnp.sum(oh0 * base, axis=-1, keepdims=True)
        p1 = jnp.sum(oh1 * base, axis=-1, keepdims=True)
        pos_ref[...] = jnp.where(lane == 0, p0, jnp.where(lane == 1, p1, 0.0)).astype(I32)
        run_ref[...] += jnp.sum(both, axis=0, keepdims=True)


def _rank(idx):
    t = idx.shape[0]
    return pl.pallas_call(
        _rank_kernel,
        out_shape=(jax.ShapeDtypeStruct((t, LANE), I32), jax.ShapeDtypeStruct((LANE, 2), I32)),
        grid=(2, t // RANK_TILE),
        in_specs=[pl.BlockSpec((RANK_TILE, LANE), lambda p, i: (i, 0))],
        out_specs=(pl.BlockSpec((RANK_TILE, LANE), lambda p, i: (i * p, 0)),
                   pl.BlockSpec((LANE, 2), lambda p, i: (0, 0))),
        scratch_shapes=[pltpu.VMEM((1, LANE), F32), pltpu.VMEM((1, LANE), F32)],
        compiler_params=_cparams(("arbitrary", "arbitrary")),
        name="moe_rank",
    )(idx)


def _permute_kernel(src_idx_ref, dst_idx_ref, src_ref, init_ref, dst_ref, sem):
    del init_ref
    n = src_idx_ref.shape[0]
    nchunk = n // DMA_CHUNK

    def copy(a):
        return pltpu.make_async_copy(src_ref.at[src_idx_ref[a]], dst_ref.at[dst_idx_ref[a]], sem)

    def start_chunk(c):
        def body(r, carry):
            copy(c * DMA_CHUNK + r).start()
            return carry
        lax.fori_loop(0, DMA_CHUNK, body, 0)

    def wait_chunk():
        def body(r, carry):
            pltpu.make_async_copy(src_ref.at[0], dst_ref.at[0], sem).wait()
            return carry
        lax.fori_loop(0, DMA_CHUNK, body, 0)

    start_chunk(0)

    def body(c, carry):
        start_chunk(c)
        wait_chunk()
        return carry

    lax.fori_loop(1, nchunk, body, 0)
    wait_chunk()


def _permute_rows(src, src_idx, dst_idx, dst_init):
    assert src_idx.shape == dst_idx.shape and src_idx.shape[0] % DMA_CHUNK == 0
    return pl.pallas_call(
        _permute_kernel,
        out_shape=jax.ShapeDtypeStruct(dst_init.shape, dst_init.dtype),
        grid_spec=pltpu.PrefetchScalarGridSpec(
            num_scalar_prefetch=2,
            grid=(1,),
            in_specs=[pl.BlockSpec(memory_space=pl.ANY), pl.BlockSpec(memory_space=pl.ANY)],
            out_specs=pl.BlockSpec(memory_space=pl.ANY),
            scratch_shapes=[pltpu.SemaphoreType.DMA],
        ),
        input_output_aliases={3: 0},
        compiler_params=pltpu.CompilerParams(dimension_semantics=("arbitrary",)),
        name="permute_rows",
    )(src_idx, dst_idx, src, dst_init)


def _moe_kernel(expert_ref, ntiles_ref, x_ref, wg_ref, wu_ref, wd_ref, out_ref, acc_ref):
    del expert_ref
    i, j = pl.program_id(0), pl.program_id(1)
    active = i < ntiles_ref[0]

    @pl.when(j == 0)
    def _():
        acc_ref[...] = jnp.zeros_like(acc_ref)

    @pl.when(active)
    def _():
        x = x_ref[...]
        gate = _dot(x, wg_ref[0])
        up = _dot(x, wu_ref[0])
        acc_ref[...] += _dot((jax.nn.silu(gate) * up).astype(BF16), wd_ref[0])

    @pl.when(j == pl.num_programs(1) - 1)
    def _():
        out_ref[...] = acc_ref[...].astype(BF16)


def _moe_experts(tile_expert, n_tiles, xs, w_gate, w_up, w_down):
    rows = xs.shape[0]
    tm, tf = MOE_ROW_TILE, FFN_FF_TILE
    nj = D_FF // tf

    def w_idx(i, j, expert_ref, ntiles_ref):
        e = jnp.minimum(expert_ref[i], N_EXPERTS - 1)
        return e, jnp.where(i < ntiles_ref[0], j, nj - 1)

    def w_up_map(i, j, e_ref, n_ref):
        e, jj = w_idx(i, j, e_ref, n_ref)
        return e, 0, jj

    def w_down_map(i, j, e_ref, n_ref):
        e, jj = w_idx(i, j, e_ref, n_ref)
        return e, jj, 0

    return pl.pallas_call(
        _moe_kernel,
        out_shape=jax.ShapeDtypeStruct((rows, D_MODEL), BF16),
        grid_spec=pltpu.PrefetchScalarGridSpec(
            num_scalar_prefetch=2,
            grid=(rows // tm, nj),
            in_specs=[pl.BlockSpec((tm, D_MODEL), lambda i, j, e, n: (i, 0)),
                      pl.BlockSpec((1, D_MODEL, tf), w_up_map),
                      pl.BlockSpec((1, D_MODEL, tf), w_up_map),
                      pl.BlockSpec((1, tf, D_MODEL), w_down_map)],
            out_specs=pl.BlockSpec((tm, D_MODEL), lambda i, j, e, n: (i, 0)),
            scratch_shapes=[pltpu.VMEM((tm, D_MODEL), F32)],
        ),
        compiler_params=_cparams(("arbitrary", "arbitrary")),
        name="moe_experts",
    )(tile_expert, n_tiles, xs, w_gate, w_up, w_down)


def _combine_kernel(x_ref, y_ref, gate_ref, out_ref):
    g = gate_ref[...]
    y = y_ref[...]
    out_ref[...] = (x_ref[...] + g[:, 0:1] * y[:, :D_MODEL].astype(F32)
                    + g[:, 1:2] * y[:, D_MODEL:].astype(F32))


def _combine(x2, y_pairs, gates):
    t = x2.shape[0]
    row = lambda n: pl.BlockSpec((ROW_TILE, n), lambda i: (i, 0))
    return pl.pallas_call(
        _combine_kernel,
        out_shape=jax.ShapeDtypeStruct((t, D_MODEL), F32),
        grid=(t // ROW_TILE,),
        in_specs=[row(D_MODEL), row(TOP_K * D_MODEL), row(LANE)],
        out_specs=row(D_MODEL),
        compiler_params=_cparams(("parallel",)),
        name="moe_combine",
    )(x2, y_pairs, gates)


def _moe_ffn(x2, g, w_router, w_gate_up, w_down):
    t = x2.shape[0]
    slab = (D_MODEL // LANE, LANE)
    hn, idx, gates = _router(x2, g, w_router)
    pos, tile_info = _rank(idx)
    pair_pos = pos[:, :TOP_K].reshape(-1)
    pair_tok = jnp.arange(t * TOP_K, dtype=I32) // TOP_K
    rows = t * TOP_K + N_EXPERTS * MOE_ROW_TILE
    xs = _permute_rows(hn.reshape((t,) + slab), pair_tok, pair_pos, jnp.zeros((rows,) + slab, BF16))
    ys = _moe_experts(tile_info[:, 0], tile_info[:1, 1], xs.reshape(rows, D_MODEL),
                      w_gate_up[..., :D_FF].astype(BF16), w_gate_up[..., D_FF:].astype(BF16),
                      w_down.astype(BF16))
    pair_rows = jnp.arange(t * TOP_K, dtype=I32)
    y_pairs = _permute_rows(ys.reshape((rows,) + slab), pair_pos, pair_rows,
                            jnp.zeros((t * TOP_K,) + slab, BF16))
    return _combine(x2, y_pairs.reshape(t, TOP_K * D_MODEL), gates)


def _pad_heads(w, heads, dim, pad):
    w = w.reshape(w.shape[:-1] + (heads, dim))
    return _pad_lanes(w, pad).reshape(w.shape[:-2] + (heads * pad,))


def kernel(x, mem, positions, a_norm, a_w_in, a_gate_bias, a_head_norm, a_w_out, b_norm, b_w_in, b_q_a_norm, b_w_uq, b_q_head_norm, b_w_out, kv_norm, w_dkv, kv_a_norm, w_ukv, k_head_norm, mem_norm, mem_w_kv, mem_q_norm, mem_k_norm, ffn_norm, dense_w_gate_up, dense_w_down, moe_router, moe_w_gate_up, moe_w_down):
    nb, seq, _ = x.shape
    t = nb * seq
    x2 = x.reshape(t, D_MODEL)
    gmat = jnp.kron(jnp.eye(M_HEADS, dtype=F32), jnp.full((M_HEAD_DIM, M_HEAD_DIM), 1.0 / M_HEAD_DIM, F32))

    kbd0, vbd0 = _memory_kv(mem, mem_norm[0], mem_w_kv[0], mem_k_norm[0], gmat)
    w_in = a_w_in[0]
    qk_w, v_w = A_HEADS * A_QK_DIM, A_HEADS * A_V_DIM
    o0, o1, o2, o3, o4 = qk_w, 2 * qk_w, 2 * qk_w + v_w, 2 * qk_w + 2 * v_w, 2 * qk_w + 2 * v_w + 2 * A_HEADS
    w_main = jnp.concatenate([
        _pad_heads(w_in[:, :o0], A_HEADS, A_QK_DIM, A_QK_PAD),
        _pad_heads(w_in[:, o0:o1], A_HEADS, A_QK_DIM, A_QK_PAD),
        _pad_heads(w_in[:, o1:o2], A_HEADS, A_V_DIM, A_V_PAD),
        _pad_heads(w_in[:, o2:o3], A_HEADS, A_V_DIM, A_V_PAD),
        w_in[:, o4:]], axis=1).astype(BF16)
    q, k, v, o, mq, gc, gr = _a_projection(x2, a_norm[0], w_main, w_in[:, o3:o4], a_gate_bias[0])
    gc, gr = _mlstm_gates(gc, gr)
    gcol = gc.reshape(nb, seq, 2, A_HEADS).transpose(0, 3, 1, 2)
    grow = gr.reshape(2, A_HEADS, nb, seq).transpose(2, 1, 0, 3)
    three = lambda a: a.reshape(nb, seq, a.shape[-1])
    hm = _mlstm(three(q), three(k), three(v), three(o), gcol, grow,
                _pad_heads(a_head_norm[0].reshape(1, -1), A_HEADS, A_V_DIM, A_V_PAD))
    w_out = a_w_out[0]
    w_out_h = jnp.pad(w_out[:v_w].reshape(A_HEADS, A_V_DIM, D_MODEL), ((0, 0), (0, A_V_PAD - A_V_DIM), (0, 0)))
    w_out_h = w_out_h.reshape(A_HEADS * A_V_PAD, D_MODEL).astype(BF16)
    x2 = _mix_out(x2, hm.reshape(t, -1), mq, kbd0, vbd0, gmat, mem_q_norm[0],
                  w_out_h, w_out[v_w:].astype(BF16), seq)
    wgu = dense_w_gate_up[0]
    x2 = _dense_ffn(x2, ffn_norm[0], wgu[:, :D_FF].astype(BF16), wgu[:, D_FF:].astype(BF16),
                    dense_w_down[0].astype(BF16))

    cs, sn = _rope_tables(positions)
    k_sh, v_sh = _latent_kv(x2, kv_norm, w_dkv, kv_a_norm, w_ukv, k_head_norm, cs, sn)

    kbd1, vbd1 = _memory_kv(mem, mem_norm[1], mem_w_kv[1], mem_k_norm[1], gmat)
    qh, mq1 = _b_projection(x2, b_norm[0], b_w_in[0], b_q_a_norm[0], b_w_uq[0], b_q_head_norm[0], cs, sn)
    att = _causal_attention(three(qh), three(k_sh), three(v_sh))
    w_out = b_w_out[0]
    n_att = B_HEADS * V_HEAD
    x2 = _mix_out(x2, att.reshape(t, -1), mq1, kbd1, vbd1, gmat, mem_q_norm[1],
                  w_out[:n_att].astype(BF16), w_out[n_att:].astype(BF16), seq)
    x2 = _moe_ffn(x2, ffn_norm[1], moe_router[0], moe_w_gate_up[0], moe_w_down[0])
    return x2.reshape(nb, seq, D_MODEL)
```

```python
import functools

import jax
import jax.numpy as jnp
from jax import lax
from jax.experimental import pallas as pl
from jax.experimental.pallas import tpu as pltpu

F32 = jnp.float32
BF16 = jnp.bfloat16
I32 = jnp.int32

EPS = 1e-6
LANE = 128
VMEM_LIMIT = 48 * 1024 * 1024

D_MODEL = 1024
N_MEM = 256
M_HEADS, M_HEAD_DIM = 4, 64
M_W = M_HEADS * M_HEAD_DIM
A_HEADS, A_QK_DIM, A_V_DIM = 4, 96, 192
A_QK_PAD, A_V_PAD = 128, 256
B_HEADS, Q_LORA, KV_LORA = 6, 384, 256
QK_NOPE, QK_ROPE, V_HEAD = 128, 64, 128
B_QK_HEAD = QK_NOPE + QK_ROPE
B_QK_PAD = 256
VT_HEAD_ROWS = V_HEAD + 16
ROPE_THETA = 10000.0
D_FF = 3584
N_EXPERTS, TOP_K = 8, 2

MLSTM_CHUNK = 256
ROW_TILE = 512
FFN_ROW_TILE = 1024
FFN_FF_TILE = 512
MOE_ROW_TILE = 512
ATT_Q_TILE = 256
ATT_K_TILE = 256
ATT_HEADS_PER_STEP = 6
RANK_TILE = 512

HIGHEST = lax.Precision.HIGHEST


def _cparams(sem):
    return pltpu.CompilerParams(dimension_semantics=sem, vmem_limit_bytes=VMEM_LIMIT)


def _rms(x, g):
    return x * lax.rsqrt(jnp.mean(x * x, axis=-1, keepdims=True) + EPS) * g


def _dot(a, b):
    return jnp.dot(a, b, preferred_element_type=F32)


def _dot_nt(a, b):
    return lax.dot_general(a, b, (((1,), (1,)), ((), ())), preferred_element_type=F32)


def _dot_tn(a, b):
    return lax.dot_general(a, b, (((0,), (0,)), ((), ())), preferred_element_type=F32)


def _group_mean_sq(x, gmat):
    return jnp.dot(x * x, gmat, precision=HIGHEST, preferred_element_type=F32)


def _memkv_kernel(mem_ref, g_ref, w_ref, kg_ref, gmat_ref, kbd_ref, vbd_ref):
    hn = _rms(mem_ref[0], g_ref[...]).astype(BF16)
    kv = _dot(hn, w_ref[...])
    k, v = kv[:, :M_W], kv[:, M_W:]
    kn = k * lax.rsqrt(_group_mean_sq(k, gmat_ref[...]) + EPS) * kg_ref[...]
    lane_head = lax.broadcasted_iota(I32, (1, M_W), 1) // M_HEAD_DIM
    for h in range(M_HEADS):
        keep = lane_head == h
        kbd_ref[0, h * N_MEM:(h + 1) * N_MEM, :] = jnp.where(keep, kn, 0.0).astype(BF16)
        vbd_ref[0, h * N_MEM:(h + 1) * N_MEM, :] = jnp.where(keep, v, 0.0).astype(BF16)


def _memory_kv(mem, g, w_kv, k_g, gmat):
    nb = mem.shape[0]
    out = jax.ShapeDtypeStruct((nb, M_HEADS * N_MEM, M_W), BF16)
    return pl.pallas_call(
        _memkv_kernel,
        out_shape=(out, out),
        grid=(nb,),
        in_specs=[
            pl.BlockSpec((1, N_MEM, D_MODEL), lambda b: (b, 0, 0)),
            pl.BlockSpec((1, D_MODEL), lambda b: (0, 0)),
            pl.BlockSpec((D_MODEL, 2 * M_W), lambda b: (0, 0)),
            pl.BlockSpec((1, M_W), lambda b: (0, 0)),
            pl.BlockSpec((M_W, M_W), lambda b: (0, 0)),
        ],
        out_specs=(pl.BlockSpec((1, M_HEADS * N_MEM, M_W), lambda b: (b, 0, 0)),) * 2,
        compiler_params=_cparams(("parallel",)),
        name="memory_kv",
    )(mem, g.reshape(1, -1), w_kv.astype(BF16), jnp.tile(k_g, M_HEADS).reshape(1, -1), gmat)


def _memory_attention(mq, kbd, vbd, gmat, qg):
    qn = mq * lax.rsqrt(_group_mean_sq(mq, gmat) + EPS) * (qg * (M_HEAD_DIM ** -0.5))
    s = _dot_nt(qn.astype(BF16), kbd)
    ps = []
    for h in range(M_HEADS):
        sh = s[:, h * N_MEM:(h + 1) * N_MEM]
        e = jnp.exp(sh - jnp.max(sh, axis=-1, keepdims=True))
        ps.append((e / jnp.sum(e, axis=-1, keepdims=True)).astype(BF16))
    return _dot(jnp.concatenate(ps, axis=-1), vbd)


def _a_proj_kernel(x_ref, g_ref, w_ref, wif_ref, wift_ref, bc_ref, br_ref,
                   q_ref, k_ref, v_ref, o_ref, mq_ref, gc_ref, gr_ref):
    hn = _rms(x_ref[...], g_ref[...]).astype(BF16)
    nq = A_HEADS * A_QK_PAD
    nv = A_HEADS * A_V_PAD
    q_ref[...] = _dot(hn, w_ref[:, :nq]).astype(BF16)
    k_ref[...] = (_dot(hn, w_ref[:, nq:2 * nq]) * (A_QK_DIM ** -0.5)).astype(BF16)
    v = _dot(hn, w_ref[:, 2 * nq:2 * nq + nv])
    ones_lane = lax.broadcasted_iota(I32, (1, nv), 1) % A_V_PAD == A_V_DIM
    v_ref[...] = jnp.where(ones_lane, 1.0, v).astype(BF16)
    o_ref[...] = _dot(hn, w_ref[:, 2 * nq + nv:2 * nq + 2 * nv])
    mq_ref[...] = _dot(hn, w_ref[:, 2 * nq + 2 * nv:])
    gc_ref[...] = _dot(hn, wif_ref[...])[:, :2 * A_HEADS] + bc_ref[...]
    gr_ref[...] = _dot_nt(wift_ref[...], hn) + br_ref[...]


def _a_projection(x2, g, w_main, w_if, gate_bias):
    t = x2.shape[0]
    nq, nv = A_HEADS * A_QK_PAD, A_HEADS * A_V_PAD
    ng = 2 * A_HEADS
    wif_pad = jnp.pad(w_if, ((0, 0), (0, LANE - ng))).astype(BF16)
    row = lambda n: pl.BlockSpec((ROW_TILE, n), lambda i: (i, 0))
    full = lambda a: pl.BlockSpec(a.shape, lambda i: (0,) * a.ndim)
    args = (x2, g.reshape(1, -1), w_main, wif_pad, w_if.T.astype(BF16),
            gate_bias.reshape(1, ng), gate_bias.reshape(ng, 1))
    return pl.pallas_call(
        _a_proj_kernel,
        out_shape=(jax.ShapeDtypeStruct((t, nq), BF16), jax.ShapeDtypeStruct((t, nq), BF16),
                   jax.ShapeDtypeStruct((t, nv), BF16), jax.ShapeDtypeStruct((t, nv), F32),
                   jax.ShapeDtypeStruct((t, M_W), F32), jax.ShapeDtypeStruct((t, ng), F32),
                   jax.ShapeDtypeStruct((ng, t), F32)),
        grid=(t // ROW_TILE,),
        in_specs=[row(D_MODEL)] + [full(a) for a in args[1:]],
        out_specs=(row(nq), row(nq), row(nv), row(nv), row(M_W), row(ng),
                   pl.BlockSpec((ng, ROW_TILE), lambda i: (0, i))),
        compiler_params=_cparams(("parallel",)),
        name="mlstm_in_proj",
    )(*args)


def _log_sigmoid(f):
    return jnp.minimum(f, 0.0) - jnp.log(1.0 + jnp.exp(-jnp.abs(f)))


def _gates_kernel(gc_ref, gr_ref, oc_ref, or_ref):
    L = MLSTM_CHUNK
    gc = gc_ref[...]
    gr = gr_ref[...]
    r = lax.broadcasted_iota(I32, (L, L), 0)
    c = lax.broadcasted_iota(I32, (L, L), 1)
    lower = (c <= r).astype(F32)
    is_f_col = lax.broadcasted_iota(I32, gc.shape, 1) >= A_HEADS
    is_f_row = lax.broadcasted_iota(I32, gr.shape, 0) >= A_HEADS
    lf_c = jnp.where(is_f_col, _log_sigmoid(gc), 0.0)
    lf_r = jnp.where(is_f_row, _log_sigmoid(gr), 0.0)
    lf_c = jnp.concatenate([lf_c, jnp.zeros((L, LANE - gc.shape[1]), F32)], axis=1)
    cum_c = jnp.dot(lower, lf_c, precision=HIGHEST, preferred_element_type=F32)[:, :gc.shape[1]]
    cum_r = _dot_nt_highest(lf_r, lower)
    oc_ref[...] = jnp.where(is_f_col, cum_c, gc)
    or_ref[...] = jnp.where(is_f_row, cum_r, gr)


def _dot_nt_highest(a, b):
    return lax.dot_general(a, b, (((1,), (1,)), ((), ())), precision=HIGHEST,
                           preferred_element_type=F32)


def _mlstm_gates(gc, gr):
    t, ng = gc.shape
    L = MLSTM_CHUNK
    return pl.pallas_call(
        _gates_kernel,
        out_shape=(jax.ShapeDtypeStruct((t, ng), F32), jax.ShapeDtypeStruct((ng, t), F32)),
        grid=(t // L,),
        in_specs=[pl.BlockSpec((L, ng), lambda i: (i, 0)), pl.BlockSpec((ng, L), lambda i: (0, i))],
        out_specs=(pl.BlockSpec((L, ng), lambda i: (i, 0)), pl.BlockSpec((ng, L), lambda i: (0, i))),
        compiler_params=_cparams(("parallel",)),
        name="mlstm_gates",
    )(gc, gr)


def _mlstm_kernel(q_ref, k_ref, v_ref, o_ref, gc_ref, gr_ref, hg_ref, out_ref, c_ref, m_ref):
    L = MLSTM_CHUNK

    @pl.when(pl.program_id(2) == 0)
    def _():
        c_ref[...] = jnp.zeros_like(c_ref)
        m_ref[...] = jnp.zeros_like(m_ref)

    q, k, v = q_ref[0], k_ref[0], v_ref[0]
    gcol = gc_ref[0, 0]
    grow = gr_ref[0, 0]
    i_c, g_c = gcol[:, 0:1], gcol[:, 1:2]
    i_r, g_r = grow[0:1, :], grow[1:2, :]
    g_last = g_r[:, L - 1:L]
    m_prev = m_ref[0:1, 0:1]

    a_log = g_c + m_prev
    t_idx = lax.broadcasted_iota(I32, (L, L), 0)
    s_idx = lax.broadcasted_iota(I32, (L, L), 1)
    d_log = jnp.where(s_idx <= t_idx, g_c - g_r + i_r, -jnp.inf)
    m_t = jnp.maximum(a_log, jnp.max(d_log, axis=-1, keepdims=True))
    inter = jnp.exp(a_log - m_t)
    p = (jnp.exp(d_log - m_t) * _dot_nt(q, k)).astype(BF16)
    num = inter * _dot(q, c_ref[...].astype(BF16)) + _dot(p, v)
    den = num[:, A_V_DIM:A_V_DIM + 1]
    h = num / jnp.maximum(jnp.abs(den), jnp.exp(-m_t))
    real = lax.broadcasted_iota(I32, (1, A_V_PAD), 1) < A_V_DIM
    h = jnp.where(real, h, 0.0)
    hn = h * lax.rsqrt(jnp.sum(h * h, axis=-1, keepdims=True) * (1.0 / A_V_DIM) + EPS) * hg_ref[...]
    out_ref[0] = (hn * jax.nn.sigmoid(o_ref[0])).astype(BF16)

    w_r = g_last - g_r + i_r
    m_new = jnp.maximum(g_last + m_prev, jnp.max(w_r, axis=-1, keepdims=True))
    e_c = jnp.exp(g_last - g_c + i_c - m_new)
    ev = (e_c * v.astype(F32)).astype(BF16)
    c_ref[...] = jnp.exp(g_last + m_prev - m_new) * c_ref[...] + _dot_tn(k, ev)
    m_ref[...] = jnp.broadcast_to(m_new, m_ref.shape)


def _mlstm(q, k, v, o, gcol, grow, head_g):
    nb, s, _ = q.shape
    L = MLSTM_CHUNK
    blk = lambda w: pl.BlockSpec((1, L, w), lambda b, h, c: (b, c, h))
    return pl.pallas_call(
        _mlstm_kernel,
        out_shape=jax.ShapeDtypeStruct((nb, s, A_HEADS * A_V_PAD), BF16),
        grid=(nb, A_HEADS, s // L),
        in_specs=[blk(A_QK_PAD), blk(A_QK_PAD), blk(A_V_PAD), blk(A_V_PAD),
                  pl.BlockSpec((1, 1, L, 2), lambda b, h, c: (b, h, c, 0)),
                  pl.BlockSpec((1, 1, 2, L), lambda b, h, c: (b, h, 0, c)),
                  pl.BlockSpec((1, A_V_PAD), lambda b, h, c: (0, h))],
        out_specs=blk(A_V_PAD),
        scratch_shapes=[pltpu.VMEM((A_QK_PAD, A_V_PAD), F32), pltpu.VMEM((8, LANE), F32)],
        compiler_params=_cparams(("parallel", "parallel", "arbitrary")),
        name="mlstm_chunkwise",
    )(q, k, v, o, gcol, grow, head_g)


def _mix_out_kernel(x_ref, h_ref, mq_ref, kbd_ref, vbd_ref, gmat_ref, qg_ref, w1_ref, w2_ref, out_ref):
    mo = _memory_attention(mq_ref[...], kbd_ref[0], vbd_ref[0], gmat_ref[...], qg_ref[...])
    out_ref[...] = x_ref[...] + _dot(h_ref[...], w1_ref[...]) + _dot(mo.astype(BF16), w2_ref[...])


def _mix_out(x2, h2, mq, kbd, vbd, gmat, qg, w_main, w_mem, seq):
    t = x2.shape[0]
    tm = ROW_TILE
    row = lambda n: pl.BlockSpec((tm, n), lambda i: (i, 0))
    full = lambda a: pl.BlockSpec(a.shape, lambda i: (0,) * a.ndim)
    per_batch = pl.BlockSpec((1,) + kbd.shape[1:], lambda i: ((i * tm) // seq, 0, 0))
    qg_t = jnp.tile(qg, M_HEADS).reshape(1, -1)
    return pl.pallas_call(
        _mix_out_kernel,
        out_shape=jax.ShapeDtypeStruct((t, D_MODEL), F32),
        grid=(t // tm,),
        in_specs=[row(D_MODEL), row(h2.shape[1]), row(M_W), per_batch, per_batch,
                  full(gmat), full(qg_t), full(w_main), full(w_mem)],
        out_specs=row(D_MODEL),
        compiler_params=_cparams(("parallel",)),
        name="mixer_out_proj",
    )(x2, h2, mq, kbd, vbd, gmat, qg_t, w_main, w_mem)


def _ffn_kernel(x_ref, g_ref, wg_ref, wu_ref, wd_ref, out_ref, hn_ref, acc_ref):
    j = pl.program_id(1)

    @pl.when(j == 0)
    def _():
        hn_ref[...] = _rms(x_ref[...], g_ref[...]).astype(BF16)
        acc_ref[...] = jnp.zeros_like(acc_ref)

    hn = hn_ref[...]
    gate = _dot(hn, wg_ref[...])
    up = _dot(hn, wu_ref[...])
    acc_ref[...] += _dot((jax.nn.silu(gate) * up).astype(BF16), wd_ref[...])

    @pl.when(j == pl.num_programs(1) - 1)
    def _():
        out_ref[...] = x_ref[...] + acc_ref[...]


def _dense_ffn(x2, g, w_gate, w_up, w_down):
    t = x2.shape[0]
    tm, tf = FFN_ROW_TILE, FFN_FF_TILE
    return pl.pallas_call(
        _ffn_kernel,
        out_shape=jax.ShapeDtypeStruct((t, D_MODEL), F32),
        grid=(t // tm, D_FF // tf),
        in_specs=[pl.BlockSpec((tm, D_MODEL), lambda i, j: (i, 0)),
                  pl.BlockSpec((1, D_MODEL), lambda i, j: (0, 0)),
                  pl.BlockSpec((D_MODEL, tf), lambda i, j: (0, j)),
                  pl.BlockSpec((D_MODEL, tf), lambda i, j: (0, j)),
                  pl.BlockSpec((tf, D_MODEL), lambda i, j: (j, 0))],
        out_specs=pl.BlockSpec((tm, D_MODEL), lambda i, j: (i, 0)),
        scratch_shapes=[pltpu.VMEM((tm, D_MODEL), BF16), pltpu.VMEM((tm, D_MODEL), F32)],
        compiler_params=_cparams(("parallel", "arbitrary")),
        name="dense_swiglu",
    )(x2, g.reshape(1, -1), w_gate, w_up, w_down)


def _rope_kernel(pos_ref, inv_ref, sign_ref, cs_ref, sn_ref):
    ang = pos_ref[...].astype(F32) * inv_ref[...]
    cs_ref[...] = jnp.cos(ang)
    sn_ref[...] = jnp.sin(ang) * sign_ref[...]


def _rope_tables(positions):
    t = positions.size
    half = QK_ROPE // 2
    inv = 1.0 / (ROPE_THETA ** (jnp.arange(0, QK_ROPE, 2, dtype=F32) / QK_ROPE))
    pad = jnp.zeros((LANE - QK_ROPE,), F32)
    inv_l = jnp.concatenate([inv, inv, pad]).reshape(1, LANE)
    sign = jnp.concatenate([-jnp.ones((half,), F32), jnp.ones((half,), F32), pad]).reshape(1, LANE)
    out = jax.ShapeDtypeStruct((t, LANE), F32)
    return pl.pallas_call(
        _rope_kernel,
        out_shape=(out, out),
        grid=(t // ROW_TILE,),
        in_specs=[pl.BlockSpec((ROW_TILE, 1), lambda i: (i, 0)),
                  pl.BlockSpec((1, LANE), lambda i: (0, 0)),
                  pl.BlockSpec((1, LANE), lambda i: (0, 0))],
        out_specs=(pl.BlockSpec((ROW_TILE, LANE), lambda i: (i, 0)),) * 2,
        compiler_params=_cparams(("parallel",)),
        name="rope_tables",
    )(positions.reshape(t, 1), inv_l, sign)


def _head_qk_norm_rope(nope, rope, rope_sw, g_nope, g_rope, g_rope_sw, cs, sn, scale):
    ss = jnp.sum(nope * nope, axis=-1, keepdims=True) + jnp.sum(rope * rope, axis=-1, keepdims=True)
    r = lax.rsqrt(ss * (1.0 / B_QK_HEAD) + EPS) * scale
    return nope * r * g_nope, (rope * g_rope * cs + rope_sw * g_rope_sw * sn) * r


def _latent_kv_kernel(x_ref, g_ref, wd_ref, ga_ref, wuk_ref, wuvt_ref, kg_ref, cs_ref, sn_ref, k_ref, vt_ref):
    hn = _rms(x_ref[...], g_ref[...]).astype(BF16)
    z = _dot(hn, wd_ref[...])
    c_kv = z[:, :KV_LORA]
    rope, rope_sw = z[:, KV_LORA:KV_LORA + LANE], z[:, KV_LORA + LANE:]
    cn = _rms(c_kv, ga_ref[...]).astype(BF16)
    kv = _dot(cn, wuk_ref[...])
    vt = _dot_nt(wuvt_ref[...], cn)
    tm = vt.shape[1]
    ones_row = (lax.broadcasted_iota(I32, (VT_HEAD_ROWS - V_HEAD, tm), 0) == 0).astype(BF16)
    for h in range(B_HEADS):
        vt_ref[h * VT_HEAD_ROWS:h * VT_HEAD_ROWS + V_HEAD, :] = vt[h * V_HEAD:(h + 1) * V_HEAD].astype(BF16)
        vt_ref[h * VT_HEAD_ROWS + V_HEAD:(h + 1) * VT_HEAD_ROWS, :] = ones_row
    kg = kg_ref[...]
    for h in range(B_HEADS):
        kn, kr = _head_qk_norm_rope(kv[:, h * QK_NOPE:(h + 1) * QK_NOPE], rope, rope_sw,
                                    kg[:, :LANE], kg[:, LANE:2 * LANE], kg[:, 2 * LANE:],
                                    cs_ref[...], sn_ref[...], 1.0)
        k_ref[:, h * B_QK_PAD:h * B_QK_PAD + QK_NOPE] = kn.astype(BF16)
        k_ref[:, h * B_QK_PAD + QK_NOPE:(h + 1) * B_QK_PAD] = kr.astype(BF16)


def _rope_swap(w):
    half = QK_ROPE // 2
    return jnp.concatenate([w[..., half:], w[..., :half]], axis=-1)


def _pad_lanes(w, n=LANE):
    return jnp.pad(w, [(0, 0)] * (w.ndim - 1) + [(0, n - w.shape[-1])])


def _head_gain(g):
    g_rope = g[QK_NOPE:]
    return jnp.concatenate([g[:QK_NOPE], _pad_lanes(g_rope), _pad_lanes(_rope_swap(g_rope))]).reshape(1, -1)


def _latent_kv(x2, kv_norm, w_dkv, kv_a_norm, w_ukv, k_head_norm, cs, sn):
    t = x2.shape[0]
    w_rope = w_dkv[:, KV_LORA:]
    wd = jnp.concatenate([w_dkv[:, :KV_LORA], _pad_lanes(w_rope), _pad_lanes(_rope_swap(w_rope))],
                         axis=1).astype(BF16)
    wu = w_ukv.reshape(KV_LORA, B_HEADS, QK_NOPE + V_HEAD)
    wuk = wu[:, :, :QK_NOPE].reshape(KV_LORA, -1).astype(BF16)
    wuvt = wu[:, :, QK_NOPE:].reshape(KV_LORA, -1).T.astype(BF16)
    args = (x2, kv_norm.reshape(1, -1), wd, kv_a_norm.reshape(1, -1), wuk, wuvt, _head_gain(k_head_norm), cs, sn)
    row = lambda n: pl.BlockSpec((ROW_TILE, n), lambda i: (i, 0))
    full = lambda a: pl.BlockSpec(a.shape, lambda i: (0,) * a.ndim)
    return pl.pallas_call(
        _latent_kv_kernel,
        out_shape=(jax.ShapeDtypeStruct((t, B_HEADS * B_QK_PAD), BF16),
                   jax.ShapeDtypeStruct((B_HEADS * VT_HEAD_ROWS, t), BF16)),
        grid=(t // ROW_TILE,),
        in_specs=[row(D_MODEL)] + [full(a) for a in args[1:7]] + [row(LANE), row(LANE)],
        out_specs=(row(B_HEADS * B_QK_PAD), pl.BlockSpec((B_HEADS * VT_HEAD_ROWS, ROW_TILE), lambda i: (0, i))),
        compiler_params=_cparams(("parallel",)),
        name="latent_kv",
    )(*args)


def _b_proj_kernel(x_ref, g_ref, win_ref, ga_ref, wuq_ref, qg_ref, cs_ref, sn_ref, q_ref, mq_ref):
    hn = _rms(x_ref[...], g_ref[...]).astype(BF16)
    proj = _dot(hn, win_ref[...])
    mq_ref[...] = proj[:, Q_LORA:]
    qall = _dot(_rms(proj[:, :Q_LORA], ga_ref[...]).astype(BF16), wuq_ref[...])
    qg = qg_ref[...]
    per_head = QK_NOPE + 2 * LANE
    for h in range(B_HEADS):
        base = h * per_head
        qn, qr = _head_qk_norm_rope(qall[:, base:base + QK_NOPE],
                                    qall[:, base + QK_NOPE:base + QK_NOPE + LANE],
                                    qall[:, base + QK_NOPE + LANE:base + per_head],
                                    qg[:, :LANE], qg[:, LANE:2 * LANE], qg[:, 2 * LANE:],
                                    cs_ref[...], sn_ref[...], B_QK_HEAD ** -0.5)
        q_ref[:, h * B_QK_PAD:h * B_QK_PAD + QK_NOPE] = qn.astype(BF16)
        q_ref[:, h * B_QK_PAD + QK_NOPE:(h + 1) * B_QK_PAD] = qr.astype(BF16)


def _b_projection(x2, g, w_in, q_a_g, w_uq, q_head_g, cs, sn):
    t = x2.shape[0]
    wq = w_uq.reshape(Q_LORA, B_HEADS, B_QK_HEAD)
    w_rope = wq[:, :, QK_NOPE:]
    wq = jnp.concatenate([wq[:, :, :QK_NOPE], _pad_lanes(w_rope), _pad_lanes(_rope_swap(w_rope))],
                         axis=-1).reshape(Q_LORA, -1).astype(BF16)
    args = (x2, g.reshape(1, -1), w_in.astype(BF16), q_a_g.reshape(1, -1), wq, _head_gain(q_head_g), cs, sn)
    row = lambda n: pl.BlockSpec((ROW_TILE, n), lambda i: (i, 0))
    full = lambda a: pl.BlockSpec(a.shape, lambda i: (0,) * a.ndim)
    return pl.pallas_call(
        _b_proj_kernel,
        out_shape=(jax.ShapeDtypeStruct((t, B_HEADS * B_QK_PAD), BF16),
                   jax.ShapeDtypeStruct((t, M_W), F32)),
        grid=(t // ROW_TILE,),
        in_specs=[row(D_MODEL)] + [full(a) for a in args[1:6]] + [row(LANE), row(LANE)],
        out_specs=(row(B_HEADS * B_QK_PAD), row(M_W)),
        compiler_params=_cparams(("parallel",)),
        name="mla_q_proj",
    )(*args)


def _attn_kernel(q_ref, k_ref, vt_ref, out_ref):
    tq, tk = ATT_Q_TILE, ATT_K_TILE
    i = pl.program_id(2)

    def block(h, j, carry, masked):
        m, acc = carry
        off = pl.multiple_of(j * tk, tk)
        q = q_ref[0, :, h * B_QK_PAD:(h + 1) * B_QK_PAD]
        st = _dot_nt(k_ref[0, pl.ds(off, tk), h * B_QK_PAD:(h + 1) * B_QK_PAD], q)
        if masked:
            key = lax.broadcasted_iota(I32, (tk, tq), 0)
            qry = lax.broadcasted_iota(I32, (tk, tq), 1)
            st = jnp.where(key <= qry, st, -jnp.inf)
        m_new = jnp.maximum(m, jnp.max(st, axis=0, keepdims=True))
        p = jnp.exp(st - m_new).astype(BF16)
        vt = vt_ref[h * VT_HEAD_ROWS:(h + 1) * VT_HEAD_ROWS, pl.ds(off, tk)]
        return m_new, jnp.exp(m - m_new) * acc + _dot(vt, p)

    heads = range(ATT_HEADS_PER_STEP)
    init = tuple((jnp.full((1, tq), -jnp.inf, F32), jnp.zeros((VT_HEAD_ROWS, tq), F32)) for _ in heads)
    carry = lax.fori_loop(0, i, lambda j, c: tuple(block(h, j, c[h], False) for h in heads), init)
    for h in heads:
        _, acc = block(h, i, carry[h], True)
        out_t = acc[:V_HEAD] / acc[V_HEAD:V_HEAD + 1]
        out_ref[0, :, h * V_HEAD:(h + 1) * V_HEAD] = out_t.T.astype(BF16)


def _causal_attention(q, k, vt, seq):
    nb = q.shape[0]
    g = ATT_HEADS_PER_STEP
    assert ATT_Q_TILE == ATT_K_TILE and B_HEADS % g == 0
    return pl.pallas_call(
        _attn_kernel,
        out_shape=jax.ShapeDtypeStruct((nb, seq, B_HEADS * V_HEAD), BF16),
        grid=(nb, B_HEADS // g, seq // ATT_Q_TILE),
        in_specs=[pl.BlockSpec((1, ATT_Q_TILE, g * B_QK_PAD), lambda b, h, i: (b, i, h)),
                  pl.BlockSpec((1, seq, g * B_QK_PAD), lambda b, h, i: (b, 0, h)),
                  pl.BlockSpec((g * VT_HEAD_ROWS, seq), lambda b, h, i: (h, b))],
        out_specs=pl.BlockSpec((1, ATT_Q_TILE, g * V_HEAD), lambda b, h, i: (b, i, h)),
        compiler_params=_cparams(("parallel", "parallel", "arbitrary")),
        name="causal_attention",
    )(q, k, vt)


def _router_kernel(x_ref, g_ref, wr_ref, hn_ref, idx_ref, gate_ref):
    hn = _rms(x_ref[...], g_ref[...])
    hn_ref[...] = hn.astype(BF16)
    logits = jnp.dot(hn, wr_ref[...], precision=HIGHEST, preferred_element_type=F32)
    lane = lax.broadcasted_iota(I32, logits.shape, 1)
    logits = jnp.where(lane < N_EXPERTS, logits, -jnp.inf)
    v1 = jnp.max(logits, axis=-1, keepdims=True)
    i1 = jnp.min(jnp.where(logits == v1, lane, LANE), axis=-1, keepdims=True)
    rest = jnp.where(lane == i1, -jnp.inf, logits)
    v2 = jnp.max(rest, axis=-1, keepdims=True)
    i2 = jnp.min(jnp.where(rest == v2, lane, LANE), axis=-1, keepdims=True)
    e2 = jnp.exp(v2 - v1)
    den = 1.0 + e2
    idx_ref[...] = jnp.where(lane == 0, i1, jnp.where(lane == 1, i2, 0))
    gate_ref[...] = jnp.where(lane == 0, 1.0 / den, jnp.where(lane == 1, e2 / den, 0.0))


def _router(x2, g, w_router):
    t = x2.shape[0]
    row = lambda n: pl.BlockSpec((ROW_TILE, n), lambda i: (i, 0))
    wr = _pad_lanes(w_router)
    return pl.pallas_call(
        _router_kernel,
        out_shape=(jax.ShapeDtypeStruct((t, D_MODEL), BF16), jax.ShapeDtypeStruct((t, LANE), I32),
                   jax.ShapeDtypeStruct((t, LANE), F32)),
        grid=(t // ROW_TILE,),
        in_specs=[row(D_MODEL), pl.BlockSpec((1, D_MODEL), lambda i: (0, 0)),
                  pl.BlockSpec(wr.shape, lambda i: (0, 0))],
        out_specs=(row(D_MODEL), row(LANE), row(LANE)),
        compiler_params=_cparams(("parallel",)),
        name="moe_router",
    )(x2, g.reshape(1, -1), wr)


def _rank_kernel(idx_ref, pos_ref, tile_ref, slot_ref, run_ref, start_ref, lo_ref, hi_ref):
    phase, blk = pl.program_id(0), pl.program_id(1)
    nblk = pl.num_programs(1)
    tb = RANK_TILE
    lane = lax.broadcasted_iota(I32, (tb, LANE), 1)
    idx = idx_ref[...]
    oh0 = (lane == idx[:, 0:1]).astype(F32)
    oh1 = (lane == idx[:, 1:2]).astype(F32)
    both = oh0 + oh1

    @pl.when((phase == 0) & (blk == 0))
    def _():
        run_ref[...] = jnp.zeros_like(run_ref)

    @pl.when(phase == 0)
    def _():
        run_ref[...] += jnp.sum(both, axis=0, keepdims=True)
        pos_ref[...] = jnp.zeros_like(pos_ref)

    @pl.when((phase == 1) & (blk == 0))
    def _():
        counts = run_ref[...]
        tiles = jnp.ceil(counts * (1.0 / MOE_ROW_TILE))
        r = lax.broadcasted_iota(I32, (LANE, LANE), 0)
        c = lax.broadcasted_iota(I32, (LANE, LANE), 1)
        before = (r < c).astype(F32)
        tile_start = jnp.dot(tiles, before, precision=HIGHEST, preferred_element_type=F32)
        start_ref[...] = tile_start * MOE_ROW_TILE
        tile_end = tile_start + tiles
        n_col = lax.broadcasted_iota(I32, (LANE, LANE), 0).astype(F32)
        ended = ((n_col >= tile_end) & (c < N_EXPERTS)).astype(F32)
        expert = jnp.sum(ended, axis=-1, keepdims=True)
        total = jnp.max(tile_end, axis=-1, keepdims=True)
        col = lax.broadcasted_iota(I32, tile_ref.shape, 1)
        tile_ref[...] = jnp.where(col == 0, expert, total).astype(I32)
        run_ref[...] = jnp.zeros_like(run_ref)
        lo_ref[...] = jnp.zeros_like(lo_ref)
        hi_ref[...] = jnp.zeros_like(hi_ref)

    @pl.when(phase == 1)
    def _():
        r = lax.broadcasted_iota(I32, (tb, tb), 0)
        c = lax.broadcasted_iota(I32, (tb, tb), 1)
        strict = (c < r).astype(BF16)
        lo = run_ref[...] + start_ref[...]
        base = _dot(strict, both.astype(BF16)) + lo
        p0 = jnp.sum(oh0 * base, axis=-1, keepdims=True)
        p1 = jnp.sum(oh1 * base, axis=-1, keepdims=True)
        pos_ref[...] = jnp.where(lane == 0, p0, jnp.where(lane == 1, p1, 0.0)).astype(I32)
        run_ref[...] += jnp.sum(both, axis=0, keepdims=True)
        mine = lax.broadcasted_iota(I32, lo_ref.shape, 0) == blk
        lo_ref[...] = jnp.where(mine, lo, lo_ref[...])
        hi_ref[...] = jnp.where(mine, run_ref[...] + start_ref[...], hi_ref[...])

    @pl.when((phase == 1) & (blk == nblk - 1))
    def _():
        lo, hi = lo_ref[...], hi_ref[...]
        first = jnp.floor(lo * (1.0 / MOE_ROW_TILE))
        last = jnp.floor((hi - 1.0) * (1.0 / MOE_ROW_TILE))
        some = hi > lo
        two = some & (last > first)
        n = lo_ref.shape[0]
        slot_ref[0 * n:1 * n, :] = first.astype(I32)
        slot_ref[1 * n:2 * n, :] = jnp.where(two, last, first).astype(I32)
        slot_ref[2 * n:3 * n, :] = some.astype(I32)
        slot_ref[3 * n:4 * n, :] = two.astype(I32)


def _rank(idx):
    t = idx.shape[0]
    nblk = t // RANK_TILE
    const = lambda shape: pl.BlockSpec(shape, lambda p, i: (0, 0))
    return pl.pallas_call(
        _rank_kernel,
        out_shape=(jax.ShapeDtypeStruct((t, LANE), I32), jax.ShapeDtypeStruct((LANE, 2), I32),
                   jax.ShapeDtypeStruct((4 * nblk, LANE), I32)),
        grid=(2, nblk),
        in_specs=[pl.BlockSpec((RANK_TILE, LANE), lambda p, i: (i, 0))],
        out_specs=(pl.BlockSpec((RANK_TILE, LANE), lambda p, i: (i * p, 0)),
                   const((LANE, 2)), const((4 * nblk, LANE))),
        scratch_shapes=[pltpu.VMEM((1, LANE), F32), pltpu.VMEM((1, LANE), F32),
                        pltpu.VMEM((nblk, LANE), F32), pltpu.VMEM((nblk, LANE), F32)],
        compiler_params=_cparams(("arbitrary", "arbitrary")),
        name="moe_rank",
    )(idx)


def _selection(pos_ref, tile):
    rows = tile * MOE_ROW_TILE + lax.broadcasted_iota(I32, (1, MOE_ROW_TILE), 1)
    pos = pos_ref[...]
    return jnp.where(pos[:, 0:1] == rows, 1.0, jnp.where(pos[:, 1:2] == rows, 1.0, 0.0)).astype(BF16)


def _dispatch_kernel(tile_ref, blk_ref, valid_ref, hn_ref, pos_ref, idx_ref, gate_ref, xs_ref, gs_ref):
    del blk_ref
    p = pl.program_id(0)
    tile = tile_ref[p]

    @pl.when((p == 0) | (tile != tile_ref[jnp.maximum(p - 1, 0)]))
    def _():
        xs_ref[...] = jnp.zeros_like(xs_ref)
        gs_ref[...] = jnp.zeros_like(gs_ref)

    @pl.when(valid_ref[p] != 0)
    def _():
        sel = _selection(pos_ref, tile)
        xs_ref[...] = (xs_ref[...].astype(F32) + _dot_tn(sel, hn_ref[...])).astype(BF16)
        expert = p // (pl.num_programs(0) // N_EXPERTS)
        g = gate_ref[...]
        gsel = jnp.broadcast_to(jnp.where(idx_ref[:, 0:1] == expert, g[:, 0:1], g[:, 1:2]), g.shape)
        hi = gsel.astype(BF16).astype(F32)
        mid = (gsel - hi).astype(BF16).astype(F32)
        lo = gsel - hi - mid
        lane = lax.broadcasted_iota(I32, g.shape, 1)
        parts = jnp.where(lane == 0, hi, jnp.where(lane == 1, mid, jnp.where(lane == 2, lo, 0.0)))
        gs_ref[...] += _dot_tn(sel, parts.astype(BF16))


def _slot_spec(shape, which):
    return pl.BlockSpec(shape, lambda p, tile, blk, valid: ((tile, blk)[which][p], 0))


def _dispatch(slots, hn, pos, idx, gates, rows):
    n_slots = slots[0].shape[0]
    tok = lambda n: _slot_spec((RANK_TILE, n), 1)
    return pl.pallas_call(
        _dispatch_kernel,
        out_shape=(jax.ShapeDtypeStruct((rows, D_MODEL), BF16), jax.ShapeDtypeStruct((rows, LANE), F32)),
        grid_spec=pltpu.PrefetchScalarGridSpec(
            num_scalar_prefetch=3,
            grid=(n_slots,),
            in_specs=[tok(D_MODEL), tok(LANE), tok(LANE), tok(LANE)],
            out_specs=(_slot_spec((MOE_ROW_TILE, D_MODEL), 0), _slot_spec((MOE_ROW_TILE, LANE), 0)),
        ),
        compiler_params=_cparams(("arbitrary",)),
        name="moe_dispatch",
    )(*slots, hn, pos, idx, gates)


def _moe_kernel(expert_ref, ntiles_ref, x_ref, gs_ref, wg_ref, wu_ref, wd_ref, out_ref, acc_ref):
    del expert_ref
    i, j = pl.program_id(0), pl.program_id(1)
    active = i < ntiles_ref[0]

    @pl.when(j == 0)
    def _():
        acc_ref[...] = jnp.zeros_like(acc_ref)

    @pl.when(active)
    def _():
        x = x_ref[...]
        gate = _dot(x, wg_ref[0])
        up = _dot(x, wu_ref[0])
        acc_ref[...] += _dot((jax.nn.silu(gate) * up).astype(BF16), wd_ref[0])

    @pl.when(j == pl.num_programs(1) - 1)
    def _():
        gs = gs_ref[...]
        row_gate = gs[:, 0:1] + gs[:, 1:2] + gs[:, 2:3]
        out_ref[...] = jnp.where(active, acc_ref[...] * row_gate, 0.0).astype(BF16)


def _moe_experts(tile_expert, n_tiles, xs, gs, w_gate, w_up, w_down):
    rows = xs.shape[0]
    tm, tf = MOE_ROW_TILE, FFN_FF_TILE
    nj = D_FF // tf

    def x_map(i, j, e_ref, n_ref):
        return jnp.minimum(i, n_ref[0] - 1), 0

    def w_idx(i, j, expert_ref, ntiles_ref):
        e = jnp.minimum(expert_ref[i], N_EXPERTS - 1)
        return e, jnp.where(i < ntiles_ref[0], j, nj - 1)

    def w_up_map(i, j, e_ref, n_ref):
        e, jj = w_idx(i, j, e_ref, n_ref)
        return e, 0, jj

    def w_down_map(i, j, e_ref, n_ref):
        e, jj = w_idx(i, j, e_ref, n_ref)
        return e, jj, 0

    return pl.pallas_call(
        _moe_kernel,
        out_shape=jax.ShapeDtypeStruct((rows, D_MODEL), BF16),
        grid_spec=pltpu.PrefetchScalarGridSpec(
            num_scalar_prefetch=2,
            grid=(rows // tm, nj),
            in_specs=[pl.BlockSpec((tm, D_MODEL), x_map),
                      pl.BlockSpec((tm, LANE), x_map),
                      pl.BlockSpec((1, D_MODEL, tf), w_up_map),
                      pl.BlockSpec((1, D_MODEL, tf), w_up_map),
                      pl.BlockSpec((1, tf, D_MODEL), w_down_map)],
            out_specs=pl.BlockSpec((tm, D_MODEL), lambda i, j, e, n: (i, 0)),
            scratch_shapes=[pltpu.VMEM((tm, D_MODEL), F32)],
        ),
        compiler_params=_cparams(("arbitrary", "arbitrary")),
        name="moe_experts",
    )(tile_expert, n_tiles, xs, gs, w_gate, w_up, w_down)


def _combine_kernel(tile_ref, blk_ref, valid_ref, x_ref, pos_ref, ys_ref, out_ref):
    del blk_ref
    p = pl.program_id(0)

    @pl.when(p % (2 * N_EXPERTS) == 0)
    def _():
        out_ref[...] = x_ref[...]

    @pl.when(valid_ref[p] != 0)
    def _():
        out_ref[...] += _dot(_selection(pos_ref, tile_ref[p]), ys_ref[...])


def _combine(slots, x2, pos, ys):
    n_slots = slots[0].shape[0]
    return pl.pallas_call(
        _combine_kernel,
        out_shape=jax.ShapeDtypeStruct(x2.shape, F32),
        grid_spec=pltpu.PrefetchScalarGridSpec(
            num_scalar_prefetch=3,
            grid=(n_slots,),
            in_specs=[_slot_spec((RANK_TILE, D_MODEL), 1), _slot_spec((RANK_TILE, LANE), 1),
                      _slot_spec((MOE_ROW_TILE, D_MODEL), 0)],
            out_specs=_slot_spec((RANK_TILE, D_MODEL), 1),
        ),
        compiler_params=_cparams(("arbitrary",)),
        name="moe_combine",
    )(*slots, x2, pos, ys)


def _moe_ffn(x2, g, w_router, w_gate_up, w_down):
    t = x2.shape[0]
    nblk = t // RANK_TILE
    hn, idx, gates = _router(x2, g, w_router)
    pos, tile_info, slot_tab = _rank(idx)
    tab = slot_tab.reshape(4, nblk, LANE)[:, :, :N_EXPERTS]
    tile = jnp.stack([tab[0], tab[1]], axis=-1)
    valid = jnp.stack([tab[2], tab[3]], axis=-1)
    blk = jnp.broadcast_to(jnp.arange(nblk, dtype=I32)[:, None, None], tile.shape)
    by_expert = lambda a: a.transpose(1, 0, 2).reshape(-1)
    by_block = lambda a: a.reshape(-1)
    rows = t * TOP_K + N_EXPERTS * MOE_ROW_TILE
    xs, gs = _dispatch(tuple(map(by_expert, (tile, blk, valid))), hn, pos, idx, gates, rows)
    ys = _moe_experts(tile_info[:, 0], tile_info[:1, 1], xs, gs,
                      w_gate_up[..., :D_FF].astype(BF16), w_gate_up[..., D_FF:].astype(BF16),
                      w_down.astype(BF16))
    return _combine(tuple(map(by_block, (tile, blk, valid))), x2, pos, ys)


def _pad_heads(w, heads, dim, pad):
    w = w.reshape(w.shape[:-1] + (heads, dim))
    return _pad_lanes(w, pad).reshape(w.shape[:-2] + (heads * pad,))


def kernel(x, mem, positions, a_norm, a_w_in, a_gate_bias, a_head_norm, a_w_out, b_norm, b_w_in, b_q_a_norm, b_w_uq, b_q_head_norm, b_w_out, kv_norm, w_dkv, kv_a_norm, w_ukv, k_head_norm, mem_norm, mem_w_kv, mem_q_norm, mem_k_norm, ffn_norm, dense_w_gate_up, dense_w_down, moe_router, moe_w_gate_up, moe_w_down):
    nb, seq, _ = x.shape
    t = nb * seq
    x2 = x.reshape(t, D_MODEL)
    gmat = jnp.kron(jnp.eye(M_HEADS, dtype=F32), jnp.full((M_HEAD_DIM, M_HEAD_DIM), 1.0 / M_HEAD_DIM, F32))

    kbd0, vbd0 = _memory_kv(mem, mem_norm[0], mem_w_kv[0], mem_k_norm[0], gmat)
    w_in = a_w_in[0]
    qk_w, v_w = A_HEADS * A_QK_DIM, A_HEADS * A_V_DIM
    o0, o1, o2, o3, o4 = qk_w, 2 * qk_w, 2 * qk_w + v_w, 2 * qk_w + 2 * v_w, 2 * qk_w + 2 * v_w + 2 * A_HEADS
    w_main = jnp.concatenate([
        _pad_heads(w_in[:, :o0], A_HEADS, A_QK_DIM, A_QK_PAD),
        _pad_heads(w_in[:, o0:o1], A_HEADS, A_QK_DIM, A_QK_PAD),
        _pad_heads(w_in[:, o1:o2], A_HEADS, A_V_DIM, A_V_PAD),
        _pad_heads(w_in[:, o2:o3], A_HEADS, A_V_DIM, A_V_PAD),
        w_in[:, o4:]], axis=1).astype(BF16)
    q, k, v, o, mq, gc, gr = _a_projection(x2, a_norm[0], w_main, w_in[:, o3:o4], a_gate_bias[0])
    gc, gr = _mlstm_gates(gc, gr)
    gcol = gc.reshape(nb, seq, 2, A_HEADS).transpose(0, 3, 1, 2)
    grow = gr.reshape(2, A_HEADS, nb, seq).transpose(2, 1, 0, 3)
    three = lambda a: a.reshape(nb, seq, a.shape[-1])
    hm = _mlstm(three(q), three(k), three(v), three(o), gcol, grow,
                _pad_heads(a_head_norm[0].reshape(1, -1), A_HEADS, A_V_DIM, A_V_PAD))
    w_out = a_w_out[0]
    w_out_h = jnp.pad(w_out[:v_w].reshape(A_HEADS, A_V_DIM, D_MODEL), ((0, 0), (0, A_V_PAD - A_V_DIM), (0, 0)))
    w_out_h = w_out_h.reshape(A_HEADS * A_V_PAD, D_MODEL).astype(BF16)
    x2 = _mix_out(x2, hm.reshape(t, -1), mq, kbd0, vbd0, gmat, mem_q_norm[0],
                  w_out_h, w_out[v_w:].astype(BF16), seq)
    wgu = dense_w_gate_up[0]
    x2 = _dense_ffn(x2, ffn_norm[0], wgu[:, :D_FF].astype(BF16), wgu[:, D_FF:].astype(BF16),
                    dense_w_down[0].astype(BF16))

    cs, sn = _rope_tables(positions)
    k_sh, vt_sh = _latent_kv(x2, kv_norm, w_dkv, kv_a_norm, w_ukv, k_head_norm, cs, sn)

    kbd1, vbd1 = _memory_kv(mem, mem_norm[1], mem_w_kv[1], mem_k_norm[1], gmat)
    qh, mq1 = _b_projection(x2, b_norm[0], b_w_in[0], b_q_a_norm[0], b_w_uq[0], b_q_head_norm[0], cs, sn)
    att = _causal_attention(three(qh), three(k_sh), vt_sh, seq)
    w_out = b_w_out[0]
    n_att = B_HEADS * V_HEAD
    x2 = _mix_out(x2, att.reshape(t, -1), mq1, kbd1, vbd1, gmat, mem_q_norm[1],
                  w_out[:n_att].astype(BF16), w_out[n_att:].astype(BF16), seq)
    x2 = _moe_ffn(x2, ffn_norm[1], moe_router[0], moe_w_gate_up[0], moe_w_down[0])
    return x2.reshape(nb, seq, D_MODEL)
```

```python
import functools

import jax
import jax.numpy as jnp
from jax import lax
from jax.experimental import pallas as pl
from jax.experimental.pallas import tpu as pltpu

F32 = jnp.float32
BF16 = jnp.bfloat16
I32 = jnp.int32

EPS = 1e-6
LANE = 128
VMEM_LIMIT = 48 * 1024 * 1024

D_MODEL = 1024
N_MEM = 256
M_HEADS, M_HEAD_DIM = 4, 64
M_W = M_HEADS * M_HEAD_DIM
A_HEADS, A_QK_DIM, A_V_DIM = 4, 96, 192
A_QK_PAD, A_V_PAD = 128, 256
B_HEADS, Q_LORA, KV_LORA = 6, 384, 256
QK_NOPE, QK_ROPE, V_HEAD = 128, 64, 128
B_QK_HEAD = QK_NOPE + QK_ROPE
B_QK_PAD = 256
VT_HEAD_ROWS = V_HEAD + 16
ROPE_THETA = 10000.0
D_FF = 3584
N_EXPERTS, TOP_K = 8, 2

MLSTM_CHUNK = 256
ROW_TILE = 512
FFN_ROW_TILE = 1024
FFN_FF_TILE = 512
MOE_ROW_TILE = 512
ATT_Q_TILE = 256
ATT_K_TILE = 256
ATT_HEADS_PER_STEP = 6
RANK_TILE = 512

HIGHEST = lax.Precision.HIGHEST


def _cparams(sem):
    return pltpu.CompilerParams(dimension_semantics=sem, vmem_limit_bytes=VMEM_LIMIT)


def _rms(x, g):
    return x * lax.rsqrt(jnp.mean(x * x, axis=-1, keepdims=True) + EPS) * g


def _dot(a, b):
    return jnp.dot(a, b, preferred_element_type=F32)


def _dot_nt(a, b):
    return lax.dot_general(a, b, (((1,), (1,)), ((), ())), preferred_element_type=F32)


def _dot_tn(a, b):
    return lax.dot_general(a, b, (((0,), (0,)), ((), ())), preferred_element_type=F32)


def _group_mean_sq(x, gmat):
    return jnp.dot(x * x, gmat, precision=HIGHEST, preferred_element_type=F32)


def _memkv_kernel(mem_ref, g_ref, w_ref, kg_ref, gmat_ref, kbd_ref, vbd_ref):
    hn = _rms(mem_ref[0], g_ref[...]).astype(BF16)
    kv = _dot(hn, w_ref[...])
    k, v = kv[:, :M_W], kv[:, M_W:]
    kn = k * lax.rsqrt(_group_mean_sq(k, gmat_ref[...]) + EPS) * kg_ref[...]
    lane_head = lax.broadcasted_iota(I32, (1, M_W), 1) // M_HEAD_DIM
    for h in range(M_HEADS):
        keep = lane_head == h
        kbd_ref[0, h * N_MEM:(h + 1) * N_MEM, :] = jnp.where(keep, kn, 0.0).astype(BF16)
        vbd_ref[0, h * N_MEM:(h + 1) * N_MEM, :] = jnp.where(keep, v, 0.0).astype(BF16)


def _memory_kv(mem, g, w_kv, k_g, gmat):
    nb = mem.shape[0]
    out = jax.ShapeDtypeStruct((nb, M_HEADS * N_MEM, M_W), BF16)
    return pl.pallas_call(
        _memkv_kernel,
        out_shape=(out, out),
        grid=(nb,),
        in_specs=[
            pl.BlockSpec((1, N_MEM, D_MODEL), lambda b: (b, 0, 0)),
            pl.BlockSpec((1, D_MODEL), lambda b: (0, 0)),
            pl.BlockSpec((D_MODEL, 2 * M_W), lambda b: (0, 0)),
            pl.BlockSpec((1, M_W), lambda b: (0, 0)),
            pl.BlockSpec((M_W, M_W), lambda b: (0, 0)),
        ],
        out_specs=(pl.BlockSpec((1, M_HEADS * N_MEM, M_W), lambda b: (b, 0, 0)),) * 2,
        compiler_params=_cparams(("parallel",)),
        name="memory_kv",
    )(mem, g.reshape(1, -1), w_kv.astype(BF16), jnp.tile(k_g, M_HEADS).reshape(1, -1), gmat)


def _memory_attention(mq, kbd, vbd, gmat, qg):
    qn = mq * lax.rsqrt(_group_mean_sq(mq, gmat) + EPS) * (qg * (M_HEAD_DIM ** -0.5))
    s = _dot_nt(qn.astype(BF16), kbd)
    ps = []
    for h in range(M_HEADS):
        sh = s[:, h * N_MEM:(h + 1) * N_MEM]
        e = jnp.exp(sh - jnp.max(sh, axis=-1, keepdims=True))
        ps.append((e / jnp.sum(e, axis=-1, keepdims=True)).astype(BF16))
    return _dot(jnp.concatenate(ps, axis=-1), vbd)


def _a_proj_kernel(x_ref, g_ref, w_ref, wif_ref, wift_ref, bc_ref, br_ref,
                   q_ref, k_ref, v_ref, o_ref, mq_ref, gc_ref, gr_ref):
    hn = _rms(x_ref[...], g_ref[...]).astype(BF16)
    nq = A_HEADS * A_QK_PAD
    nv = A_HEADS * A_V_PAD
    q_ref[...] = _dot(hn, w_ref[:, :nq]).astype(BF16)
    k_ref[...] = (_dot(hn, w_ref[:, nq:2 * nq]) * (A_QK_DIM ** -0.5)).astype(BF16)
    v = _dot(hn, w_ref[:, 2 * nq:2 * nq + nv])
    ones_lane = lax.broadcasted_iota(I32, (1, nv), 1) % A_V_PAD == A_V_DIM
    v_ref[...] = jnp.where(ones_lane, 1.0, v).astype(BF16)
    o_ref[...] = _dot(hn, w_ref[:, 2 * nq + nv:2 * nq + 2 * nv])
    mq_ref[...] = _dot(hn, w_ref[:, 2 * nq + 2 * nv:])
    gc_ref[...] = _dot(hn, wif_ref[...])[:, :2 * A_HEADS] + bc_ref[...]
    gr_ref[...] = _dot_nt(wift_ref[...], hn) + br_ref[...]


def _a_projection(x2, g, w_main, w_if, gate_bias):
    t = x2.shape[0]
    nq, nv = A_HEADS * A_QK_PAD, A_HEADS * A_V_PAD
    ng = 2 * A_HEADS
    wif_pad = jnp.pad(w_if, ((0, 0), (0, LANE - ng))).astype(BF16)
    row = lambda n: pl.BlockSpec((ROW_TILE, n), lambda i: (i, 0))
    full = lambda a: pl.BlockSpec(a.shape, lambda i: (0,) * a.ndim)
    args = (x2, g.reshape(1, -1), w_main, wif_pad, w_if.T.astype(BF16),
            gate_bias.reshape(1, ng), gate_bias.reshape(ng, 1))
    return pl.pallas_call(
        _a_proj_kernel,
        out_shape=(jax.ShapeDtypeStruct((t, nq), BF16), jax.ShapeDtypeStruct((t, nq), BF16),
                   jax.ShapeDtypeStruct((t, nv), BF16), jax.ShapeDtypeStruct((t, nv), F32),
                   jax.ShapeDtypeStruct((t, M_W), F32), jax.ShapeDtypeStruct((t, ng), F32),
                   jax.ShapeDtypeStruct((ng, t), F32)),
        grid=(t // ROW_TILE,),
        in_specs=[row(D_MODEL)] + [full(a) for a in args[1:]],
        out_specs=(row(nq), row(nq), row(nv), row(nv), row(M_W), row(ng),
                   pl.BlockSpec((ng, ROW_TILE), lambda i: (0, i))),
        compiler_params=_cparams(("parallel",)),
        name="mlstm_in_proj",
    )(*args)


def _log_sigmoid(f):
    return jnp.minimum(f, 0.0) - jnp.log(1.0 + jnp.exp(-jnp.abs(f)))


def _gates_kernel(gc_ref, gr_ref, oc_ref, or_ref):
    L = MLSTM_CHUNK
    gc = gc_ref[...]
    gr = gr_ref[...]
    r = lax.broadcasted_iota(I32, (L, L), 0)
    c = lax.broadcasted_iota(I32, (L, L), 1)
    lower = (c <= r).astype(F32)
    is_f_col = lax.broadcasted_iota(I32, gc.shape, 1) >= A_HEADS
    is_f_row = lax.broadcasted_iota(I32, gr.shape, 0) >= A_HEADS
    lf_c = jnp.where(is_f_col, _log_sigmoid(gc), 0.0)
    lf_r = jnp.where(is_f_row, _log_sigmoid(gr), 0.0)
    lf_c = jnp.concatenate([lf_c, jnp.zeros((L, LANE - gc.shape[1]), F32)], axis=1)
    cum_c = jnp.dot(lower, lf_c, precision=HIGHEST, preferred_element_type=F32)[:, :gc.shape[1]]
    cum_r = _dot_nt_highest(lf_r, lower)
    oc_ref[...] = jnp.where(is_f_col, cum_c, gc)
    or_ref[...] = jnp.where(is_f_row, cum_r, gr)


def _dot_nt_highest(a, b):
    return lax.dot_general(a, b, (((1,), (1,)), ((), ())), precision=HIGHEST,
                           preferred_element_type=F32)


def _mlstm_gates(gc, gr):
    t, ng = gc.shape
    L = MLSTM_CHUNK
    return pl.pallas_call(
        _gates_kernel,
        out_shape=(jax.ShapeDtypeStruct((t, ng), F32), jax.ShapeDtypeStruct((ng, t), F32)),
        grid=(t // L,),
        in_specs=[pl.BlockSpec((L, ng), lambda i: (i, 0)), pl.BlockSpec((ng, L), lambda i: (0, i))],
        out_specs=(pl.BlockSpec((L, ng), lambda i: (i, 0)), pl.BlockSpec((ng, L), lambda i: (0, i))),
        compiler_params=_cparams(("parallel",)),
        name="mlstm_gates",
    )(gc, gr)


def _mlstm_kernel(q_ref, k_ref, v_ref, o_ref, gc_ref, gr_ref, hg_ref, out_ref, c_ref, m_ref):
    L = MLSTM_CHUNK

    @pl.when(pl.program_id(1) == 0)
    def _():
        c_ref[...] = jnp.zeros_like(c_ref)
        m_ref[...] = jnp.zeros_like(m_ref)

    t_idx = lax.broadcasted_iota(I32, (L, L), 0)
    s_idx = lax.broadcasted_iota(I32, (L, L), 1)
    real = lax.broadcasted_iota(I32, (1, A_V_PAD), 1) < A_V_DIM
    for hd in range(A_HEADS):
        qk = slice(hd * A_QK_PAD, (hd + 1) * A_QK_PAD)
        vv = slice(hd * A_V_PAD, (hd + 1) * A_V_PAD)
        q, k, v = q_ref[0, :, qk], k_ref[0, :, qk], v_ref[0, :, vv]
        gcol = gc_ref[0, hd]
        grow = gr_ref[0, hd]
        i_c, g_c = gcol[:, 0:1], gcol[:, 1:2]
        i_r, g_r = grow[0:1, :], grow[1:2, :]
        g_last = g_r[:, L - 1:L]
        m_prev = m_ref[hd, 0:1, 0:1]
        c_prev = c_ref[hd]

        a_log = g_c + m_prev
        d_log = jnp.where(s_idx <= t_idx, g_c - g_r + i_r, -jnp.inf)
        m_t = jnp.maximum(a_log, jnp.max(d_log, axis=-1, keepdims=True))
        inter = jnp.exp(a_log - m_t)
        p = (jnp.exp(d_log - m_t) * _dot_nt(q, k)).astype(BF16)
        num = inter * _dot(q, c_prev.astype(BF16)) + _dot(p, v)
        den = num[:, A_V_DIM:A_V_DIM + 1]
        h = num / jnp.maximum(jnp.abs(den), jnp.exp(-m_t))
        h = jnp.where(real, h, 0.0)
        hn = h * lax.rsqrt(jnp.sum(h * h, axis=-1, keepdims=True) * (1.0 / A_V_DIM) + EPS) * hg_ref[:, vv]
        out_ref[0, :, vv] = (hn * jax.nn.sigmoid(o_ref[0, :, vv])).astype(BF16)

        w_r = g_last - g_r + i_r
        m_new = jnp.maximum(g_last + m_prev, jnp.max(w_r, axis=-1, keepdims=True))
        e_c = jnp.exp(g_last - g_c + i_c - m_new)
        ev = (e_c * v.astype(F32)).astype(BF16)
        c_ref[hd] = jnp.exp(g_last + m_prev - m_new) * c_prev + _dot_tn(k, ev)
        m_ref[hd] = jnp.broadcast_to(m_new, m_ref.shape[1:])


def _mlstm(q, k, v, o, gcol, grow, head_g):
    nb, s, _ = q.shape
    L = MLSTM_CHUNK
    blk = lambda w: pl.BlockSpec((1, L, w), lambda b, c: (b, c, 0))
    return pl.pallas_call(
        _mlstm_kernel,
        out_shape=jax.ShapeDtypeStruct((nb, s, A_HEADS * A_V_PAD), BF16),
        grid=(nb, s // L),
        in_specs=[blk(A_HEADS * A_QK_PAD), blk(A_HEADS * A_QK_PAD), blk(A_HEADS * A_V_PAD), blk(A_HEADS * A_V_PAD),
                  pl.BlockSpec((1, A_HEADS, L, 2), lambda b, c: (b, 0, c, 0)),
                  pl.BlockSpec((1, A_HEADS, 2, L), lambda b, c: (b, 0, 0, c)),
                  pl.BlockSpec((1, A_HEADS * A_V_PAD), lambda b, c: (0, 0))],
        out_specs=blk(A_HEADS * A_V_PAD),
        scratch_shapes=[pltpu.VMEM((A_HEADS, A_QK_PAD, A_V_PAD), F32), pltpu.VMEM((A_HEADS, 8, LANE), F32)],
        compiler_params=_cparams(("parallel", "arbitrary")),
        name="mlstm_chunkwise",
    )(q, k, v, o, gcol, grow, head_g)


def _mix_out_kernel(x_ref, h_ref, mq_ref, kbd_ref, vbd_ref, gmat_ref, qg_ref, w1_ref, w2_ref, out_ref):
    mo = _memory_attention(mq_ref[...], kbd_ref[0], vbd_ref[0], gmat_ref[...], qg_ref[...])
    out_ref[...] = x_ref[...] + _dot(h_ref[...], w1_ref[...]) + _dot(mo.astype(BF16), w2_ref[...])


def _mix_out(x2, h2, mq, kbd, vbd, gmat, qg, w_main, w_mem, seq):
    t = x2.shape[0]
    tm = ROW_TILE
    row = lambda n: pl.BlockSpec((tm, n), lambda i: (i, 0))
    full = lambda a: pl.BlockSpec(a.shape, lambda i: (0,) * a.ndim)
    per_batch = pl.BlockSpec((1,) + kbd.shape[1:], lambda i: ((i * tm) // seq, 0, 0))
    qg_t = jnp.tile(qg, M_HEADS).reshape(1, -1)
    return pl.pallas_call(
        _mix_out_kernel,
        out_shape=jax.ShapeDtypeStruct((t, D_MODEL), F32),
        grid=(t // tm,),
        in_specs=[row(D_MODEL), row(h2.shape[1]), row(M_W), per_batch, per_batch,
                  full(gmat), full(qg_t), full(w_main), full(w_mem)],
        out_specs=row(D_MODEL),
        compiler_params=_cparams(("parallel",)),
        name="mixer_out_proj",
    )(x2, h2, mq, kbd, vbd, gmat, qg_t, w_main, w_mem)


def _ffn_kernel(x_ref, g_ref, wg_ref, wu_ref, wd_ref, out_ref, hn_ref, acc_ref):
    j = pl.program_id(1)

    @pl.when(j == 0)
    def _():
        hn_ref[...] = _rms(x_ref[...], g_ref[...]).astype(BF16)
        acc_ref[...] = jnp.zeros_like(acc_ref)

    hn = hn_ref[...]
    gate = _dot(hn, wg_ref[...])
    up = _dot(hn, wu_ref[...])
    acc_ref[...] += _dot((jax.nn.silu(gate) * up).astype(BF16), wd_ref[...])

    @pl.when(j == pl.num_programs(1) - 1)
    def _():
        out_ref[...] = x_ref[...] + acc_ref[...]


def _dense_ffn(x2, g, w_gate, w_up, w_down):
    t = x2.shape[0]
    tm, tf = FFN_ROW_TILE, FFN_FF_TILE
    return pl.pallas_call(
        _ffn_kernel,
        out_shape=jax.ShapeDtypeStruct((t, D_MODEL), F32),
        grid=(t // tm, D_FF // tf),
        in_specs=[pl.BlockSpec((tm, D_MODEL), lambda i, j: (i, 0)),
                  pl.BlockSpec((1, D_MODEL), lambda i, j: (0, 0)),
                  pl.BlockSpec((D_MODEL, tf), lambda i, j: (0, j)),
                  pl.BlockSpec((D_MODEL, tf), lambda i, j: (0, j)),
                  pl.BlockSpec((tf, D_MODEL), lambda i, j: (j, 0))],
        out_specs=pl.BlockSpec((tm, D_MODEL), lambda i, j: (i, 0)),
        scratch_shapes=[pltpu.VMEM((tm, D_MODEL), BF16), pltpu.VMEM((tm, D_MODEL), F32)],
        compiler_params=_cparams(("parallel", "arbitrary")),
        name="dense_swiglu",
    )(x2, g.reshape(1, -1), w_gate, w_up, w_down)


def _rope_kernel(pos_ref, inv_ref, sign_ref, cs_ref, sn_ref):
    ang = pos_ref[...].astype(F32) * inv_ref[...]
    cs_ref[...] = jnp.cos(ang)
    sn_ref[...] = jnp.sin(ang) * sign_ref[...]


def _rope_tables(positions):
    t = positions.size
    half = QK_ROPE // 2
    inv = 1.0 / (ROPE_THETA ** (jnp.arange(0, QK_ROPE, 2, dtype=F32) / QK_ROPE))
    pad = jnp.zeros((LANE - QK_ROPE,), F32)
    inv_l = jnp.concatenate([inv, inv, pad]).reshape(1, LANE)
    sign = jnp.concatenate([-jnp.ones((half,), F32), jnp.ones((half,), F32), pad]).reshape(1, LANE)
    out = jax.ShapeDtypeStruct((t, LANE), F32)
    return pl.pallas_call(
        _rope_kernel,
        out_shape=(out, out),
        grid=(t // ROW_TILE,),
        in_specs=[pl.BlockSpec((ROW_TILE, 1), lambda i: (i, 0)),
                  pl.BlockSpec((1, LANE), lambda i: (0, 0)),
                  pl.BlockSpec((1, LANE), lambda i: (0, 0))],
        out_specs=(pl.BlockSpec((ROW_TILE, LANE), lambda i: (i, 0)),) * 2,
        compiler_params=_cparams(("parallel",)),
        name="rope_tables",
    )(positions.reshape(t, 1), inv_l, sign)


def _head_qk_norm_rope(nope, rope, rope_sw, g_nope, g_rope, g_rope_sw, cs, sn, scale):
    ss = jnp.sum(nope * nope, axis=-1, keepdims=True) + jnp.sum(rope * rope, axis=-1, keepdims=True)
    r = lax.rsqrt(ss * (1.0 / B_QK_HEAD) + EPS) * scale
    return nope * r * g_nope, (rope * g_rope * cs + rope_sw * g_rope_sw * sn) * r


def _latent_kv_kernel(x_ref, g_ref, wd_ref, ga_ref, wuk_ref, wuvt_ref, kg_ref, cs_ref, sn_ref, k_ref, vt_ref):
    hn = _rms(x_ref[...], g_ref[...]).astype(BF16)
    z = _dot(hn, wd_ref[...])
    c_kv = z[:, :KV_LORA]
    rope, rope_sw = z[:, KV_LORA:KV_LORA + LANE], z[:, KV_LORA + LANE:]
    cn = _rms(c_kv, ga_ref[...]).astype(BF16)
    kv = _dot(cn, wuk_ref[...])
    vt = _dot_nt(wuvt_ref[...], cn)
    tm = vt.shape[1]
    ones_row = (lax.broadcasted_iota(I32, (VT_HEAD_ROWS - V_HEAD, tm), 0) == 0).astype(BF16)
    for h in range(B_HEADS):
        vt_ref[h * VT_HEAD_ROWS:h * VT_HEAD_ROWS + V_HEAD, :] = vt[h * V_HEAD:(h + 1) * V_HEAD].astype(BF16)
        vt_ref[h * VT_HEAD_ROWS + V_HEAD:(h + 1) * VT_HEAD_ROWS, :] = ones_row
    kg = kg_ref[...]
    for h in range(B_HEADS):
        kn, kr = _head_qk_norm_rope(kv[:, h * QK_NOPE:(h + 1) * QK_NOPE], rope, rope_sw,
                                    kg[:, :LANE], kg[:, LANE:2 * LANE], kg[:, 2 * LANE:],
                                    cs_ref[...], sn_ref[...], 1.0)
        k_ref[:, h * B_QK_PAD:h * B_QK_PAD + QK_NOPE] = kn.astype(BF16)
        k_ref[:, h * B_QK_PAD + QK_NOPE:(h + 1) * B_QK_PAD] = kr.astype(BF16)


def _rope_swap(w):
    half = QK_ROPE // 2
    return jnp.concatenate([w[..., half:], w[..., :half]], axis=-1)


def _pad_lanes(w, n=LANE):
    return jnp.pad(w, [(0, 0)] * (w.ndim - 1) + [(0, n - w.shape[-1])])


def _head_gain(g):
    g_rope = g[QK_NOPE:]
    return jnp.concatenate([g[:QK_NOPE], _pad_lanes(g_rope), _pad_lanes(_rope_swap(g_rope))]).reshape(1, -1)


def _latent_kv(x2, kv_norm, w_dkv, kv_a_norm, w_ukv, k_head_norm, cs, sn):
    t = x2.shape[0]
    w_rope = w_dkv[:, KV_LORA:]
    wd = jnp.concatenate([w_dkv[:, :KV_LORA], _pad_lanes(w_rope), _pad_lanes(_rope_swap(w_rope))],
                         axis=1).astype(BF16)
    wu = w_ukv.reshape(KV_LORA, B_HEADS, QK_NOPE + V_HEAD)
    wuk = wu[:, :, :QK_NOPE].reshape(KV_LORA, -1).astype(BF16)
    wuvt = wu[:, :, QK_NOPE:].reshape(KV_LORA, -1).T.astype(BF16)
    args = (x2, kv_norm.reshape(1, -1), wd, kv_a_norm.reshape(1, -1), wuk, wuvt, _head_gain(k_head_norm), cs, sn)
    row = lambda n: pl.BlockSpec((ROW_TILE, n), lambda i: (i, 0))
    full = lambda a: pl.BlockSpec(a.shape, lambda i: (0,) * a.ndim)
    return pl.pallas_call(
        _latent_kv_kernel,
        out_shape=(jax.ShapeDtypeStruct((t, B_HEADS * B_QK_PAD), BF16),
                   jax.ShapeDtypeStruct((B_HEADS * VT_HEAD_ROWS, t), BF16)),
        grid=(t // ROW_TILE,),
        in_specs=[row(D_MODEL)] + [full(a) for a in args[1:7]] + [row(LANE), row(LANE)],
        out_specs=(row(B_HEADS * B_QK_PAD), pl.BlockSpec((B_HEADS * VT_HEAD_ROWS, ROW_TILE), lambda i: (0, i))),
        compiler_params=_cparams(("parallel",)),
        name="latent_kv",
    )(*args)


def _b_proj_kernel(x_ref, g_ref, win_ref, ga_ref, wuq_ref, qg_ref, cs_ref, sn_ref, q_ref, mq_ref):
    hn = _rms(x_ref[...], g_ref[...]).astype(BF16)
    proj = _dot(hn, win_ref[...])
    mq_ref[...] = proj[:, Q_LORA:]
    qall = _dot(_rms(proj[:, :Q_LORA], ga_ref[...]).astype(BF16), wuq_ref[...])
    qg = qg_ref[...]
    per_head = QK_NOPE + 2 * LANE
    for h in range(B_HEADS):
        base = h * per_head
        qn, qr = _head_qk_norm_rope(qall[:, base:base + QK_NOPE],
                                    qall[:, base + QK_NOPE:base + QK_NOPE + LANE],
                                    qall[:, base + QK_NOPE + LANE:base + per_head],
                                    qg[:, :LANE], qg[:, LANE:2 * LANE], qg[:, 2 * LANE:],
                                    cs_ref[...], sn_ref[...], B_QK_HEAD ** -0.5)
        q_ref[:, h * B_QK_PAD:h * B_QK_PAD + QK_NOPE] = qn.astype(BF16)
        q_ref[:, h * B_QK_PAD + QK_NOPE:(h + 1) * B_QK_PAD] = qr.astype(BF16)


def _b_projection(x2, g, w_in, q_a_g, w_uq, q_head_g, cs, sn):
    t = x2.shape[0]
    wq = w_uq.reshape(Q_LORA, B_HEADS, B_QK_HEAD)
    w_rope = wq[:, :, QK_NOPE:]
    wq = jnp.concatenate([wq[:, :, :QK_NOPE], _pad_lanes(w_rope), _pad_lanes(_rope_swap(w_rope))],
                         axis=-1).reshape(Q_LORA, -1).astype(BF16)
    args = (x2, g.reshape(1, -1), w_in.astype(BF16), q_a_g.reshape(1, -1), wq, _head_gain(q_head_g), cs, sn)
    row = lambda n: pl.BlockSpec((ROW_TILE, n), lambda i: (i, 0))
    full = lambda a: pl.BlockSpec(a.shape, lambda i: (0,) * a.ndim)
    return pl.pallas_call(
        _b_proj_kernel,
        out_shape=(jax.ShapeDtypeStruct((t, B_HEADS * B_QK_PAD), BF16),
                   jax.ShapeDtypeStruct((t, M_W), F32)),
        grid=(t // ROW_TILE,),
        in_specs=[row(D_MODEL)] + [full(a) for a in args[1:6]] + [row(LANE), row(LANE)],
        out_specs=(row(B_HEADS * B_QK_PAD), row(M_W)),
        compiler_params=_cparams(("parallel",)),
        name="mla_q_proj",
    )(*args)


def _attn_kernel(q_ref, k_ref, vt_ref, out_ref):
    tq, tk = ATT_Q_TILE, ATT_K_TILE
    i = pl.program_id(2)

    def block(h, j, carry, masked):
        m, acc = carry
        off = pl.multiple_of(j * tk, tk)
        q = q_ref[0, :, h * B_QK_PAD:(h + 1) * B_QK_PAD]
        st = _dot_nt(k_ref[0, pl.ds(off, tk), h * B_QK_PAD:(h + 1) * B_QK_PAD], q)
        if masked:
            key = lax.broadcasted_iota(I32, (tk, tq), 0)
            qry = lax.broadcasted_iota(I32, (tk, tq), 1)
            st = jnp.where(key <= qry, st, -jnp.inf)
        m_new = jnp.maximum(m, jnp.max(st, axis=0, keepdims=True))
        p = jnp.exp(st - m_new).astype(BF16)
        vt = vt_ref[h * VT_HEAD_ROWS:(h + 1) * VT_HEAD_ROWS, pl.ds(off, tk)]
        return m_new, jnp.exp(m - m_new) * acc + _dot(vt, p)

    heads = range(ATT_HEADS_PER_STEP)
    init = tuple((jnp.full((1, tq), -jnp.inf, F32), jnp.zeros((VT_HEAD_ROWS, tq), F32)) for _ in heads)
    carry = lax.fori_loop(0, i, lambda j, c: tuple(block(h, j, c[h], False) for h in heads), init)
    for h in heads:
        _, acc = block(h, i, carry[h], True)
        out_t = acc[:V_HEAD] / acc[V_HEAD:V_HEAD + 1]
        out_ref[0, :, h * V_HEAD:(h + 1) * V_HEAD] = out_t.T.astype(BF16)


def _causal_attention(q, k, vt, seq):
    nb = q.shape[0]
    g = ATT_HEADS_PER_STEP
    assert ATT_Q_TILE == ATT_K_TILE and B_HEADS % g == 0
    return pl.pallas_call(
        _attn_kernel,
        out_shape=jax.ShapeDtypeStruct((nb, seq, B_HEADS * V_HEAD), BF16),
        grid=(nb, B_HEADS // g, seq // ATT_Q_TILE),
        in_specs=[pl.BlockSpec((1, ATT_Q_TILE, g * B_QK_PAD), lambda b, h, i: (b, i, h)),
                  pl.BlockSpec((1, seq, g * B_QK_PAD), lambda b, h, i: (b, 0, h)),
                  pl.BlockSpec((g * VT_HEAD_ROWS, seq), lambda b, h, i: (h, b))],
        out_specs=pl.BlockSpec((1, ATT_Q_TILE, g * V_HEAD), lambda b, h, i: (b, i, h)),
        compiler_params=_cparams(("parallel", "parallel", "arbitrary")),
        name="causal_attention",
    )(q, k, vt)


def _router_kernel(x_ref, g_ref, wr_ref, hn_ref, idx_ref, gate_ref):
    hn = _rms(x_ref[...], g_ref[...])
    hn_ref[...] = hn.astype(BF16)
    logits = jnp.dot(hn, wr_ref[...], precision=HIGHEST, preferred_element_type=F32)
    lane = lax.broadcasted_iota(I32, logits.shape, 1)
    logits = jnp.where(lane < N_EXPERTS, logits, -jnp.inf)
    v1 = jnp.max(logits, axis=-1, keepdims=True)
    i1 = jnp.min(jnp.where(logits == v1, lane, LANE), axis=-1, keepdims=True)
    rest = jnp.where(lane == i1, -jnp.inf, logits)
    v2 = jnp.max(rest, axis=-1, keepdims=True)
    i2 = jnp.min(jnp.where(rest == v2, lane, LANE), axis=-1, keepdims=True)
    e2 = jnp.exp(v2 - v1)
    den = 1.0 + e2
    idx_ref[...] = jnp.where(lane == 0, i1, jnp.where(lane == 1, i2, 0))
    gate_ref[...] = jnp.where(lane == 0, 1.0 / den, jnp.where(lane == 1, e2 / den, 0.0))


def _router(x2, g, w_router):
    t = x2.shape[0]
    row = lambda n: pl.BlockSpec((ROW_TILE, n), lambda i: (i, 0))
    wr = _pad_lanes(w_router)
    return pl.pallas_call(
        _router_kernel,
        out_shape=(jax.ShapeDtypeStruct((t, D_MODEL), BF16), jax.ShapeDtypeStruct((t, LANE), I32),
                   jax.ShapeDtypeStruct((t, LANE), F32)),
        grid=(t // ROW_TILE,),
        in_specs=[row(D_MODEL), pl.BlockSpec((1, D_MODEL), lambda i: (0, 0)),
                  pl.BlockSpec(wr.shape, lambda i: (0, 0))],
        out_specs=(row(D_MODEL), row(LANE), row(LANE)),
        compiler_params=_cparams(("parallel",)),
        name="moe_router",
    )(x2, g.reshape(1, -1), wr)


def _rank_kernel(idx_ref, pos_ref, posr_ref, tile_ref, slot_ref, run_ref, start_ref, lo_ref, hi_ref):
    phase, blk = pl.program_id(0), pl.program_id(1)
    nblk = pl.num_programs(1)
    tb = RANK_TILE
    lane = lax.broadcasted_iota(I32, (tb, LANE), 1)
    idx = idx_ref[...]
    oh0 = (lane == idx[:, 0:1]).astype(F32)
    oh1 = (lane == idx[:, 1:2]).astype(F32)
    both = oh0 + oh1

    @pl.when((phase == 0) & (blk == 0))
    def _():
        run_ref[...] = jnp.zeros_like(run_ref)

    @pl.when(phase == 0)
    def _():
        run_ref[...] += jnp.sum(both, axis=0, keepdims=True)
        pos_ref[...] = jnp.zeros_like(pos_ref)
        posr_ref[...] = jnp.zeros_like(posr_ref)

    @pl.when((phase == 1) & (blk == 0))
    def _():
        counts = run_ref[...]
        tiles = jnp.ceil(counts * (1.0 / MOE_ROW_TILE))
        r = lax.broadcasted_iota(I32, (LANE, LANE), 0)
        c = lax.broadcasted_iota(I32, (LANE, LANE), 1)
        before = (r < c).astype(F32)
        tile_start = jnp.dot(tiles, before, precision=HIGHEST, preferred_element_type=F32)
        start_ref[...] = tile_start * MOE_ROW_TILE
        tile_end = tile_start + tiles
        n_col = lax.broadcasted_iota(I32, (LANE, LANE), 0).astype(F32)
        ended = ((n_col >= tile_end) & (c < N_EXPERTS)).astype(F32)
        expert = jnp.sum(ended, axis=-1, keepdims=True)
        total = jnp.max(tile_end, axis=-1, keepdims=True)
        col = lax.broadcasted_iota(I32, tile_ref.shape, 1)
        tile_ref[...] = jnp.where(col == 0, expert, total).astype(I32)
        run_ref[...] = jnp.zeros_like(run_ref)
        lo_ref[...] = jnp.zeros_like(lo_ref)
        hi_ref[...] = jnp.zeros_like(hi_ref)

    @pl.when(phase == 1)
    def _():
        r = lax.broadcasted_iota(I32, (tb, tb), 0)
        c = lax.broadcasted_iota(I32, (tb, tb), 1)
        strict = (c < r).astype(BF16)
        lo = run_ref[...] + start_ref[...]
        base = _dot(strict, both.astype(BF16)) + lo
        p0 = jnp.sum(oh0 * base, axis=-1, keepdims=True)
        p1 = jnp.sum(oh1 * base, axis=-1, keepdims=True)
        posf = jnp.where(lane == 0, p0, jnp.where(lane == 1, p1, 0.0))
        pos_ref[...] = posf.astype(I32)
        pick = (lax.broadcasted_iota(I32, (8, LANE), 0) == lax.broadcasted_iota(I32, (8, LANE), 1)).astype(F32)
        posr_ref[...] = _dot_nt_highest(pick, posf).astype(I32)
        run_ref[...] += jnp.sum(both, axis=0, keepdims=True)
        mine = lax.broadcasted_iota(I32, lo_ref.shape, 0) == blk
        lo_ref[...] = jnp.where(mine, lo, lo_ref[...])
        hi_ref[...] = jnp.where(mine, run_ref[...] + start_ref[...], hi_ref[...])

    @pl.when((phase == 1) & (blk == nblk - 1))
    def _():
        lo, hi = lo_ref[...], hi_ref[...]
        first = jnp.floor(lo * (1.0 / MOE_ROW_TILE))
        last = jnp.floor((hi - 1.0) * (1.0 / MOE_ROW_TILE))
        some = hi > lo
        two = some & (last > first)
        n = lo_ref.shape[0]
        slot_ref[0 * n:1 * n, :] = first.astype(I32)
        slot_ref[1 * n:2 * n, :] = jnp.where(two, last, first).astype(I32)
        slot_ref[2 * n:3 * n, :] = some.astype(I32)
        slot_ref[3 * n:4 * n, :] = two.astype(I32)


def _rank(idx):
    t = idx.shape[0]
    nblk = t // RANK_TILE
    const = lambda shape: pl.BlockSpec(shape, lambda p, i: (0, 0))
    return pl.pallas_call(
        _rank_kernel,
        out_shape=(jax.ShapeDtypeStruct((t, LANE), I32), jax.ShapeDtypeStruct((8, t), I32),
                   jax.ShapeDtypeStruct((LANE, 2), I32), jax.ShapeDtypeStruct((4 * nblk, LANE), I32)),
        grid=(2, nblk),
        in_specs=[pl.BlockSpec((RANK_TILE, LANE), lambda p, i: (i, 0))],
        out_specs=(pl.BlockSpec((RANK_TILE, LANE), lambda p, i: (i * p, 0)),
                   pl.BlockSpec((8, RANK_TILE), lambda p, i: (0, i * p)),
                   const((LANE, 2)), const((4 * nblk, LANE))),
        scratch_shapes=[pltpu.VMEM((1, LANE), F32), pltpu.VMEM((1, LANE), F32),
                        pltpu.VMEM((nblk, LANE), F32), pltpu.VMEM((nblk, LANE), F32)],
        compiler_params=_cparams(("arbitrary", "arbitrary")),
        name="moe_rank",
    )(idx)


def _selection(pos_ref, tile):
    rows = tile * MOE_ROW_TILE + lax.broadcasted_iota(I32, (1, MOE_ROW_TILE), 1)
    pos = pos_ref[...]
    return jnp.where(pos[:, 0:1] == rows, 1.0, jnp.where(pos[:, 1:2] == rows, 1.0, 0.0)).astype(BF16)


def _dispatch_kernel(tile_ref, blk_ref, valid_ref, hn_ref, posr_ref, idx_ref, gate_ref, xs_ref, gs_ref,
                     acc_ref, gacc_ref):
    del blk_ref
    p = pl.program_id(0)
    last_slot = pl.num_programs(0) - 1
    tile = tile_ref[p]

    @pl.when((p == 0) | (tile != tile_ref[jnp.maximum(p - 1, 0)]))
    def _():
        acc_ref[...] = jnp.zeros_like(acc_ref)
        gacc_ref[...] = jnp.zeros_like(gacc_ref)

    @pl.when(valid_ref[p] != 0)
    def _():
        rows = tile * MOE_ROW_TILE + lax.broadcasted_iota(I32, (MOE_ROW_TILE, 1), 0)
        sel = jnp.where(posr_ref[0:1, :] == rows, 1.0, jnp.where(posr_ref[1:2, :] == rows, 1.0, 0.0)).astype(BF16)
        acc_ref[...] += _dot(sel, hn_ref[...])
        expert = p // (pl.num_programs(0) // N_EXPERTS)
        g = gate_ref[...]
        gsel = jnp.broadcast_to(jnp.where(idx_ref[:, 0:1] == expert, g[:, 0:1], g[:, 1:2]), g.shape)
        hi = gsel.astype(BF16).astype(F32)
        mid = (gsel - hi).astype(BF16).astype(F32)
        lo = gsel - hi - mid
        lane = lax.broadcasted_iota(I32, g.shape, 1)
        parts = jnp.where(lane == 0, hi, jnp.where(lane == 1, mid, jnp.where(lane == 2, lo, 0.0)))
        gacc_ref[...] += _dot(sel, parts.astype(BF16))

    @pl.when((p == last_slot) | (tile != tile_ref[jnp.minimum(p + 1, last_slot)]))
    def _():
        xs_ref[...] = acc_ref[...].astype(BF16)
        gs_ref[...] = gacc_ref[...]


def _slot_spec(shape, which, transposed=False):
    if transposed:
        return pl.BlockSpec(shape, lambda p, tile, blk, valid: (0, (tile, blk)[which][p]))
    return pl.BlockSpec(shape, lambda p, tile, blk, valid: ((tile, blk)[which][p], 0))


def _dispatch(slots, hn, pos_rows, idx, gates, rows):
    n_slots = slots[0].shape[0]
    tok = lambda n: _slot_spec((RANK_TILE, n), 1)
    return pl.pallas_call(
        _dispatch_kernel,
        out_shape=(jax.ShapeDtypeStruct((rows, D_MODEL), BF16), jax.ShapeDtypeStruct((rows, LANE), F32)),
        grid_spec=pltpu.PrefetchScalarGridSpec(
            num_scalar_prefetch=3,
            grid=(n_slots,),
            in_specs=[tok(D_MODEL), _slot_spec((8, RANK_TILE), 1, transposed=True), tok(LANE), tok(LANE)],
            out_specs=(_slot_spec((MOE_ROW_TILE, D_MODEL), 0), _slot_spec((MOE_ROW_TILE, LANE), 0)),
            scratch_shapes=[pltpu.VMEM((MOE_ROW_TILE, D_MODEL), F32), pltpu.VMEM((MOE_ROW_TILE, LANE), F32)],
        ),
        compiler_params=_cparams(("arbitrary",)),
        name="moe_dispatch",
    )(*slots, hn, pos_rows, idx, gates)


def _moe_kernel(expert_ref, ntiles_ref, x_ref, gs_ref, wg_ref, wu_ref, wd_ref, out_ref, acc_ref):
    del expert_ref
    i, j = pl.program_id(0), pl.program_id(1)
    active = i < ntiles_ref[0]

    @pl.when(j == 0)
    def _():
        acc_ref[...] = jnp.zeros_like(acc_ref)

    @pl.when(active)
    def _():
        x = x_ref[...]
        gate = _dot(x, wg_ref[0])
        up = _dot(x, wu_ref[0])
        acc_ref[...] += _dot((jax.nn.silu(gate) * up).astype(BF16), wd_ref[0])

    @pl.when(j == pl.num_programs(1) - 1)
    def _():
        gs = gs_ref[...]
        row_gate = gs[:, 0:1] + gs[:, 1:2] + gs[:, 2:3]
        out_ref[...] = jnp.where(active, acc_ref[...] * row_gate, 0.0).astype(BF16)


def _moe_experts(tile_expert, n_tiles, xs, gs, w_gate, w_up, w_down):
    rows = xs.shape[0]
    tm, tf = MOE_ROW_TILE, FFN_FF_TILE
    nj = D_FF // tf

    def x_map(i, j, e_ref, n_ref):
        return jnp.clip(i, 0, jnp.maximum(n_ref[0] - 1, 0)), 0

    def w_idx(i, j, expert_ref, ntiles_ref):
        e = jnp.minimum(expert_ref[i], N_EXPERTS - 1)
        return e, jnp.where(i < ntiles_ref[0], j, nj - 1)

    def w_up_map(i, j, e_ref, n_ref):
        e, jj = w_idx(i, j, e_ref, n_ref)
        return e, 0, jj

    def w_down_map(i, j, e_ref, n_ref):
        e, jj = w_idx(i, j, e_ref, n_ref)
        return e, jj, 0

    return pl.pallas_call(
        _moe_kernel,
        out_shape=jax.ShapeDtypeStruct((rows, D_MODEL), BF16),
        grid_spec=pltpu.PrefetchScalarGridSpec(
            num_scalar_prefetch=2,
            grid=(rows // tm, nj),
            in_specs=[pl.BlockSpec((tm, D_MODEL), x_map),
                      pl.BlockSpec((tm, LANE), x_map),
                      pl.BlockSpec((1, D_MODEL, tf), w_up_map),
                      pl.BlockSpec((1, D_MODEL, tf), w_up_map),
                      pl.BlockSpec((1, tf, D_MODEL), w_down_map)],
            out_specs=pl.BlockSpec((tm, D_MODEL), lambda i, j, e, n: (i, 0)),
            scratch_shapes=[pltpu.VMEM((tm, D_MODEL), F32)],
        ),
        compiler_params=_cparams(("arbitrary", "arbitrary")),
        name="moe_experts",
    )(tile_expert, n_tiles, xs, gs, w_gate, w_up, w_down)


def _combine_kernel(tile_ref, blk_ref, valid_ref, x_ref, pos_ref, ys_ref, out_ref):
    del blk_ref
    p = pl.program_id(0)

    @pl.when(p % (2 * N_EXPERTS) == 0)
    def _():
        out_ref[...] = x_ref[...]

    @pl.when(valid_ref[p] != 0)
    def _():
        out_ref[...] += _dot(_selection(pos_ref, tile_ref[p]), ys_ref[...])


def _combine(slots, x2, pos, ys):
    n_slots = slots[0].shape[0]
    return pl.pallas_call(
        _combine_kernel,
        out_shape=jax.ShapeDtypeStruct(x2.shape, F32),
        grid_spec=pltpu.PrefetchScalarGridSpec(
            num_scalar_prefetch=3,
            grid=(n_slots,),
            in_specs=[_slot_spec((RANK_TILE, D_MODEL), 1), _slot_spec((RANK_TILE, LANE), 1),
                      _slot_spec((MOE_ROW_TILE, D_MODEL), 0)],
            out_specs=_slot_spec((RANK_TILE, D_MODEL), 1),
        ),
        compiler_params=_cparams(("arbitrary",)),
        name="moe_combine",
    )(*slots, x2, pos, ys)


def _moe_ffn(x2, g, w_router, w_gate_up, w_down):
    t = x2.shape[0]
    nblk = t // RANK_TILE
    hn, idx, gates = _router(x2, g, w_router)
    pos, pos_rows, tile_info, slot_tab = _rank(idx)
    tab = slot_tab.reshape(4, nblk, LANE)[:, :, :N_EXPERTS]
    tile = jnp.stack([tab[0], tab[1]], axis=-1)
    valid = jnp.stack([tab[2], tab[3]], axis=-1)
    blk = jnp.broadcast_to(jnp.arange(nblk, dtype=I32)[:, None, None], tile.shape)
    by_expert = lambda a: a.transpose(1, 0, 2).reshape(-1)
    by_block = lambda a: a.reshape(-1)
    rows = t * TOP_K + N_EXPERTS * MOE_ROW_TILE
    xs, gs = _dispatch(tuple(map(by_expert, (tile, blk, valid))), hn, pos_rows, idx, gates, rows)
    ys = _moe_experts(tile_info[:, 0], tile_info[:1, 1], xs, gs,
                      w_gate_up[..., :D_FF].astype(BF16), w_gate_up[..., D_FF:].astype(BF16),
                      w_down.astype(BF16))
    return _combine(tuple(map(by_block, (tile, blk, valid))), x2, pos, ys)


def _pad_heads(w, heads, dim, pad):
    w = w.reshape(w.shape[:-1] + (heads, dim))
    return _pad_lanes(w, pad).reshape(w.shape[:-2] + (heads * pad,))


def kernel(x, mem, positions, a_norm, a_w_in, a_gate_bias, a_head_norm, a_w_out, b_norm, b_w_in, b_q_a_norm, b_w_uq, b_q_head_norm, b_w_out, kv_norm, w_dkv, kv_a_norm, w_ukv, k_head_norm, mem_norm, mem_w_kv, mem_q_norm, mem_k_norm, ffn_norm, dense_w_gate_up, dense_w_down, moe_router, moe_w_gate_up, moe_w_down):
    nb, seq, _ = x.shape
    t = nb * seq
    x2 = x.reshape(t, D_MODEL)
    gmat = jnp.kron(jnp.eye(M_HEADS, dtype=F32), jnp.full((M_HEAD_DIM, M_HEAD_DIM), 1.0 / M_HEAD_DIM, F32))

    kbd0, vbd0 = _memory_kv(mem, mem_norm[0], mem_w_kv[0], mem_k_norm[0], gmat)
    w_in = a_w_in[0]
    qk_w, v_w = A_HEADS * A_QK_DIM, A_HEADS * A_V_DIM
    o0, o1, o2, o3, o4 = qk_w, 2 * qk_w, 2 * qk_w + v_w, 2 * qk_w + 2 * v_w, 2 * qk_w + 2 * v_w + 2 * A_HEADS
    w_main = jnp.concatenate([
        _pad_heads(w_in[:, :o0], A_HEADS, A_QK_DIM, A_QK_PAD),
        _pad_heads(w_in[:, o0:o1], A_HEADS, A_QK_DIM, A_QK_PAD),
        _pad_heads(w_in[:, o1:o2], A_HEADS, A_V_DIM, A_V_PAD),
        _pad_heads(w_in[:, o2:o3], A_HEADS, A_V_DIM, A_V_PAD),
        w_in[:, o4:]], axis=1).astype(BF16)
    q, k, v, o, mq, gc, gr = _a_projection(x2, a_norm[0], w_main, w_in[:, o3:o4], a_gate_bias[0])
    gc, gr = _mlstm_gates(gc, gr)
    gcol = gc.reshape(nb, seq, 2, A_HEADS).transpose(0, 3, 1, 2)
    grow = gr.reshape(2, A_HEADS, nb, seq).transpose(2, 1, 0, 3)
    three = lambda a: a.reshape(nb, seq, a.shape[-1])
    hm = _mlstm(three(q), three(k), three(v), three(o), gcol, grow,
                _pad_heads(a_head_norm[0].reshape(1, -1), A_HEADS, A_V_DIM, A_V_PAD))
    w_out = a_w_out[0]
    w_out_h = jnp.pad(w_out[:v_w].reshape(A_HEADS, A_V_DIM, D_MODEL), ((0, 0), (0, A_V_PAD - A_V_DIM), (0, 0)))
    w_out_h = w_out_h.reshape(A_HEADS * A_V_PAD, D_MODEL).astype(BF16)
    x2 = _mix_out(x2, hm.reshape(t, -1), mq, kbd0, vbd0, gmat, mem_q_norm[0],
                  w_out_h, w_out[v_w:].astype(BF16), seq)
    wgu = dense_w_gate_up[0]
    x2 = _dense_ffn(x2, ffn_norm[0], wgu[:, :D_FF].astype(BF16), wgu[:, D_FF:].astype(BF16),
                    dense_w_down[0].astype(BF16))

    cs, sn = _rope_tables(positions)
    k_sh, vt_sh = _latent_kv(x2, kv_norm, w_dkv, kv_a_norm, w_ukv, k_head_norm, cs, sn)

    kbd1, vbd1 = _memory_kv(mem, mem_norm[1], mem_w_kv[1], mem_k_norm[1], gmat)
    qh, mq1 = _b_projection(x2, b_norm[0], b_w_in[0], b_q_a_norm[0], b_w_uq[0], b_q_head_norm[0], cs, sn)
    att = _causal_attention(three(qh), three(k_sh), vt_sh, seq)
    w_out = b_w_out[0]
    n_att = B_HEADS * V_HEAD
    x2 = _mix_out(x2, att.reshape(t, -1), mq1, kbd1, vbd1, gmat, mem_q_norm[1],
                  w_out[:n_att].astype(BF16), w_out[n_att:].astype(BF16), seq)
    x2 = _moe_ffn(x2, ffn_norm[1], moe_router[0], moe_w_gate_up[0], moe_w_down[0])
    return x2.reshape(nb, seq, D_MODEL)
```

```python
import functools

import jax
import jax.numpy as jnp
from jax import lax
from jax.experimental import pallas as pl
from jax.experimental.pallas import tpu as pltpu

F32 = jnp.float32
BF16 = jnp.bfloat16
I32 = jnp.int32

EPS = 1e-6
LANE = 128
VMEM_LIMIT = 48 * 1024 * 1024

D_MODEL = 1024
N_MEM = 256
M_HEADS, M_HEAD_DIM = 4, 64
M_W = M_HEADS * M_HEAD_DIM
A_HEADS, A_QK_DIM, A_V_DIM = 4, 96, 192
A_QK_PAD, A_V_PAD = 128, 256
B_HEADS, Q_LORA, KV_LORA = 6, 384, 256
QK_NOPE, QK_ROPE, V_HEAD = 128, 64, 128
B_QK_HEAD = QK_NOPE + QK_ROPE
B_QK_PAD = 256
VT_HEAD_ROWS = V_HEAD + 16
ROPE_THETA = 10000.0
D_FF = 3584
N_EXPERTS, TOP_K = 8, 2
GATE_TERMS = 3

MLSTM_CHUNK = 256
ROW_TILE = 512
FFN_ROW_TILE = 1024
FFN_FF_TILE = 1792
FFN_CHUNK = 256
MOE_ROW_TILE = 512
ATT_Q_TILE = 256
ATT_K_TILE = 256
ATT_HEADS_PER_STEP = 6
RANK_TILE = 512

HIGHEST = lax.Precision.HIGHEST


def _cparams(sem):
    return pltpu.CompilerParams(dimension_semantics=sem, vmem_limit_bytes=VMEM_LIMIT)


def _rms(x, g):
    return x * lax.rsqrt(jnp.mean(x * x, axis=-1, keepdims=True) + EPS) * g


def _dot(a, b):
    return jnp.dot(a, b, preferred_element_type=F32)


def _dot_nt(a, b):
    return lax.dot_general(a, b, (((1,), (1,)), ((), ())), preferred_element_type=F32)


def _dot_tn(a, b):
    return lax.dot_general(a, b, (((0,), (0,)), ((), ())), preferred_element_type=F32)


def _group_mean_sq(x, gmat):
    return jnp.dot(x * x, gmat, precision=HIGHEST, preferred_element_type=F32)


def _memkv_kernel(mem_ref, g_ref, w_ref, kg_ref, gmat_ref, kbd_ref, vbd_ref):
    hn = _rms(mem_ref[0], g_ref[...]).astype(BF16)
    kv = _dot(hn, w_ref[...])
    k, v = kv[:, :M_W], kv[:, M_W:]
    kn = k * lax.rsqrt(_group_mean_sq(k, gmat_ref[...]) + EPS) * kg_ref[...]
    lane_head = lax.broadcasted_iota(I32, (1, M_W), 1) // M_HEAD_DIM
    for h in range(M_HEADS):
        keep = lane_head == h
        kbd_ref[0, h * N_MEM:(h + 1) * N_MEM, :] = jnp.where(keep, kn, 0.0).astype(BF16)
        vbd_ref[0, h * N_MEM:(h + 1) * N_MEM, :] = jnp.where(keep, v, 0.0).astype(BF16)


def _memory_kv(mem, g, w_kv, k_g, gmat):
    nb = mem.shape[0]
    out = jax.ShapeDtypeStruct((nb, M_HEADS * N_MEM, M_W), BF16)
    return pl.pallas_call(
        _memkv_kernel,
        out_shape=(out, out),
        grid=(nb,),
        in_specs=[
            pl.BlockSpec((1, N_MEM, D_MODEL), lambda b: (b, 0, 0)),
            pl.BlockSpec((1, D_MODEL), lambda b: (0, 0)),
            pl.BlockSpec((D_MODEL, 2 * M_W), lambda b: (0, 0)),
            pl.BlockSpec((1, M_W), lambda b: (0, 0)),
            pl.BlockSpec((M_W, M_W), lambda b: (0, 0)),
        ],
        out_specs=(pl.BlockSpec((1, M_HEADS * N_MEM, M_W), lambda b: (b, 0, 0)),) * 2,
        compiler_params=_cparams(("parallel",)),
        name="memory_kv",
    )(mem, g.reshape(1, -1), w_kv.astype(BF16), jnp.tile(k_g, M_HEADS).reshape(1, -1), gmat)


def _memory_attention(mq, kbd, vbd, gmat, qg):
    qn = mq * lax.rsqrt(_group_mean_sq(mq, gmat) + EPS) * (qg * (M_HEAD_DIM ** -0.5))
    s = _dot_nt(qn.astype(BF16), kbd)
    ps = []
    for h in range(M_HEADS):
        sh = s[:, h * N_MEM:(h + 1) * N_MEM]
        e = jnp.exp(sh - jnp.max(sh, axis=-1, keepdims=True))
        ps.append((e / jnp.sum(e, axis=-1, keepdims=True)).astype(BF16))
    return _dot(jnp.concatenate(ps, axis=-1), vbd)


def _a_proj_kernel(x_ref, g_ref, w_ref, wif_ref, wift_ref, bc_ref, br_ref,
                   q_ref, k_ref, v_ref, o_ref, mq_ref, gc_ref, gr_ref):
    hn = _rms(x_ref[...], g_ref[...]).astype(BF16)
    nq = A_HEADS * A_QK_PAD
    nv = A_HEADS * A_V_PAD
    q_ref[...] = _dot(hn, w_ref[:, :nq]).astype(BF16)
    k_ref[...] = (_dot(hn, w_ref[:, nq:2 * nq]) * (A_QK_DIM ** -0.5)).astype(BF16)
    v = _dot(hn, w_ref[:, 2 * nq:2 * nq + nv])
    ones_lane = lax.broadcasted_iota(I32, (1, nv), 1) % A_V_PAD == A_V_DIM
    v_ref[...] = jnp.where(ones_lane, 1.0, v).astype(BF16)
    o_ref[...] = _dot(hn, w_ref[:, 2 * nq + nv:2 * nq + 2 * nv])
    mq_ref[...] = _dot(hn, w_ref[:, 2 * nq + 2 * nv:])
    gc_ref[...] = _dot(hn, wif_ref[...])[:, :2 * A_HEADS] + bc_ref[...]
    gr_ref[...] = _dot_nt(wift_ref[...], hn) + br_ref[...]


def _a_projection(x2, g, w_main, w_if, gate_bias):
    t = x2.shape[0]
    nq, nv = A_HEADS * A_QK_PAD, A_HEADS * A_V_PAD
    ng = 2 * A_HEADS
    wif_pad = jnp.pad(w_if, ((0, 0), (0, LANE - ng))).astype(BF16)
    row = lambda n: pl.BlockSpec((ROW_TILE, n), lambda i: (i, 0))
    full = lambda a: pl.BlockSpec(a.shape, lambda i: (0,) * a.ndim)
    args = (x2, g.reshape(1, -1), w_main, wif_pad, w_if.T.astype(BF16),
            gate_bias.reshape(1, ng), gate_bias.reshape(ng, 1))
    return pl.pallas_call(
        _a_proj_kernel,
        out_shape=(jax.ShapeDtypeStruct((t, nq), BF16), jax.ShapeDtypeStruct((t, nq), BF16),
                   jax.ShapeDtypeStruct((t, nv), BF16), jax.ShapeDtypeStruct((t, nv), F32),
                   jax.ShapeDtypeStruct((t, M_W), F32), jax.ShapeDtypeStruct((t, ng), F32),
                   jax.ShapeDtypeStruct((ng, t), F32)),
        grid=(t // ROW_TILE,),
        in_specs=[row(D_MODEL)] + [full(a) for a in args[1:]],
        out_specs=(row(nq), row(nq), row(nv), row(nv), row(M_W), row(ng),
                   pl.BlockSpec((ng, ROW_TILE), lambda i: (0, i))),
        compiler_params=_cparams(("parallel",)),
        name="mlstm_in_proj",
    )(*args)


def _log_sigmoid(f):
    return jnp.minimum(f, 0.0) - jnp.log(1.0 + jnp.exp(-jnp.abs(f)))


def _gates_kernel(gc_ref, gr_ref, oc_ref, or_ref):
    L = MLSTM_CHUNK
    gc = gc_ref[...]
    gr = gr_ref[...]
    r = lax.broadcasted_iota(I32, (L, L), 0)
    c = lax.broadcasted_iota(I32, (L, L), 1)
    lower = (c <= r).astype(F32)
    is_f_col = lax.broadcasted_iota(I32, gc.shape, 1) >= A_HEADS
    is_f_row = lax.broadcasted_iota(I32, gr.shape, 0) >= A_HEADS
    lf_c = jnp.where(is_f_col, _log_sigmoid(gc), 0.0)
    lf_r = jnp.where(is_f_row, _log_sigmoid(gr), 0.0)
    lf_c = jnp.concatenate([lf_c, jnp.zeros((L, LANE - gc.shape[1]), F32)], axis=1)
    cum_c = jnp.dot(lower, lf_c, precision=HIGHEST, preferred_element_type=F32)[:, :gc.shape[1]]
    cum_r = _dot_nt_highest(lf_r, lower)
    oc_ref[...] = jnp.where(is_f_col, cum_c, gc)
    or_ref[...] = jnp.where(is_f_row, cum_r, gr)


def _dot_nt_highest(a, b):
    return lax.dot_general(a, b, (((1,), (1,)), ((), ())), precision=HIGHEST,
                           preferred_element_type=F32)


def _mlstm_gates(gc, gr):
    t, ng = gc.shape
    L = MLSTM_CHUNK
    return pl.pallas_call(
        _gates_kernel,
        out_shape=(jax.ShapeDtypeStruct((t, ng), F32), jax.ShapeDtypeStruct((ng, t), F32)),
        grid=(t // L,),
        in_specs=[pl.BlockSpec((L, ng), lambda i: (i, 0)), pl.BlockSpec((ng, L), lambda i: (0, i))],
        out_specs=(pl.BlockSpec((L, ng), lambda i: (i, 0)), pl.BlockSpec((ng, L), lambda i: (0, i))),
        compiler_params=_cparams(("parallel",)),
        name="mlstm_gates",
    )(gc, gr)


def _mlstm_kernel(q_ref, k_ref, v_ref, o_ref, gc_ref, gr_ref, hg_ref, out_ref, c_ref, m_ref):
    L = MLSTM_CHUNK

    @pl.when(pl.program_id(1) == 0)
    def _():
        c_ref[...] = jnp.zeros_like(c_ref)
        m_ref[...] = jnp.zeros_like(m_ref)

    t_idx = lax.broadcasted_iota(I32, (L, L), 0)
    s_idx = lax.broadcasted_iota(I32, (L, L), 1)
    real = lax.broadcasted_iota(I32, (1, A_V_PAD), 1) < A_V_DIM
    for hd in range(A_HEADS):
        qk = slice(hd * A_QK_PAD, (hd + 1) * A_QK_PAD)
        vv = slice(hd * A_V_PAD, (hd + 1) * A_V_PAD)
        q, k, v = q_ref[0, :, qk], k_ref[0, :, qk], v_ref[0, :, vv]
        gcol = gc_ref[0, hd]
        grow = gr_ref[0, hd]
        i_c, g_c = gcol[:, 0:1], gcol[:, 1:2]
        i_r, g_r = grow[0:1, :], grow[1:2, :]
        g_last = g_r[:, L - 1:L]
        m_prev = m_ref[hd, 0:1, 0:1]
        c_prev = c_ref[hd]

        a_log = g_c + m_prev
        d_log = jnp.where(s_idx <= t_idx, g_c - g_r + i_r, -jnp.inf)
        m_t = jnp.maximum(a_log, jnp.max(d_log, axis=-1, keepdims=True))
        inter = jnp.exp(a_log - m_t)
        p = (jnp.exp(d_log - m_t) * _dot_nt(q, k)).astype(BF16)
        num = inter * _dot(q, c_prev.astype(BF16)) + _dot(p, v)
        den = num[:, A_V_DIM:A_V_DIM + 1]
        h = num / jnp.maximum(jnp.abs(den), jnp.exp(-m_t))
        h = jnp.where(real, h, 0.0)
        hn = h * lax.rsqrt(jnp.sum(h * h, axis=-1, keepdims=True) * (1.0 / A_V_DIM) + EPS) * hg_ref[:, vv]
        out_ref[0, :, vv] = (hn * jax.nn.sigmoid(o_ref[0, :, vv])).astype(BF16)

        w_r = g_last - g_r + i_r
        m_new = jnp.maximum(g_last + m_prev, jnp.max(w_r, axis=-1, keepdims=True))
        e_c = jnp.exp(g_last - g_c + i_c - m_new)
        ev = (e_c * v.astype(F32)).astype(BF16)
        c_ref[hd] = jnp.exp(g_last + m_prev - m_new) * c_prev + _dot_tn(k, ev)
        m_ref[hd] = jnp.broadcast_to(m_new, m_ref.shape[1:])


def _mlstm(q, k, v, o, gcol, grow, head_g):
    nb, s, _ = q.shape
    L = MLSTM_CHUNK
    blk = lambda w: pl.BlockSpec((1, L, w), lambda b, c: (b, c, 0))
    return pl.pallas_call(
        _mlstm_kernel,
        out_shape=jax.ShapeDtypeStruct((nb, s, A_HEADS * A_V_PAD), BF16),
        grid=(nb, s // L),
        in_specs=[blk(A_HEADS * A_QK_PAD), blk(A_HEADS * A_QK_PAD), blk(A_HEADS * A_V_PAD), blk(A_HEADS * A_V_PAD),
                  pl.BlockSpec((1, A_HEADS, L, 2), lambda b, c: (b, 0, c, 0)),
                  pl.BlockSpec((1, A_HEADS, 2, L), lambda b, c: (b, 0, 0, c)),
                  pl.BlockSpec((1, A_HEADS * A_V_PAD), lambda b, c: (0, 0))],
        out_specs=blk(A_HEADS * A_V_PAD),
        scratch_shapes=[pltpu.VMEM((A_HEADS, A_QK_PAD, A_V_PAD), F32), pltpu.VMEM((A_HEADS, 8, LANE), F32)],
        compiler_params=_cparams(("parallel", "arbitrary")),
        name="mlstm_chunkwise",
    )(q, k, v, o, gcol, grow, head_g)


def _mix_out_kernel(x_ref, h_ref, mq_ref, kbd_ref, vbd_ref, gmat_ref, qg_ref, w1_ref, w2_ref, out_ref):
    mo = _memory_attention(mq_ref[...], kbd_ref[0], vbd_ref[0], gmat_ref[...], qg_ref[...])
    out_ref[...] = x_ref[...] + _dot(h_ref[...], w1_ref[...]) + _dot(mo.astype(BF16), w2_ref[...])


def _mix_out(x2, h2, mq, kbd, vbd, gmat, qg, w_main, w_mem, seq):
    t = x2.shape[0]
    tm = ROW_TILE
    row = lambda n: pl.BlockSpec((tm, n), lambda i: (i, 0))
    full = lambda a: pl.BlockSpec(a.shape, lambda i: (0,) * a.ndim)
    per_batch = pl.BlockSpec((1,) + kbd.shape[1:], lambda i: ((i * tm) // seq, 0, 0))
    qg_t = jnp.tile(qg, M_HEADS).reshape(1, -1)
    return pl.pallas_call(
        _mix_out_kernel,
        out_shape=jax.ShapeDtypeStruct((t, D_MODEL), F32),
        grid=(t // tm,),
        in_specs=[row(D_MODEL), row(h2.shape[1]), row(M_W), per_batch, per_batch,
                  full(gmat), full(qg_t), full(w_main), full(w_mem)],
        out_specs=row(D_MODEL),
        compiler_params=_cparams(("parallel",)),
        name="mixer_out_proj",
    )(x2, h2, mq, kbd, vbd, gmat, qg_t, w_main, w_mem)


def _swiglu_accumulate(acc_ref, x, wg_ref, wu_ref, wd_ref):
    for c in range(wg_ref.shape[-1] // FFN_CHUNK):
        cols = slice(c * FFN_CHUNK, (c + 1) * FFN_CHUNK)
        gate = _dot(x, wg_ref[:, cols])
        up = _dot(x, wu_ref[:, cols])
        acc_ref[...] += _dot((jax.nn.silu(gate) * up).astype(BF16), wd_ref[cols, :])


def _ffn_kernel(x_ref, g_ref, wg_ref, wu_ref, wd_ref, out_ref, hn_ref, acc_ref):
    j = pl.program_id(1)

    @pl.when(j == 0)
    def _():
        hn_ref[...] = _rms(x_ref[...], g_ref[...]).astype(BF16)
        acc_ref[...] = jnp.zeros_like(acc_ref)

    _swiglu_accumulate(acc_ref, hn_ref[...], wg_ref, wu_ref, wd_ref)

    @pl.when(j == pl.num_programs(1) - 1)
    def _():
        out_ref[...] = x_ref[...] + acc_ref[...]


def _dense_ffn(x2, g, w_gate, w_up, w_down):
    t = x2.shape[0]
    tm, tf = FFN_ROW_TILE, FFN_FF_TILE
    return pl.pallas_call(
        _ffn_kernel,
        out_shape=jax.ShapeDtypeStruct((t, D_MODEL), F32),
        grid=(t // tm, D_FF // tf),
        in_specs=[pl.BlockSpec((tm, D_MODEL), lambda i, j: (i, 0)),
                  pl.BlockSpec((1, D_MODEL), lambda i, j: (0, 0)),
                  pl.BlockSpec((D_MODEL, tf), lambda i, j: (0, j)),
                  pl.BlockSpec((D_MODEL, tf), lambda i, j: (0, j)),
                  pl.BlockSpec((tf, D_MODEL), lambda i, j: (j, 0))],
        out_specs=pl.BlockSpec((tm, D_MODEL), lambda i, j: (i, 0)),
        scratch_shapes=[pltpu.VMEM((tm, D_MODEL), BF16), pltpu.VMEM((tm, D_MODEL), F32)],
        compiler_params=_cparams(("parallel", "arbitrary")),
        name="dense_swiglu",
    )(x2, g.reshape(1, -1), w_gate, w_up, w_down)


def _rope_kernel(pos_ref, inv_ref, sign_ref, cs_ref, sn_ref):
    ang = pos_ref[...].astype(F32) * inv_ref[...]
    cs_ref[...] = jnp.cos(ang)
    sn_ref[...] = jnp.sin(ang) * sign_ref[...]


def _rope_tables(positions):
    t = positions.size
    half = QK_ROPE // 2
    inv = 1.0 / (ROPE_THETA ** (jnp.arange(0, QK_ROPE, 2, dtype=F32) / QK_ROPE))
    pad = jnp.zeros((LANE - QK_ROPE,), F32)
    inv_l = jnp.concatenate([inv, inv, pad]).reshape(1, LANE)
    sign = jnp.concatenate([-jnp.ones((half,), F32), jnp.ones((half,), F32), pad]).reshape(1, LANE)
    out = jax.ShapeDtypeStruct((t, LANE), F32)
    return pl.pallas_call(
        _rope_kernel,
        out_shape=(out, out),
        grid=(t // ROW_TILE,),
        in_specs=[pl.BlockSpec((ROW_TILE, 1), lambda i: (i, 0)),
                  pl.BlockSpec((1, LANE), lambda i: (0, 0)),
                  pl.BlockSpec((1, LANE), lambda i: (0, 0))],
        out_specs=(pl.BlockSpec((ROW_TILE, LANE), lambda i: (i, 0)),) * 2,
        compiler_params=_cparams(("parallel",)),
        name="rope_tables",
    )(positions.reshape(t, 1), inv_l, sign)


def _head_qk_norm_rope(nope, rope, rope_sw, g_nope, g_rope, g_rope_sw, cs, sn, scale):
    ss = jnp.sum(nope * nope, axis=-1, keepdims=True) + jnp.sum(rope * rope, axis=-1, keepdims=True)
    r = lax.rsqrt(ss * (1.0 / B_QK_HEAD) + EPS) * scale
    return nope * r * g_nope, (rope * g_rope * cs + rope_sw * g_rope_sw * sn) * r


def _latent_kv_kernel(x_ref, g_ref, wd_ref, ga_ref, wuk_ref, wuvt_ref, kg_ref, cs_ref, sn_ref, k_ref, vt_ref):
    hn = _rms(x_ref[...], g_ref[...]).astype(BF16)
    z = _dot(hn, wd_ref[...])
    c_kv = z[:, :KV_LORA]
    rope, rope_sw = z[:, KV_LORA:KV_LORA + LANE], z[:, KV_LORA + LANE:]
    cn = _rms(c_kv, ga_ref[...]).astype(BF16)
    kv = _dot(cn, wuk_ref[...])
    vt = _dot_nt(wuvt_ref[...], cn)
    tm = vt.shape[1]
    ones_row = (lax.broadcasted_iota(I32, (VT_HEAD_ROWS - V_HEAD, tm), 0) == 0).astype(BF16)
    for h in range(B_HEADS):
        vt_ref[h * VT_HEAD_ROWS:h * VT_HEAD_ROWS + V_HEAD, :] = vt[h * V_HEAD:(h + 1) * V_HEAD].astype(BF16)
        vt_ref[h * VT_HEAD_ROWS + V_HEAD:(h + 1) * VT_HEAD_ROWS, :] = ones_row
    kg = kg_ref[...]
    for h in range(B_HEADS):
        kn, kr = _head_qk_norm_rope(kv[:, h * QK_NOPE:(h + 1) * QK_NOPE], rope, rope_sw,
                                    kg[:, :LANE], kg[:, LANE:2 * LANE], kg[:, 2 * LANE:],
                                    cs_ref[...], sn_ref[...], 1.0)
        k_ref[:, h * B_QK_PAD:h * B_QK_PAD + QK_NOPE] = kn.astype(BF16)
        k_ref[:, h * B_QK_PAD + QK_NOPE:(h + 1) * B_QK_PAD] = kr.astype(BF16)


def _rope_swap(w):
    half = QK_ROPE // 2
    return jnp.concatenate([w[..., half:], w[..., :half]], axis=-1)


def _pad_lanes(w, n=LANE):
    return jnp.pad(w, [(0, 0)] * (w.ndim - 1) + [(0, n - w.shape[-1])])


def _head_gain(g):
    g_rope = g[QK_NOPE:]
    return jnp.concatenate([g[:QK_NOPE], _pad_lanes(g_rope), _pad_lanes(_rope_swap(g_rope))]).reshape(1, -1)


def _latent_kv(x2, kv_norm, w_dkv, kv_a_norm, w_ukv, k_head_norm, cs, sn):
    t = x2.shape[0]
    w_rope = w_dkv[:, KV_LORA:]
    wd = jnp.concatenate([w_dkv[:, :KV_LORA], _pad_lanes(w_rope), _pad_lanes(_rope_swap(w_rope))],
                         axis=1).astype(BF16)
    wu = w_ukv.reshape(KV_LORA, B_HEADS, QK_NOPE + V_HEAD)
    wuk = wu[:, :, :QK_NOPE].reshape(KV_LORA, -1).astype(BF16)
    wuvt = wu[:, :, QK_NOPE:].reshape(KV_LORA, -1).T.astype(BF16)
    args = (x2, kv_norm.reshape(1, -1), wd, kv_a_norm.reshape(1, -1), wuk, wuvt, _head_gain(k_head_norm), cs, sn)
    row = lambda n: pl.BlockSpec((ROW_TILE, n), lambda i: (i, 0))
    full = lambda a: pl.BlockSpec(a.shape, lambda i: (0,) * a.ndim)
    return pl.pallas_call(
        _latent_kv_kernel,
        out_shape=(jax.ShapeDtypeStruct((t, B_HEADS * B_QK_PAD), BF16),
                   jax.ShapeDtypeStruct((B_HEADS * VT_HEAD_ROWS, t), BF16)),
        grid=(t // ROW_TILE,),
        in_specs=[row(D_MODEL)] + [full(a) for a in args[1:7]] + [row(LANE), row(LANE)],
        out_specs=(row(B_HEADS * B_QK_PAD), pl.BlockSpec((B_HEADS * VT_HEAD_ROWS, ROW_TILE), lambda i: (0, i))),
        compiler_params=_cparams(("parallel",)),
        name="latent_kv",
    )(*args)


def _b_proj_kernel(x_ref, g_ref, win_ref, ga_ref, wuq_ref, qg_ref, cs_ref, sn_ref, q_ref, mq_ref):
    hn = _rms(x_ref[...], g_ref[...]).astype(BF16)
    proj = _dot(hn, win_ref[...])
    mq_ref[...] = proj[:, Q_LORA:]
    qall = _dot(_rms(proj[:, :Q_LORA], ga_ref[...]).astype(BF16), wuq_ref[...])
    qg = qg_ref[...]
    per_head = QK_NOPE + 2 * LANE
    for h in range(B_HEADS):
        base = h * per_head
        qn, qr = _head_qk_norm_rope(qall[:, base:base + QK_NOPE],
                                    qall[:, base + QK_NOPE:base + QK_NOPE + LANE],
                                    qall[:, base + QK_NOPE + LANE:base + per_head],
                                    qg[:, :LANE], qg[:, LANE:2 * LANE], qg[:, 2 * LANE:],
                                    cs_ref[...], sn_ref[...], B_QK_HEAD ** -0.5)
        q_ref[:, h * B_QK_PAD:h * B_QK_PAD + QK_NOPE] = qn.astype(BF16)
        q_ref[:, h * B_QK_PAD + QK_NOPE:(h + 1) * B_QK_PAD] = qr.astype(BF16)


def _b_projection(x2, g, w_in, q_a_g, w_uq, q_head_g, cs, sn):
    t = x2.shape[0]
    wq = w_uq.reshape(Q_LORA, B_HEADS, B_QK_HEAD)
    w_rope = wq[:, :, QK_NOPE:]
    wq = jnp.concatenate([wq[:, :, :QK_NOPE], _pad_lanes(w_rope), _pad_lanes(_rope_swap(w_rope))],
                         axis=-1).reshape(Q_LORA, -1).astype(BF16)
    args = (x2, g.reshape(1, -1), w_in.astype(BF16), q_a_g.reshape(1, -1), wq, _head_gain(q_head_g), cs, sn)
    row = lambda n: pl.BlockSpec((ROW_TILE, n), lambda i: (i, 0))
    full = lambda a: pl.BlockSpec(a.shape, lambda i: (0,) * a.ndim)
    return pl.pallas_call(
        _b_proj_kernel,
        out_shape=(jax.ShapeDtypeStruct((t, B_HEADS * B_QK_PAD), BF16),
                   jax.ShapeDtypeStruct((t, M_W), F32)),
        grid=(t // ROW_TILE,),
        in_specs=[row(D_MODEL)] + [full(a) for a in args[1:6]] + [row(LANE), row(LANE)],
        out_specs=(row(B_HEADS * B_QK_PAD), row(M_W)),
        compiler_params=_cparams(("parallel",)),
        name="mla_q_proj",
    )(*args)


def _attn_kernel(q_ref, k_ref, vt_ref, out_ref):
    tq, tk = ATT_Q_TILE, ATT_K_TILE
    i = pl.program_id(2)

    def block(h, j, carry, masked):
        m, acc = carry
        off = pl.multiple_of(j * tk, tk)
        q = q_ref[0, :, h * B_QK_PAD:(h + 1) * B_QK_PAD]
        st = _dot_nt(k_ref[0, pl.ds(off, tk), h * B_QK_PAD:(h + 1) * B_QK_PAD], q)
        if masked:
            key = lax.broadcasted_iota(I32, (tk, tq), 0)
            qry = lax.broadcasted_iota(I32, (tk, tq), 1)
            st = jnp.where(key <= qry, st, -jnp.inf)
        m_new = jnp.maximum(m, jnp.max(st, axis=0, keepdims=True))
        p = jnp.exp(st - m_new).astype(BF16)
        vt = vt_ref[h * VT_HEAD_ROWS:(h + 1) * VT_HEAD_ROWS, pl.ds(off, tk)]
        return m_new, jnp.exp(m - m_new) * acc + _dot(vt, p)

    heads = range(ATT_HEADS_PER_STEP)
    init = tuple((jnp.full((1, tq), -jnp.inf, F32), jnp.zeros((VT_HEAD_ROWS, tq), F32)) for _ in heads)
    carry = lax.fori_loop(0, i, lambda j, c: tuple(block(h, j, c[h], False) for h in heads), init)
    for h in heads:
        _, acc = block(h, i, carry[h], True)
        out_t = acc[:V_HEAD] / acc[V_HEAD:V_HEAD + 1]
        out_ref[0, :, h * V_HEAD:(h + 1) * V_HEAD] = out_t.T.astype(BF16)


def _causal_attention(q, k, vt, seq):
    nb = q.shape[0]
    g = ATT_HEADS_PER_STEP
    assert ATT_Q_TILE == ATT_K_TILE and B_HEADS % g == 0
    return pl.pallas_call(
        _attn_kernel,
        out_shape=jax.ShapeDtypeStruct((nb, seq, B_HEADS * V_HEAD), BF16),
        grid=(nb, B_HEADS // g, seq // ATT_Q_TILE),
        in_specs=[pl.BlockSpec((1, ATT_Q_TILE, g * B_QK_PAD), lambda b, h, i: (b, i, h)),
                  pl.BlockSpec((1, seq, g * B_QK_PAD), lambda b, h, i: (b, 0, h)),
                  pl.BlockSpec((g * VT_HEAD_ROWS, seq), lambda b, h, i: (h, b))],
        out_specs=pl.BlockSpec((1, ATT_Q_TILE, g * V_HEAD), lambda b, h, i: (b, i, h)),
        compiler_params=_cparams(("parallel", "parallel", "arbitrary")),
        name="causal_attention",
    )(q, k, vt)


def _router_kernel(x_ref, g_ref, wr_ref, hn_ref, idx_ref, gate_ref):
    hn = _rms(x_ref[...], g_ref[...])
    hn_ref[...] = hn.astype(BF16)
    logits = jnp.dot(hn, wr_ref[...], precision=HIGHEST, preferred_element_type=F32)
    lane = lax.broadcasted_iota(I32, logits.shape, 1)
    logits = jnp.where(lane < N_EXPERTS, logits, -jnp.inf)
    v1 = jnp.max(logits, axis=-1, keepdims=True)
    i1 = jnp.min(jnp.where(logits == v1, lane, LANE), axis=-1, keepdims=True)
    rest = jnp.where(lane == i1, -jnp.inf, logits)
    v2 = jnp.max(rest, axis=-1, keepdims=True)
    i2 = jnp.min(jnp.where(rest == v2, lane, LANE), axis=-1, keepdims=True)
    e2 = jnp.exp(v2 - v1)
    den = 1.0 + e2
    idx_ref[...] = jnp.where(lane == 0, i1, jnp.where(lane == 1, i2, 0))
    record = jnp.zeros(logits.shape, F32)
    for k, gate in enumerate((1.0 / den, e2 / den)):
        hi = gate.astype(BF16).astype(F32)
        mid = (gate - hi).astype(BF16).astype(F32)
        for part, term in enumerate((hi, mid, gate - hi - mid)):
            record = jnp.where(lane == GATE_TERMS * k + part, term, record)
    record = jnp.where(lane == 2 * GATE_TERMS, i1.astype(F32), record)
    record = jnp.where(lane == 2 * GATE_TERMS + 1, i2.astype(F32), record)
    gate_ref[...] = record.astype(BF16)


def _router(x2, g, w_router):
    t = x2.shape[0]
    row = lambda n: pl.BlockSpec((ROW_TILE, n), lambda i: (i, 0))
    wr = _pad_lanes(w_router)
    return pl.pallas_call(
        _router_kernel,
        out_shape=(jax.ShapeDtypeStruct((t, D_MODEL), BF16), jax.ShapeDtypeStruct((t, LANE), I32),
                   jax.ShapeDtypeStruct((t, LANE), BF16)),
        grid=(t // ROW_TILE,),
        in_specs=[row(D_MODEL), pl.BlockSpec((1, D_MODEL), lambda i: (0, 0)),
                  pl.BlockSpec(wr.shape, lambda i: (0, 0))],
        out_specs=(row(D_MODEL), row(LANE), row(LANE)),
        compiler_params=_cparams(("parallel",)),
        name="moe_router",
    )(x2, g.reshape(1, -1), wr)


def _rank_kernel(idx_ref, pos_ref, posr_ref, tile_ref, slot_ref, run_ref, start_ref, lo_ref, hi_ref):
    phase, blk = pl.program_id(0), pl.program_id(1)
    nblk = pl.num_programs(1)
    tb = RANK_TILE
    lane = lax.broadcasted_iota(I32, (tb, LANE), 1)
    idx = idx_ref[...]
    oh0 = (lane == idx[:, 0:1]).astype(F32)
    oh1 = (lane == idx[:, 1:2]).astype(F32)
    both = oh0 + oh1

    @pl.when((phase == 0) & (blk == 0))
    def _():
        run_ref[...] = jnp.zeros_like(run_ref)

    @pl.when(phase == 0)
    def _():
        run_ref[...] += jnp.sum(both, axis=0, keepdims=True)
        pos_ref[...] = jnp.zeros_like(pos_ref)
        posr_ref[...] = jnp.zeros_like(posr_ref)

    @pl.when((phase == 1) & (blk == 0))
    def _():
        counts = run_ref[...]
        tiles = jnp.ceil(counts * (1.0 / MOE_ROW_TILE))
        r = lax.broadcasted_iota(I32, (LANE, LANE), 0)
        c = lax.broadcasted_iota(I32, (LANE, LANE), 1)
        before = (r < c).astype(F32)
        tile_start = jnp.dot(tiles, before, precision=HIGHEST, preferred_element_type=F32)
        start_ref[...] = tile_start * MOE_ROW_TILE
        tile_end = tile_start + tiles
        n_col = lax.broadcasted_iota(I32, (LANE, LANE), 0).astype(F32)
        ended = ((n_col >= tile_end) & (c < N_EXPERTS)).astype(F32)
        expert = jnp.sum(ended, axis=-1, keepdims=True)
        total = jnp.max(tile_end, axis=-1, keepdims=True)
        col = lax.broadcasted_iota(I32, tile_ref.shape, 1)
        tile_ref[...] = jnp.where(col == 0, expert, total).astype(I32)
        run_ref[...] = jnp.zeros_like(run_ref)
        lo_ref[...] = jnp.zeros_like(lo_ref)
        hi_ref[...] = jnp.zeros_like(hi_ref)

    @pl.when(phase == 1)
    def _():
        r = lax.broadcasted_iota(I32, (tb, tb), 0)
        c = lax.broadcasted_iota(I32, (tb, tb), 1)
        strict = (c < r).astype(BF16)
        lo = run_ref[...] + start_ref[...]
        base = _dot(strict, both.astype(BF16)) + lo
        p0 = jnp.sum(oh0 * base, axis=-1, keepdims=True)
        p1 = jnp.sum(oh1 * base, axis=-1, keepdims=True)
        idf = idx.astype(F32)
        posf = jnp.where(lane == 0, p0, jnp.where(lane == 1, p1, jnp.where(
            lane == 2, idf[:, 0:1], jnp.where(lane == 3, idf[:, 1:2], 0.0))))
        pos_ref[...] = posf.astype(I32)
        pick = (lax.broadcasted_iota(I32, (8, LANE), 0) == lax.broadcasted_iota(I32, (8, LANE), 1)).astype(F32)
        posr_ref[...] = _dot_nt_highest(pick, posf).astype(I32)
        run_ref[...] += jnp.sum(both, axis=0, keepdims=True)
        mine = lax.broadcasted_iota(I32, lo_ref.shape, 0) == blk
        lo_ref[...] = jnp.where(mine, lo, lo_ref[...])
        hi_ref[...] = jnp.where(mine, run_ref[...] + start_ref[...], hi_ref[...])

    @pl.when((phase == 1) & (blk == nblk - 1))
    def _():
        lo, hi = lo_ref[...], hi_ref[...]
        first = jnp.floor(lo * (1.0 / MOE_ROW_TILE))
        last = jnp.floor((hi - 1.0) * (1.0 / MOE_ROW_TILE))
        some = hi > lo
        two = some & (last > first)
        n = lo_ref.shape[0]
        slot_ref[0 * n:1 * n, :] = first.astype(I32)
        slot_ref[1 * n:2 * n, :] = jnp.where(two, last, first).astype(I32)
        slot_ref[2 * n:3 * n, :] = some.astype(I32)
        slot_ref[3 * n:4 * n, :] = two.astype(I32)


def _rank(idx):
    t = idx.shape[0]
    nblk = t // RANK_TILE
    const = lambda shape: pl.BlockSpec(shape, lambda p, i: (0, 0))
    return pl.pallas_call(
        _rank_kernel,
        out_shape=(jax.ShapeDtypeStruct((t, LANE), I32), jax.ShapeDtypeStruct((8, t), I32),
                   jax.ShapeDtypeStruct((LANE, 2), I32), jax.ShapeDtypeStruct((4 * nblk, LANE), I32)),
        grid=(2, nblk),
        in_specs=[pl.BlockSpec((RANK_TILE, LANE), lambda p, i: (i, 0))],
        out_specs=(pl.BlockSpec((RANK_TILE, LANE), lambda p, i: (i * p, 0)),
                   pl.BlockSpec((8, RANK_TILE), lambda p, i: (0, i * p)),
                   const((LANE, 2)), const((4 * nblk, LANE))),
        scratch_shapes=[pltpu.VMEM((1, LANE), F32), pltpu.VMEM((1, LANE), F32),
                        pltpu.VMEM((nblk, LANE), F32), pltpu.VMEM((nblk, LANE), F32)],
        compiler_params=_cparams(("arbitrary", "arbitrary")),
        name="moe_rank",
    )(idx)


def _dispatch_kernel(tile_ref, blk_ref, valid_ref, expert_ref, hn_ref, posr_ref, gate_ref, xs_ref, gs_ref,
                     acc_ref, gacc_ref):
    del blk_ref
    p = pl.program_id(0)
    last_slot = pl.num_programs(0) - 1
    tile = tile_ref[p]

    @pl.when((p == 0) | (tile != tile_ref[jnp.maximum(p - 1, 0)]))
    def _():
        acc_ref[...] = jnp.zeros_like(acc_ref)
        gacc_ref[...] = jnp.zeros_like(gacc_ref)

    @pl.when(valid_ref[p] != 0)
    def _():
        expert = expert_ref[p]
        rows = tile * MOE_ROW_TILE + lax.broadcasted_iota(I32, (MOE_ROW_TILE, 1), 0)
        pair_row = jnp.where(posr_ref[2:3, :] == expert, posr_ref[0:1, :], posr_ref[1:2, :])
        sel = jnp.where(pair_row == rows, 1.0, 0.0).astype(BF16)
        acc_ref[...] += _dot(sel, hn_ref[...])
        gacc_ref[...] += _dot(sel, gate_ref[...])

    @pl.when((p == last_slot) | (tile != tile_ref[jnp.minimum(p + 1, last_slot)]))
    def _():
        xs_ref[...] = acc_ref[...].astype(BF16)
        gs_ref[...] = gacc_ref[...]


def _slot_spec(shape, which, transposed=False):
    if transposed:
        return pl.BlockSpec(shape, lambda p, tile, blk, valid, expert: (0, (tile, blk)[which][p]))
    return pl.BlockSpec(shape, lambda p, tile, blk, valid, expert: ((tile, blk)[which][p], 0))


def _dispatch(slots, hn, pos_rows, gates, rows):
    n_slots = slots[0].shape[0]
    tok = lambda n: _slot_spec((RANK_TILE, n), 1)
    return pl.pallas_call(
        _dispatch_kernel,
        out_shape=(jax.ShapeDtypeStruct((rows, D_MODEL), BF16), jax.ShapeDtypeStruct((rows, LANE), F32)),
        grid_spec=pltpu.PrefetchScalarGridSpec(
            num_scalar_prefetch=4,
            grid=(n_slots,),
            in_specs=[tok(D_MODEL), _slot_spec((8, RANK_TILE), 1, transposed=True), tok(LANE)],
            out_specs=(_slot_spec((MOE_ROW_TILE, D_MODEL), 0), _slot_spec((MOE_ROW_TILE, LANE), 0)),
            scratch_shapes=[pltpu.VMEM((MOE_ROW_TILE, D_MODEL), F32), pltpu.VMEM((MOE_ROW_TILE, LANE), F32)],
        ),
        compiler_params=_cparams(("arbitrary",)),
        name="moe_dispatch",
    )(*slots, hn, pos_rows, gates)


def _moe_kernel(expert_ref, ntiles_ref, x_ref, gs_ref, wg_ref, wu_ref, wd_ref, out_ref, acc_ref):
    i, j = pl.program_id(0), pl.program_id(1)
    active = i < ntiles_ref[0]

    @pl.when(j == 0)
    def _():
        acc_ref[...] = jnp.zeros_like(acc_ref)

    @pl.when(active)
    def _():
        _swiglu_accumulate(acc_ref, x_ref[...], wg_ref.at[0], wu_ref.at[0], wd_ref.at[0])

    @pl.when(j == pl.num_programs(1) - 1)
    def _():
        gs = gs_ref[...]
        gate = [sum(gs[:, GATE_TERMS * k + n:GATE_TERMS * k + n + 1] for n in range(GATE_TERMS)) for k in range(TOP_K)]
        first = gs[:, TOP_K * GATE_TERMS:TOP_K * GATE_TERMS + 1] == expert_ref[i].astype(F32)
        row_gate = jnp.where(first, gate[0], gate[1])
        out_ref[...] = jnp.where(active, acc_ref[...] * row_gate, 0.0).astype(BF16)


def _moe_experts(tile_expert, n_tiles, xs, gs, w_gate, w_up, w_down):
    rows = xs.shape[0]
    tm, tf = MOE_ROW_TILE, FFN_FF_TILE
    nj = D_FF // tf

    def x_map(i, j, e_ref, n_ref):
        return jnp.clip(i, 0, jnp.maximum(n_ref[0] - 1, 0)), 0

    def w_idx(i, j, expert_ref, ntiles_ref):
        e = jnp.minimum(expert_ref[i], N_EXPERTS - 1)
        return e, jnp.where(i < ntiles_ref[0], j, nj - 1)

    def w_up_map(i, j, e_ref, n_ref):
        e, jj = w_idx(i, j, e_ref, n_ref)
        return e, 0, jj

    def w_down_map(i, j, e_ref, n_ref):
        e, jj = w_idx(i, j, e_ref, n_ref)
        return e, jj, 0

    return pl.pallas_call(
        _moe_kernel,
        out_shape=jax.ShapeDtypeStruct((rows, D_MODEL), BF16),
        grid_spec=pltpu.PrefetchScalarGridSpec(
            num_scalar_prefetch=2,
            grid=(rows // tm, nj),
            in_specs=[pl.BlockSpec((tm, D_MODEL), x_map),
                      pl.BlockSpec((tm, LANE), x_map),
                      pl.BlockSpec((1, D_MODEL, tf), w_up_map),
                      pl.BlockSpec((1, D_MODEL, tf), w_up_map),
                      pl.BlockSpec((1, tf, D_MODEL), w_down_map)],
            out_specs=pl.BlockSpec((tm, D_MODEL), lambda i, j, e, n: (i, 0)),
            scratch_shapes=[pltpu.VMEM((tm, D_MODEL), F32)],
        ),
        compiler_params=_cparams(("arbitrary", "arbitrary")),
        name="moe_experts",
    )(tile_expert, n_tiles, xs, gs, w_gate, w_up, w_down)


def _combine_kernel(tile_ref, blk_ref, valid_ref, expert_ref, x_ref, pos_ref, ys_ref, out_ref):
    p = pl.program_id(0)

    @pl.when((p == 0) | (blk_ref[p] != blk_ref[jnp.maximum(p - 1, 0)]))
    def _():
        out_ref[...] = x_ref[...]

    @pl.when(valid_ref[p] != 0)
    def _():
        expert = expert_ref[p]
        rows = tile_ref[p] * MOE_ROW_TILE + lax.broadcasted_iota(I32, (1, MOE_ROW_TILE), 1)
        pos = pos_ref[...]
        pair_row = jnp.where(pos[:, 2:3] == expert, pos[:, 0:1], pos[:, 1:2])
        out_ref[...] += _dot(jnp.where(pair_row == rows, 1.0, 0.0).astype(BF16), ys_ref[...])


def _combine(slots, x2, pos, ys):
    n_slots = slots[0].shape[0]
    return pl.pallas_call(
        _combine_kernel,
        out_shape=jax.ShapeDtypeStruct(x2.shape, F32),
        grid_spec=pltpu.PrefetchScalarGridSpec(
            num_scalar_prefetch=4,
            grid=(n_slots,),
            in_specs=[_slot_spec((RANK_TILE, D_MODEL), 1), _slot_spec((RANK_TILE, LANE), 1),
                      _slot_spec((MOE_ROW_TILE, D_MODEL), 0)],
            out_specs=_slot_spec((RANK_TILE, D_MODEL), 1),
        ),
        compiler_params=_cparams(("arbitrary",)),
        name="moe_combine",
    )(*slots, x2, pos, ys)


def _moe_ffn(x2, g, w_router, w_gate_up, w_down):
    t = x2.shape[0]
    nblk = t // RANK_TILE
    hn, idx, gates = _router(x2, g, w_router)
    pos, pos_rows, tile_info, slot_tab = _rank(idx)
    rows = t * TOP_K + N_EXPERTS * MOE_ROW_TILE
    tab = slot_tab.reshape(4, nblk, LANE)[:, :, :N_EXPERTS]
    tile = jnp.stack([tab[0], tab[1]], axis=-1)
    valid = jnp.stack([tab[2], tab[3]], axis=-1)
    blk = jnp.broadcast_to(jnp.arange(nblk, dtype=I32)[:, None, None], tile.shape)
    expert = jnp.broadcast_to(jnp.arange(N_EXPERTS, dtype=I32)[None, :, None], tile.shape)
    n_slots = nblk * N_EXPERTS + rows // MOE_ROW_TILE

    def visiting_order(flatten):
        t_, b_, v_, e_ = (flatten(a) for a in (tile, blk, valid, expert))
        order = jnp.argsort(1 - v_, stable=True)[:n_slots]
        n_valid = jnp.sum(v_)
        keep = jnp.arange(n_slots) < n_valid
        pick = jnp.where(keep, order, order[jnp.maximum(n_valid - 1, 0)])
        return t_[pick], b_[pick], v_[pick] * keep.astype(I32), e_[pick]

    by_expert = visiting_order(lambda a: a.transpose(1, 0, 2).reshape(-1))
    by_block = visiting_order(lambda a: a.reshape(-1))
    xs, gs = _dispatch(by_expert, hn, pos_rows, gates, rows)
    ys = _moe_experts(tile_info[:, 0], tile_info[:1, 1], xs, gs,
                      w_gate_up[..., :D_FF].astype(BF16), w_gate_up[..., D_FF:].astype(BF16),
                      w_down.astype(BF16))
    return _combine(by_block, x2, pos, ys)


def _pad_heads(w, heads, dim, pad):
    w = w.reshape(w.shape[:-1] + (heads, dim))
    return _pad_lanes(w, pad).reshape(w.shape[:-2] + (heads * pad,))


def kernel(x, mem, positions, a_norm, a_w_in, a_gate_bias, a_head_norm, a_w_out, b_norm, b_w_in, b_q_a_norm, b_w_uq, b_q_head_norm, b_w_out, kv_norm, w_dkv, kv_a_norm, w_ukv, k_head_norm, mem_norm, mem_w_kv, mem_q_norm, mem_k_norm, ffn_norm, dense_w_gate_up, dense_w_down, moe_router, moe_w_gate_up, moe_w_down):
    nb, seq, _ = x.shape
    t = nb * seq
    x2 = x.reshape(t, D_MODEL)
    gmat = jnp.kron(jnp.eye(M_HEADS, dtype=F32), jnp.full((M_HEAD_DIM, M_HEAD_DIM), 1.0 / M_HEAD_DIM, F32))

    kbd0, vbd0 = _memory_kv(mem, mem_norm[0], mem_w_kv[0], mem_k_norm[0], gmat)
    w_in = a_w_in[0]
    qk_w, v_w = A_HEADS * A_QK_DIM, A_HEADS * A_V_DIM
    o0, o1, o2, o3, o4 = qk_w, 2 * qk_w, 2 * qk_w + v_w, 2 * qk_w + 2 * v_w, 2 * qk_w + 2 * v_w + 2 * A_HEADS
    w_main = jnp.concatenate([
        _pad_heads(w_in[:, :o0], A_HEADS, A_QK_DIM, A_QK_PAD),
        _pad_heads(w_in[:, o0:o1], A_HEADS, A_QK_DIM, A_QK_PAD),
        _pad_heads(w_in[:, o1:o2], A_HEADS, A_V_DIM, A_V_PAD),
        _pad_heads(w_in[:, o2:o3], A_HEADS, A_V_DIM, A_V_PAD),
        w_in[:, o4:]], axis=1).astype(BF16)
    q, k, v, o, mq, gc, gr = _a_projection(x2, a_norm[0], w_main, w_in[:, o3:o4], a_gate_bias[0])
    gc, gr = _mlstm_gates(gc, gr)
    gcol = gc.reshape(nb, seq, 2, A_HEADS).transpose(0, 3, 1, 2)
    grow = gr.reshape(2, A_HEADS, nb, seq).transpose(2, 1, 0, 3)
    three = lambda a: a.reshape(nb, seq, a.shape[-1])
    hm = _mlstm(three(q), three(k), three(v), three(o), gcol, grow,
                _pad_heads(a_head_norm[0].reshape(1, -1), A_HEADS, A_V_DIM, A_V_PAD))
    w_out = a_w_out[0]
    w_out_h = jnp.pad(w_out[:v_w].reshape(A_HEADS, A_V_DIM, D_MODEL), ((0, 0), (0, A_V_PAD - A_V_DIM), (0, 0)))
    w_out_h = w_out_h.reshape(A_HEADS * A_V_PAD, D_MODEL).astype(BF16)
    x2 = _mix_out(x2, hm.reshape(t, -1), mq, kbd0, vbd0, gmat, mem_q_norm[0],
                  w_out_h, w_out[v_w:].astype(BF16), seq)
    wgu = dense_w_gate_up[0]
    x2 = _dense_ffn(x2, ffn_norm[0], wgu[:, :D_FF].astype(BF16), wgu[:, D_FF:].astype(BF16),
                    dense_w_down[0].astype(BF16))

    cs, sn = _rope_tables(positions)
    k_sh, vt_sh = _latent_kv(x2, kv_norm, w_dkv, kv_a_norm, w_ukv, k_head_norm, cs, sn)

    kbd1, vbd1 = _memory_kv(mem, mem_norm[1], mem_w_kv[1], mem_k_norm[1], gmat)
    qh, mq1 = _b_projection(x2, b_norm[0], b_w_in[0], b_q_a_norm[0], b_w_uq[0], b_q_head_norm[0], cs, sn)
    att = _causal_attention(three(qh), three(k_sh), vt_sh, seq)
    w_out = b_w_out[0]
    n_att = B_HEADS * V_HEAD
    x2 = _mix_out(x2, att.reshape(t, -1), mq1, kbd1, vbd1, gmat, mem_q_norm[1],
                  w_out[:n_att].astype(BF16), w_out[n_att:].astype(BF16), seq)
    x2 = _moe_ffn(x2, ffn_norm[1], moe_router[0], moe_w_gate_up[0], moe_w_down[0])
    return x2.reshape(nb, seq, D_MODEL)
```

```python
import functools

import jax
import jax.numpy as jnp
from jax import lax
from jax.experimental import pallas as pl
from jax.experimental.pallas import tpu as pltpu

F32 = jnp.float32
BF16 = jnp.bfloat16
I32 = jnp.int32

EPS = 1e-6
LANE = 128
VMEM_LIMIT = 48 * 1024 * 1024

D_MODEL = 1024
N_MEM = 256
M_HEADS, M_HEAD_DIM = 4, 64
M_W = M_HEADS * M_HEAD_DIM
A_HEADS, A_QK_DIM, A_V_DIM = 4, 96, 192
A_QK_PAD, A_V_PAD = 128, 256
B_HEADS, Q_LORA, KV_LORA = 6, 384, 256
QK_NOPE, QK_ROPE, V_HEAD = 128, 64, 128
B_QK_HEAD = QK_NOPE + QK_ROPE
B_QK_PAD = 256
VT_HEAD_ROWS = V_HEAD + 16
ROPE_THETA = 10000.0
D_FF = 3584
N_EXPERTS, TOP_K = 8, 2
GATE_TERMS = 3

MLSTM_CHUNK = 256
ROW_TILE = 512
FFN_ROW_TILE = 1024
FFN_FF_TILE = 1792
FFN_CHUNK = 256
MOE_ROW_TILE = 512
DISPATCH_TILE = 256
ATT_Q_TILE = 512
ATT_K_TILE = 512
ATT_HEADS_PER_STEP = 6
RANK_TILE = 512
SLOTS_PER_PAIR = (RANK_TILE + MOE_ROW_TILE) // DISPATCH_TILE + 1

HIGHEST = lax.Precision.HIGHEST


def _cparams(sem):
    return pltpu.CompilerParams(dimension_semantics=sem, vmem_limit_bytes=VMEM_LIMIT)


def _rms(x, g):
    return x * lax.rsqrt(jnp.mean(x * x, axis=-1, keepdims=True) + EPS) * g


def _dot(a, b):
    return jnp.dot(a, b, preferred_element_type=F32)


def _dot_nt(a, b):
    return lax.dot_general(a, b, (((1,), (1,)), ((), ())), preferred_element_type=F32)


def _dot_tn(a, b):
    return lax.dot_general(a, b, (((0,), (0,)), ((), ())), preferred_element_type=F32)


def _group_mean_sq(x, gmat):
    return jnp.dot(x * x, gmat, precision=HIGHEST, preferred_element_type=F32)


def _memkv_kernel(mem_ref, g_ref, w_ref, kg_ref, gmat_ref, kbd_ref, vbd_ref):
    hn = _rms(mem_ref[0], g_ref[...]).astype(BF16)
    kv = _dot(hn, w_ref[...])
    k, v = kv[:, :M_W], kv[:, M_W:]
    kn = k * lax.rsqrt(_group_mean_sq(k, gmat_ref[...]) + EPS) * kg_ref[...]
    lane_head = lax.broadcasted_iota(I32, (1, M_W), 1) // M_HEAD_DIM
    for h in range(M_HEADS):
        keep = lane_head == h
        kbd_ref[0, h * N_MEM:(h + 1) * N_MEM, :] = jnp.where(keep, kn, 0.0).astype(BF16)
        vbd_ref[0, h * N_MEM:(h + 1) * N_MEM, :] = jnp.where(keep, v, 0.0).astype(BF16)


def _memory_kv(mem, g, w_kv, k_g, gmat):
    nb = mem.shape[0]
    out = jax.ShapeDtypeStruct((nb, M_HEADS * N_MEM, M_W), BF16)
    return pl.pallas_call(
        _memkv_kernel,
        out_shape=(out, out),
        grid=(nb,),
        in_specs=[
            pl.BlockSpec((1, N_MEM, D_MODEL), lambda b: (b, 0, 0)),
            pl.BlockSpec((1, D_MODEL), lambda b: (0, 0)),
            pl.BlockSpec((D_MODEL, 2 * M_W), lambda b: (0, 0)),
            pl.BlockSpec((1, M_W), lambda b: (0, 0)),
            pl.BlockSpec((M_W, M_W), lambda b: (0, 0)),
        ],
        out_specs=(pl.BlockSpec((1, M_HEADS * N_MEM, M_W), lambda b: (b, 0, 0)),) * 2,
        compiler_params=_cparams(("parallel",)),
        name="memory_kv",
    )(mem, g.reshape(1, -1), w_kv.astype(BF16), jnp.tile(k_g, M_HEADS).reshape(1, -1), gmat)


def _memory_attention(mq, kbd, vbd, gmat, qg):
    qn = mq * lax.rsqrt(_group_mean_sq(mq, gmat) + EPS) * (qg * (M_HEAD_DIM ** -0.5))
    s = _dot_nt(qn.astype(BF16), kbd)
    ps = []
    for h in range(M_HEADS):
        sh = s[:, h * N_MEM:(h + 1) * N_MEM]
        e = jnp.exp(sh - jnp.max(sh, axis=-1, keepdims=True))
        ps.append((e / jnp.sum(e, axis=-1, keepdims=True)).astype(BF16))
    return _dot(jnp.concatenate(ps, axis=-1), vbd)


def _a_proj_kernel(x_ref, g_ref, w_ref, wif_ref, wift_ref, bc_ref, br_ref,
                   q_ref, k_ref, v_ref, o_ref, mq_ref, gc_ref, gr_ref):
    hn = _rms(x_ref[...], g_ref[...]).astype(BF16)
    nq = A_HEADS * A_QK_PAD
    nv = A_HEADS * A_V_PAD
    q_ref[...] = _dot(hn, w_ref[:, :nq]).astype(BF16)
    k_ref[...] = (_dot(hn, w_ref[:, nq:2 * nq]) * (A_QK_DIM ** -0.5)).astype(BF16)
    v = _dot(hn, w_ref[:, 2 * nq:2 * nq + nv])
    ones_lane = lax.broadcasted_iota(I32, (1, nv), 1) % A_V_PAD == A_V_DIM
    v_ref[...] = jnp.where(ones_lane, 1.0, v).astype(BF16)
    o_ref[...] = _dot(hn, w_ref[:, 2 * nq + nv:2 * nq + 2 * nv])
    mq_ref[...] = _dot(hn, w_ref[:, 2 * nq + 2 * nv:])
    gc_ref[...] = _dot(hn, wif_ref[...])[:, :2 * A_HEADS] + bc_ref[...]
    gr_ref[...] = _dot_nt(wift_ref[...], hn) + br_ref[...]


def _a_projection(x2, g, w_main, w_if, gate_bias):
    t = x2.shape[0]
    nq, nv = A_HEADS * A_QK_PAD, A_HEADS * A_V_PAD
    ng = 2 * A_HEADS
    wif_pad = jnp.pad(w_if, ((0, 0), (0, LANE - ng))).astype(BF16)
    row = lambda n: pl.BlockSpec((ROW_TILE, n), lambda i: (i, 0))
    full = lambda a: pl.BlockSpec(a.shape, lambda i: (0,) * a.ndim)
    args = (x2, g.reshape(1, -1), w_main, wif_pad, w_if.T.astype(BF16),
            gate_bias.reshape(1, ng), gate_bias.reshape(ng, 1))
    return pl.pallas_call(
        _a_proj_kernel,
        out_shape=(jax.ShapeDtypeStruct((t, nq), BF16), jax.ShapeDtypeStruct((t, nq), BF16),
                   jax.ShapeDtypeStruct((t, nv), BF16), jax.ShapeDtypeStruct((t, nv), F32),
                   jax.ShapeDtypeStruct((t, M_W), F32), jax.ShapeDtypeStruct((t, ng), F32),
                   jax.ShapeDtypeStruct((ng, t), F32)),
        grid=(t // ROW_TILE,),
        in_specs=[row(D_MODEL)] + [full(a) for a in args[1:]],
        out_specs=(row(nq), row(nq), row(nv), row(nv), row(M_W), row(ng),
                   pl.BlockSpec((ng, ROW_TILE), lambda i: (0, i))),
        compiler_params=_cparams(("parallel",)),
        name="mlstm_in_proj",
    )(*args)


def _log_sigmoid(f):
    return jnp.minimum(f, 0.0) - jnp.log(1.0 + jnp.exp(-jnp.abs(f)))


def _gates_kernel(gc_ref, gr_ref, oc_ref, or_ref):
    L = MLSTM_CHUNK
    gc = gc_ref[...]
    gr = gr_ref[...]
    r = lax.broadcasted_iota(I32, (L, L), 0)
    c = lax.broadcasted_iota(I32, (L, L), 1)
    lower = (c <= r).astype(F32)
    is_f_col = lax.broadcasted_iota(I32, gc.shape, 1) >= A_HEADS
    is_f_row = lax.broadcasted_iota(I32, gr.shape, 0) >= A_HEADS
    lf_c = jnp.where(is_f_col, _log_sigmoid(gc), 0.0)
    lf_r = jnp.where(is_f_row, _log_sigmoid(gr), 0.0)
    lf_c = jnp.concatenate([lf_c, jnp.zeros((L, LANE - gc.shape[1]), F32)], axis=1)
    cum_c = jnp.dot(lower, lf_c, precision=HIGHEST, preferred_element_type=F32)[:, :gc.shape[1]]
    cum_r = _dot_nt_highest(lf_r, lower)
    oc_ref[...] = jnp.where(is_f_col, cum_c, gc)
    or_ref[...] = jnp.where(is_f_row, cum_r, gr)


def _dot_nt_highest(a, b):
    return lax.dot_general(a, b, (((1,), (1,)), ((), ())), precision=HIGHEST,
                           preferred_element_type=F32)


def _mlstm_gates(gc, gr):
    t, ng = gc.shape
    L = MLSTM_CHUNK
    return pl.pallas_call(
        _gates_kernel,
        out_shape=(jax.ShapeDtypeStruct((t, ng), F32), jax.ShapeDtypeStruct((ng, t), F32)),
        grid=(t // L,),
        in_specs=[pl.BlockSpec((L, ng), lambda i: (i, 0)), pl.BlockSpec((ng, L), lambda i: (0, i))],
        out_specs=(pl.BlockSpec((L, ng), lambda i: (i, 0)), pl.BlockSpec((ng, L), lambda i: (0, i))),
        compiler_params=_cparams(("parallel",)),
        name="mlstm_gates",
    )(gc, gr)


def _mlstm_kernel(q_ref, k_ref, v_ref, o_ref, gc_ref, gr_ref, hg_ref, out_ref, c_ref, m_ref):
    L = MLSTM_CHUNK

    @pl.when(pl.program_id(1) == 0)
    def _():
        c_ref[...] = jnp.zeros_like(c_ref)
        m_ref[...] = jnp.zeros_like(m_ref)

    t_idx = lax.broadcasted_iota(I32, (L, L), 0)
    s_idx = lax.broadcasted_iota(I32, (L, L), 1)
    real = lax.broadcasted_iota(I32, (1, A_V_PAD), 1) < A_V_DIM
    for hd in range(A_HEADS):
        qk = slice(hd * A_QK_PAD, (hd + 1) * A_QK_PAD)
        vv = slice(hd * A_V_PAD, (hd + 1) * A_V_PAD)
        q, k, v = q_ref[0, :, qk], k_ref[0, :, qk], v_ref[0, :, vv]
        gcol = gc_ref[0, hd]
        grow = gr_ref[0, hd]
        i_c, g_c = gcol[:, 0:1], gcol[:, 1:2]
        i_r, g_r = grow[0:1, :], grow[1:2, :]
        g_last = g_r[:, L - 1:L]
        m_prev = m_ref[hd, 0:1, 0:1]
        c_prev = c_ref[hd]

        a_log = g_c + m_prev
        d_log = jnp.where(s_idx <= t_idx, g_c - g_r + i_r, -jnp.inf)
        m_t = jnp.maximum(a_log, jnp.max(d_log, axis=-1, keepdims=True))
        inter = jnp.exp(a_log - m_t)
        p = (jnp.exp(d_log - m_t) * _dot_nt(q, k)).astype(BF16)
        num = inter * _dot(q, c_prev.astype(BF16)) + _dot(p, v)
        den = num[:, A_V_DIM:A_V_DIM + 1]
        h = num / jnp.maximum(jnp.abs(den), jnp.exp(-m_t))
        h = jnp.where(real, h, 0.0)
        hn = h * lax.rsqrt(jnp.sum(h * h, axis=-1, keepdims=True) * (1.0 / A_V_DIM) + EPS) * hg_ref[:, vv]
        out_ref[0, :, vv] = (hn * jax.nn.sigmoid(o_ref[0, :, vv])).astype(BF16)

        w_r = g_last - g_r + i_r
        m_new = jnp.maximum(g_last + m_prev, jnp.max(w_r, axis=-1, keepdims=True))
        e_c = jnp.exp(g_last - g_c + i_c - m_new)
        ev = (e_c * v.astype(F32)).astype(BF16)
        c_ref[hd] = jnp.exp(g_last + m_prev - m_new) * c_prev + _dot_tn(k, ev)
        m_ref[hd] = jnp.broadcast_to(m_new, m_ref.shape[1:])


def _mlstm(q, k, v, o, gcol, grow, head_g):
    nb, s, _ = q.shape
    L = MLSTM_CHUNK
    blk = lambda w: pl.BlockSpec((1, L, w), lambda b, c: (b, c, 0))
    return pl.pallas_call(
        _mlstm_kernel,
        out_shape=jax.ShapeDtypeStruct((nb, s, A_HEADS * A_V_PAD), BF16),
        grid=(nb, s // L),
        in_specs=[blk(A_HEADS * A_QK_PAD), blk(A_HEADS * A_QK_PAD), blk(A_HEADS * A_V_PAD), blk(A_HEADS * A_V_PAD),
                  pl.BlockSpec((1, A_HEADS, L, 2), lambda b, c: (b, 0, c, 0)),
                  pl.BlockSpec((1, A_HEADS, 2, L), lambda b, c: (b, 0, 0, c)),
                  pl.BlockSpec((1, A_HEADS * A_V_PAD), lambda b, c: (0, 0))],
        out_specs=blk(A_HEADS * A_V_PAD),
        scratch_shapes=[pltpu.VMEM((A_HEADS, A_QK_PAD, A_V_PAD), F32), pltpu.VMEM((A_HEADS, 8, LANE), F32)],
        compiler_params=_cparams(("parallel", "arbitrary")),
        name="mlstm_chunkwise",
    )(q, k, v, o, gcol, grow, head_g)


def _mix_out_kernel(x_ref, h_ref, mq_ref, kbd_ref, vbd_ref, gmat_ref, qg_ref, w1_ref, w2_ref, out_ref):
    mo = _memory_attention(mq_ref[...], kbd_ref[0], vbd_ref[0], gmat_ref[...], qg_ref[...])
    out_ref[...] = x_ref[...] + _dot(h_ref[...], w1_ref[...]) + _dot(mo.astype(BF16), w2_ref[...])


def _mix_out(x2, h2, mq, kbd, vbd, gmat, qg, w_main, w_mem, seq):
    t = x2.shape[0]
    tm = ROW_TILE
    row = lambda n: pl.BlockSpec((tm, n), lambda i: (i, 0))
    full = lambda a: pl.BlockSpec(a.shape, lambda i: (0,) * a.ndim)
    per_batch = pl.BlockSpec((1,) + kbd.shape[1:], lambda i: ((i * tm) // seq, 0, 0))
    qg_t = jnp.tile(qg, M_HEADS).reshape(1, -1)
    return pl.pallas_call(
        _mix_out_kernel,
        out_shape=jax.ShapeDtypeStruct((t, D_MODEL), F32),
        grid=(t // tm,),
        in_specs=[row(D_MODEL), row(h2.shape[1]), row(M_W), per_batch, per_batch,
                  full(gmat), full(qg_t), full(w_main), full(w_mem)],
        out_specs=row(D_MODEL),
        compiler_params=_cparams(("parallel",)),
        name="mixer_out_proj",
    )(x2, h2, mq, kbd, vbd, gmat, qg_t, w_main, w_mem)


def _swiglu_accumulate(acc_ref, x, wg_ref, wu_ref, wd_ref):
    for c in range(wg_ref.shape[-1] // FFN_CHUNK):
        cols = slice(c * FFN_CHUNK, (c + 1) * FFN_CHUNK)
        gate = _dot(x, wg_ref[:, cols])
        up = _dot(x, wu_ref[:, cols])
        acc_ref[...] += _dot((jax.nn.silu(gate) * up).astype(BF16), wd_ref[cols, :])


def _ffn_kernel(x_ref, g_ref, wg_ref, wu_ref, wd_ref, out_ref, hn_ref, acc_ref):
    j = pl.program_id(1)

    @pl.when(j == 0)
    def _():
        hn_ref[...] = _rms(x_ref[...], g_ref[...]).astype(BF16)
        acc_ref[...] = jnp.zeros_like(acc_ref)

    _swiglu_accumulate(acc_ref, hn_ref[...], wg_ref, wu_ref, wd_ref)

    @pl.when(j == pl.num_programs(1) - 1)
    def _():
        out_ref[...] = x_ref[...] + acc_ref[...]


def _dense_ffn(x2, g, w_gate, w_up, w_down):
    t = x2.shape[0]
    tm, tf = FFN_ROW_TILE, FFN_FF_TILE
    return pl.pallas_call(
        _ffn_kernel,
        out_shape=jax.ShapeDtypeStruct((t, D_MODEL), F32),
        grid=(t // tm, D_FF // tf),
        in_specs=[pl.BlockSpec((tm, D_MODEL), lambda i, j: (i, 0)),
                  pl.BlockSpec((1, D_MODEL), lambda i, j: (0, 0)),
                  pl.BlockSpec((D_MODEL, tf), lambda i, j: (0, j)),
                  pl.BlockSpec((D_MODEL, tf), lambda i, j: (0, j)),
                  pl.BlockSpec((tf, D_MODEL), lambda i, j: (j, 0))],
        out_specs=pl.BlockSpec((tm, D_MODEL), lambda i, j: (i, 0)),
        scratch_shapes=[pltpu.VMEM((tm, D_MODEL), BF16), pltpu.VMEM((tm, D_MODEL), F32)],
        compiler_params=_cparams(("parallel", "arbitrary")),
        name="dense_swiglu",
    )(x2, g.reshape(1, -1), w_gate, w_up, w_down)


def _rope_kernel(pos_ref, inv_ref, sign_ref, cs_ref, sn_ref):
    ang = pos_ref[...].astype(F32) * inv_ref[...]
    cs_ref[...] = jnp.cos(ang)
    sn_ref[...] = jnp.sin(ang) * sign_ref[...]


def _rope_tables(positions):
    t = positions.size
    half = QK_ROPE // 2
    inv = 1.0 / (ROPE_THETA ** (jnp.arange(0, QK_ROPE, 2, dtype=F32) / QK_ROPE))
    pad = jnp.zeros((LANE - QK_ROPE,), F32)
    inv_l = jnp.concatenate([inv, inv, pad]).reshape(1, LANE)
    sign = jnp.concatenate([-jnp.ones((half,), F32), jnp.ones((half,), F32), pad]).reshape(1, LANE)
    out = jax.ShapeDtypeStruct((t, LANE), F32)
    return pl.pallas_call(
        _rope_kernel,
        out_shape=(out, out),
        grid=(t // ROW_TILE,),
        in_specs=[pl.BlockSpec((ROW_TILE, 1), lambda i: (i, 0)),
                  pl.BlockSpec((1, LANE), lambda i: (0, 0)),
                  pl.BlockSpec((1, LANE), lambda i: (0, 0))],
        out_specs=(pl.BlockSpec((ROW_TILE, LANE), lambda i: (i, 0)),) * 2,
        compiler_params=_cparams(("parallel",)),
        name="rope_tables",
    )(positions.reshape(t, 1), inv_l, sign)


def _head_qk_norm_rope(nope, rope, rope_sw, g_nope, g_rope, g_rope_sw, cs, sn, scale):
    ss = jnp.sum(nope * nope, axis=-1, keepdims=True) + jnp.sum(rope * rope, axis=-1, keepdims=True)
    r = lax.rsqrt(ss * (1.0 / B_QK_HEAD) + EPS) * scale
    return nope * r * g_nope, (rope * g_rope * cs + rope_sw * g_rope_sw * sn) * r


def _latent_kv_kernel(x_ref, g_ref, wd_ref, ga_ref, wuk_ref, wuvt_ref, kg_ref, cs_ref, sn_ref, k_ref, vt_ref):
    hn = _rms(x_ref[...], g_ref[...]).astype(BF16)
    z = _dot(hn, wd_ref[...])
    c_kv = z[:, :KV_LORA]
    rope, rope_sw = z[:, KV_LORA:KV_LORA + LANE], z[:, KV_LORA + LANE:]
    cn = _rms(c_kv, ga_ref[...]).astype(BF16)
    kv = _dot(cn, wuk_ref[...])
    vt = _dot_nt(wuvt_ref[...], cn)
    tm = vt.shape[1]
    ones_row = (lax.broadcasted_iota(I32, (VT_HEAD_ROWS - V_HEAD, tm), 0) == 0).astype(BF16)
    for h in range(B_HEADS):
        vt_ref[h * VT_HEAD_ROWS:h * VT_HEAD_ROWS + V_HEAD, :] = vt[h * V_HEAD:(h + 1) * V_HEAD].astype(BF16)
        vt_ref[h * VT_HEAD_ROWS + V_HEAD:(h + 1) * VT_HEAD_ROWS, :] = ones_row
    kg = kg_ref[...]
    for h in range(B_HEADS):
        kn, kr = _head_qk_norm_rope(kv[:, h * QK_NOPE:(h + 1) * QK_NOPE], rope, rope_sw,
                                    kg[:, :LANE], kg[:, LANE:2 * LANE], kg[:, 2 * LANE:],
                                    cs_ref[...], sn_ref[...], 1.0)
        k_ref[:, h * B_QK_PAD:h * B_QK_PAD + QK_NOPE] = kn.astype(BF16)
        k_ref[:, h * B_QK_PAD + QK_NOPE:(h + 1) * B_QK_PAD] = kr.astype(BF16)


def _rope_swap(w):
    half = QK_ROPE // 2
    return jnp.concatenate([w[..., half:], w[..., :half]], axis=-1)


def _pad_lanes(w, n=LANE):
    return jnp.pad(w, [(0, 0)] * (w.ndim - 1) + [(0, n - w.shape[-1])])


def _head_gain(g):
    g_rope = g[QK_NOPE:]
    return jnp.concatenate([g[:QK_NOPE], _pad_lanes(g_rope), _pad_lanes(_rope_swap(g_rope))]).reshape(1, -1)


def _latent_kv(x2, kv_norm, w_dkv, kv_a_norm, w_ukv, k_head_norm, cs, sn):
    t = x2.shape[0]
    w_rope = w_dkv[:, KV_LORA:]
    wd = jnp.concatenate([w_dkv[:, :KV_LORA], _pad_lanes(w_rope), _pad_lanes(_rope_swap(w_rope))],
                         axis=1).astype(BF16)
    wu = w_ukv.reshape(KV_LORA, B_HEADS, QK_NOPE + V_HEAD)
    wuk = wu[:, :, :QK_NOPE].reshape(KV_LORA, -1).astype(BF16)
    wuvt = wu[:, :, QK_NOPE:].reshape(KV_LORA, -1).T.astype(BF16)
    args = (x2, kv_norm.reshape(1, -1), wd, kv_a_norm.reshape(1, -1), wuk, wuvt, _head_gain(k_head_norm), cs, sn)
    row = lambda n: pl.BlockSpec((ROW_TILE, n), lambda i: (i, 0))
    full = lambda a: pl.BlockSpec(a.shape, lambda i: (0,) * a.ndim)
    return pl.pallas_call(
        _latent_kv_kernel,
        out_shape=(jax.ShapeDtypeStruct((t, B_HEADS * B_QK_PAD), BF16),
                   jax.ShapeDtypeStruct((B_HEADS * VT_HEAD_ROWS, t), BF16)),
        grid=(t // ROW_TILE,),
        in_specs=[row(D_MODEL)] + [full(a) for a in args[1:7]] + [row(LANE), row(LANE)],
        out_specs=(row(B_HEADS * B_QK_PAD), pl.BlockSpec((B_HEADS * VT_HEAD_ROWS, ROW_TILE), lambda i: (0, i))),
        compiler_params=_cparams(("parallel",)),
        name="latent_kv",
    )(*args)


def _b_proj_kernel(x_ref, g_ref, win_ref, ga_ref, wuq_ref, qg_ref, cs_ref, sn_ref, q_ref, mq_ref):
    hn = _rms(x_ref[...], g_ref[...]).astype(BF16)
    proj = _dot(hn, win_ref[...])
    mq_ref[...] = proj[:, Q_LORA:]
    qall = _dot(_rms(proj[:, :Q_LORA], ga_ref[...]).astype(BF16), wuq_ref[...])
    qg = qg_ref[...]
    per_head = QK_NOPE + 2 * LANE
    for h in range(B_HEADS):
        base = h * per_head
        qn, qr = _head_qk_norm_rope(qall[:, base:base + QK_NOPE],
                                    qall[:, base + QK_NOPE:base + QK_NOPE + LANE],
                                    qall[:, base + QK_NOPE + LANE:base + per_head],
                                    qg[:, :LANE], qg[:, LANE:2 * LANE], qg[:, 2 * LANE:],
                                    cs_ref[...], sn_ref[...], B_QK_HEAD ** -0.5)
        q_ref[:, h * B_QK_PAD:h * B_QK_PAD + QK_NOPE] = qn.astype(BF16)
        q_ref[:, h * B_QK_PAD + QK_NOPE:(h + 1) * B_QK_PAD] = qr.astype(BF16)


def _b_projection(x2, g, w_in, q_a_g, w_uq, q_head_g, cs, sn):
    t = x2.shape[0]
    wq = w_uq.reshape(Q_LORA, B_HEADS, B_QK_HEAD)
    w_rope = wq[:, :, QK_NOPE:]
    wq = jnp.concatenate([wq[:, :, :QK_NOPE], _pad_lanes(w_rope), _pad_lanes(_rope_swap(w_rope))],
                         axis=-1).reshape(Q_LORA, -1).astype(BF16)
    args = (x2, g.reshape(1, -1), w_in.astype(BF16), q_a_g.reshape(1, -1), wq, _head_gain(q_head_g), cs, sn)
    row = lambda n: pl.BlockSpec((ROW_TILE, n), lambda i: (i, 0))
    full = lambda a: pl.BlockSpec(a.shape, lambda i: (0,) * a.ndim)
    return pl.pallas_call(
        _b_proj_kernel,
        out_shape=(jax.ShapeDtypeStruct((t, B_HEADS * B_QK_PAD), BF16),
                   jax.ShapeDtypeStruct((t, M_W), F32)),
        grid=(t // ROW_TILE,),
        in_specs=[row(D_MODEL)] + [full(a) for a in args[1:6]] + [row(LANE), row(LANE)],
        out_specs=(row(B_HEADS * B_QK_PAD), row(M_W)),
        compiler_params=_cparams(("parallel",)),
        name="mla_q_proj",
    )(*args)


def _attn_kernel(q_ref, k_ref, vt_ref, out_ref):
    tq, tk = ATT_Q_TILE, ATT_K_TILE
    i = pl.program_id(2)

    def block(h, j, carry, masked):
        m, acc = carry
        off = pl.multiple_of(j * tk, tk)
        q = q_ref[0, :, h * B_QK_PAD:(h + 1) * B_QK_PAD]
        st = _dot_nt(k_ref[0, pl.ds(off, tk), h * B_QK_PAD:(h + 1) * B_QK_PAD], q)
        if masked:
            key = lax.broadcasted_iota(I32, (tk, tq), 0)
            qry = lax.broadcasted_iota(I32, (tk, tq), 1)
            st = jnp.where(key <= qry, st, -jnp.inf)
        m_new = jnp.maximum(m, jnp.max(st, axis=0, keepdims=True))
        p = jnp.exp(st - m_new).astype(BF16)
        vt = vt_ref[h * VT_HEAD_ROWS:(h + 1) * VT_HEAD_ROWS, pl.ds(off, tk)]
        return m_new, jnp.exp(m - m_new) * acc + _dot(vt, p)

    heads = range(ATT_HEADS_PER_STEP)
    init = tuple((jnp.full((1, tq), -jnp.inf, F32), jnp.zeros((VT_HEAD_ROWS, tq), F32)) for _ in heads)
    carry = lax.fori_loop(0, i, lambda j, c: tuple(block(h, j, c[h], False) for h in heads), init)
    for h in heads:
        _, acc = block(h, i, carry[h], True)
        out_t = acc[:V_HEAD] / acc[V_HEAD:V_HEAD + 1]
        out_ref[0, :, h * V_HEAD:(h + 1) * V_HEAD] = out_t.T.astype(BF16)


def _causal_attention(q, k, vt, seq):
    nb = q.shape[0]
    g = ATT_HEADS_PER_STEP
    assert ATT_Q_TILE == ATT_K_TILE and B_HEADS % g == 0
    return pl.pallas_call(
        _attn_kernel,
        out_shape=jax.ShapeDtypeStruct((nb, seq, B_HEADS * V_HEAD), BF16),
        grid=(nb, B_HEADS // g, seq // ATT_Q_TILE),
        in_specs=[pl.BlockSpec((1, ATT_Q_TILE, g * B_QK_PAD), lambda b, h, i: (b, i, h)),
                  pl.BlockSpec((1, seq, g * B_QK_PAD), lambda b, h, i: (b, 0, h)),
                  pl.BlockSpec((g * VT_HEAD_ROWS, seq), lambda b, h, i: (h, b))],
        out_specs=pl.BlockSpec((1, ATT_Q_TILE, g * V_HEAD), lambda b, h, i: (b, i, h)),
        compiler_params=_cparams(("parallel", "parallel", "arbitrary")),
        name="causal_attention",
    )(q, k, vt)


def _router_kernel(x_ref, g_ref, wr_ref, hn_ref, idx_ref, gate_ref):
    hn = _rms(x_ref[...], g_ref[...])
    hn_ref[...] = hn.astype(BF16)
    logits = jnp.dot(hn, wr_ref[...], precision=HIGHEST, preferred_element_type=F32)
    lane = lax.broadcasted_iota(I32, logits.shape, 1)
    logits = jnp.where(lane < N_EXPERTS, logits, -jnp.inf)
    v1 = jnp.max(logits, axis=-1, keepdims=True)
    i1 = jnp.min(jnp.where(logits == v1, lane, LANE), axis=-1, keepdims=True)
    rest = jnp.where(lane == i1, -jnp.inf, logits)
    v2 = jnp.max(rest, axis=-1, keepdims=True)
    i2 = jnp.min(jnp.where(rest == v2, lane, LANE), axis=-1, keepdims=True)
    e2 = jnp.exp(v2 - v1)
    den = 1.0 + e2
    idx_ref[...] = jnp.where(lane == 0, i1, jnp.where(lane == 1, i2, 0))
    record = jnp.zeros(logits.shape, F32)
    for k, gate in enumerate((1.0 / den, e2 / den)):
        hi = gate.astype(BF16).astype(F32)
        mid = (gate - hi).astype(BF16).astype(F32)
        for part, term in enumerate((hi, mid, gate - hi - mid)):
            record = jnp.where(lane == GATE_TERMS * k + part, term, record)
    record = jnp.where(lane == 2 * GATE_TERMS, i1.astype(F32), record)
    record = jnp.where(lane == 2 * GATE_TERMS + 1, i2.astype(F32), record)
    gate_ref[...] = record.astype(BF16)


def _router(x2, g, w_router):
    t = x2.shape[0]
    row = lambda n: pl.BlockSpec((ROW_TILE, n), lambda i: (i, 0))
    wr = _pad_lanes(w_router)
    return pl.pallas_call(
        _router_kernel,
        out_shape=(jax.ShapeDtypeStruct((t, D_MODEL), BF16), jax.ShapeDtypeStruct((t, LANE), I32),
                   jax.ShapeDtypeStruct((t, LANE), BF16)),
        grid=(t // ROW_TILE,),
        in_specs=[row(D_MODEL), pl.BlockSpec((1, D_MODEL), lambda i: (0, 0)),
                  pl.BlockSpec(wr.shape, lambda i: (0, 0))],
        out_specs=(row(D_MODEL), row(LANE), row(LANE)),
        compiler_params=_cparams(("parallel",)),
        name="moe_router",
    )(x2, g.reshape(1, -1), wr)


def _rank_kernel(idx_ref, pos_ref, posr_ref, tile_ref, slot_ref, run_ref, start_ref, end_ref, lo_ref, hi_ref):
    phase, blk = pl.program_id(0), pl.program_id(1)
    nblk = pl.num_programs(1)
    tb = RANK_TILE
    lane = lax.broadcasted_iota(I32, (tb, LANE), 1)
    idx = idx_ref[...]
    oh0 = (lane == idx[:, 0:1]).astype(F32)
    oh1 = (lane == idx[:, 1:2]).astype(F32)
    both = oh0 + oh1

    @pl.when((phase == 0) & (blk == 0))
    def _():
        run_ref[...] = jnp.zeros_like(run_ref)

    @pl.when(phase == 0)
    def _():
        run_ref[...] += jnp.sum(both, axis=0, keepdims=True)
        pos_ref[...] = jnp.zeros_like(pos_ref)
        posr_ref[...] = jnp.zeros_like(posr_ref)

    @pl.when((phase == 1) & (blk == 0))
    def _():
        counts = run_ref[...]
        tiles = jnp.ceil(counts * (1.0 / MOE_ROW_TILE))
        r = lax.broadcasted_iota(I32, (LANE, LANE), 0)
        c = lax.broadcasted_iota(I32, (LANE, LANE), 1)
        before = (r < c).astype(F32)
        tile_start = jnp.dot(tiles, before, precision=HIGHEST, preferred_element_type=F32)
        start_ref[...] = tile_start * MOE_ROW_TILE
        tile_end = tile_start + tiles
        end_ref[...] = tile_end * MOE_ROW_TILE
        n_col = lax.broadcasted_iota(I32, (LANE, LANE), 0).astype(F32)
        ended = ((n_col >= tile_end) & (c < N_EXPERTS)).astype(F32)
        expert = jnp.sum(ended, axis=-1, keepdims=True)
        total = jnp.max(tile_end, axis=-1, keepdims=True)
        col = lax.broadcasted_iota(I32, tile_ref.shape, 1)
        tile_ref[...] = jnp.where(col == 0, expert, total).astype(I32)
        run_ref[...] = jnp.zeros_like(run_ref)
        lo_ref[...] = jnp.zeros_like(lo_ref)
        hi_ref[...] = jnp.zeros_like(hi_ref)

    @pl.when(phase == 1)
    def _():
        r = lax.broadcasted_iota(I32, (tb, tb), 0)
        c = lax.broadcasted_iota(I32, (tb, tb), 1)
        strict = (c < r).astype(BF16)
        lo = run_ref[...] + start_ref[...]
        base = _dot(strict, both.astype(BF16)) + lo
        p0 = jnp.sum(oh0 * base, axis=-1, keepdims=True)
        p1 = jnp.sum(oh1 * base, axis=-1, keepdims=True)
        idf = idx.astype(F32)
        posf = jnp.where(lane == 0, p0, jnp.where(lane == 1, p1, jnp.where(
            lane == 2, idf[:, 0:1], jnp.where(lane == 3, idf[:, 1:2], 0.0))))
        pos_ref[...] = posf.astype(I32)
        pick = (lax.broadcasted_iota(I32, (8, LANE), 0) == lax.broadcasted_iota(I32, (8, LANE), 1)).astype(F32)
        posr_ref[...] = _dot_nt_highest(pick, posf).astype(I32)
        run_ref[...] += jnp.sum(both, axis=0, keepdims=True)
        mine = lax.broadcasted_iota(I32, lo_ref.shape, 0) == blk
        lo_ref[...] = jnp.where(mine, lo, lo_ref[...])
        hi_ref[...] = jnp.where(mine, run_ref[...] + start_ref[...], hi_ref[...])

    @pl.when((phase == 1) & (blk == nblk - 1))
    def _():
        lo = lo_ref[...]
        last_block = lax.broadcasted_iota(I32, lo.shape, 0) == lo.shape[0] - 1
        hi = jnp.where(last_block, end_ref[...], hi_ref[...])
        first = jnp.floor(lo * (1.0 / DISPATCH_TILE))
        last = jnp.floor((hi - 1.0) * (1.0 / DISPATCH_TILE))
        n = lo_ref.shape[0]
        for k in range(SLOTS_PER_PAIR):
            used = (hi > lo) & (first + k <= last)
            slot_ref[k * n:(k + 1) * n, :] = jnp.where(used, first + k, first).astype(I32)
            slot_ref[(SLOTS_PER_PAIR + k) * n:(SLOTS_PER_PAIR + k + 1) * n, :] = used.astype(I32)


def _rank(idx):
    t = idx.shape[0]
    nblk = t // RANK_TILE
    const = lambda shape: pl.BlockSpec(shape, lambda p, i: (0, 0))
    return pl.pallas_call(
        _rank_kernel,
        out_shape=(jax.ShapeDtypeStruct((t, LANE), I32), jax.ShapeDtypeStruct((8, t), I32),
                   jax.ShapeDtypeStruct((LANE, 2), I32),
                   jax.ShapeDtypeStruct((2 * SLOTS_PER_PAIR * nblk, LANE), I32)),
        grid=(2, nblk),
        in_specs=[pl.BlockSpec((RANK_TILE, LANE), lambda p, i: (i, 0))],
        out_specs=(pl.BlockSpec((RANK_TILE, LANE), lambda p, i: (i * p, 0)),
                   pl.BlockSpec((8, RANK_TILE), lambda p, i: (0, i * p)),
                   const((LANE, 2)), const((2 * SLOTS_PER_PAIR * nblk, LANE))),
        scratch_shapes=[pltpu.VMEM((1, LANE), F32), pltpu.VMEM((1, LANE), F32), pltpu.VMEM((1, LANE), F32),
                        pltpu.VMEM((nblk, LANE), F32), pltpu.VMEM((nblk, LANE), F32)],
        compiler_params=_cparams(("arbitrary", "arbitrary")),
        name="moe_rank",
    )(idx)


def _dispatch_kernel(tile_ref, blk_ref, valid_ref, expert_ref, hn_ref, posr_ref, gate_ref, xs_ref, gs_ref,
                     acc_ref, gacc_ref):
    del blk_ref
    p = pl.program_id(0)
    last_slot = pl.num_programs(0) - 1
    tile = tile_ref[p]

    @pl.when((p == 0) | (tile != tile_ref[jnp.maximum(p - 1, 0)]))
    def _():
        acc_ref[...] = jnp.zeros_like(acc_ref)
        gacc_ref[...] = jnp.zeros_like(gacc_ref)

    @pl.when(valid_ref[p] != 0)
    def _():
        expert = expert_ref[p]
        rows = tile * DISPATCH_TILE + lax.broadcasted_iota(I32, (DISPATCH_TILE, 1), 0)
        pair_row = jnp.where(posr_ref[2:3, :] == expert, posr_ref[0:1, :], posr_ref[1:2, :])
        sel = jnp.where(pair_row == rows, 1.0, 0.0).astype(BF16)
        acc_ref[...] += _dot(sel, hn_ref[...])
        gacc_ref[...] += _dot(sel, gate_ref[...])

    @pl.when((p == last_slot) | (tile != tile_ref[jnp.minimum(p + 1, last_slot)]))
    def _():
        xs_ref[...] = acc_ref[...].astype(BF16)
        gs_ref[...] = gacc_ref[...]


def _slot_spec(shape, which, transposed=False):
    if transposed:
        return pl.BlockSpec(shape, lambda p, tile, blk, valid, expert: (0, (tile, blk)[which][p]))
    return pl.BlockSpec(shape, lambda p, tile, blk, valid, expert: ((tile, blk)[which][p], 0))


def _dispatch(slots, hn, pos_rows, gates, rows):
    n_slots = slots[0].shape[0]
    tok = lambda n: _slot_spec((RANK_TILE, n), 1)
    return pl.pallas_call(
        _dispatch_kernel,
        out_shape=(jax.ShapeDtypeStruct((rows, D_MODEL), BF16), jax.ShapeDtypeStruct((rows, LANE), F32)),
        grid_spec=pltpu.PrefetchScalarGridSpec(
            num_scalar_prefetch=4,
            grid=(n_slots,),
            in_specs=[tok(D_MODEL), _slot_spec((8, RANK_TILE), 1, transposed=True), tok(LANE)],
            out_specs=(_slot_spec((DISPATCH_TILE, D_MODEL), 0), _slot_spec((DISPATCH_TILE, LANE), 0)),
            scratch_shapes=[pltpu.VMEM((DISPATCH_TILE, D_MODEL), F32), pltpu.VMEM((DISPATCH_TILE, LANE), F32)],
        ),
        compiler_params=_cparams(("arbitrary",)),
        name="moe_dispatch",
    )(*slots, hn, pos_rows, gates)


def _moe_kernel(expert_ref, ntiles_ref, x_ref, gs_ref, wg_ref, wu_ref, wd_ref, out_ref, acc_ref):
    i, j = pl.program_id(0), pl.program_id(1)
    active = i < ntiles_ref[0]

    @pl.when(j == 0)
    def _():
        acc_ref[...] = jnp.zeros_like(acc_ref)

    @pl.when(active)
    def _():
        _swiglu_accumulate(acc_ref, x_ref[...], wg_ref.at[0], wu_ref.at[0], wd_ref.at[0])

    @pl.when(j == pl.num_programs(1) - 1)
    def _():
        gs = gs_ref[...]
        gate = [sum(gs[:, GATE_TERMS * k + n:GATE_TERMS * k + n + 1] for n in range(GATE_TERMS)) for k in range(TOP_K)]
        first = gs[:, TOP_K * GATE_TERMS:TOP_K * GATE_TERMS + 1] == expert_ref[i].astype(F32)
        row_gate = jnp.where(first, gate[0], gate[1])
        out_ref[...] = jnp.where(active, acc_ref[...] * row_gate, 0.0).astype(BF16)


def _moe_experts(tile_expert, n_tiles, xs, gs, w_gate, w_up, w_down):
    rows = xs.shape[0]
    tm, tf = MOE_ROW_TILE, FFN_FF_TILE
    nj = D_FF // tf

    def x_map(i, j, e_ref, n_ref):
        return jnp.clip(i, 0, jnp.maximum(n_ref[0] - 1, 0)), 0

    def w_idx(i, j, expert_ref, ntiles_ref):
        e = jnp.minimum(expert_ref[i], N_EXPERTS - 1)
        return e, jnp.where(i < ntiles_ref[0], j, nj - 1)

    def w_up_map(i, j, e_ref, n_ref):
        e, jj = w_idx(i, j, e_ref, n_ref)
        return e, 0, jj

    def w_down_map(i, j, e_ref, n_ref):
        e, jj = w_idx(i, j, e_ref, n_ref)
        return e, jj, 0

    return pl.pallas_call(
        _moe_kernel,
        out_shape=jax.ShapeDtypeStruct((rows, D_MODEL), BF16),
        grid_spec=pltpu.PrefetchScalarGridSpec(
            num_scalar_prefetch=2,
            grid=(rows // tm, nj),
            in_specs=[pl.BlockSpec((tm, D_MODEL), x_map),
                      pl.BlockSpec((tm, LANE), x_map),
                      pl.BlockSpec((1, D_MODEL, tf), w_up_map),
                      pl.BlockSpec((1, D_MODEL, tf), w_up_map),
                      pl.BlockSpec((1, tf, D_MODEL), w_down_map)],
            out_specs=pl.BlockSpec((tm, D_MODEL), lambda i, j, e, n: (i, 0)),
            scratch_shapes=[pltpu.VMEM((tm, D_MODEL), F32)],
        ),
        compiler_params=_cparams(("arbitrary", "arbitrary")),
        name="moe_experts",
    )(tile_expert, n_tiles, xs, gs, w_gate, w_up, w_down)


def _combine_kernel(tile_ref, blk_ref, valid_ref, expert_ref, x_ref, pos_ref, ys_ref, out_ref):
    p = pl.program_id(0)

    @pl.when((p == 0) | (blk_ref[p] != blk_ref[jnp.maximum(p - 1, 0)]))
    def _():
        out_ref[...] = x_ref[...]

    @pl.when(valid_ref[p] != 0)
    def _():
        expert = expert_ref[p]
        rows = tile_ref[p] * DISPATCH_TILE + lax.broadcasted_iota(I32, (1, DISPATCH_TILE), 1)
        pos = pos_ref[...]
        pair_row = jnp.where(pos[:, 2:3] == expert, pos[:, 0:1], pos[:, 1:2])
        out_ref[...] += _dot(jnp.where(pair_row == rows, 1.0, 0.0).astype(BF16), ys_ref[...])


def _combine(slots, x2, pos, ys):
    n_slots = slots[0].shape[0]
    return pl.pallas_call(
        _combine_kernel,
        out_shape=jax.ShapeDtypeStruct(x2.shape, F32),
        grid_spec=pltpu.PrefetchScalarGridSpec(
            num_scalar_prefetch=4,
            grid=(n_slots,),
            in_specs=[_slot_spec((RANK_TILE, D_MODEL), 1), _slot_spec((RANK_TILE, LANE), 1),
                      _slot_spec((DISPATCH_TILE, D_MODEL), 0)],
            out_specs=_slot_spec((RANK_TILE, D_MODEL), 1),
        ),
        compiler_params=_cparams(("arbitrary",)),
        name="moe_combine",
    )(*slots, x2, pos, ys)


def _moe_ffn(x2, g, w_router, w_gate_up, w_down):
    t = x2.shape[0]
    nblk = t // RANK_TILE
    hn, idx, gates = _router(x2, g, w_router)
    pos, pos_rows, tile_info, slot_tab = _rank(idx)
    rows = t * TOP_K + N_EXPERTS * MOE_ROW_TILE
    tab = slot_tab.reshape(2, SLOTS_PER_PAIR, nblk, LANE)[..., :N_EXPERTS]
    tile, valid = tab[0].transpose(1, 2, 0), tab[1].transpose(1, 2, 0)
    blk = jnp.broadcast_to(jnp.arange(nblk, dtype=I32)[:, None, None], tile.shape)
    expert = jnp.broadcast_to(jnp.arange(N_EXPERTS, dtype=I32)[None, :, None], tile.shape)
    n_slots = nblk * N_EXPERTS + rows // DISPATCH_TILE

    def visiting_order(flatten):
        t_, b_, v_, e_ = (flatten(a) for a in (tile, blk, valid, expert))
        order = jnp.argsort(1 - v_, stable=True)[:n_slots]
        n_valid = jnp.sum(v_)
        keep = jnp.arange(n_slots) < n_valid
        pick = jnp.where(keep, order, order[jnp.maximum(n_valid - 1, 0)])
        return t_[pick], b_[pick], v_[pick] * keep.astype(I32), e_[pick]

    by_expert = visiting_order(lambda a: a.transpose(1, 0, 2).reshape(-1))
    by_block = visiting_order(lambda a: a.reshape(-1))
    xs, gs = _dispatch(by_expert, hn, pos_rows, gates, rows)
    ys = _moe_experts(tile_info[:, 0], tile_info[:1, 1], xs, gs,
                      w_gate_up[..., :D_FF].astype(BF16), w_gate_up[..., D_FF:].astype(BF16),
                      w_down.astype(BF16))
    return _combine(by_block, x2, pos, ys)


def _pad_heads(w, heads, dim, pad):
    w = w.reshape(w.shape[:-1] + (heads, dim))
    return _pad_lanes(w, pad).reshape(w.shape[:-2] + (heads * pad,))


def kernel(x, mem, positions, a_norm, a_w_in, a_gate_bias, a_head_norm, a_w_out, b_norm, b_w_in, b_q_a_norm, b_w_uq, b_q_head_norm, b_w_out, kv_norm, w_dkv, kv_a_norm, w_ukv, k_head_norm, mem_norm, mem_w_kv, mem_q_norm, mem_k_norm, ffn_norm, dense_w_gate_up, dense_w_down, moe_router, moe_w_gate_up, moe_w_down):
    nb, seq, _ = x.shape
    t = nb * seq
    x2 = x.reshape(t, D_MODEL)
    gmat = jnp.kron(jnp.eye(M_HEADS, dtype=F32), jnp.full((M_HEAD_DIM, M_HEAD_DIM), 1.0 / M_HEAD_DIM, F32))

    kbd0, vbd0 = _memory_kv(mem, mem_norm[0], mem_w_kv[0], mem_k_norm[0], gmat)
    w_in = a_w_in[0]
    qk_w, v_w = A_HEADS * A_QK_DIM, A_HEADS * A_V_DIM
    o0, o1, o2, o3, o4 = qk_w, 2 * qk_w, 2 * qk_w + v_w, 2 * qk_w + 2 * v_w, 2 * qk_w + 2 * v_w + 2 * A_HEADS
    w_main = jnp.concatenate([
        _pad_heads(w_in[:, :o0], A_HEADS, A_QK_DIM, A_QK_PAD),
        _pad_heads(w_in[:, o0:o1], A_HEADS, A_QK_DIM, A_QK_PAD),
        _pad_heads(w_in[:, o1:o2], A_HEADS, A_V_DIM, A_V_PAD),
        _pad_heads(w_in[:, o2:o3], A_HEADS, A_V_DIM, A_V_PAD),
        w_in[:, o4:]], axis=1).astype(BF16)
    q, k, v, o, mq, gc, gr = _a_projection(x2, a_norm[0], w_main, w_in[:, o3:o4], a_gate_bias[0])
    gc, gr = _mlstm_gates(gc, gr)
    gcol = gc.reshape(nb, seq, 2, A_HEADS).transpose(0, 3, 1, 2)
    grow = gr.reshape(2, A_HEADS, nb, seq).transpose(2, 1, 0, 3)
    three = lambda a: a.reshape(nb, seq, a.shape[-1])
    hm = _mlstm(three(q), three(k), three(v), three(o), gcol, grow,
                _pad_heads(a_head_norm[0].reshape(1, -1), A_HEADS, A_V_DIM, A_V_PAD))
    w_out = a_w_out[0]
    w_out_h = jnp.pad(w_out[:v_w].reshape(A_HEADS, A_V_DIM, D_MODEL), ((0, 0), (0, A_V_PAD - A_V_DIM), (0, 0)))
    w_out_h = w_out_h.reshape(A_HEADS * A_V_PAD, D_MODEL).astype(BF16)
    x2 = _mix_out(x2, hm.reshape(t, -1), mq, kbd0, vbd0, gmat, mem_q_norm[0],
                  w_out_h, w_out[v_w:].astype(BF16), seq)
    wgu = dense_w_gate_up[0]
    x2 = _dense_ffn(x2, ffn_norm[0], wgu[:, :D_FF].astype(BF16), wgu[:, D_FF:].astype(BF16),
                    dense_w_down[0].astype(BF16))

    cs, sn = _rope_tables(positions)
    k_sh, vt_sh = _latent_kv(x2, kv_norm, w_dkv, kv_a_norm, w_ukv, k_head_norm, cs, sn)

    kbd1, vbd1 = _memory_kv(mem, mem_norm[1], mem_w_kv[1], mem_k_norm[1], gmat)
    qh, mq1 = _b_projection(x2, b_norm[0], b_w_in[0], b_q_a_norm[0], b_w_uq[0], b_q_head_norm[0], cs, sn)
    att = _causal_attention(three(qh), three(k_sh), vt_sh, seq)
    w_out = b_w_out[0]
    n_att = B_HEADS * V_HEAD
    x2 = _mix_out(x2, att.reshape(t, -1), mq1, kbd1, vbd1, gmat, mem_q_norm[1],
                  w_out[:n_att].astype(BF16), w_out[n_att:].astype(BF16), seq)
    x2 = _moe_ffn(x2, ffn_norm[1], moe_router[0], moe_w_gate_up[0], moe_w_down[0])
    return x2.reshape(nb, seq, D_MODEL)
```

```python
import functools

import jax
import jax.numpy as jnp
from jax import lax
from jax.experimental import pallas as pl
from jax.experimental.pallas import tpu as pltpu

F32 = jnp.float32
BF16 = jnp.bfloat16
I32 = jnp.int32

EPS = 1e-6
LANE = 128
VMEM_LIMIT = 48 * 1024 * 1024

D_MODEL = 1024
N_MEM = 256
M_HEADS, M_HEAD_DIM = 4, 64
M_W = M_HEADS * M_HEAD_DIM
A_HEADS, A_QK_DIM, A_V_DIM = 4, 96, 192
A_QK_PAD, A_V_PAD = 128, 256
B_HEADS, Q_LORA, KV_LORA = 6, 384, 256
QK_NOPE, QK_ROPE, V_HEAD = 128, 64, 128
B_QK_HEAD = QK_NOPE + QK_ROPE
B_QK_PAD = 256
VT_HEAD_ROWS = V_HEAD + 16
ROPE_THETA = 10000.0
D_FF = 3584
N_EXPERTS, TOP_K = 8, 2
GATE_TERMS = 3

MLSTM_CHUNK = 256
ROW_TILE = 512
FFN_ROW_TILE = 1024
FFN_FF_TILE = 1792
FFN_CHUNK = 256
MOE_ROW_TILE = 512
ATT_Q_TILE = 512
ATT_K_TILE = 512
ATT_HEADS_PER_STEP = 6
RANK_TILE = 512
RUN_ALIGN = 16
RUN_PIECES = tuple(RANK_TILE >> s for s in range((RANK_TILE // RUN_ALIGN).bit_length()))
LOCAL_ROWS = TOP_K * RANK_TILE + N_EXPERTS * RUN_ALIGN
XS_WIDTH = D_MODEL + LANE

HIGHEST = lax.Precision.HIGHEST


def _cparams(sem):
    return pltpu.CompilerParams(dimension_semantics=sem, vmem_limit_bytes=VMEM_LIMIT)


def _rms(x, g):
    return x * lax.rsqrt(jnp.mean(x * x, axis=-1, keepdims=True) + EPS) * g


def _dot(a, b):
    return jnp.dot(a, b, preferred_element_type=F32)


def _dot_nt(a, b):
    return lax.dot_general(a, b, (((1,), (1,)), ((), ())), preferred_element_type=F32)


def _dot_tn(a, b):
    return lax.dot_general(a, b, (((0,), (0,)), ((), ())), preferred_element_type=F32)


def _group_mean_sq(x, gmat):
    return jnp.dot(x * x, gmat, precision=HIGHEST, preferred_element_type=F32)


def _memkv_kernel(mem_ref, g_ref, w_ref, kg_ref, gmat_ref, kbd_ref, vbd_ref):
    hn = _rms(mem_ref[0], g_ref[...]).astype(BF16)
    kv = _dot(hn, w_ref[...])
    k, v = kv[:, :M_W], kv[:, M_W:]
    kn = k * lax.rsqrt(_group_mean_sq(k, gmat_ref[...]) + EPS) * kg_ref[...]
    lane_head = lax.broadcasted_iota(I32, (1, M_W), 1) // M_HEAD_DIM
    for h in range(M_HEADS):
        keep = lane_head == h
        kbd_ref[0, h * N_MEM:(h + 1) * N_MEM, :] = jnp.where(keep, kn, 0.0).astype(BF16)
        vbd_ref[0, h * N_MEM:(h + 1) * N_MEM, :] = jnp.where(keep, v, 0.0).astype(BF16)


def _memory_kv(mem, g, w_kv, k_g, gmat):
    nb = mem.shape[0]
    out = jax.ShapeDtypeStruct((nb, M_HEADS * N_MEM, M_W), BF16)
    return pl.pallas_call(
        _memkv_kernel,
        out_shape=(out, out),
        grid=(nb,),
        in_specs=[
            pl.BlockSpec((1, N_MEM, D_MODEL), lambda b: (b, 0, 0)),
            pl.BlockSpec((1, D_MODEL), lambda b: (0, 0)),
            pl.BlockSpec((D_MODEL, 2 * M_W), lambda b: (0, 0)),
            pl.BlockSpec((1, M_W), lambda b: (0, 0)),
            pl.BlockSpec((M_W, M_W), lambda b: (0, 0)),
        ],
        out_specs=(pl.BlockSpec((1, M_HEADS * N_MEM, M_W), lambda b: (b, 0, 0)),) * 2,
        compiler_params=_cparams(("parallel",)),
        name="memory_kv",
    )(mem, g.reshape(1, -1), w_kv.astype(BF16), jnp.tile(k_g, M_HEADS).reshape(1, -1), gmat)


def _memory_attention(mq, kbd, vbd, gmat, qg):
    qn = mq * lax.rsqrt(_group_mean_sq(mq, gmat) + EPS) * (qg * (M_HEAD_DIM ** -0.5))
    s = _dot_nt(qn.astype(BF16), kbd)
    ps = []
    for h in range(M_HEADS):
        sh = s[:, h * N_MEM:(h + 1) * N_MEM]
        e = jnp.exp(sh - jnp.max(sh, axis=-1, keepdims=True))
        ps.append((e / jnp.sum(e, axis=-1, keepdims=True)).astype(BF16))
    return _dot(jnp.concatenate(ps, axis=-1), vbd)


def _a_proj_kernel(x_ref, g_ref, w_ref, wif_ref, wift_ref, bc_ref, br_ref,
                   q_ref, k_ref, v_ref, o_ref, mq_ref, gc_ref, gr_ref):
    hn = _rms(x_ref[...], g_ref[...]).astype(BF16)
    nq = A_HEADS * A_QK_PAD
    nv = A_HEADS * A_V_PAD
    q_ref[...] = _dot(hn, w_ref[:, :nq]).astype(BF16)
    k_ref[...] = (_dot(hn, w_ref[:, nq:2 * nq]) * (A_QK_DIM ** -0.5)).astype(BF16)
    v = _dot(hn, w_ref[:, 2 * nq:2 * nq + nv])
    ones_lane = lax.broadcasted_iota(I32, (1, nv), 1) % A_V_PAD == A_V_DIM
    v_ref[...] = jnp.where(ones_lane, 1.0, v).astype(BF16)
    o_ref[...] = _dot(hn, w_ref[:, 2 * nq + nv:2 * nq + 2 * nv])
    mq_ref[...] = _dot(hn, w_ref[:, 2 * nq + 2 * nv:])
    gc_ref[...] = _dot(hn, wif_ref[...])[:, :2 * A_HEADS] + bc_ref[...]
    gr_ref[...] = _dot_nt(wift_ref[...], hn) + br_ref[...]


def _a_projection(x2, g, w_main, w_if, gate_bias):
    t = x2.shape[0]
    nq, nv = A_HEADS * A_QK_PAD, A_HEADS * A_V_PAD
    ng = 2 * A_HEADS
    wif_pad = jnp.pad(w_if, ((0, 0), (0, LANE - ng))).astype(BF16)
    row = lambda n: pl.BlockSpec((ROW_TILE, n), lambda i: (i, 0))
    full = lambda a: pl.BlockSpec(a.shape, lambda i: (0,) * a.ndim)
    args = (x2, g.reshape(1, -1), w_main, wif_pad, w_if.T.astype(BF16),
            gate_bias.reshape(1, ng), gate_bias.reshape(ng, 1))
    return pl.pallas_call(
        _a_proj_kernel,
        out_shape=(jax.ShapeDtypeStruct((t, nq), BF16), jax.ShapeDtypeStruct((t, nq), BF16),
                   jax.ShapeDtypeStruct((t, nv), BF16), jax.ShapeDtypeStruct((t, nv), F32),
                   jax.ShapeDtypeStruct((t, M_W), F32), jax.ShapeDtypeStruct((t, ng), F32),
                   jax.ShapeDtypeStruct((ng, t), F32)),
        grid=(t // ROW_TILE,),
        in_specs=[row(D_MODEL)] + [full(a) for a in args[1:]],
        out_specs=(row(nq), row(nq), row(nv), row(nv), row(M_W), row(ng),
                   pl.BlockSpec((ng, ROW_TILE), lambda i: (0, i))),
        compiler_params=_cparams(("parallel",)),
        name="mlstm_in_proj",
    )(*args)


def _log_sigmoid(f):
    return jnp.minimum(f, 0.0) - jnp.log(1.0 + jnp.exp(-jnp.abs(f)))


def _gates_kernel(gc_ref, gr_ref, oc_ref, or_ref):
    L = MLSTM_CHUNK
    gc = gc_ref[...]
    gr = gr_ref[...]
    r = lax.broadcasted_iota(I32, (L, L), 0)
    c = lax.broadcasted_iota(I32, (L, L), 1)
    lower = (c <= r).astype(F32)
    is_f_col = lax.broadcasted_iota(I32, gc.shape, 1) >= A_HEADS
    is_f_row = lax.broadcasted_iota(I32, gr.shape, 0) >= A_HEADS
    lf_c = jnp.where(is_f_col, _log_sigmoid(gc), 0.0)
    lf_r = jnp.where(is_f_row, _log_sigmoid(gr), 0.0)
    lf_c = jnp.concatenate([lf_c, jnp.zeros((L, LANE - gc.shape[1]), F32)], axis=1)
    cum_c = jnp.dot(lower, lf_c, precision=HIGHEST, preferred_element_type=F32)[:, :gc.shape[1]]
    cum_r = _dot_nt_highest(lf_r, lower)
    oc_ref[...] = jnp.where(is_f_col, cum_c, gc)
    or_ref[...] = jnp.where(is_f_row, cum_r, gr)


def _dot_nt_highest(a, b):
    return lax.dot_general(a, b, (((1,), (1,)), ((), ())), precision=HIGHEST,
                           preferred_element_type=F32)


def _mlstm_gates(gc, gr):
    t, ng = gc.shape
    L = MLSTM_CHUNK
    return pl.pallas_call(
        _gates_kernel,
        out_shape=(jax.ShapeDtypeStruct((t, ng), F32), jax.ShapeDtypeStruct((ng, t), F32)),
        grid=(t // L,),
        in_specs=[pl.BlockSpec((L, ng), lambda i: (i, 0)), pl.BlockSpec((ng, L), lambda i: (0, i))],
        out_specs=(pl.BlockSpec((L, ng), lambda i: (i, 0)), pl.BlockSpec((ng, L), lambda i: (0, i))),
        compiler_params=_cparams(("parallel",)),
        name="mlstm_gates",
    )(gc, gr)


def _mlstm_kernel(q_ref, k_ref, v_ref, o_ref, gc_ref, gr_ref, hg_ref, out_ref, c_ref, m_ref):
    L = MLSTM_CHUNK

    @pl.when(pl.program_id(1) == 0)
    def _():
        c_ref[...] = jnp.zeros_like(c_ref)
        m_ref[...] = jnp.zeros_like(m_ref)

    t_idx = lax.broadcasted_iota(I32, (L, L), 0)
    s_idx = lax.broadcasted_iota(I32, (L, L), 1)
    real = lax.broadcasted_iota(I32, (1, A_V_PAD), 1) < A_V_DIM
    for hd in range(A_HEADS):
        qk = slice(hd * A_QK_PAD, (hd + 1) * A_QK_PAD)
        vv = slice(hd * A_V_PAD, (hd + 1) * A_V_PAD)
        q, k, v = q_ref[0, :, qk], k_ref[0, :, qk], v_ref[0, :, vv]
        gcol = gc_ref[0, hd]
        grow = gr_ref[0, hd]
        i_c, g_c = gcol[:, 0:1], gcol[:, 1:2]
        i_r, g_r = grow[0:1, :], grow[1:2, :]
        g_last = g_r[:, L - 1:L]
        m_prev = m_ref[hd, 0:1, 0:1]
        c_prev = c_ref[hd]

        a_log = g_c + m_prev
        d_log = jnp.where(s_idx <= t_idx, g_c - g_r + i_r, -jnp.inf)
        m_t = jnp.maximum(a_log, jnp.max(d_log, axis=-1, keepdims=True))
        inter = jnp.exp(a_log - m_t)
        p = (jnp.exp(d_log - m_t) * _dot_nt(q, k)).astype(BF16)
        num = inter * _dot(q, c_prev.astype(BF16)) + _dot(p, v)
        den = num[:, A_V_DIM:A_V_DIM + 1]
        h = num / jnp.maximum(jnp.abs(den), jnp.exp(-m_t))
        h = jnp.where(real, h, 0.0)
        hn = h * lax.rsqrt(jnp.sum(h * h, axis=-1, keepdims=True) * (1.0 / A_V_DIM) + EPS) * hg_ref[:, vv]
        out_ref[0, :, vv] = (hn * jax.nn.sigmoid(o_ref[0, :, vv])).astype(BF16)

        w_r = g_last - g_r + i_r
        m_new = jnp.maximum(g_last + m_prev, jnp.max(w_r, axis=-1, keepdims=True))
        e_c = jnp.exp(g_last - g_c + i_c - m_new)
        ev = (e_c * v.astype(F32)).astype(BF16)
        c_ref[hd] = jnp.exp(g_last + m_prev - m_new) * c_prev + _dot_tn(k, ev)
        m_ref[hd] = jnp.broadcast_to(m_new, m_ref.shape[1:])


def _mlstm(q, k, v, o, gcol, grow, head_g):
    nb, s, _ = q.shape
    L = MLSTM_CHUNK
    blk = lambda w: pl.BlockSpec((1, L, w), lambda b, c: (b, c, 0))
    return pl.pallas_call(
        _mlstm_kernel,
        out_shape=jax.ShapeDtypeStruct((nb, s, A_HEADS * A_V_PAD), BF16),
        grid=(nb, s // L),
        in_specs=[blk(A_HEADS * A_QK_PAD), blk(A_HEADS * A_QK_PAD), blk(A_HEADS * A_V_PAD), blk(A_HEADS * A_V_PAD),
                  pl.BlockSpec((1, A_HEADS, L, 2), lambda b, c: (b, 0, c, 0)),
                  pl.BlockSpec((1, A_HEADS, 2, L), lambda b, c: (b, 0, 0, c)),
                  pl.BlockSpec((1, A_HEADS * A_V_PAD), lambda b, c: (0, 0))],
        out_specs=blk(A_HEADS * A_V_PAD),
        scratch_shapes=[pltpu.VMEM((A_HEADS, A_QK_PAD, A_V_PAD), F32), pltpu.VMEM((A_HEADS, 8, LANE), F32)],
        compiler_params=_cparams(("parallel", "arbitrary")),
        name="mlstm_chunkwise",
    )(q, k, v, o, gcol, grow, head_g)


def _mix_out_kernel(x_ref, h_ref, mq_ref, kbd_ref, vbd_ref, gmat_ref, qg_ref, w1_ref, w2_ref, out_ref):
    mo = _memory_attention(mq_ref[...], kbd_ref[0], vbd_ref[0], gmat_ref[...], qg_ref[...])
    out_ref[...] = x_ref[...] + _dot(h_ref[...], w1_ref[...]) + _dot(mo.astype(BF16), w2_ref[...])


def _mix_out(x2, h2, mq, kbd, vbd, gmat, qg, w_main, w_mem, seq):
    t = x2.shape[0]
    tm = ROW_TILE
    row = lambda n: pl.BlockSpec((tm, n), lambda i: (i, 0))
    full = lambda a: pl.BlockSpec(a.shape, lambda i: (0,) * a.ndim)
    per_batch = pl.BlockSpec((1,) + kbd.shape[1:], lambda i: ((i * tm) // seq, 0, 0))
    qg_t = jnp.tile(qg, M_HEADS).reshape(1, -1)
    return pl.pallas_call(
        _mix_out_kernel,
        out_shape=jax.ShapeDtypeStruct((t, D_MODEL), F32),
        grid=(t // tm,),
        in_specs=[row(D_MODEL), row(h2.shape[1]), row(M_W), per_batch, per_batch,
                  full(gmat), full(qg_t), full(w_main), full(w_mem)],
        out_specs=row(D_MODEL),
        compiler_params=_cparams(("parallel",)),
        name="mixer_out_proj",
    )(x2, h2, mq, kbd, vbd, gmat, qg_t, w_main, w_mem)


def _swiglu_accumulate(acc_ref, x, wg_ref, wu_ref, wd_ref):
    for c in range(wg_ref.shape[-1] // FFN_CHUNK):
        cols = slice(c * FFN_CHUNK, (c + 1) * FFN_CHUNK)
        gate = _dot(x, wg_ref[:, cols])
        up = _dot(x, wu_ref[:, cols])
        acc_ref[...] += _dot((jax.nn.silu(gate) * up).astype(BF16), wd_ref[cols, :])


def _ffn_kernel(x_ref, g_ref, wg_ref, wu_ref, wd_ref, out_ref, hn_ref, acc_ref):
    j = pl.program_id(1)

    @pl.when(j == 0)
    def _():
        hn_ref[...] = _rms(x_ref[...], g_ref[...]).astype(BF16)
        acc_ref[...] = jnp.zeros_like(acc_ref)

    _swiglu_accumulate(acc_ref, hn_ref[...], wg_ref, wu_ref, wd_ref)

    @pl.when(j == pl.num_programs(1) - 1)
    def _():
        out_ref[...] = x_ref[...] + acc_ref[...]


def _dense_ffn(x2, g, w_gate, w_up, w_down):
    t = x2.shape[0]
    tm, tf = FFN_ROW_TILE, FFN_FF_TILE
    return pl.pallas_call(
        _ffn_kernel,
        out_shape=jax.ShapeDtypeStruct((t, D_MODEL), F32),
        grid=(t // tm, D_FF // tf),
        in_specs=[pl.BlockSpec((tm, D_MODEL), lambda i, j: (i, 0)),
                  pl.BlockSpec((1, D_MODEL), lambda i, j: (0, 0)),
                  pl.BlockSpec((D_MODEL, tf), lambda i, j: (0, j)),
                  pl.BlockSpec((D_MODEL, tf), lambda i, j: (0, j)),
                  pl.BlockSpec((tf, D_MODEL), lambda i, j: (j, 0))],
        out_specs=pl.BlockSpec((tm, D_MODEL), lambda i, j: (i, 0)),
        scratch_shapes=[pltpu.VMEM((tm, D_MODEL), BF16), pltpu.VMEM((tm, D_MODEL), F32)],
        compiler_params=_cparams(("parallel", "arbitrary")),
        name="dense_swiglu",
    )(x2, g.reshape(1, -1), w_gate, w_up, w_down)


def _rope_kernel(pos_ref, inv_ref, sign_ref, cs_ref, sn_ref):
    ang = pos_ref[...].astype(F32) * inv_ref[...]
    cs_ref[...] = jnp.cos(ang)
    sn_ref[...] = jnp.sin(ang) * sign_ref[...]


def _rope_tables(positions):
    t = positions.size
    half = QK_ROPE // 2
    inv = 1.0 / (ROPE_THETA ** (jnp.arange(0, QK_ROPE, 2, dtype=F32) / QK_ROPE))
    pad = jnp.zeros((LANE - QK_ROPE,), F32)
    inv_l = jnp.concatenate([inv, inv, pad]).reshape(1, LANE)
    sign = jnp.concatenate([-jnp.ones((half,), F32), jnp.ones((half,), F32), pad]).reshape(1, LANE)
    out = jax.ShapeDtypeStruct((t, LANE), F32)
    return pl.pallas_call(
        _rope_kernel,
        out_shape=(out, out),
        grid=(t // ROW_TILE,),
        in_specs=[pl.BlockSpec((ROW_TILE, 1), lambda i: (i, 0)),
                  pl.BlockSpec((1, LANE), lambda i: (0, 0)),
                  pl.BlockSpec((1, LANE), lambda i: (0, 0))],
        out_specs=(pl.BlockSpec((ROW_TILE, LANE), lambda i: (i, 0)),) * 2,
        compiler_params=_cparams(("parallel",)),
        name="rope_tables",
    )(positions.reshape(t, 1), inv_l, sign)


def _head_qk_norm_rope(nope, rope, rope_sw, g_nope, g_rope, g_rope_sw, cs, sn, scale):
    ss = jnp.sum(nope * nope, axis=-1, keepdims=True) + jnp.sum(rope * rope, axis=-1, keepdims=True)
    r = lax.rsqrt(ss * (1.0 / B_QK_HEAD) + EPS) * scale
    return nope * r * g_nope, (rope * g_rope * cs + rope_sw * g_rope_sw * sn) * r


def _latent_kv_kernel(x_ref, g_ref, wd_ref, ga_ref, wuk_ref, wuvt_ref, kg_ref, cs_ref, sn_ref, k_ref, vt_ref):
    hn = _rms(x_ref[...], g_ref[...]).astype(BF16)
    z = _dot(hn, wd_ref[...])
    c_kv = z[:, :KV_LORA]
    rope, rope_sw = z[:, KV_LORA:KV_LORA + LANE], z[:, KV_LORA + LANE:]
    cn = _rms(c_kv, ga_ref[...]).astype(BF16)
    kv = _dot(cn, wuk_ref[...])
    vt = _dot_nt(wuvt_ref[...], cn)
    tm = vt.shape[1]
    ones_row = (lax.broadcasted_iota(I32, (VT_HEAD_ROWS - V_HEAD, tm), 0) == 0).astype(BF16)
    for h in range(B_HEADS):
        vt_ref[h * VT_HEAD_ROWS:h * VT_HEAD_ROWS + V_HEAD, :] = vt[h * V_HEAD:(h + 1) * V_HEAD].astype(BF16)
        vt_ref[h * VT_HEAD_ROWS + V_HEAD:(h + 1) * VT_HEAD_ROWS, :] = ones_row
    kg = kg_ref[...]
    for h in range(B_HEADS):
        kn, kr = _head_qk_norm_rope(kv[:, h * QK_NOPE:(h + 1) * QK_NOPE], rope, rope_sw,
                                    kg[:, :LANE], kg[:, LANE:2 * LANE], kg[:, 2 * LANE:],
                                    cs_ref[...], sn_ref[...], 1.0)
        k_ref[:, h * B_QK_PAD:h * B_QK_PAD + QK_NOPE] = kn.astype(BF16)
        k_ref[:, h * B_QK_PAD + QK_NOPE:(h + 1) * B_QK_PAD] = kr.astype(BF16)


def _rope_swap(w):
    half = QK_ROPE // 2
    return jnp.concatenate([w[..., half:], w[..., :half]], axis=-1)


def _pad_lanes(w, n=LANE):
    return jnp.pad(w, [(0, 0)] * (w.ndim - 1) + [(0, n - w.shape[-1])])


def _head_gain(g):
    g_rope = g[QK_NOPE:]
    return jnp.concatenate([g[:QK_NOPE], _pad_lanes(g_rope), _pad_lanes(_rope_swap(g_rope))]).reshape(1, -1)


def _latent_kv(x2, kv_norm, w_dkv, kv_a_norm, w_ukv, k_head_norm, cs, sn):
    t = x2.shape[0]
    w_rope = w_dkv[:, KV_LORA:]
    wd = jnp.concatenate([w_dkv[:, :KV_LORA], _pad_lanes(w_rope), _pad_lanes(_rope_swap(w_rope))],
                         axis=1).astype(BF16)
    wu = w_ukv.reshape(KV_LORA, B_HEADS, QK_NOPE + V_HEAD)
    wuk = wu[:, :, :QK_NOPE].reshape(KV_LORA, -1).astype(BF16)
    wuvt = wu[:, :, QK_NOPE:].reshape(KV_LORA, -1).T.astype(BF16)
    args = (x2, kv_norm.reshape(1, -1), wd, kv_a_norm.reshape(1, -1), wuk, wuvt, _head_gain(k_head_norm), cs, sn)
    row = lambda n: pl.BlockSpec((ROW_TILE, n), lambda i: (i, 0))
    full = lambda a: pl.BlockSpec(a.shape, lambda i: (0,) * a.ndim)
    return pl.pallas_call(
        _latent_kv_kernel,
        out_shape=(jax.ShapeDtypeStruct((t, B_HEADS * B_QK_PAD), BF16),
                   jax.ShapeDtypeStruct((B_HEADS * VT_HEAD_ROWS, t), BF16)),
        grid=(t // ROW_TILE,),
        in_specs=[row(D_MODEL)] + [full(a) for a in args[1:7]] + [row(LANE), row(LANE)],
        out_specs=(row(B_HEADS * B_QK_PAD), pl.BlockSpec((B_HEADS * VT_HEAD_ROWS, ROW_TILE), lambda i: (0, i))),
        compiler_params=_cparams(("parallel",)),
        name="latent_kv",
    )(*args)


def _b_proj_kernel(x_ref, g_ref, win_ref, ga_ref, wuq_ref, qg_ref, cs_ref, sn_ref, q_ref, mq_ref):
    hn = _rms(x_ref[...], g_ref[...]).astype(BF16)
    proj = _dot(hn, win_ref[...])
    mq_ref[...] = proj[:, Q_LORA:]
    qall = _dot(_rms(proj[:, :Q_LORA], ga_ref[...]).astype(BF16), wuq_ref[...])
    qg = qg_ref[...]
    per_head = QK_NOPE + 2 * LANE
    for h in range(B_HEADS):
        base = h * per_head
        qn, qr = _head_qk_norm_rope(qall[:, base:base + QK_NOPE],
                                    qall[:, base + QK_NOPE:base + QK_NOPE + LANE],
                                    qall[:, base + QK_NOPE + LANE:base + per_head],
                                    qg[:, :LANE], qg[:, LANE:2 * LANE], qg[:, 2 * LANE:],
                                    cs_ref[...], sn_ref[...], B_QK_HEAD ** -0.5)
        q_ref[:, h * B_QK_PAD:h * B_QK_PAD + QK_NOPE] = qn.astype(BF16)
        q_ref[:, h * B_QK_PAD + QK_NOPE:(h + 1) * B_QK_PAD] = qr.astype(BF16)


def _b_projection(x2, g, w_in, q_a_g, w_uq, q_head_g, cs, sn):
    t = x2.shape[0]
    wq = w_uq.reshape(Q_LORA, B_HEADS, B_QK_HEAD)
    w_rope = wq[:, :, QK_NOPE:]
    wq = jnp.concatenate([wq[:, :, :QK_NOPE], _pad_lanes(w_rope), _pad_lanes(_rope_swap(w_rope))],
                         axis=-1).reshape(Q_LORA, -1).astype(BF16)
    args = (x2, g.reshape(1, -1), w_in.astype(BF16), q_a_g.reshape(1, -1), wq, _head_gain(q_head_g), cs, sn)
    row = lambda n: pl.BlockSpec((ROW_TILE, n), lambda i: (i, 0))
    full = lambda a: pl.BlockSpec(a.shape, lambda i: (0,) * a.ndim)
    return pl.pallas_call(
        _b_proj_kernel,
        out_shape=(jax.ShapeDtypeStruct((t, B_HEADS * B_QK_PAD), BF16),
                   jax.ShapeDtypeStruct((t, M_W), F32)),
        grid=(t // ROW_TILE,),
        in_specs=[row(D_MODEL)] + [full(a) for a in args[1:6]] + [row(LANE), row(LANE)],
        out_specs=(row(B_HEADS * B_QK_PAD), row(M_W)),
        compiler_params=_cparams(("parallel",)),
        name="mla_q_proj",
    )(*args)


def _attn_kernel(q_ref, k_ref, vt_ref, out_ref):
    tq, tk = ATT_Q_TILE, ATT_K_TILE
    i = pl.program_id(2)

    def block(h, j, carry, masked):
        m, acc = carry
        off = pl.multiple_of(j * tk, tk)
        q = q_ref[0, :, h * B_QK_PAD:(h + 1) * B_QK_PAD]
        st = _dot_nt(k_ref[0, pl.ds(off, tk), h * B_QK_PAD:(h + 1) * B_QK_PAD], q)
        if masked:
            key = lax.broadcasted_iota(I32, (tk, tq), 0)
            qry = lax.broadcasted_iota(I32, (tk, tq), 1)
            st = jnp.where(key <= qry, st, -jnp.inf)
        m_new = jnp.maximum(m, jnp.max(st, axis=0, keepdims=True))
        p = jnp.exp(st - m_new).astype(BF16)
        vt = vt_ref[h * VT_HEAD_ROWS:(h + 1) * VT_HEAD_ROWS, pl.ds(off, tk)]
        return m_new, jnp.exp(m - m_new) * acc + _dot(vt, p)

    heads = range(ATT_HEADS_PER_STEP)
    init = tuple((jnp.full((1, tq), -jnp.inf, F32), jnp.zeros((VT_HEAD_ROWS, tq), F32)) for _ in heads)
    carry = lax.fori_loop(0, i, lambda j, c: tuple(block(h, j, c[h], False) for h in heads), init)
    for h in heads:
        _, acc = block(h, i, carry[h], True)
        out_t = acc[:V_HEAD] / acc[V_HEAD:V_HEAD + 1]
        out_ref[0, :, h * V_HEAD:(h + 1) * V_HEAD] = out_t.T.astype(BF16)


def _causal_attention(q, k, vt, seq):
    nb = q.shape[0]
    g = ATT_HEADS_PER_STEP
    assert ATT_Q_TILE == ATT_K_TILE and B_HEADS % g == 0
    return pl.pallas_call(
        _attn_kernel,
        out_shape=jax.ShapeDtypeStruct((nb, seq, B_HEADS * V_HEAD), BF16),
        grid=(nb, B_HEADS // g, seq // ATT_Q_TILE),
        in_specs=[pl.BlockSpec((1, ATT_Q_TILE, g * B_QK_PAD), lambda b, h, i: (b, i, h)),
                  pl.BlockSpec((1, seq, g * B_QK_PAD), lambda b, h, i: (b, 0, h)),
                  pl.BlockSpec((g * VT_HEAD_ROWS, seq), lambda b, h, i: (h, b))],
        out_specs=pl.BlockSpec((1, ATT_Q_TILE, g * V_HEAD), lambda b, h, i: (b, i, h)),
        compiler_params=_cparams(("parallel", "parallel", "arbitrary")),
        name="causal_attention",
    )(q, k, vt)


def _router_kernel(x_ref, g_ref, wr_ref, hn_ref, idx_ref, gate_ref):
    hn = _rms(x_ref[...], g_ref[...])
    hn_ref[...] = hn.astype(BF16)
    logits = jnp.dot(hn, wr_ref[...], precision=HIGHEST, preferred_element_type=F32)
    lane = lax.broadcasted_iota(I32, logits.shape, 1)
    logits = jnp.where(lane < N_EXPERTS, logits, -jnp.inf)
    v1 = jnp.max(logits, axis=-1, keepdims=True)
    i1 = jnp.min(jnp.where(logits == v1, lane, LANE), axis=-1, keepdims=True)
    rest = jnp.where(lane == i1, -jnp.inf, logits)
    v2 = jnp.max(rest, axis=-1, keepdims=True)
    i2 = jnp.min(jnp.where(rest == v2, lane, LANE), axis=-1, keepdims=True)
    e2 = jnp.exp(v2 - v1)
    den = 1.0 + e2
    idx_ref[...] = jnp.where(lane == 0, i1, jnp.where(lane == 1, i2, 0))
    record = jnp.zeros(logits.shape, F32)
    for k, gate in enumerate((1.0 / den, e2 / den)):
        hi = gate.astype(BF16).astype(F32)
        mid = (gate - hi).astype(BF16).astype(F32)
        for part, term in enumerate((hi, mid, gate - hi - mid)):
            record = jnp.where(lane == GATE_TERMS * k + part, term, record)
    record = jnp.where(lane == 2 * GATE_TERMS, i1.astype(F32), record)
    record = jnp.where(lane == 2 * GATE_TERMS + 1, i2.astype(F32), record)
    gate_ref[...] = record.astype(BF16)


def _router(x2, g, w_router):
    t = x2.shape[0]
    row = lambda n: pl.BlockSpec((ROW_TILE, n), lambda i: (i, 0))
    wr = _pad_lanes(w_router)
    return pl.pallas_call(
        _router_kernel,
        out_shape=(jax.ShapeDtypeStruct((t, D_MODEL), BF16), jax.ShapeDtypeStruct((t, LANE), I32),
                   jax.ShapeDtypeStruct((t, LANE), BF16)),
        grid=(t // ROW_TILE,),
        in_specs=[row(D_MODEL), pl.BlockSpec((1, D_MODEL), lambda i: (0, 0)),
                  pl.BlockSpec(wr.shape, lambda i: (0, 0))],
        out_specs=(row(D_MODEL), row(LANE), row(LANE)),
        compiler_params=_cparams(("parallel",)),
        name="moe_router",
    )(x2, g.reshape(1, -1), wr)


def _rank_kernel(idx_ref, lpos_ref, lposr_ref, tile_ref, runs_ref, run_ref, start_ref):
    phase, blk = pl.program_id(0), pl.program_id(1)
    tb = RANK_TILE
    lane = lax.broadcasted_iota(I32, (tb, LANE), 1)
    idx = idx_ref[...]
    oh0 = (lane == idx[:, 0:1]).astype(F32)
    oh1 = (lane == idx[:, 1:2]).astype(F32)
    both = oh0 + oh1
    run_len = jnp.ceil(jnp.sum(both, axis=0, keepdims=True) * (1.0 / RUN_ALIGN)) * RUN_ALIGN
    r = lax.broadcasted_iota(I32, (LANE, LANE), 0)
    c = lax.broadcasted_iota(I32, (LANE, LANE), 1)
    before = (r < c).astype(F32)

    @pl.when((phase == 0) & (blk == 0))
    def _():
        run_ref[...] = jnp.zeros_like(run_ref)

    @pl.when(phase == 0)
    def _():
        run_ref[...] += run_len
        lpos_ref[...] = jnp.zeros_like(lpos_ref)
        lposr_ref[...] = jnp.zeros_like(lposr_ref)

    @pl.when((phase == 1) & (blk == 0))
    def _():
        sizes = run_ref[...]
        tiles = jnp.ceil(sizes * (1.0 / MOE_ROW_TILE))
        tile_start = jnp.dot(tiles, before, precision=HIGHEST, preferred_element_type=F32)
        start_ref[...] = tile_start * MOE_ROW_TILE
        tile_end = tile_start + tiles
        n_col = r.astype(F32)
        ended = ((n_col >= tile_end) & (c < N_EXPERTS)).astype(F32)
        expert = jnp.sum(ended, axis=-1, keepdims=True)
        total = jnp.max(tile_end, axis=-1, keepdims=True)
        col = lax.broadcasted_iota(I32, tile_ref.shape, 1)
        tile_ref[...] = jnp.where(col == 0, expert, total).astype(I32)
        run_ref[...] = jnp.zeros_like(run_ref)
        runs_ref[...] = jnp.zeros_like(runs_ref)

    @pl.when(phase == 1)
    def _():
        rr = lax.broadcasted_iota(I32, (tb, tb), 0)
        cc = lax.broadcasted_iota(I32, (tb, tb), 1)
        strict = (cc < rr).astype(BF16)
        local = jnp.dot(run_len, before, precision=HIGHEST, preferred_element_type=F32)
        base = _dot(strict, both.astype(BF16)) + local
        p0 = jnp.sum(oh0 * base, axis=-1, keepdims=True)
        p1 = jnp.sum(oh1 * base, axis=-1, keepdims=True)
        posf = jnp.where(lane == 0, p0, jnp.where(lane == 1, p1, 0.0))
        lpos_ref[...] = posf.astype(I32)
        pick = (lax.broadcasted_iota(I32, (8, LANE), 0) == lax.broadcasted_iota(I32, (8, LANE), 1)).astype(F32)
        lposr_ref[...] = _dot_nt_highest(pick, posf).astype(I32)
        n = runs_ref.shape[0] // 3
        mine = lax.broadcasted_iota(I32, (n, LANE), 0) == blk
        for k, value in enumerate((local, run_ref[...] + start_ref[...], run_len)):
            runs_ref[k * n:(k + 1) * n, :] = jnp.where(mine, value.astype(I32), runs_ref[k * n:(k + 1) * n, :])
        run_ref[...] += run_len


def _rank(idx):
    t = idx.shape[0]
    nblk = t // RANK_TILE
    const = lambda shape: pl.BlockSpec(shape, lambda p, i: (0, 0))
    return pl.pallas_call(
        _rank_kernel,
        out_shape=(jax.ShapeDtypeStruct((t, LANE), I32), jax.ShapeDtypeStruct((8, t), I32),
                   jax.ShapeDtypeStruct((LANE, 2), I32), jax.ShapeDtypeStruct((3 * nblk, LANE), I32)),
        grid=(2, nblk),
        in_specs=[pl.BlockSpec((RANK_TILE, LANE), lambda p, i: (i, 0))],
        out_specs=(pl.BlockSpec((RANK_TILE, LANE), lambda p, i: (i * p, 0)),
                   pl.BlockSpec((8, RANK_TILE), lambda p, i: (0, i * p)),
                   const((LANE, 2)), const((3 * nblk, LANE))),
        scratch_shapes=[pltpu.VMEM((1, LANE), F32), pltpu.VMEM((1, LANE), F32)],
        compiler_params=_cparams(("arbitrary", "arbitrary")),
        name="moe_rank",
    )(idx)


def _for_each_run_piece(blk, loc_ref, dst_ref, len_ref, fn):
    for e in range(N_EXPERTS):
        k = blk * N_EXPERTS + e
        loc, dst, length = loc_ref[k], dst_ref[k], len_ref[k]
        for size in RUN_PIECES:
            @pl.when((length & size) != 0)
            def _(loc=loc, dst=dst, length=length, size=size):
                done = length & (-2 * size)
                fn(pl.multiple_of(loc + done, RUN_ALIGN), pl.multiple_of(dst + done, RUN_ALIGN), size)


def _dispatch_kernel(loc_ref, dst_ref, len_ref, hn_ref, gate_ref, lposr_ref, init_ref, xs_ref, local_ref, sem):
    del init_ref
    blk = pl.program_id(0)
    rows = lax.broadcasted_iota(I32, (LOCAL_ROWS, 1), 0)
    sel = jnp.where(lposr_ref[0:1, :] == rows, 1.0, jnp.where(lposr_ref[1:2, :] == rows, 1.0, 0.0)).astype(BF16)
    local_ref[:, :D_MODEL] = _dot(sel, hn_ref[...]).astype(BF16)
    local_ref[:, D_MODEL:] = _dot(sel, gate_ref[...]).astype(BF16)

    def copy(loc, dst, size):
        return pltpu.make_async_copy(local_ref.at[pl.ds(loc, size), :], xs_ref.at[pl.ds(dst, size), :], sem)

    _for_each_run_piece(blk, loc_ref, dst_ref, len_ref, lambda *a: copy(*a).start())
    _for_each_run_piece(blk, loc_ref, dst_ref, len_ref, lambda *a: copy(*a).wait())


def _dispatch(runs, hn, gates, lpos_rows, rows):
    nblk = hn.shape[0] // RANK_TILE
    tok = lambda n: pl.BlockSpec((RANK_TILE, n), lambda b, *_: (b, 0))
    return pl.pallas_call(
        _dispatch_kernel,
        out_shape=jax.ShapeDtypeStruct((rows, XS_WIDTH), BF16),
        grid_spec=pltpu.PrefetchScalarGridSpec(
            num_scalar_prefetch=3,
            grid=(nblk,),
            in_specs=[tok(D_MODEL), tok(LANE), pl.BlockSpec((8, RANK_TILE), lambda b, *_: (0, b)),
                      pl.BlockSpec(memory_space=pl.ANY)],
            out_specs=pl.BlockSpec(memory_space=pl.ANY),
            scratch_shapes=[pltpu.VMEM((LOCAL_ROWS, XS_WIDTH), BF16), pltpu.SemaphoreType.DMA],
        ),
        input_output_aliases={6: 0},
        compiler_params=_cparams(("arbitrary",)),
        name="moe_dispatch",
    )(*runs, hn, gates, lpos_rows, jnp.zeros((rows, XS_WIDTH), BF16))


def _moe_kernel(expert_ref, ntiles_ref, x_ref, wg_ref, wu_ref, wd_ref, out_ref, acc_ref):
    i, j = pl.program_id(0), pl.program_id(1)
    active = i < ntiles_ref[0]

    @pl.when(j == 0)
    def _():
        acc_ref[...] = jnp.zeros_like(acc_ref)

    @pl.when(active)
    def _():
        _swiglu_accumulate(acc_ref, x_ref[:, :D_MODEL], wg_ref.at[0], wu_ref.at[0], wd_ref.at[0])

    @pl.when(j == pl.num_programs(1) - 1)
    def _():
        gs = x_ref[:, D_MODEL:].astype(F32)
        gate = [sum(gs[:, GATE_TERMS * k + n:GATE_TERMS * k + n + 1] for n in range(GATE_TERMS)) for k in range(TOP_K)]
        first = gs[:, TOP_K * GATE_TERMS:TOP_K * GATE_TERMS + 1] == expert_ref[i].astype(F32)
        row_gate = jnp.where(first, gate[0], gate[1])
        out_ref[...] = jnp.where(active, acc_ref[...] * row_gate, 0.0).astype(BF16)


def _moe_experts(tile_expert, n_tiles, xs, w_gate, w_up, w_down):
    rows = xs.shape[0]
    tm, tf = MOE_ROW_TILE, FFN_FF_TILE
    nj = D_FF // tf

    def x_map(i, j, e_ref, n_ref):
        return jnp.clip(i, 0, jnp.maximum(n_ref[0] - 1, 0)), 0

    def w_idx(i, j, expert_ref, ntiles_ref):
        e = jnp.minimum(expert_ref[i], N_EXPERTS - 1)
        return e, jnp.where(i < ntiles_ref[0], j, nj - 1)

    def w_up_map(i, j, e_ref, n_ref):
        e, jj = w_idx(i, j, e_ref, n_ref)
        return e, 0, jj

    def w_down_map(i, j, e_ref, n_ref):
        e, jj = w_idx(i, j, e_ref, n_ref)
        return e, jj, 0

    return pl.pallas_call(
        _moe_kernel,
        out_shape=jax.ShapeDtypeStruct((rows, D_MODEL), BF16),
        grid_spec=pltpu.PrefetchScalarGridSpec(
            num_scalar_prefetch=2,
            grid=(rows // tm, nj),
            in_specs=[pl.BlockSpec((tm, XS_WIDTH), x_map),
                      pl.BlockSpec((1, D_MODEL, tf), w_up_map),
                      pl.BlockSpec((1, D_MODEL, tf), w_up_map),
                      pl.BlockSpec((1, tf, D_MODEL), w_down_map)],
            out_specs=pl.BlockSpec((tm, D_MODEL), lambda i, j, e, n: (i, 0)),
            scratch_shapes=[pltpu.VMEM((tm, D_MODEL), F32)],
        ),
        compiler_params=_cparams(("arbitrary", "arbitrary")),
        name="moe_experts",
    )(tile_expert, n_tiles, xs, w_gate, w_up, w_down)


def _combine_kernel(loc_ref, dst_ref, len_ref, x_ref, lpos_ref, ys_ref, out_ref, local_ref, sem):
    blk = pl.program_id(0)
    local_ref[...] = jnp.zeros_like(local_ref)

    def copy(loc, dst, size):
        return pltpu.make_async_copy(ys_ref.at[pl.ds(dst, size), :], local_ref.at[pl.ds(loc, size), :], sem)

    _for_each_run_piece(blk, loc_ref, dst_ref, len_ref, lambda *a: copy(*a).start())
    _for_each_run_piece(blk, loc_ref, dst_ref, len_ref, lambda *a: copy(*a).wait())
    rows = lax.broadcasted_iota(I32, (1, LOCAL_ROWS), 1)
    lpos = lpos_ref[...]
    sel = jnp.where(lpos[:, 0:1] == rows, 1.0, jnp.where(lpos[:, 1:2] == rows, 1.0, 0.0)).astype(BF16)
    out_ref[...] = x_ref[...] + _dot(sel, local_ref[...])


def _combine(runs, x2, lpos, ys):
    tok = lambda n: pl.BlockSpec((RANK_TILE, n), lambda b, *_: (b, 0))
    return pl.pallas_call(
        _combine_kernel,
        out_shape=jax.ShapeDtypeStruct(x2.shape, F32),
        grid_spec=pltpu.PrefetchScalarGridSpec(
            num_scalar_prefetch=3,
            grid=(x2.shape[0] // RANK_TILE,),
            in_specs=[tok(D_MODEL), tok(LANE), pl.BlockSpec(memory_space=pl.ANY)],
            out_specs=tok(D_MODEL),
            scratch_shapes=[pltpu.VMEM((LOCAL_ROWS, D_MODEL), BF16), pltpu.SemaphoreType.DMA],
        ),
        compiler_params=_cparams(("arbitrary",)),
        name="moe_combine",
    )(*runs, x2, lpos, ys)


def _moe_ffn(x2, g, w_router, w_gate_up, w_down):
    t = x2.shape[0]
    nblk = t // RANK_TILE
    hn, idx, gates = _router(x2, g, w_router)
    lpos, lpos_rows, tile_info, run_tab = _rank(idx)
    runs = tuple(run_tab.reshape(3, nblk, LANE)[:, :, :N_EXPERTS].reshape(3, -1))
    rows = t * TOP_K + nblk * N_EXPERTS * (RUN_ALIGN - 1) + N_EXPERTS * (MOE_ROW_TILE - 1)
    rows = -(-rows // MOE_ROW_TILE) * MOE_ROW_TILE
    assert rows // MOE_ROW_TILE <= LANE
    xs = _dispatch(runs, hn, gates, lpos_rows, rows)
    ys = _moe_experts(tile_info[:, 0], tile_info[:1, 1], xs,
                      w_gate_up[..., :D_FF].astype(BF16), w_gate_up[..., D_FF:].astype(BF16),
                      w_down.astype(BF16))
    return _combine(runs, x2, lpos, ys)


def _pad_heads(w, heads, dim, pad):
    w = w.reshape(w.shape[:-1] + (heads, dim))
    return _pad_lanes(w, pad).reshape(w.shape[:-2] + (heads * pad,))


def kernel(x, mem, positions, a_norm, a_w_in, a_gate_bias, a_head_norm, a_w_out, b_norm, b_w_in, b_q_a_norm, b_w_uq, b_q_head_norm, b_w_out, kv_norm, w_dkv, kv_a_norm, w_ukv, k_head_norm, mem_norm, mem_w_kv, mem_q_norm, mem_k_norm, ffn_norm, dense_w_gate_up, dense_w_down, moe_router, moe_w_gate_up, moe_w_down):
    nb, seq, _ = x.shape
    t = nb * seq
    x2 = x.reshape(t, D_MODEL)
    gmat = jnp.kron(jnp.eye(M_HEADS, dtype=F32), jnp.full((M_HEAD_DIM, M_HEAD_DIM), 1.0 / M_HEAD_DIM, F32))

    kbd0, vbd0 = _memory_kv(mem, mem_norm[0], mem_w_kv[0], mem_k_norm[0], gmat)
    w_in = a_w_in[0]
    qk_w, v_w = A_HEADS * A_QK_DIM, A_HEADS * A_V_DIM
    o0, o1, o2, o3, o4 = qk_w, 2 * qk_w, 2 * qk_w + v_w, 2 * qk_w + 2 * v_w, 2 * qk_w + 2 * v_w + 2 * A_HEADS
    w_main = jnp.concatenate([
        _pad_heads(w_in[:, :o0], A_HEADS, A_QK_DIM, A_QK_PAD),
        _pad_heads(w_in[:, o0:o1], A_HEADS, A_QK_DIM, A_QK_PAD),
        _pad_heads(w_in[:, o1:o2], A_HEADS, A_V_DIM, A_V_PAD),
        _pad_heads(w_in[:, o2:o3], A_HEADS, A_V_DIM, A_V_PAD),
        w_in[:, o4:]], axis=1).astype(BF16)
    q, k, v, o, mq, gc, gr = _a_projection(x2, a_norm[0], w_main, w_in[:, o3:o4], a_gate_bias[0])
    gc, gr = _mlstm_gates(gc, gr)
    gcol = gc.reshape(nb, seq, 2, A_HEADS).transpose(0, 3, 1, 2)
    grow = gr.reshape(2, A_HEADS, nb, seq).transpose(2, 1, 0, 3)
    three = lambda a: a.reshape(nb, seq, a.shape[-1])
    hm = _mlstm(three(q), three(k), three(v), three(o), gcol, grow,
                _pad_heads(a_head_norm[0].reshape(1, -1), A_HEADS, A_V_DIM, A_V_PAD))
    w_out = a_w_out[0]
    w_out_h = jnp.pad(w_out[:v_w].reshape(A_HEADS, A_V_DIM, D_MODEL), ((0, 0), (0, A_V_PAD - A_V_DIM), (0, 0)))
    w_out_h = w_out_h.reshape(A_HEADS * A_V_PAD, D_MODEL).astype(BF16)
    x2 = _mix_out(x2, hm.reshape(t, -1), mq, kbd0, vbd0, gmat, mem_q_norm[0],
                  w_out_h, w_out[v_w:].astype(BF16), seq)
    wgu = dense_w_gate_up[0]
    x2 = _dense_ffn(x2, ffn_norm[0], wgu[:, :D_FF].astype(BF16), wgu[:, D_FF:].astype(BF16),
                    dense_w_down[0].astype(BF16))

    cs, sn = _rope_tables(positions)
    k_sh, vt_sh = _latent_kv(x2, kv_norm, w_dkv, kv_a_norm, w_ukv, k_head_norm, cs, sn)

    kbd1, vbd1 = _memory_kv(mem, mem_norm[1], mem_w_kv[1], mem_k_norm[1], gmat)
    qh, mq1 = _b_projection(x2, b_norm[0], b_w_in[0], b_q_a_norm[0], b_w_uq[0], b_q_head_norm[0], cs, sn)
    att = _causal_attention(three(qh), three(k_sh), vt_sh, seq)
    w_out = b_w_out[0]
    n_att = B_HEADS * V_HEAD
    x2 = _mix_out(x2, att.reshape(t, -1), mq1, kbd1, vbd1, gmat, mem_q_norm[1],
                  w_out[:n_att].astype(BF16), w_out[n_att:].astype(BF16), seq)
    x2 = _moe_ffn(x2, ffn_norm[1], moe_router[0], moe_w_gate_up[0], moe_w_down[0])
    return x2.reshape(nb, seq, D_MODEL)
```

```python
import functools

import jax
import jax.numpy as jnp
from jax import lax
from jax.experimental import pallas as pl
from jax.experimental.pallas import tpu as pltpu

F32 = jnp.float32
BF16 = jnp.bfloat16
I32 = jnp.int32

EPS = 1e-6
LOG2E = 1.4426950408889634
LANE = 128
VMEM_LIMIT = 48 * 1024 * 1024

D_MODEL = 1024
N_MEM = 256
M_HEADS, M_HEAD_DIM = 4, 64
M_W = M_HEADS * M_HEAD_DIM
A_HEADS, A_QK_DIM, A_V_DIM = 4, 96, 192
A_QK_PAD, A_V_PAD = 128, 256
B_HEADS, Q_LORA, KV_LORA = 6, 384, 256
QK_NOPE, QK_ROPE, V_HEAD = 128, 64, 128
B_QK_HEAD = QK_NOPE + QK_ROPE
B_QK_PAD = 256
VT_HEAD_ROWS = V_HEAD + 16
ROPE_THETA = 10000.0
D_FF = 3584
N_EXPERTS, TOP_K = 8, 2
GATE_TERMS = 3

MLSTM_CHUNK = 256
ROW_TILE = 512
FFN_ROW_TILE = 1024
FFN_FF_TILE = 1792
FFN_CHUNK = 256
MOE_ROW_TILE = 512
ATT_Q_TILE = 512
ATT_K_TILE = 512
ATT_HEADS_PER_STEP = 6
RANK_TILE = 512
RUN_ALIGN = 16
RUN_PIECES = tuple(RANK_TILE >> s for s in range((RANK_TILE // RUN_ALIGN).bit_length()))
LOCAL_ROWS = TOP_K * RANK_TILE + N_EXPERTS * RUN_ALIGN
XS_WIDTH = D_MODEL + LANE

HIGHEST = lax.Precision.HIGHEST


def _cparams(sem):
    return pltpu.CompilerParams(dimension_semantics=sem, vmem_limit_bytes=VMEM_LIMIT)


def _rms(x, g):
    return x * lax.rsqrt(jnp.mean(x * x, axis=-1, keepdims=True) + EPS) * g


def _dot(a, b):
    return jnp.dot(a, b, preferred_element_type=F32)


def _dot_nt(a, b):
    return lax.dot_general(a, b, (((1,), (1,)), ((), ())), preferred_element_type=F32)


def _dot_tn(a, b):
    return lax.dot_general(a, b, (((0,), (0,)), ((), ())), preferred_element_type=F32)


def _group_mean_sq(x, gmat):
    sq = x * x
    hi = sq.astype(BF16)
    lo = (sq - hi.astype(F32)).astype(BF16)
    return _dot(hi, gmat) + _dot(lo, gmat)


def _memkv_kernel(mem_ref, g_ref, w_ref, kg_ref, gmat_ref, kbd_ref, vbd_ref):
    hn = _rms(mem_ref[0], g_ref[...]).astype(BF16)
    kv = _dot(hn, w_ref[...])
    k, v = kv[:, :M_W], kv[:, M_W:]
    kn = k * lax.rsqrt(_group_mean_sq(k, gmat_ref[...]) + EPS) * kg_ref[...]
    lane_head = lax.broadcasted_iota(I32, (1, M_W), 1) // M_HEAD_DIM
    for h in range(M_HEADS):
        keep = lane_head == h
        kbd_ref[0, h * N_MEM:(h + 1) * N_MEM, :] = jnp.where(keep, kn, 0.0).astype(BF16)
        vbd_ref[0, h * N_MEM:(h + 1) * N_MEM, :] = jnp.where(keep, v, 0.0).astype(BF16)


def _memory_kv(mem, g, w_kv, k_g, gmat):
    nb = mem.shape[0]
    out = jax.ShapeDtypeStruct((nb, M_HEADS * N_MEM, M_W), BF16)
    return pl.pallas_call(
        _memkv_kernel,
        out_shape=(out, out),
        grid=(nb,),
        in_specs=[
            pl.BlockSpec((1, N_MEM, D_MODEL), lambda b: (b, 0, 0)),
            pl.BlockSpec((1, D_MODEL), lambda b: (0, 0)),
            pl.BlockSpec((D_MODEL, 2 * M_W), lambda b: (0, 0)),
            pl.BlockSpec((1, M_W), lambda b: (0, 0)),
            pl.BlockSpec((M_W, M_W), lambda b: (0, 0)),
        ],
        out_specs=(pl.BlockSpec((1, M_HEADS * N_MEM, M_W), lambda b: (b, 0, 0)),) * 2,
        compiler_params=_cparams(("parallel",)),
        name="memory_kv",
    )(mem, g.reshape(1, -1), w_kv.astype(BF16), jnp.tile(k_g, M_HEADS).reshape(1, -1), gmat)


def _memory_attention(mq, kbd, vbd, gmat, qg):
    qn = mq * lax.rsqrt(_group_mean_sq(mq, gmat) + EPS) * (qg * (M_HEAD_DIM ** -0.5 * LOG2E))
    s = _dot_nt(qn.astype(BF16), kbd)
    ps = []
    for h in range(M_HEADS):
        sh = s[:, h * N_MEM:(h + 1) * N_MEM]
        e = jnp.exp2(sh - jnp.max(sh, axis=-1, keepdims=True))
        ps.append((e / jnp.sum(e, axis=-1, keepdims=True)).astype(BF16))
    return _dot(jnp.concatenate(ps, axis=-1), vbd)


def _a_proj_kernel(x_ref, g_ref, w_ref, wvot_ref, wif_ref, wift_ref, bc_ref, br_ref,
                   q_ref, k_ref, vt_ref, ot_ref, mq_ref, gc_ref, gr_ref):
    hn = _rms(x_ref[...], g_ref[...]).astype(BF16)
    nq = A_HEADS * A_QK_PAD
    nv = A_HEADS * A_V_PAD
    q_ref[...] = _dot(hn, w_ref[:, :nq]).astype(BF16)
    k_ref[...] = (_dot(hn, w_ref[:, nq:2 * nq]) * (A_QK_DIM ** -0.5)).astype(BF16)
    mq_ref[...] = _dot(hn, w_ref[:, 2 * nq:])
    vt = _dot_nt(wvot_ref[:nv, :], hn)
    ones_row = lax.broadcasted_iota(I32, (nv, 1), 0) % A_V_PAD == A_V_DIM
    vt_ref[...] = jnp.where(ones_row, 1.0, vt).astype(BF16)
    ot_ref[...] = _dot_nt(wvot_ref[nv:, :], hn)
    gc_ref[...] = _dot(hn, wif_ref[...])[:, :2 * A_HEADS] + bc_ref[...]
    gr_ref[...] = _dot_nt(wift_ref[...], hn) + br_ref[...]


def _a_projection(x2, g, w_main, w_vo_t, w_if, gate_bias):
    t = x2.shape[0]
    nq, nv = A_HEADS * A_QK_PAD, A_HEADS * A_V_PAD
    ng = 2 * A_HEADS
    wif_pad = jnp.pad(w_if, ((0, 0), (0, LANE - ng))).astype(BF16)
    row = lambda n: pl.BlockSpec((ROW_TILE, n), lambda i: (i, 0))
    col = lambda n: pl.BlockSpec((n, ROW_TILE), lambda i: (0, i))
    full = lambda a: pl.BlockSpec(a.shape, lambda i: (0,) * a.ndim)
    args = (x2, g.reshape(1, -1), w_main, w_vo_t, wif_pad, w_if.T.astype(BF16),
            gate_bias.reshape(1, ng), gate_bias.reshape(ng, 1))
    return pl.pallas_call(
        _a_proj_kernel,
        out_shape=(jax.ShapeDtypeStruct((t, nq), BF16), jax.ShapeDtypeStruct((t, nq), BF16),
                   jax.ShapeDtypeStruct((nv, t), BF16), jax.ShapeDtypeStruct((nv, t), F32),
                   jax.ShapeDtypeStruct((t, M_W), F32), jax.ShapeDtypeStruct((t, ng), F32),
                   jax.ShapeDtypeStruct((ng, t), F32)),
        grid=(t // ROW_TILE,),
        in_specs=[row(D_MODEL)] + [full(a) for a in args[1:]],
        out_specs=(row(nq), row(nq), col(nv), col(nv), row(M_W), row(ng), col(ng)),
        compiler_params=_cparams(("parallel",)),
        name="mlstm_in_proj",
    )(*args)


def _log_sigmoid(f):
    return jnp.minimum(f, 0.0) - jnp.log(1.0 + jnp.exp(-jnp.abs(f)))


def _dot_nt_highest(a, b):
    return lax.dot_general(a, b, (((1,), (1,)), ((), ())), precision=HIGHEST,
                           preferred_element_type=F32)


def _chunk_gates(gc, gr):
    L = gc.shape[0]
    r = lax.broadcasted_iota(I32, (L, L), 0)
    c = lax.broadcasted_iota(I32, (L, L), 1)
    lower = (c <= r).astype(F32)
    is_f_col = lax.broadcasted_iota(I32, gc.shape, 1) >= A_HEADS
    is_f_row = lax.broadcasted_iota(I32, gr.shape, 0) >= A_HEADS
    lf_c = jnp.where(is_f_col, _log_sigmoid(gc), 0.0)
    lf_r = jnp.where(is_f_row, _log_sigmoid(gr), 0.0)
    lf_c = jnp.concatenate([lf_c, jnp.zeros((L, LANE - gc.shape[1]), F32)], axis=1)
    cum_c = jnp.dot(lower, lf_c, precision=HIGHEST, preferred_element_type=F32)[:, :gc.shape[1]]
    cum_r = _dot_nt_highest(lf_r, lower)
    return jnp.where(is_f_col, cum_c, gc), jnp.where(is_f_row, cum_r, gr)


def _prefix_max_lanes(x):
    lane = lax.broadcasted_iota(I32, x.shape, 1)
    shift = 1
    while shift < x.shape[1]:
        x = jnp.maximum(x, jnp.where(lane >= shift, pltpu.roll(x, shift, axis=1), -jnp.inf))
        shift *= 2
    return x


def _mlstm_kernel(q_ref, k_ref, vt_ref, ot_ref, gc_ref, gr_ref, hg_ref, out_ref, c_ref, m_ref):
    L = MLSTM_CHUNK

    @pl.when(pl.program_id(1) == 0)
    def _():
        c_ref[...] = jnp.zeros_like(c_ref)
        m_ref[...] = jnp.zeros_like(m_ref)

    src = lax.broadcasted_iota(I32, (L, L), 0)
    tgt = lax.broadcasted_iota(I32, (L, L), 1)
    real = lax.broadcasted_iota(I32, (A_V_PAD, 1), 0) < A_V_DIM
    gcol, grow = _chunk_gates(gc_ref[...], gr_ref[...])
    for hd in range(A_HEADS):
        qk = slice(hd * A_QK_PAD, (hd + 1) * A_QK_PAD)
        vv = slice(hd * A_V_PAD, (hd + 1) * A_V_PAD)
        q, k, vt = q_ref[0, :, qk], k_ref[0, :, qk], vt_ref[vv, :]
        u_c = gcol[:, hd:hd + 1] - gcol[:, A_HEADS + hd:A_HEADS + hd + 1]
        g_r = grow[A_HEADS + hd:A_HEADS + hd + 1, :]
        u_r = grow[hd:hd + 1, :] - g_r
        g_last = g_r[:, L - 1:L]
        m_prev = m_ref[hd, 0:1, 0:1]
        c_prev = c_ref[hd]

        run_max = jnp.maximum(_prefix_max_lanes(jnp.broadcast_to(u_r, (8, L)))[0:1, :], m_prev)
        m_t = g_r + run_max
        inter = jnp.exp(m_prev - run_max)
        decay_t = jnp.where(src <= tgt, jnp.exp(u_c - run_max), 0.0)
        p_t = (decay_t * _dot_nt(k, q)).astype(BF16)
        num_t = inter * _dot_nt(c_prev.astype(BF16), q) + _dot(vt, p_t)
        den = num_t[A_V_DIM:A_V_DIM + 1, :]
        h_t = jnp.where(real, num_t / jnp.maximum(jnp.abs(den), jnp.exp(-m_t)), 0.0)
        scale = lax.rsqrt(jnp.sum(h_t * h_t, axis=0, keepdims=True) * (1.0 / A_V_DIM) + EPS)
        out_t = h_t * scale * hg_ref[vv, :] * jax.nn.sigmoid(ot_ref[vv, :])
        out_ref[0, :, vv] = out_t.T.astype(BF16)

        w_r = g_last + u_r
        m_new = jnp.maximum(g_last + m_prev, jnp.max(w_r, axis=-1, keepdims=True))
        ev_t = (jnp.exp(w_r - m_new) * vt.astype(F32)).astype(BF16)
        c_ref[hd] = jnp.exp(g_last + m_prev - m_new) * c_prev + _dot(ev_t, k)
        m_ref[hd] = jnp.broadcast_to(m_new, m_ref.shape[1:])


def _mlstm(q, k, vt, ot, gcol, grow, head_g):
    nb, s, _ = q.shape
    L = MLSTM_CHUNK
    nv = A_HEADS * A_V_PAD
    blk = lambda w: pl.BlockSpec((1, L, w), lambda b, c: (b, c, 0))
    col = lambda n: pl.BlockSpec((n, L), lambda b, c: (0, b * (s // L) + c))
    return pl.pallas_call(
        _mlstm_kernel,
        out_shape=jax.ShapeDtypeStruct((nb, s, nv), BF16),
        grid=(nb, s // L),
        in_specs=[blk(A_HEADS * A_QK_PAD), blk(A_HEADS * A_QK_PAD), col(nv), col(nv),
                  pl.BlockSpec((L, 2 * A_HEADS), lambda b, c: (b * (s // L) + c, 0)), col(2 * A_HEADS),
                  pl.BlockSpec((nv, L), lambda b, c: (0, 0))],
        out_specs=blk(nv),
        scratch_shapes=[pltpu.VMEM((A_HEADS, A_V_PAD, A_QK_PAD), F32), pltpu.VMEM((A_HEADS, 8, LANE), F32)],
        compiler_params=_cparams(("parallel", "arbitrary")),
        name="mlstm_chunkwise",
    )(q, k, vt, ot, gcol, grow, head_g)


def _mix_out_kernel(x_ref, h_ref, mq_ref, kbd_ref, vbd_ref, gmat_ref, qg_ref, w1_ref, w2_ref, out_ref):
    mo = _memory_attention(mq_ref[...], kbd_ref[0], vbd_ref[0], gmat_ref[...], qg_ref[...])
    out_ref[...] = x_ref[...] + _dot(h_ref[...], w1_ref[...]) + _dot(mo.astype(BF16), w2_ref[...])


def _mix_out(x2, h2, mq, kbd, vbd, gmat, qg, w_main, w_mem, seq):
    t = x2.shape[0]
    tm = ROW_TILE
    row = lambda n: pl.BlockSpec((tm, n), lambda i: (i, 0))
    full = lambda a: pl.BlockSpec(a.shape, lambda i: (0,) * a.ndim)
    per_batch = pl.BlockSpec((1,) + kbd.shape[1:], lambda i: ((i * tm) // seq, 0, 0))
    qg_t = jnp.tile(qg, M_HEADS).reshape(1, -1)
    return pl.pallas_call(
        _mix_out_kernel,
        out_shape=jax.ShapeDtypeStruct((t, D_MODEL), F32),
        grid=(t // tm,),
        in_specs=[row(D_MODEL), row(h2.shape[1]), row(M_W), per_batch, per_batch,
                  full(gmat), full(qg_t), full(w_main), full(w_mem)],
        out_specs=row(D_MODEL),
        compiler_params=_cparams(("parallel",)),
        name="mixer_out_proj",
    )(x2, h2, mq, kbd, vbd, gmat, qg_t, w_main, w_mem)


def _swiglu_accumulate(acc_ref, h_ref, x, wg_ref, wu_ref, wd_ref):
    for c in range(wg_ref.shape[-1] // FFN_CHUNK):
        cols = slice(c * FFN_CHUNK, (c + 1) * FFN_CHUNK)
        gate = _dot(x, wg_ref[:, cols])
        up = _dot(x, wu_ref[:, cols])
        h_ref[:, cols] = (jax.nn.silu(gate) * up).astype(BF16)
    acc_ref[...] += _dot(h_ref[...], wd_ref[...])


def _ffn_kernel(x_ref, g_ref, wg_ref, wu_ref, wd_ref, out_ref, hn_ref, acc_ref, h_ref):
    j = pl.program_id(1)

    @pl.when(j == 0)
    def _():
        hn_ref[...] = _rms(x_ref[...], g_ref[...]).astype(BF16)
        acc_ref[...] = jnp.zeros_like(acc_ref)

    _swiglu_accumulate(acc_ref, h_ref, hn_ref[...], wg_ref, wu_ref, wd_ref)

    @pl.when(j == pl.num_programs(1) - 1)
    def _():
        out_ref[...] = x_ref[...] + acc_ref[...]


def _dense_ffn(x2, g, w_gate, w_up, w_down):
    t = x2.shape[0]
    tm, tf = FFN_ROW_TILE, FFN_FF_TILE
    return pl.pallas_call(
        _ffn_kernel,
        out_shape=jax.ShapeDtypeStruct((t, D_MODEL), F32),
        grid=(t // tm, D_FF // tf),
        in_specs=[pl.BlockSpec((tm, D_MODEL), lambda i, j: (i, 0)),
                  pl.BlockSpec((1, D_MODEL), lambda i, j: (0, 0)),
                  pl.BlockSpec((D_MODEL, tf), lambda i, j: (0, j)),
                  pl.BlockSpec((D_MODEL, tf), lambda i, j: (0, j)),
                  pl.BlockSpec((tf, D_MODEL), lambda i, j: (j, 0))],
        out_specs=pl.BlockSpec((tm, D_MODEL), lambda i, j: (i, 0)),
        scratch_shapes=[pltpu.VMEM((tm, D_MODEL), BF16), pltpu.VMEM((tm, D_MODEL), F32),
                        pltpu.VMEM((tm, tf), BF16)],
        compiler_params=_cparams(("parallel", "arbitrary")),
        name="dense_swiglu",
    )(x2, g.reshape(1, -1), w_gate, w_up, w_down)


def _rope_kernel(pos_ref, inv_ref, sign_ref, cs_ref, sn_ref):
    ang = pos_ref[...].astype(F32) * inv_ref[...]
    cs_ref[...] = jnp.cos(ang)
    sn_ref[...] = jnp.sin(ang) * sign_ref[...]


def _rope_tables(positions):
    t = positions.size
    half = QK_ROPE // 2
    inv = 1.0 / (ROPE_THETA ** (jnp.arange(0, QK_ROPE, 2, dtype=F32) / QK_ROPE))
    pad = jnp.zeros((LANE - QK_ROPE,), F32)
    inv_l = jnp.concatenate([inv, inv, pad]).reshape(1, LANE)
    sign = jnp.concatenate([-jnp.ones((half,), F32), jnp.ones((half,), F32), pad]).reshape(1, LANE)
    out = jax.ShapeDtypeStruct((t, LANE), F32)
    return pl.pallas_call(
        _rope_kernel,
        out_shape=(out, out),
        grid=(t // ROW_TILE,),
        in_specs=[pl.BlockSpec((ROW_TILE, 1), lambda i: (i, 0)),
                  pl.BlockSpec((1, LANE), lambda i: (0, 0)),
                  pl.BlockSpec((1, LANE), lambda i: (0, 0))],
        out_specs=(pl.BlockSpec((ROW_TILE, LANE), lambda i: (i, 0)),) * 2,
        compiler_params=_cparams(("parallel",)),
        name="rope_tables",
    )(positions.reshape(t, 1), inv_l, sign)


def _head_qk_norm_rope(nope, rope, rope_sw, g_nope, g_rope, g_rope_sw, cs, sn, scale):
    ss = jnp.sum(nope * nope, axis=-1, keepdims=True) + jnp.sum(rope * rope, axis=-1, keepdims=True)
    r = lax.rsqrt(ss * (1.0 / B_QK_HEAD) + EPS) * scale
    return nope * r * g_nope, (rope * g_rope * cs + rope_sw * g_rope_sw * sn) * r


def _latent_kv_kernel(x_ref, g_ref, wd_ref, ga_ref, wuk_ref, wuvt_ref, kg_ref, cs_ref, sn_ref, k_ref, vt_ref):
    hn = _rms(x_ref[...], g_ref[...]).astype(BF16)
    z = _dot(hn, wd_ref[...])
    c_kv = z[:, :KV_LORA]
    rope, rope_sw = z[:, KV_LORA:KV_LORA + LANE], z[:, KV_LORA + LANE:]
    cn = _rms(c_kv, ga_ref[...]).astype(BF16)
    kv = _dot(cn, wuk_ref[...])
    vt = _dot_nt(wuvt_ref[...], cn)
    tm = vt.shape[1]
    ones_row = (lax.broadcasted_iota(I32, (VT_HEAD_ROWS - V_HEAD, tm), 0) == 0).astype(BF16)
    for h in range(B_HEADS):
        vt_ref[h * VT_HEAD_ROWS:h * VT_HEAD_ROWS + V_HEAD, :] = vt[h * V_HEAD:(h + 1) * V_HEAD].astype(BF16)
        vt_ref[h * VT_HEAD_ROWS + V_HEAD:(h + 1) * VT_HEAD_ROWS, :] = ones_row
    kg = kg_ref[...]
    for h in range(B_HEADS):
        kn, kr = _head_qk_norm_rope(kv[:, h * QK_NOPE:(h + 1) * QK_NOPE], rope, rope_sw,
                                    kg[:, :LANE], kg[:, LANE:2 * LANE], kg[:, 2 * LANE:],
                                    cs_ref[...], sn_ref[...], 1.0)
        k_ref[:, h * B_QK_PAD:h * B_QK_PAD + QK_NOPE] = kn.astype(BF16)
        k_ref[:, h * B_QK_PAD + QK_NOPE:(h + 1) * B_QK_PAD] = kr.astype(BF16)


def _rope_swap(w):
    half = QK_ROPE // 2
    return jnp.concatenate([w[..., half:], w[..., :half]], axis=-1)


def _pad_lanes(w, n=LANE):
    return jnp.pad(w, [(0, 0)] * (w.ndim - 1) + [(0, n - w.shape[-1])])


def _head_gain(g):
    g_rope = g[QK_NOPE:]
    return jnp.concatenate([g[:QK_NOPE], _pad_lanes(g_rope), _pad_lanes(_rope_swap(g_rope))]).reshape(1, -1)


def _latent_kv(x2, kv_norm, w_dkv, kv_a_norm, w_ukv, k_head_norm, cs, sn):
    t = x2.shape[0]
    w_rope = w_dkv[:, KV_LORA:]
    wd = jnp.concatenate([w_dkv[:, :KV_LORA], _pad_lanes(w_rope), _pad_lanes(_rope_swap(w_rope))],
                         axis=1).astype(BF16)
    wu = w_ukv.reshape(KV_LORA, B_HEADS, QK_NOPE + V_HEAD)
    wuk = wu[:, :, :QK_NOPE].reshape(KV_LORA, -1).astype(BF16)
    wuvt = wu[:, :, QK_NOPE:].reshape(KV_LORA, -1).T.astype(BF16)
    args = (x2, kv_norm.reshape(1, -1), wd, kv_a_norm.reshape(1, -1), wuk, wuvt, _head_gain(k_head_norm), cs, sn)
    row = lambda n: pl.BlockSpec((ROW_TILE, n), lambda i: (i, 0))
    full = lambda a: pl.BlockSpec(a.shape, lambda i: (0,) * a.ndim)
    return pl.pallas_call(
        _latent_kv_kernel,
        out_shape=(jax.ShapeDtypeStruct((t, B_HEADS * B_QK_PAD), BF16),
                   jax.ShapeDtypeStruct((B_HEADS * VT_HEAD_ROWS, t), BF16)),
        grid=(t // ROW_TILE,),
        in_specs=[row(D_MODEL)] + [full(a) for a in args[1:7]] + [row(LANE), row(LANE)],
        out_specs=(row(B_HEADS * B_QK_PAD), pl.BlockSpec((B_HEADS * VT_HEAD_ROWS, ROW_TILE), lambda i: (0, i))),
        compiler_params=_cparams(("parallel",)),
        name="latent_kv",
    )(*args)


def _b_proj_kernel(x_ref, g_ref, win_ref, ga_ref, wuq_ref, qg_ref, cs_ref, sn_ref, q_ref, mq_ref):
    hn = _rms(x_ref[...], g_ref[...]).astype(BF16)
    proj = _dot(hn, win_ref[...])
    mq_ref[...] = proj[:, Q_LORA:]
    qall = _dot(_rms(proj[:, :Q_LORA], ga_ref[...]).astype(BF16), wuq_ref[...])
    qg = qg_ref[...]
    per_head = QK_NOPE + 2 * LANE
    for h in range(B_HEADS):
        base = h * per_head
        qn, qr = _head_qk_norm_rope(qall[:, base:base + QK_NOPE],
                                    qall[:, base + QK_NOPE:base + QK_NOPE + LANE],
                                    qall[:, base + QK_NOPE + LANE:base + per_head],
                                    qg[:, :LANE], qg[:, LANE:2 * LANE], qg[:, 2 * LANE:],
                                    cs_ref[...], sn_ref[...], B_QK_HEAD ** -0.5 * LOG2E)
        q_ref[:, h * B_QK_PAD:h * B_QK_PAD + QK_NOPE] = qn.astype(BF16)
        q_ref[:, h * B_QK_PAD + QK_NOPE:(h + 1) * B_QK_PAD] = qr.astype(BF16)


def _b_projection(x2, g, w_in, q_a_g, w_uq, q_head_g, cs, sn):
    t = x2.shape[0]
    wq = w_uq.reshape(Q_LORA, B_HEADS, B_QK_HEAD)
    w_rope = wq[:, :, QK_NOPE:]
    wq = jnp.concatenate([wq[:, :, :QK_NOPE], _pad_lanes(w_rope), _pad_lanes(_rope_swap(w_rope))],
                         axis=-1).reshape(Q_LORA, -1).astype(BF16)
    args = (x2, g.reshape(1, -1), w_in.astype(BF16), q_a_g.reshape(1, -1), wq, _head_gain(q_head_g), cs, sn)
    row = lambda n: pl.BlockSpec((ROW_TILE, n), lambda i: (i, 0))
    full = lambda a: pl.BlockSpec(a.shape, lambda i: (0,) * a.ndim)
    return pl.pallas_call(
        _b_proj_kernel,
        out_shape=(jax.ShapeDtypeStruct((t, B_HEADS * B_QK_PAD), BF16),
                   jax.ShapeDtypeStruct((t, M_W), F32)),
        grid=(t // ROW_TILE,),
        in_specs=[row(D_MODEL)] + [full(a) for a in args[1:6]] + [row(LANE), row(LANE)],
        out_specs=(row(B_HEADS * B_QK_PAD), row(M_W)),
        compiler_params=_cparams(("parallel",)),
        name="mla_q_proj",
    )(*args)


def _attn_kernel(q_ref, k_ref, vt_ref, out_ref):
    tq, tk = ATT_Q_TILE, ATT_K_TILE
    i = pl.program_id(2)

    def block(h, j, carry, masked):
        m, acc = carry
        off = pl.multiple_of(j * tk, tk)
        q = q_ref[0, :, h * B_QK_PAD:(h + 1) * B_QK_PAD]
        st = _dot_nt(k_ref[0, pl.ds(off, tk), h * B_QK_PAD:(h + 1) * B_QK_PAD], q)
        if masked:
            key = lax.broadcasted_iota(I32, (tk, tq), 0)
            qry = lax.broadcasted_iota(I32, (tk, tq), 1)
            st = jnp.where(key <= qry, st, -jnp.inf)
        m_new = jnp.maximum(m, jnp.max(st, axis=0, keepdims=True))
        p = jnp.exp2(st - m_new).astype(BF16)
        vt = vt_ref[h * VT_HEAD_ROWS:(h + 1) * VT_HEAD_ROWS, pl.ds(off, tk)]
        return m_new, jnp.exp2(m - m_new) * acc + _dot(vt, p)

    heads = range(ATT_HEADS_PER_STEP)
    init = tuple((jnp.full((1, tq), -jnp.inf, F32), jnp.zeros((VT_HEAD_ROWS, tq), F32)) for _ in heads)
    carry = lax.fori_loop(0, i, lambda j, c: tuple(block(h, j, c[h], False) for h in heads), init)
    for h in heads:
        _, acc = block(h, i, carry[h], True)
        out_t = acc[:V_HEAD] / acc[V_HEAD:V_HEAD + 1]
        out_ref[0, :, h * V_HEAD:(h + 1) * V_HEAD] = out_t.T.astype(BF16)


def _causal_attention(q, k, vt, seq):
    nb = q.shape[0]
    g = ATT_HEADS_PER_STEP
    assert ATT_Q_TILE == ATT_K_TILE and B_HEADS % g == 0
    return pl.pallas_call(
        _attn_kernel,
        out_shape=jax.ShapeDtypeStruct((nb, seq, B_HEADS * V_HEAD), BF16),
        grid=(nb, B_HEADS // g, seq // ATT_Q_TILE),
        in_specs=[pl.BlockSpec((1, ATT_Q_TILE, g * B_QK_PAD), lambda b, h, i: (b, i, h)),
                  pl.BlockSpec((1, seq, g * B_QK_PAD), lambda b, h, i: (b, 0, h)),
                  pl.BlockSpec((g * VT_HEAD_ROWS, seq), lambda b, h, i: (h, b))],
        out_specs=pl.BlockSpec((1, ATT_Q_TILE, g * V_HEAD), lambda b, h, i: (b, i, h)),
        compiler_params=_cparams(("parallel", "parallel", "arbitrary")),
        name="causal_attention",
    )(q, k, vt)


def _router_kernel(x_ref, g_ref, wr_ref, hn_ref, idx_ref, gate_ref):
    hn = _rms(x_ref[...], g_ref[...])
    hn_ref[...] = hn.astype(BF16)
    logits = jnp.dot(hn, wr_ref[...], precision=HIGHEST, preferred_element_type=F32)
    lane = lax.broadcasted_iota(I32, logits.shape, 1)
    logits = jnp.where(lane < N_EXPERTS, logits, -jnp.inf)
    v1 = jnp.max(logits, axis=-1, keepdims=True)
    i1 = jnp.min(jnp.where(logits == v1, lane, LANE), axis=-1, keepdims=True)
    rest = jnp.where(lane == i1, -jnp.inf, logits)
    v2 = jnp.max(rest, axis=-1, keepdims=True)
    i2 = jnp.min(jnp.where(rest == v2, lane, LANE), axis=-1, keepdims=True)
    e2 = jnp.exp(v2 - v1)
    den = 1.0 + e2
    idx_ref[...] = jnp.where(lane == 0, i1, jnp.where(lane == 1, i2, 0))
    record = jnp.zeros(logits.shape, F32)
    for k, gate in enumerate((1.0 / den, e2 / den)):
        hi = gate.astype(BF16).astype(F32)
        mid = (gate - hi).astype(BF16).astype(F32)
        for part, term in enumerate((hi, mid, gate - hi - mid)):
            record = jnp.where(lane == GATE_TERMS * k + part, term, record)
    record = jnp.where(lane == 2 * GATE_TERMS, i1.astype(F32), record)
    record = jnp.where(lane == 2 * GATE_TERMS + 1, i2.astype(F32), record)
    gate_ref[...] = record.astype(BF16)


def _router(x2, g, w_router):
    t = x2.shape[0]
    row = lambda n: pl.BlockSpec((ROW_TILE, n), lambda i: (i, 0))
    wr = _pad_lanes(w_router)
    return pl.pallas_call(
        _router_kernel,
        out_shape=(jax.ShapeDtypeStruct((t, D_MODEL), BF16), jax.ShapeDtypeStruct((t, LANE), I32),
                   jax.ShapeDtypeStruct((t, LANE), BF16)),
        grid=(t // ROW_TILE,),
        in_specs=[row(D_MODEL), pl.BlockSpec((1, D_MODEL), lambda i: (0, 0)),
                  pl.BlockSpec(wr.shape, lambda i: (0, 0))],
        out_specs=(row(D_MODEL), row(LANE), row(LANE)),
        compiler_params=_cparams(("parallel",)),
        name="moe_router",
    )(x2, g.reshape(1, -1), wr)


def _rank_kernel(idx_ref, lpos_ref, lposr_ref, tile_ref, runs_ref, run_ref, start_ref):
    phase, blk = pl.program_id(0), pl.program_id(1)
    tb = RANK_TILE
    lane = lax.broadcasted_iota(I32, (tb, LANE), 1)
    idx = idx_ref[...]
    oh0 = (lane == idx[:, 0:1]).astype(F32)
    oh1 = (lane == idx[:, 1:2]).astype(F32)
    both = oh0 + oh1
    run_len = jnp.ceil(jnp.sum(both, axis=0, keepdims=True) * (1.0 / RUN_ALIGN)) * RUN_ALIGN
    r = lax.broadcasted_iota(I32, (LANE, LANE), 0)
    c = lax.broadcasted_iota(I32, (LANE, LANE), 1)
    before = (r < c).astype(F32)

    @pl.when((phase == 0) & (blk == 0))
    def _():
        run_ref[...] = jnp.zeros_like(run_ref)

    @pl.when(phase == 0)
    def _():
        run_ref[...] += run_len
        lpos_ref[...] = jnp.zeros_like(lpos_ref)
        lposr_ref[...] = jnp.zeros_like(lposr_ref)

    @pl.when((phase == 1) & (blk == 0))
    def _():
        sizes = run_ref[...]
        tiles = jnp.ceil(sizes * (1.0 / MOE_ROW_TILE))
        tile_start = jnp.dot(tiles, before, precision=HIGHEST, preferred_element_type=F32)
        start_ref[...] = tile_start * MOE_ROW_TILE
        tile_end = tile_start + tiles
        n_col = r.astype(F32)
        ended = ((n_col >= tile_end) & (c < N_EXPERTS)).astype(F32)
        expert = jnp.sum(ended, axis=-1, keepdims=True)
        total = jnp.max(tile_end, axis=-1, keepdims=True)
        col = lax.broadcasted_iota(I32, tile_ref.shape, 1)
        tile_ref[...] = jnp.where(col == 0, expert, total).astype(I32)
        run_ref[...] = jnp.zeros_like(run_ref)
        runs_ref[...] = jnp.zeros_like(runs_ref)

    @pl.when(phase == 1)
    def _():
        rr = lax.broadcasted_iota(I32, (tb, tb), 0)
        cc = lax.broadcasted_iota(I32, (tb, tb), 1)
        strict = (cc < rr).astype(BF16)
        local = jnp.dot(run_len, before, precision=HIGHEST, preferred_element_type=F32)
        base = _dot(strict, both.astype(BF16)) + local
        p0 = jnp.sum(oh0 * base, axis=-1, keepdims=True)
        p1 = jnp.sum(oh1 * base, axis=-1, keepdims=True)
        posf = jnp.where(lane == 0, p0, jnp.where(lane == 1, p1, 0.0))
        lpos_ref[...] = posf.astype(I32)
        pick = (lax.broadcasted_iota(I32, (8, LANE), 0) == lax.broadcasted_iota(I32, (8, LANE), 1)).astype(F32)
        lposr_ref[...] = _dot_nt_highest(pick, posf).astype(I32)
        n = runs_ref.shape[0] // 3
        mine = lax.broadcasted_iota(I32, (n, LANE), 0) == blk
        for k, value in enumerate((local, run_ref[...] + start_ref[...], run_len)):
            runs_ref[k * n:(k + 1) * n, :] = jnp.where(mine, value.astype(I32), runs_ref[k * n:(k + 1) * n, :])
        run_ref[...] += run_len


def _rank(idx):
    t = idx.shape[0]
    nblk = t // RANK_TILE
    const = lambda shape: pl.BlockSpec(shape, lambda p, i: (0, 0))
    return pl.pallas_call(
        _rank_kernel,
        out_shape=(jax.ShapeDtypeStruct((t, LANE), I32), jax.ShapeDtypeStruct((8, t), I32),
                   jax.ShapeDtypeStruct((LANE, 2), I32), jax.ShapeDtypeStruct((3 * nblk, LANE), I32)),
        grid=(2, nblk),
        in_specs=[pl.BlockSpec((RANK_TILE, LANE), lambda p, i: (i, 0))],
        out_specs=(pl.BlockSpec((RANK_TILE, LANE), lambda p, i: (i * p, 0)),
                   pl.BlockSpec((8, RANK_TILE), lambda p, i: (0, i * p)),
                   const((LANE, 2)), const((3 * nblk, LANE))),
        scratch_shapes=[pltpu.VMEM((1, LANE), F32), pltpu.VMEM((1, LANE), F32)],
        compiler_params=_cparams(("arbitrary", "arbitrary")),
        name="moe_rank",
    )(idx)


def _for_each_run_piece(blk, loc_ref, dst_ref, len_ref, fn):
    for e in range(N_EXPERTS):
        k = blk * N_EXPERTS + e
        loc, dst, length = loc_ref[k], dst_ref[k], len_ref[k]
        for size in RUN_PIECES:
            @pl.when((length & size) != 0)
            def _(loc=loc, dst=dst, length=length, size=size):
                done = length & (-2 * size)
                fn(pl.multiple_of(loc + done, RUN_ALIGN), pl.multiple_of(dst + done, RUN_ALIGN), size)


def _dispatch_kernel(loc_ref, dst_ref, len_ref, hn_ref, gate_ref, lposr_ref, init_ref, xs_ref, local_ref, sem):
    del init_ref
    blk = pl.program_id(0)
    rows = lax.broadcasted_iota(I32, (LOCAL_ROWS, 1), 0)
    sel = jnp.where(lposr_ref[0:1, :] == rows, 1.0, jnp.where(lposr_ref[1:2, :] == rows, 1.0, 0.0)).astype(BF16)
    local_ref[:, :D_MODEL] = _dot(sel, hn_ref[...]).astype(BF16)
    local_ref[:, D_MODEL:] = _dot(sel, gate_ref[...]).astype(BF16)

    def copy(loc, dst, size):
        return pltpu.make_async_copy(local_ref.at[pl.ds(loc, size), :], xs_ref.at[pl.ds(dst, size), :], sem)

    _for_each_run_piece(blk, loc_ref, dst_ref, len_ref, lambda *a: copy(*a).start())
    _for_each_run_piece(blk, loc_ref, dst_ref, len_ref, lambda *a: copy(*a).wait())


def _dispatch(runs, hn, gates, lpos_rows, rows):
    nblk = hn.shape[0] // RANK_TILE
    tok = lambda n: pl.BlockSpec((RANK_TILE, n), lambda b, *_: (b, 0))
    return pl.pallas_call(
        _dispatch_kernel,
        out_shape=jax.ShapeDtypeStruct((rows, XS_WIDTH), BF16),
        grid_spec=pltpu.PrefetchScalarGridSpec(
            num_scalar_prefetch=3,
            grid=(nblk,),
            in_specs=[tok(D_MODEL), tok(LANE), pl.BlockSpec((8, RANK_TILE), lambda b, *_: (0, b)),
                      pl.BlockSpec(memory_space=pl.ANY)],
            out_specs=pl.BlockSpec(memory_space=pl.ANY),
            scratch_shapes=[pltpu.VMEM((LOCAL_ROWS, XS_WIDTH), BF16), pltpu.SemaphoreType.DMA],
        ),
        input_output_aliases={6: 0},
        compiler_params=_cparams(("arbitrary",)),
        name="moe_dispatch",
    )(*runs, hn, gates, lpos_rows, jnp.zeros((rows, XS_WIDTH), BF16))


def _moe_kernel(expert_ref, ntiles_ref, x_ref, wg_ref, wu_ref, wd_ref, out_ref, acc_ref, h_ref):
    i, j = pl.program_id(0), pl.program_id(1)
    active = i < ntiles_ref[0]

    @pl.when(j == 0)
    def _():
        acc_ref[...] = jnp.zeros_like(acc_ref)

    @pl.when(active)
    def _():
        _swiglu_accumulate(acc_ref, h_ref, x_ref[:, :D_MODEL], wg_ref.at[0], wu_ref.at[0], wd_ref.at[0])

    @pl.when(j == pl.num_programs(1) - 1)
    def _():
        gs = x_ref[:, D_MODEL:].astype(F32)
        gate = [sum(gs[:, GATE_TERMS * k + n:GATE_TERMS * k + n + 1] for n in range(GATE_TERMS)) for k in range(TOP_K)]
        first = gs[:, TOP_K * GATE_TERMS:TOP_K * GATE_TERMS + 1] == expert_ref[i].astype(F32)
        row_gate = jnp.where(first, gate[0], gate[1])
        out_ref[...] = jnp.where(active, acc_ref[...] * row_gate, 0.0).astype(BF16)


def _moe_experts(tile_expert, n_tiles, xs, w_gate, w_up, w_down):
    rows = xs.shape[0]
    tm, tf = MOE_ROW_TILE, FFN_FF_TILE
    nj = D_FF // tf

    def x_map(i, j, e_ref, n_ref):
        return jnp.clip(i, 0, jnp.maximum(n_ref[0] - 1, 0)), 0

    def w_idx(i, j, expert_ref, ntiles_ref):
        e = jnp.minimum(expert_ref[i], N_EXPERTS - 1)
        return e, jnp.where(i < ntiles_ref[0], j, nj - 1)

    def w_up_map(i, j, e_ref, n_ref):
        e, jj = w_idx(i, j, e_ref, n_ref)
        return e, 0, jj

    def w_down_map(i, j, e_ref, n_ref):
        e, jj = w_idx(i, j, e_ref, n_ref)
        return e, jj, 0

    return pl.pallas_call(
        _moe_kernel,
        out_shape=jax.ShapeDtypeStruct((rows, D_MODEL), BF16),
        grid_spec=pltpu.PrefetchScalarGridSpec(
            num_scalar_prefetch=2,
            grid=(rows // tm, nj),
            in_specs=[pl.BlockSpec((tm, XS_WIDTH), x_map),
                      pl.BlockSpec((1, D_MODEL, tf), w_up_map),
                      pl.BlockSpec((1, D_MODEL, tf), w_up_map),
                      pl.BlockSpec((1, tf, D_MODEL), w_down_map)],
            out_specs=pl.BlockSpec((tm, D_MODEL), lambda i, j, e, n: (i, 0)),
            scratch_shapes=[pltpu.VMEM((tm, D_MODEL), F32), pltpu.VMEM((tm, tf), BF16)],
        ),
        compiler_params=_cparams(("arbitrary", "arbitrary")),
        name="moe_experts",
    )(tile_expert, n_tiles, xs, w_gate, w_up, w_down)


def _combine_kernel(loc_ref, dst_ref, len_ref, x_ref, lpos_ref, ys_ref, out_ref, local_ref, sem):
    blk = pl.program_id(0)
    local_ref[...] = jnp.zeros_like(local_ref)

    def copy(loc, dst, size):
        return pltpu.make_async_copy(ys_ref.at[pl.ds(dst, size), :], local_ref.at[pl.ds(loc, size), :], sem)

    _for_each_run_piece(blk, loc_ref, dst_ref, len_ref, lambda *a: copy(*a).start())
    _for_each_run_piece(blk, loc_ref, dst_ref, len_ref, lambda *a: copy(*a).wait())
    rows = lax.broadcasted_iota(I32, (1, LOCAL_ROWS), 1)
    lpos = lpos_ref[...]
    sel = jnp.where(lpos[:, 0:1] == rows, 1.0, jnp.where(lpos[:, 1:2] == rows, 1.0, 0.0)).astype(BF16)
    out_ref[...] = x_ref[...] + _dot(sel, local_ref[...])


def _combine(runs, x2, lpos, ys):
    tok = lambda n: pl.BlockSpec((RANK_TILE, n), lambda b, *_: (b, 0))
    return pl.pallas_call(
        _combine_kernel,
        out_shape=jax.ShapeDtypeStruct(x2.shape, F32),
        grid_spec=pltpu.PrefetchScalarGridSpec(
            num_scalar_prefetch=3,
            grid=(x2.shape[0] // RANK_TILE,),
            in_specs=[tok(D_MODEL), tok(LANE), pl.BlockSpec(memory_space=pl.ANY)],
            out_specs=tok(D_MODEL),
            scratch_shapes=[pltpu.VMEM((LOCAL_ROWS, D_MODEL), BF16), pltpu.SemaphoreType.DMA],
        ),
        compiler_params=_cparams(("arbitrary",)),
        name="moe_combine",
    )(*runs, x2, lpos, ys)


def _moe_ffn(x2, g, w_router, w_gate_up, w_down):
    t = x2.shape[0]
    nblk = t // RANK_TILE
    hn, idx, gates = _router(x2, g, w_router)
    lpos, lpos_rows, tile_info, run_tab = _rank(idx)
    runs = tuple(run_tab.reshape(3, nblk, LANE)[:, :, :N_EXPERTS].reshape(3, -1))
    rows = t * TOP_K + nblk * N_EXPERTS * (RUN_ALIGN - 1) + N_EXPERTS * (MOE_ROW_TILE - 1)
    rows = -(-rows // MOE_ROW_TILE) * MOE_ROW_TILE
    assert rows // MOE_ROW_TILE <= LANE
    xs = _dispatch(runs, hn, gates, lpos_rows, rows)
    ys = _moe_experts(tile_info[:, 0], tile_info[:1, 1], xs,
                      w_gate_up[..., :D_FF].astype(BF16), w_gate_up[..., D_FF:].astype(BF16),
                      w_down.astype(BF16))
    return _combine(runs, x2, lpos, ys)


def _pad_heads(w, heads, dim, pad):
    w = w.reshape(w.shape[:-1] + (heads, dim))
    return _pad_lanes(w, pad).reshape(w.shape[:-2] + (heads * pad,))


def kernel(x, mem, positions, a_norm, a_w_in, a_gate_bias, a_head_norm, a_w_out, b_norm, b_w_in, b_q_a_norm, b_w_uq, b_q_head_norm, b_w_out, kv_norm, w_dkv, kv_a_norm, w_ukv, k_head_norm, mem_norm, mem_w_kv, mem_q_norm, mem_k_norm, ffn_norm, dense_w_gate_up, dense_w_down, moe_router, moe_w_gate_up, moe_w_down):
    nb, seq, _ = x.shape
    t = nb * seq
    x2 = x.reshape(t, D_MODEL)
    gmat = jnp.kron(jnp.eye(M_HEADS, dtype=F32), jnp.full((M_HEAD_DIM, M_HEAD_DIM), 1.0 / M_HEAD_DIM, F32)).astype(BF16)

    kbd0, vbd0 = _memory_kv(mem, mem_norm[0], mem_w_kv[0], mem_k_norm[0], gmat)
    w_in = a_w_in[0]
    qk_w, v_w = A_HEADS * A_QK_DIM, A_HEADS * A_V_DIM
    o0, o1, o2, o3, o4 = qk_w, 2 * qk_w, 2 * qk_w + v_w, 2 * qk_w + 2 * v_w, 2 * qk_w + 2 * v_w + 2 * A_HEADS
    w_main = jnp.concatenate([
        _pad_heads(w_in[:, :o0], A_HEADS, A_QK_DIM, A_QK_PAD),
        _pad_heads(w_in[:, o0:o1], A_HEADS, A_QK_DIM, A_QK_PAD),
        w_in[:, o4:]], axis=1).astype(BF16)
    w_vo_t = jnp.concatenate([
        _pad_heads(w_in[:, o1:o2], A_HEADS, A_V_DIM, A_V_PAD),
        _pad_heads(w_in[:, o2:o3], A_HEADS, A_V_DIM, A_V_PAD)], axis=1).T.astype(BF16)
    q, k, vt, ot, mq, gc, gr = _a_projection(x2, a_norm[0], w_main, w_vo_t, w_in[:, o3:o4], a_gate_bias[0])
    three = lambda a: a.reshape(nb, seq, a.shape[-1])
    head_g = _pad_heads(a_head_norm[0].reshape(1, -1), A_HEADS, A_V_DIM, A_V_PAD)
    hm = _mlstm(three(q), three(k), vt, ot, gc, gr,
                jnp.broadcast_to(head_g.reshape(-1, 1), (A_HEADS * A_V_PAD, MLSTM_CHUNK)))
    w_out = a_w_out[0]
    w_out_h = jnp.pad(w_out[:v_w].reshape(A_HEADS, A_V_DIM, D_MODEL), ((0, 0), (0, A_V_PAD - A_V_DIM), (0, 0)))
    w_out_h = w_out_h.reshape(A_HEADS * A_V_PAD, D_MODEL).astype(BF16)
    x2 = _mix_out(x2, hm.reshape(t, -1), mq, kbd0, vbd0, gmat, mem_q_norm[0],
                  w_out_h, w_out[v_w:].astype(BF16), seq)
    wgu = dense_w_gate_up[0]
    x2 = _dense_ffn(x2, ffn_norm[0], wgu[:, :D_FF].astype(BF16), wgu[:, D_FF:].astype(BF16),
                    dense_w_down[0].astype(BF16))

    cs, sn = _rope_tables(positions)
    k_sh, vt_sh = _latent_kv(x2, kv_norm, w_dkv, kv_a_norm, w_ukv, k_head_norm, cs, sn)

    kbd1, vbd1 = _memory_kv(mem, mem_norm[1], mem_w_kv[1], mem_k_norm[1], gmat)
    qh, mq1 = _b_projection(x2, b_norm[0], b_w_in[0], b_q_a_norm[0], b_w_uq[0], b_q_head_norm[0], cs, sn)
    att = _causal_attention(three(qh), three(k_sh), vt_sh, seq)
    w_out = b_w_out[0]
    n_att = B_HEADS * V_HEAD
    x2 = _mix_out(x2, att.reshape(t, -1), mq1, kbd1, vbd1, gmat, mem_q_norm[1],
                  w_out[:n_att].astype(BF16), w_out[n_att:].astype(BF16), seq)
    x2 = _moe_ffn(x2, ffn_norm[1], moe_router[0], moe_w_gate_up[0], moe_w_down[0])
    return x2.reshape(nb, seq, D_MODEL)
```

```python
import functools

import jax
import jax.numpy as jnp
from jax import lax
from jax.experimental import pallas as pl
from jax.experimental.pallas import tpu as pltpu

F32 = jnp.float32
BF16 = jnp.bfloat16
I32 = jnp.int32

EPS = 1e-6
LOG2E = 1.4426950408889634
LANE = 128
VMEM_LIMIT = 48 * 1024 * 1024

D_MODEL = 1024
N_MEM = 256
M_HEADS, M_HEAD_DIM = 4, 64
M_W = M_HEADS * M_HEAD_DIM
A_HEADS, A_QK_DIM, A_V_DIM = 4, 96, 192
A_QK_PAD, A_V_PAD = 128, 256
B_HEADS, Q_LORA, KV_LORA = 6, 384, 256
QK_NOPE, QK_ROPE, V_HEAD = 128, 64, 128
B_QK_HEAD = QK_NOPE + QK_ROPE
B_QK_PAD = 256
VT_HEAD_ROWS = V_HEAD + 16
ROPE_THETA = 10000.0
D_FF = 3584
N_EXPERTS, TOP_K = 8, 2
GATE_TERMS = 3

MLSTM_CHUNK = 256
ROW_TILE = 512
FFN_ROW_TILE = 1024
FFN_FF_TILE = 1792
FFN_CHUNK = 256
MOE_ROW_TILE = 512
ATT_Q_TILE = 512
ATT_K_TILE = 512
ATT_HEADS_PER_STEP = 6
RANK_TILE = 512
RUN_ALIGN = 16
RUN_PIECES = tuple(RANK_TILE >> s for s in range((RANK_TILE // RUN_ALIGN).bit_length()))
LOCAL_ROWS = TOP_K * RANK_TILE + N_EXPERTS * RUN_ALIGN
XS_WIDTH = D_MODEL + LANE

HIGHEST = lax.Precision.HIGHEST


def _cparams(sem):
    return pltpu.CompilerParams(dimension_semantics=sem, vmem_limit_bytes=VMEM_LIMIT)


def _rms(x, g):
    return x * lax.rsqrt(jnp.mean(x * x, axis=-1, keepdims=True) + EPS) * g


def _dot(a, b):
    return jnp.dot(a, b, preferred_element_type=F32)


def _dot_nt(a, b):
    return lax.dot_general(a, b, (((1,), (1,)), ((), ())), preferred_element_type=F32)


def _dot_tn(a, b):
    return lax.dot_general(a, b, (((0,), (0,)), ((), ())), preferred_element_type=F32)


def _group_mean_sq(x, gmat):
    sq = x * x
    hi = sq.astype(BF16)
    lo = (sq - hi.astype(F32)).astype(BF16)
    return _dot(hi, gmat) + _dot(lo, gmat)


def _memkv_kernel(mem_ref, g_ref, w_ref, kg_ref, gmat_ref, kbd_ref, vbd_ref):
    hn = _rms(mem_ref[0], g_ref[...]).astype(BF16)
    kv = _dot(hn, w_ref[...])
    k, v = kv[:, :M_W], kv[:, M_W:]
    kn = k * lax.rsqrt(_group_mean_sq(k, gmat_ref[...]) + EPS) * kg_ref[...]
    lane_head = lax.broadcasted_iota(I32, (1, M_W), 1) // M_HEAD_DIM
    for h in range(M_HEADS):
        keep = lane_head == h
        kbd_ref[0, h * N_MEM:(h + 1) * N_MEM, :] = jnp.where(keep, kn, 0.0).astype(BF16)
        vbd_ref[0, h * N_MEM:(h + 1) * N_MEM, :] = jnp.where(keep, v, 0.0).astype(BF16)


def _memory_kv(mem, g, w_kv, k_g, gmat):
    nb = mem.shape[0]
    out = jax.ShapeDtypeStruct((nb, M_HEADS * N_MEM, M_W), BF16)
    return pl.pallas_call(
        _memkv_kernel,
        out_shape=(out, out),
        grid=(nb,),
        in_specs=[
            pl.BlockSpec((1, N_MEM, D_MODEL), lambda b: (b, 0, 0)),
            pl.BlockSpec((1, D_MODEL), lambda b: (0, 0)),
            pl.BlockSpec((D_MODEL, 2 * M_W), lambda b: (0, 0)),
            pl.BlockSpec((1, M_W), lambda b: (0, 0)),
            pl.BlockSpec((M_W, M_W), lambda b: (0, 0)),
        ],
        out_specs=(pl.BlockSpec((1, M_HEADS * N_MEM, M_W), lambda b: (b, 0, 0)),) * 2,
        compiler_params=_cparams(("parallel",)),
        name="memory_kv",
    )(mem, g.reshape(1, -1), w_kv.astype(BF16), jnp.tile(k_g, M_HEADS).reshape(1, -1), gmat)


def _memory_attention(mq, kbd, vbd, gmat, qg):
    qn = mq * lax.rsqrt(_group_mean_sq(mq, gmat) + EPS) * (qg * (M_HEAD_DIM ** -0.5 * LOG2E))
    s = _dot_nt(qn.astype(BF16), kbd)
    ps = []
    for h in range(M_HEADS):
        sh = s[:, h * N_MEM:(h + 1) * N_MEM]
        e = jnp.exp2(sh - jnp.max(sh, axis=-1, keepdims=True))
        ps.append((e / jnp.sum(e, axis=-1, keepdims=True)).astype(BF16))
    return _dot(jnp.concatenate(ps, axis=-1), vbd)


def _a_proj_kernel(x_ref, g_ref, w_ref, wvot_ref, wif_ref, wift_ref, bc_ref, br_ref,
                   q_ref, k_ref, vt_ref, ot_ref, mq_ref, gc_ref, gr_ref):
    hn = _rms(x_ref[...], g_ref[...]).astype(BF16)
    nq = A_HEADS * A_QK_PAD
    nv = A_HEADS * A_V_PAD
    q_ref[...] = _dot(hn, w_ref[:, :nq]).astype(BF16)
    k_ref[...] = (_dot(hn, w_ref[:, nq:2 * nq]) * (A_QK_DIM ** -0.5)).astype(BF16)
    mq_ref[...] = _dot(hn, w_ref[:, 2 * nq:])
    vt = _dot_nt(wvot_ref[:nv, :], hn)
    ones_row = lax.broadcasted_iota(I32, (nv, 1), 0) % A_V_PAD == A_V_DIM
    vt_ref[...] = jnp.where(ones_row, 1.0, vt).astype(BF16)
    ot_ref[...] = _dot_nt(wvot_ref[nv:, :], hn)
    gc_ref[...] = _dot(hn, wif_ref[...])[:, :2 * A_HEADS] + bc_ref[...]
    gr_ref[...] = _dot_nt(wift_ref[...], hn) + br_ref[...]


def _a_projection(x2, g, w_main, w_vo_t, w_if, gate_bias):
    t = x2.shape[0]
    nq, nv = A_HEADS * A_QK_PAD, A_HEADS * A_V_PAD
    ng = 2 * A_HEADS
    wif_pad = jnp.pad(w_if, ((0, 0), (0, LANE - ng))).astype(BF16)
    row = lambda n: pl.BlockSpec((ROW_TILE, n), lambda i: (i, 0))
    col = lambda n: pl.BlockSpec((n, ROW_TILE), lambda i: (0, i))
    full = lambda a: pl.BlockSpec(a.shape, lambda i: (0,) * a.ndim)
    args = (x2, g.reshape(1, -1), w_main, w_vo_t, wif_pad, w_if.T.astype(BF16),
            gate_bias.reshape(1, ng), gate_bias.reshape(ng, 1))
    return pl.pallas_call(
        _a_proj_kernel,
        out_shape=(jax.ShapeDtypeStruct((t, nq), BF16), jax.ShapeDtypeStruct((t, nq), BF16),
                   jax.ShapeDtypeStruct((nv, t), BF16), jax.ShapeDtypeStruct((nv, t), F32),
                   jax.ShapeDtypeStruct((t, M_W), F32), jax.ShapeDtypeStruct((t, ng), F32),
                   jax.ShapeDtypeStruct((ng, t), F32)),
        grid=(t // ROW_TILE,),
        in_specs=[row(D_MODEL)] + [full(a) for a in args[1:]],
        out_specs=(row(nq), row(nq), col(nv), col(nv), row(M_W), row(ng), col(ng)),
        compiler_params=_cparams(("parallel",)),
        name="mlstm_in_proj",
    )(*args)


def _log_sigmoid(f):
    return jnp.minimum(f, 0.0) - jnp.log(1.0 + jnp.exp(-jnp.abs(f)))


def _dot_nt_highest(a, b):
    return lax.dot_general(a, b, (((1,), (1,)), ((), ())), precision=HIGHEST,
                           preferred_element_type=F32)


def _chunk_gates(gc, gr):
    L = gc.shape[0]
    r = lax.broadcasted_iota(I32, (L, L), 0)
    c = lax.broadcasted_iota(I32, (L, L), 1)
    lower = (c <= r).astype(F32)
    is_f_col = lax.broadcasted_iota(I32, gc.shape, 1) >= A_HEADS
    is_f_row = lax.broadcasted_iota(I32, gr.shape, 0) >= A_HEADS
    lf_c = jnp.where(is_f_col, _log_sigmoid(gc), 0.0)
    lf_r = jnp.where(is_f_row, _log_sigmoid(gr), 0.0)
    lf_c = jnp.concatenate([lf_c, jnp.zeros((L, LANE - gc.shape[1]), F32)], axis=1)
    cum_c = jnp.dot(lower, lf_c, precision=HIGHEST, preferred_element_type=F32)[:, :gc.shape[1]]
    cum_r = _dot_nt_highest(lf_r, lower)
    return jnp.where(is_f_col, cum_c, gc), jnp.where(is_f_row, cum_r, gr)


def _prefix_max_lanes(x):
    lane = lax.broadcasted_iota(I32, x.shape, 1)
    shift = 1
    while shift < x.shape[1]:
        x = jnp.maximum(x, jnp.where(lane >= shift, pltpu.roll(x, shift, axis=1), -jnp.inf))
        shift *= 2
    return x


def _mlstm_kernel(q_ref, k_ref, vt_ref, ot_ref, gc_ref, gr_ref, hg_ref, out_ref, c_ref, m_ref):
    L = MLSTM_CHUNK

    @pl.when(pl.program_id(1) == 0)
    def _():
        c_ref[...] = jnp.zeros_like(c_ref)
        m_ref[...] = jnp.zeros_like(m_ref)

    src = lax.broadcasted_iota(I32, (L, L), 0)
    tgt = lax.broadcasted_iota(I32, (L, L), 1)
    real = lax.broadcasted_iota(I32, (A_V_PAD, 1), 0) < A_V_DIM
    gcol, grow = _chunk_gates(gc_ref[...], gr_ref[...])
    for hd in range(A_HEADS):
        qk = slice(hd * A_QK_PAD, (hd + 1) * A_QK_PAD)
        vv = slice(hd * A_V_PAD, (hd + 1) * A_V_PAD)
        q, k, vt = q_ref[0, :, qk], k_ref[0, :, qk], vt_ref[vv, :]
        u_c = gcol[:, hd:hd + 1] - gcol[:, A_HEADS + hd:A_HEADS + hd + 1]
        g_r = grow[A_HEADS + hd:A_HEADS + hd + 1, :]
        u_r = grow[hd:hd + 1, :] - g_r
        g_last = g_r[:, L - 1:L]
        m_prev = m_ref[hd, 0:1, 0:1]
        c_prev = c_ref[hd]

        run_max = jnp.maximum(_prefix_max_lanes(jnp.broadcast_to(u_r, (8, L)))[0:1, :], m_prev)
        m_t = g_r + run_max
        inter = jnp.exp(m_prev - run_max)
        decay_t = jnp.where(src <= tgt, jnp.exp(u_c - run_max), 0.0)
        p_t = (decay_t * _dot_nt(k, q)).astype(BF16)
        num_t = inter * _dot_nt(c_prev.astype(BF16), q) + _dot(vt, p_t)
        den = num_t[A_V_DIM:A_V_DIM + 1, :]
        h_t = jnp.where(real, num_t / jnp.maximum(jnp.abs(den), jnp.exp(-m_t)), 0.0)
        scale = lax.rsqrt(jnp.sum(h_t * h_t, axis=0, keepdims=True) * (1.0 / A_V_DIM) + EPS)
        out_t = h_t * scale * hg_ref[vv, :] * jax.nn.sigmoid(ot_ref[vv, :])
        out_ref[0, :, vv] = out_t.T.astype(BF16)

        w_r = g_last + u_r
        m_new = jnp.maximum(g_last + m_prev, jnp.max(w_r, axis=-1, keepdims=True))
        ev_t = (jnp.exp(w_r - m_new) * vt.astype(F32)).astype(BF16)
        c_ref[hd] = jnp.exp(g_last + m_prev - m_new) * c_prev + _dot(ev_t, k)
        m_ref[hd] = jnp.broadcast_to(m_new, m_ref.shape[1:])


def _mlstm(q, k, vt, ot, gcol, grow, head_g):
    nb, s, _ = q.shape
    L = MLSTM_CHUNK
    nv = A_HEADS * A_V_PAD
    blk = lambda w: pl.BlockSpec((1, L, w), lambda b, c: (b, c, 0))
    col = lambda n: pl.BlockSpec((n, L), lambda b, c: (0, b * (s // L) + c))
    return pl.pallas_call(
        _mlstm_kernel,
        out_shape=jax.ShapeDtypeStruct((nb, s, nv), BF16),
        grid=(nb, s // L),
        in_specs=[blk(A_HEADS * A_QK_PAD), blk(A_HEADS * A_QK_PAD), col(nv), col(nv),
                  pl.BlockSpec((L, 2 * A_HEADS), lambda b, c: (b * (s // L) + c, 0)), col(2 * A_HEADS),
                  pl.BlockSpec((nv, L), lambda b, c: (0, 0))],
        out_specs=blk(nv),
        scratch_shapes=[pltpu.VMEM((A_HEADS, A_V_PAD, A_QK_PAD), F32), pltpu.VMEM((A_HEADS, 8, LANE), F32)],
        compiler_params=_cparams(("parallel", "arbitrary")),
        name="mlstm_chunkwise",
    )(q, k, vt, ot, gcol, grow, head_g)


def _mix_out_kernel(x_ref, h_ref, mq_ref, kbd_ref, vbd_ref, gmat_ref, qg_ref, w1_ref, w2_ref, *rest):
    out_ref = rest[0] if len(rest) == 1 else rest[2]
    mo = _memory_attention(mq_ref[...], kbd_ref[0], vbd_ref[0], gmat_ref[...], qg_ref[...])
    x_new = x_ref[...] + _dot(h_ref[...], w1_ref[...]) + _dot(mo.astype(BF16), w2_ref[...])
    out_ref[...] = x_new
    if len(rest) > 1:
        _router_body(x_new, rest[0], rest[1], *rest[3:])


def _mix_out(x2, h2, mq, kbd, vbd, gmat, qg, w_main, w_mem, seq, router=None):
    t = x2.shape[0]
    tm = ROW_TILE
    row = lambda n: pl.BlockSpec((tm, n), lambda i: (i, 0))
    full = lambda a: pl.BlockSpec(a.shape, lambda i: (0,) * a.ndim)
    per_batch = pl.BlockSpec((1,) + kbd.shape[1:], lambda i: ((i * tm) // seq, 0, 0))
    qg_t = jnp.tile(qg, M_HEADS).reshape(1, -1)
    args = [x2, h2, mq, kbd, vbd, gmat, qg_t, w_main, w_mem]
    in_specs = [row(D_MODEL), row(h2.shape[1]), row(M_W), per_batch, per_batch,
                full(gmat), full(qg_t), full(w_main), full(w_mem)]
    out_shape = [jax.ShapeDtypeStruct((t, D_MODEL), F32)]
    out_specs = [row(D_MODEL)]
    if router is not None:
        extra = (router[0].reshape(1, -1), _pad_lanes(router[1]))
        args += extra
        in_specs += [full(a) for a in extra]
        assert tm == RANK_TILE
        out_shape += [jax.ShapeDtypeStruct((t, D_MODEL), BF16), jax.ShapeDtypeStruct((t, LANE), I32),
                      jax.ShapeDtypeStruct((t, LANE), BF16), jax.ShapeDtypeStruct((8 * (t // tm), LANE), F32)]
        out_specs += [row(D_MODEL), row(LANE), row(LANE), pl.BlockSpec((8, LANE), lambda i: (i, 0))]
    out = pl.pallas_call(
        _mix_out_kernel,
        out_shape=tuple(out_shape),
        grid=(t // tm,),
        in_specs=in_specs,
        out_specs=tuple(out_specs),
        compiler_params=_cparams(("parallel",)),
        name="mixer_out_proj",
    )(*args)
    return out[0] if router is None else out


def _swiglu_accumulate(acc_ref, h_ref, x, wg_ref, wu_ref, wd_ref):
    for c in range(wg_ref.shape[-1] // FFN_CHUNK):
        cols = slice(c * FFN_CHUNK, (c + 1) * FFN_CHUNK)
        gate = _dot(x, wg_ref[:, cols])
        up = _dot(x, wu_ref[:, cols])
        h_ref[:, cols] = (jax.nn.silu(gate) * up).astype(BF16)
    acc_ref[...] += _dot(h_ref[...], wd_ref[...])


def _ffn_kernel(x_ref, g_ref, wg_ref, wu_ref, wd_ref, out_ref, hn_ref, acc_ref, h_ref):
    j = pl.program_id(1)

    @pl.when(j == 0)
    def _():
        hn_ref[...] = _rms(x_ref[...], g_ref[...]).astype(BF16)
        acc_ref[...] = jnp.zeros_like(acc_ref)

    _swiglu_accumulate(acc_ref, h_ref, hn_ref[...], wg_ref, wu_ref, wd_ref)

    @pl.when(j == pl.num_programs(1) - 1)
    def _():
        out_ref[...] = x_ref[...] + acc_ref[...]


def _dense_ffn(x2, g, w_gate, w_up, w_down):
    t = x2.shape[0]
    tm, tf = FFN_ROW_TILE, FFN_FF_TILE
    return pl.pallas_call(
        _ffn_kernel,
        out_shape=jax.ShapeDtypeStruct((t, D_MODEL), F32),
        grid=(t // tm, D_FF // tf),
        in_specs=[pl.BlockSpec((tm, D_MODEL), lambda i, j: (i, 0)),
                  pl.BlockSpec((1, D_MODEL), lambda i, j: (0, 0)),
                  pl.BlockSpec((D_MODEL, tf), lambda i, j: (0, j)),
                  pl.BlockSpec((D_MODEL, tf), lambda i, j: (0, j)),
                  pl.BlockSpec((tf, D_MODEL), lambda i, j: (j, 0))],
        out_specs=pl.BlockSpec((tm, D_MODEL), lambda i, j: (i, 0)),
        scratch_shapes=[pltpu.VMEM((tm, D_MODEL), BF16), pltpu.VMEM((tm, D_MODEL), F32),
                        pltpu.VMEM((tm, tf), BF16)],
        compiler_params=_cparams(("parallel", "arbitrary")),
        name="dense_swiglu",
    )(x2, g.reshape(1, -1), w_gate, w_up, w_down)


def _rope_kernel(pos_ref, inv_ref, sign_ref, cs_ref, sn_ref):
    ang = pos_ref[...].astype(F32) * inv_ref[...]
    cs_ref[...] = jnp.cos(ang)
    sn_ref[...] = jnp.sin(ang) * sign_ref[...]


def _rope_tables(positions):
    t = positions.size
    half = QK_ROPE // 2
    inv = 1.0 / (ROPE_THETA ** (jnp.arange(0, QK_ROPE, 2, dtype=F32) / QK_ROPE))
    pad = jnp.zeros((LANE - QK_ROPE,), F32)
    inv_l = jnp.concatenate([inv, inv, pad]).reshape(1, LANE)
    sign = jnp.concatenate([-jnp.ones((half,), F32), jnp.ones((half,), F32), pad]).reshape(1, LANE)
    out = jax.ShapeDtypeStruct((t, LANE), F32)
    return pl.pallas_call(
        _rope_kernel,
        out_shape=(out, out),
        grid=(t // ROW_TILE,),
        in_specs=[pl.BlockSpec((ROW_TILE, 1), lambda i: (i, 0)),
                  pl.BlockSpec((1, LANE), lambda i: (0, 0)),
                  pl.BlockSpec((1, LANE), lambda i: (0, 0))],
        out_specs=(pl.BlockSpec((ROW_TILE, LANE), lambda i: (i, 0)),) * 2,
        compiler_params=_cparams(("parallel",)),
        name="rope_tables",
    )(positions.reshape(t, 1), inv_l, sign)


def _head_qk_norm_rope(nope, rope, rope_sw, g_nope, g_rope, g_rope_sw, cs, sn, scale):
    ss = jnp.sum(nope * nope, axis=-1, keepdims=True) + jnp.sum(rope * rope, axis=-1, keepdims=True)
    r = lax.rsqrt(ss * (1.0 / B_QK_HEAD) + EPS) * scale
    return nope * r * g_nope, (rope * g_rope * cs + rope_sw * g_rope_sw * sn) * r


def _latent_kv_body(hn, wd_ref, ga_ref, wuk_ref, wuvt_ref, kg_ref, cs_ref, sn_ref, k_ref, vt_ref):
    z = _dot(hn, wd_ref[...])
    c_kv = z[:, :KV_LORA]
    rope, rope_sw = z[:, KV_LORA:KV_LORA + LANE], z[:, KV_LORA + LANE:]
    cn = _rms(c_kv, ga_ref[...]).astype(BF16)
    kv = _dot(cn, wuk_ref[...])
    vt = _dot_nt(wuvt_ref[...], cn)
    tm = vt.shape[1]
    ones_row = (lax.broadcasted_iota(I32, (VT_HEAD_ROWS - V_HEAD, tm), 0) == 0).astype(BF16)
    for h in range(B_HEADS):
        vt_ref[h * VT_HEAD_ROWS:h * VT_HEAD_ROWS + V_HEAD, :] = vt[h * V_HEAD:(h + 1) * V_HEAD].astype(BF16)
        vt_ref[h * VT_HEAD_ROWS + V_HEAD:(h + 1) * VT_HEAD_ROWS, :] = ones_row
    kg = kg_ref[...]
    for h in range(B_HEADS):
        kn, kr = _head_qk_norm_rope(kv[:, h * QK_NOPE:(h + 1) * QK_NOPE], rope, rope_sw,
                                    kg[:, :LANE], kg[:, LANE:2 * LANE], kg[:, 2 * LANE:],
                                    cs_ref[...], sn_ref[...], 1.0)
        k_ref[:, h * B_QK_PAD:h * B_QK_PAD + QK_NOPE] = kn.astype(BF16)
        k_ref[:, h * B_QK_PAD + QK_NOPE:(h + 1) * B_QK_PAD] = kr.astype(BF16)


def _rope_swap(w):
    half = QK_ROPE // 2
    return jnp.concatenate([w[..., half:], w[..., :half]], axis=-1)


def _pad_lanes(w, n=LANE):
    return jnp.pad(w, [(0, 0)] * (w.ndim - 1) + [(0, n - w.shape[-1])])


def _head_gain(g):
    g_rope = g[QK_NOPE:]
    return jnp.concatenate([g[:QK_NOPE], _pad_lanes(g_rope), _pad_lanes(_rope_swap(g_rope))]).reshape(1, -1)


def _query_body(hn, win_ref, ga_ref, wuq_ref, qg_ref, cs_ref, sn_ref, q_ref, mq_ref):
    proj = _dot(hn, win_ref[...])
    mq_ref[...] = proj[:, Q_LORA:]
    qall = _dot(_rms(proj[:, :Q_LORA], ga_ref[...]).astype(BF16), wuq_ref[...])
    qg = qg_ref[...]
    per_head = QK_NOPE + 2 * LANE
    for h in range(B_HEADS):
        base = h * per_head
        qn, qr = _head_qk_norm_rope(qall[:, base:base + QK_NOPE],
                                    qall[:, base + QK_NOPE:base + QK_NOPE + LANE],
                                    qall[:, base + QK_NOPE + LANE:base + per_head],
                                    qg[:, :LANE], qg[:, LANE:2 * LANE], qg[:, 2 * LANE:],
                                    cs_ref[...], sn_ref[...], B_QK_HEAD ** -0.5)
        q_ref[:, h * B_QK_PAD:h * B_QK_PAD + QK_NOPE] = qn.astype(BF16)
        q_ref[:, h * B_QK_PAD + QK_NOPE:(h + 1) * B_QK_PAD] = qr.astype(BF16)


def _mla_proj_kernel(x_ref, gkv_ref, wd_ref, ga_ref, wuk_ref, wuvt_ref, kg_ref,
                     gq_ref, win_ref, gqa_ref, wuq_ref, qg_ref, cs_ref, sn_ref,
                     k_ref, vt_ref, q_ref, mq_ref):
    x = x_ref[...]
    xn = x * lax.rsqrt(jnp.mean(x * x, axis=-1, keepdims=True) + EPS)
    _latent_kv_body((xn * gkv_ref[...]).astype(BF16), wd_ref, ga_ref, wuk_ref, wuvt_ref, kg_ref,
                    cs_ref, sn_ref, k_ref, vt_ref)
    _query_body((xn * gq_ref[...]).astype(BF16), win_ref, gqa_ref, wuq_ref, qg_ref, cs_ref, sn_ref, q_ref, mq_ref)


def _mla_projection(x2, kv_norm, w_dkv, kv_a_norm, w_ukv, k_head_norm,
                    q_norm, w_in, q_a_g, w_uq, q_head_g, cs, sn):
    t = x2.shape[0]
    w_rope = w_dkv[:, KV_LORA:]
    wd = jnp.concatenate([w_dkv[:, :KV_LORA], _pad_lanes(w_rope), _pad_lanes(_rope_swap(w_rope))],
                         axis=1).astype(BF16)
    wu = w_ukv.reshape(KV_LORA, B_HEADS, QK_NOPE + V_HEAD)
    wuk = wu[:, :, :QK_NOPE].reshape(KV_LORA, -1).astype(BF16)
    wuvt = wu[:, :, QK_NOPE:].reshape(KV_LORA, -1).T.astype(BF16)
    wq = w_uq.reshape(Q_LORA, B_HEADS, B_QK_HEAD)
    wq_rope = wq[:, :, QK_NOPE:]
    wq = jnp.concatenate([wq[:, :, :QK_NOPE], _pad_lanes(wq_rope), _pad_lanes(_rope_swap(wq_rope))],
                         axis=-1).reshape(Q_LORA, -1).astype(BF16)
    consts = (kv_norm.reshape(1, -1), wd, kv_a_norm.reshape(1, -1), wuk, wuvt, _head_gain(k_head_norm),
              q_norm.reshape(1, -1), w_in.astype(BF16), q_a_g.reshape(1, -1), wq, _head_gain(q_head_g))
    row = lambda n: pl.BlockSpec((ROW_TILE, n), lambda i: (i, 0))
    full = lambda a: pl.BlockSpec(a.shape, lambda i: (0,) * a.ndim)
    return pl.pallas_call(
        _mla_proj_kernel,
        out_shape=(jax.ShapeDtypeStruct((t, B_HEADS * B_QK_PAD), BF16),
                   jax.ShapeDtypeStruct((B_HEADS * VT_HEAD_ROWS, t), BF16),
                   jax.ShapeDtypeStruct((t, B_HEADS * B_QK_PAD), BF16),
                   jax.ShapeDtypeStruct((t, M_W), F32)),
        grid=(t // ROW_TILE,),
        in_specs=[row(D_MODEL)] + [full(a) for a in consts] + [row(LANE), row(LANE)],
        out_specs=(row(B_HEADS * B_QK_PAD), pl.BlockSpec((B_HEADS * VT_HEAD_ROWS, ROW_TILE), lambda i: (0, i)),
                   row(B_HEADS * B_QK_PAD), row(M_W)),
        compiler_params=_cparams(("parallel",)),
        name="mla_proj",
    )(x2, *consts, cs, sn)


def _attn_kernel(q_ref, k_ref, vt_ref, out_ref):
    tq, tk = ATT_Q_TILE, ATT_K_TILE
    i = pl.program_id(2)

    def block(h, j, carry, masked):
        m, acc = carry
        off = pl.multiple_of(j * tk, tk)
        q = q_ref[0, :, h * B_QK_PAD:(h + 1) * B_QK_PAD]
        st = _dot_nt(k_ref[0, pl.ds(off, tk), h * B_QK_PAD:(h + 1) * B_QK_PAD], q)
        if masked:
            key = lax.broadcasted_iota(I32, (tk, tq), 0)
            qry = lax.broadcasted_iota(I32, (tk, tq), 1)
            st = jnp.where(key <= qry, st, -jnp.inf)
        m_new = jnp.maximum(m, jnp.max(st, axis=0, keepdims=True))
        p = jnp.exp(st - m_new).astype(BF16)
        vt = vt_ref[h * VT_HEAD_ROWS:(h + 1) * VT_HEAD_ROWS, pl.ds(off, tk)]
        return m_new, jnp.exp(m - m_new) * acc + _dot(vt, p)

    heads = range(ATT_HEADS_PER_STEP)
    init = tuple((jnp.full((1, tq), -jnp.inf, F32), jnp.zeros((VT_HEAD_ROWS, tq), F32)) for _ in heads)
    carry = lax.fori_loop(0, i, lambda j, c: tuple(block(h, j, c[h], False) for h in heads), init)
    for h in heads:
        _, acc = block(h, i, carry[h], True)
        out_t = acc[:V_HEAD] / acc[V_HEAD:V_HEAD + 1]
        out_ref[0, :, h * V_HEAD:(h + 1) * V_HEAD] = out_t.T.astype(BF16)


def _causal_attention(q, k, vt, seq):
    nb = q.shape[0]
    g = ATT_HEADS_PER_STEP
    assert ATT_Q_TILE == ATT_K_TILE and B_HEADS % g == 0
    return pl.pallas_call(
        _attn_kernel,
        out_shape=jax.ShapeDtypeStruct((nb, seq, B_HEADS * V_HEAD), BF16),
        grid=(nb, B_HEADS // g, seq // ATT_Q_TILE),
        in_specs=[pl.BlockSpec((1, ATT_Q_TILE, g * B_QK_PAD), lambda b, h, i: (b, i, h)),
                  pl.BlockSpec((1, seq, g * B_QK_PAD), lambda b, h, i: (b, 0, h)),
                  pl.BlockSpec((g * VT_HEAD_ROWS, seq), lambda b, h, i: (h, b))],
        out_specs=pl.BlockSpec((1, ATT_Q_TILE, g * V_HEAD), lambda b, h, i: (b, i, h)),
        compiler_params=_cparams(("parallel", "parallel", "arbitrary")),
        name="causal_attention",
    )(q, k, vt)


def _router_body(x, g_ref, wr_ref, hn_ref, idx_ref, gate_ref, count_ref):
    hn = _rms(x, g_ref[...])
    hn_ref[...] = hn.astype(BF16)
    logits = jnp.dot(hn, wr_ref[...], precision=HIGHEST, preferred_element_type=F32)
    lane = lax.broadcasted_iota(I32, logits.shape, 1)
    logits = jnp.where(lane < N_EXPERTS, logits, -jnp.inf)
    v1 = jnp.max(logits, axis=-1, keepdims=True)
    i1 = jnp.min(jnp.where(logits == v1, lane, LANE), axis=-1, keepdims=True)
    rest = jnp.where(lane == i1, -jnp.inf, logits)
    v2 = jnp.max(rest, axis=-1, keepdims=True)
    i2 = jnp.min(jnp.where(rest == v2, lane, LANE), axis=-1, keepdims=True)
    e2 = jnp.exp(v2 - v1)
    den = 1.0 + e2
    idx_ref[...] = jnp.where(lane == 0, i1, jnp.where(lane == 1, i2, 0))
    record = jnp.zeros(logits.shape, F32)
    for k, gate in enumerate((1.0 / den, e2 / den)):
        hi = gate.astype(BF16).astype(F32)
        mid = (gate - hi).astype(BF16).astype(F32)
        for part, term in enumerate((hi, mid, gate - hi - mid)):
            record = jnp.where(lane == GATE_TERMS * k + part, term, record)
    record = jnp.where(lane == 2 * GATE_TERMS, i1.astype(F32), record)
    record = jnp.where(lane == 2 * GATE_TERMS + 1, i2.astype(F32), record)
    gate_ref[...] = record.astype(BF16)
    pairs = (lane == i1).astype(F32) + (lane == i2).astype(F32)
    count_ref[...] = jnp.broadcast_to(jnp.sum(pairs, axis=0, keepdims=True), count_ref.shape)


def _aligned(count):
    return jnp.ceil(count * (1.0 / RUN_ALIGN)) * RUN_ALIGN


def _rank_kernel(idx_ref, counts_ref, lpos_ref, lposr_ref, tile_ref, runs_ref, run_ref, start_ref):
    blk = pl.program_id(0)
    tb = RANK_TILE
    lane = lax.broadcasted_iota(I32, (tb, LANE), 1)
    idx = idx_ref[...]
    oh0 = (lane == idx[:, 0:1]).astype(F32)
    oh1 = (lane == idx[:, 1:2]).astype(F32)
    both = oh0 + oh1
    run_len = _aligned(jnp.sum(both, axis=0, keepdims=True))
    r = lax.broadcasted_iota(I32, (LANE, LANE), 0)
    c = lax.broadcasted_iota(I32, (LANE, LANE), 1)
    before = (r < c).astype(F32)

    @pl.when(blk == 0)
    def _():
        sizes = jnp.sum(_aligned(counts_ref[...]), axis=0, keepdims=True) * 0.125
        tiles = jnp.ceil(sizes * (1.0 / MOE_ROW_TILE))
        tile_start = jnp.dot(tiles, before, precision=HIGHEST, preferred_element_type=F32)
        start_ref[...] = tile_start * MOE_ROW_TILE
        tile_end = tile_start + tiles
        n_col = r.astype(F32)
        ended = ((n_col >= tile_end) & (c < N_EXPERTS)).astype(F32)
        expert = jnp.sum(ended, axis=-1, keepdims=True)
        total = jnp.max(tile_end, axis=-1, keepdims=True)
        as_column = lambda row: jnp.sum(jnp.where(r == c, row, 0.0), axis=-1, keepdims=True)
        tail_start = as_column(start_ref[...] + sizes)
        tail_len = as_column(tiles * MOE_ROW_TILE - sizes)
        col = lax.broadcasted_iota(I32, tile_ref.shape, 1)
        tile_ref[...] = jnp.where(col == 0, expert, jnp.where(col == 1, total, jnp.where(
            col == 2, tail_start, tail_len))).astype(I32)
        run_ref[...] = jnp.zeros_like(run_ref)
        runs_ref[...] = jnp.zeros_like(runs_ref)

    rr = lax.broadcasted_iota(I32, (tb, tb), 0)
    cc = lax.broadcasted_iota(I32, (tb, tb), 1)
    strict = (cc < rr).astype(BF16)
    local = jnp.dot(run_len, before, precision=HIGHEST, preferred_element_type=F32)
    base = _dot(strict, both.astype(BF16)) + local
    p0 = jnp.sum(oh0 * base, axis=-1, keepdims=True)
    p1 = jnp.sum(oh1 * base, axis=-1, keepdims=True)
    posf = jnp.where(lane == 0, p0, jnp.where(lane == 1, p1, 0.0))
    lpos_ref[...] = posf.astype(I32)
    pick = (lax.broadcasted_iota(I32, (8, LANE), 0) == lax.broadcasted_iota(I32, (8, LANE), 1)).astype(F32)
    lposr_ref[...] = _dot_nt_highest(pick, posf).astype(I32)
    n = runs_ref.shape[0] // 3
    mine = lax.broadcasted_iota(I32, (n, LANE), 0) == blk
    for k, value in enumerate((local, run_ref[...] + start_ref[...], run_len)):
        runs_ref[k * n:(k + 1) * n, :] = jnp.where(mine, value.astype(I32), runs_ref[k * n:(k + 1) * n, :])
    run_ref[...] += run_len


def _rank(idx, counts):
    t = idx.shape[0]
    nblk = t // RANK_TILE
    const = lambda shape: pl.BlockSpec(shape, lambda i: (0, 0))
    return pl.pallas_call(
        _rank_kernel,
        out_shape=(jax.ShapeDtypeStruct((t, LANE), I32), jax.ShapeDtypeStruct((8, t), I32),
                   jax.ShapeDtypeStruct((LANE, 4), I32), jax.ShapeDtypeStruct((3 * nblk, LANE), I32)),
        grid=(nblk,),
        in_specs=[pl.BlockSpec((RANK_TILE, LANE), lambda i: (i, 0)), const(counts.shape)],
        out_specs=(pl.BlockSpec((RANK_TILE, LANE), lambda i: (i, 0)),
                   pl.BlockSpec((8, RANK_TILE), lambda i: (0, i)),
                   const((LANE, 4)), const((3 * nblk, LANE))),
        scratch_shapes=[pltpu.VMEM((1, LANE), F32), pltpu.VMEM((1, LANE), F32)],
        compiler_params=_cparams(("arbitrary",)),
        name="moe_rank",
    )(idx, counts)


def _for_each_piece(length, fn):
    for size in RUN_PIECES:
        @pl.when((length & size) != 0)
        def _(size=size):
            fn(pl.multiple_of(length & (-2 * size), RUN_ALIGN), size)


def _for_each_run_piece(blk, loc_ref, dst_ref, len_ref, fn):
    for e in range(N_EXPERTS):
        k = blk * N_EXPERTS + e
        loc, dst = loc_ref[k], dst_ref[k]
        _for_each_piece(len_ref[k], lambda done, size, loc=loc, dst=dst: fn(
            pl.multiple_of(loc + done, RUN_ALIGN), pl.multiple_of(dst + done, RUN_ALIGN), size))


def _dispatch_kernel(loc_ref, dst_ref, len_ref, tail_ref, tail_len_ref, hn_ref, gate_ref, lposr_ref,
                     xs_ref, local_ref, zero_ref, sem):
    blk = pl.program_id(0)

    @pl.when(blk == 0)
    def _():
        zero_ref[...] = jnp.zeros_like(zero_ref)

        def fill(e):
            start = tail_ref[e]
            return lambda done, size: pltpu.make_async_copy(
                zero_ref.at[pl.ds(0, size), :], xs_ref.at[pl.ds(pl.multiple_of(start + done, RUN_ALIGN), size), :], sem)

        for e in range(N_EXPERTS):
            _for_each_piece(tail_len_ref[e], lambda *a, e=e: fill(e)(*a).start())
        for e in range(N_EXPERTS):
            _for_each_piece(tail_len_ref[e], lambda *a, e=e: fill(e)(*a).wait())

    rows = lax.broadcasted_iota(I32, (LOCAL_ROWS, 1), 0)
    sel = jnp.where(lposr_ref[0:1, :] == rows, 1.0, jnp.where(lposr_ref[1:2, :] == rows, 1.0, 0.0)).astype(BF16)
    local_ref[:, :D_MODEL] = _dot(sel, hn_ref[...]).astype(BF16)
    local_ref[:, D_MODEL:] = _dot(sel, gate_ref[...]).astype(BF16)

    def copy(loc, dst, size):
        return pltpu.make_async_copy(local_ref.at[pl.ds(loc, size), :], xs_ref.at[pl.ds(dst, size), :], sem)

    _for_each_run_piece(blk, loc_ref, dst_ref, len_ref, lambda *a: copy(*a).start())
    _for_each_run_piece(blk, loc_ref, dst_ref, len_ref, lambda *a: copy(*a).wait())


def _dispatch(runs, tails, hn, gates, lpos_rows, rows):
    nblk = hn.shape[0] // RANK_TILE
    tok = lambda n: pl.BlockSpec((RANK_TILE, n), lambda b, *_: (b, 0))
    return pl.pallas_call(
        _dispatch_kernel,
        out_shape=jax.ShapeDtypeStruct((rows, XS_WIDTH), BF16),
        grid_spec=pltpu.PrefetchScalarGridSpec(
            num_scalar_prefetch=5,
            grid=(nblk,),
            in_specs=[tok(D_MODEL), tok(LANE), pl.BlockSpec((8, RANK_TILE), lambda b, *_: (0, b))],
            out_specs=pl.BlockSpec(memory_space=pl.ANY),
            scratch_shapes=[pltpu.VMEM((LOCAL_ROWS, XS_WIDTH), BF16), pltpu.VMEM((RANK_TILE, XS_WIDTH), BF16),
                            pltpu.SemaphoreType.DMA],
        ),
        compiler_params=_cparams(("arbitrary",)),
        name="moe_dispatch",
    )(*runs, *tails, hn, gates, lpos_rows)


def _moe_kernel(expert_ref, ntiles_ref, x_ref, wg_ref, wu_ref, wd_ref, out_ref, acc_ref, h_ref):
    i, j = pl.program_id(0), pl.program_id(1)
    active = i < ntiles_ref[0]

    @pl.when(j == 0)
    def _():
        acc_ref[...] = jnp.zeros_like(acc_ref)

    @pl.when(active)
    def _():
        _swiglu_accumulate(acc_ref, h_ref, x_ref[:, :D_MODEL], wg_ref.at[0], wu_ref.at[0], wd_ref.at[0])

    @pl.when(j == pl.num_programs(1) - 1)
    def _():
        gs = x_ref[:, D_MODEL:].astype(F32)
        gate = [sum(gs[:, GATE_TERMS * k + n:GATE_TERMS * k + n + 1] for n in range(GATE_TERMS)) for k in range(TOP_K)]
        first = gs[:, TOP_K * GATE_TERMS:TOP_K * GATE_TERMS + 1] == expert_ref[i].astype(F32)
        row_gate = jnp.where(first, gate[0], gate[1])
        out_ref[...] = jnp.where(active, acc_ref[...] * row_gate, 0.0).astype(BF16)


def _moe_experts(tile_expert, n_tiles, xs, w_gate, w_up, w_down):
    rows = xs.shape[0]
    tm, tf = MOE_ROW_TILE, FFN_FF_TILE
    nj = D_FF // tf

    def x_map(i, j, e_ref, n_ref):
        return jnp.clip(i, 0, jnp.maximum(n_ref[0] - 1, 0)), 0

    def w_idx(i, j, expert_ref, ntiles_ref):
        e = jnp.minimum(expert_ref[i], N_EXPERTS - 1)
        return e, jnp.where(i < ntiles_ref[0], j, nj - 1)

    def w_up_map(i, j, e_ref, n_ref):
        e, jj = w_idx(i, j, e_ref, n_ref)
        return e, 0, jj

    def w_down_map(i, j, e_ref, n_ref):
        e, jj = w_idx(i, j, e_ref, n_ref)
        return e, jj, 0

    return pl.pallas_call(
        _moe_kernel,
        out_shape=jax.ShapeDtypeStruct((rows, D_MODEL), BF16),
        grid_spec=pltpu.PrefetchScalarGridSpec(
            num_scalar_prefetch=2,
            grid=(rows // tm, nj),
            in_specs=[pl.BlockSpec((tm, XS_WIDTH), x_map),
                      pl.BlockSpec((1, D_MODEL, tf), w_up_map),
                      pl.BlockSpec((1, D_MODEL, tf), w_up_map),
                      pl.BlockSpec((1, tf, D_MODEL), w_down_map)],
            out_specs=pl.BlockSpec((tm, D_MODEL), lambda i, j, e, n: (i, 0)),
            scratch_shapes=[pltpu.VMEM((tm, D_MODEL), F32), pltpu.VMEM((tm, tf), BF16)],
        ),
        compiler_params=_cparams(("arbitrary", "arbitrary")),
        name="moe_experts",
    )(tile_expert, n_tiles, xs, w_gate, w_up, w_down)


def _combine_kernel(loc_ref, dst_ref, len_ref, x_ref, lpos_ref, ys_ref, out_ref, local_ref, sem):
    blk = pl.program_id(0)
    local_ref[...] = jnp.zeros_like(local_ref)

    def copy(loc, dst, size):
        return pltpu.make_async_copy(ys_ref.at[pl.ds(dst, size), :], local_ref.at[pl.ds(loc, size), :], sem)

    _for_each_run_piece(blk, loc_ref, dst_ref, len_ref, lambda *a: copy(*a).start())
    _for_each_run_piece(blk, loc_ref, dst_ref, len_ref, lambda *a: copy(*a).wait())
    rows = lax.broadcasted_iota(I32, (1, LOCAL_ROWS), 1)
    lpos = lpos_ref[...]
    sel = jnp.where(lpos[:, 0:1] == rows, 1.0, jnp.where(lpos[:, 1:2] == rows, 1.0, 0.0)).astype(BF16)
    out_ref[...] = x_ref[...] + _dot(sel, local_ref[...])


def _combine(runs, x2, lpos, ys):
    tok = lambda n: pl.BlockSpec((RANK_TILE, n), lambda b, *_: (b, 0))
    return pl.pallas_call(
        _combine_kernel,
        out_shape=jax.ShapeDtypeStruct(x2.shape, F32),
        grid_spec=pltpu.PrefetchScalarGridSpec(
            num_scalar_prefetch=3,
            grid=(x2.shape[0] // RANK_TILE,),
            in_specs=[tok(D_MODEL), tok(LANE), pl.BlockSpec(memory_space=pl.ANY)],
            out_specs=tok(D_MODEL),
            scratch_shapes=[pltpu.VMEM((LOCAL_ROWS, D_MODEL), BF16), pltpu.SemaphoreType.DMA],
        ),
        compiler_params=_cparams(("arbitrary",)),
        name="moe_combine",
    )(*runs, x2, lpos, ys)


def _moe_ffn(x2, hn, idx, gates, counts, w_gate_up, w_down):
    t = x2.shape[0]
    nblk = t // RANK_TILE
    lpos, lpos_rows, tile_info, run_tab = _rank(idx, counts)
    runs = tuple(run_tab.reshape(3, nblk, LANE)[:, :, :N_EXPERTS].reshape(3, -1))
    rows = t * TOP_K + nblk * N_EXPERTS * (RUN_ALIGN - 1) + N_EXPERTS * (MOE_ROW_TILE - 1)
    rows = -(-rows // MOE_ROW_TILE) * MOE_ROW_TILE
    assert rows // MOE_ROW_TILE <= LANE
    xs = _dispatch(runs, (tile_info[:, 2], tile_info[:, 3]), hn, gates, lpos_rows, rows)
    ys = _moe_experts(tile_info[:, 0], tile_info[:1, 1], xs,
                      w_gate_up[..., :D_FF].astype(BF16), w_gate_up[..., D_FF:].astype(BF16),
                      w_down.astype(BF16))
    return _combine(runs, x2, lpos, ys)


def _pad_heads(w, heads, dim, pad):
    w = w.reshape(w.shape[:-1] + (heads, dim))
    return _pad_lanes(w, pad).reshape(w.shape[:-2] + (heads * pad,))


def kernel(x, mem, positions, a_norm, a_w_in, a_gate_bias, a_head_norm, a_w_out, b_norm, b_w_in, b_q_a_norm, b_w_uq, b_q_head_norm, b_w_out, kv_norm, w_dkv, kv_a_norm, w_ukv, k_head_norm, mem_norm, mem_w_kv, mem_q_norm, mem_k_norm, ffn_norm, dense_w_gate_up, dense_w_down, moe_router, moe_w_gate_up, moe_w_down):
    nb, seq, _ = x.shape
    t = nb * seq
    x2 = x.reshape(t, D_MODEL)
    gmat = jnp.kron(jnp.eye(M_HEADS, dtype=F32), jnp.full((M_HEAD_DIM, M_HEAD_DIM), 1.0 / M_HEAD_DIM, F32)).astype(BF16)

    kbd0, vbd0 = _memory_kv(mem, mem_norm[0], mem_w_kv[0], mem_k_norm[0], gmat)
    w_in = a_w_in[0]
    qk_w, v_w = A_HEADS * A_QK_DIM, A_HEADS * A_V_DIM
    o0, o1, o2, o3, o4 = qk_w, 2 * qk_w, 2 * qk_w + v_w, 2 * qk_w + 2 * v_w, 2 * qk_w + 2 * v_w + 2 * A_HEADS
    w_main = jnp.concatenate([
        _pad_heads(w_in[:, :o0], A_HEADS, A_QK_DIM, A_QK_PAD),
        _pad_heads(w_in[:, o0:o1], A_HEADS, A_QK_DIM, A_QK_PAD),
        w_in[:, o4:]], axis=1).astype(BF16)
    w_vo_t = jnp.concatenate([
        _pad_heads(w_in[:, o1:o2], A_HEADS, A_V_DIM, A_V_PAD),
        _pad_heads(w_in[:, o2:o3], A_HEADS, A_V_DIM, A_V_PAD)], axis=1).T.astype(BF16)
    q, k, vt, ot, mq, gc, gr = _a_projection(x2, a_norm[0], w_main, w_vo_t, w_in[:, o3:o4], a_gate_bias[0])
    three = lambda a: a.reshape(nb, seq, a.shape[-1])
    head_g = _pad_heads(a_head_norm[0].reshape(1, -1), A_HEADS, A_V_DIM, A_V_PAD)
    hm = _mlstm(three(q), three(k), vt, ot, gc, gr,
                jnp.broadcast_to(head_g.reshape(-1, 1), (A_HEADS * A_V_PAD, MLSTM_CHUNK)))
    w_out = a_w_out[0]
    w_out_h = jnp.pad(w_out[:v_w].reshape(A_HEADS, A_V_DIM, D_MODEL), ((0, 0), (0, A_V_PAD - A_V_DIM), (0, 0)))
    w_out_h = w_out_h.reshape(A_HEADS * A_V_PAD, D_MODEL).astype(BF16)
    x2 = _mix_out(x2, hm.reshape(t, -1), mq, kbd0, vbd0, gmat, mem_q_norm[0],
                  w_out_h, w_out[v_w:].astype(BF16), seq)
    wgu = dense_w_gate_up[0]
    x2 = _dense_ffn(x2, ffn_norm[0], wgu[:, :D_FF].astype(BF16), wgu[:, D_FF:].astype(BF16),
                    dense_w_down[0].astype(BF16))

    cs, sn = _rope_tables(positions)
    k_sh, vt_sh, qh, mq1 = _mla_projection(x2, kv_norm, w_dkv, kv_a_norm, w_ukv, k_head_norm,
                                           b_norm[0], b_w_in[0], b_q_a_norm[0], b_w_uq[0], b_q_head_norm[0], cs, sn)

    kbd1, vbd1 = _memory_kv(mem, mem_norm[1], mem_w_kv[1], mem_k_norm[1], gmat)
    att = _causal_attention(three(qh), three(k_sh), vt_sh, seq)
    w_out = b_w_out[0]
    n_att = B_HEADS * V_HEAD
    x2, hn, idx, gates, counts = _mix_out(x2, att.reshape(t, -1), mq1, kbd1, vbd1, gmat, mem_q_norm[1],
                                          w_out[:n_att].astype(BF16), w_out[n_att:].astype(BF16), seq,
                                          router=(ffn_norm[1], moe_router[0]))
    x2 = _moe_ffn(x2, hn, idx, gates, counts, moe_w_gate_up[0], moe_w_down[0])
    return x2.reshape(nb, seq, D_MODEL)
```

```python
import functools

import jax
import jax.numpy as jnp
from jax import lax
from jax.experimental import pallas as pl
from jax.experimental.pallas import tpu as pltpu

F32 = jnp.float32
BF16 = jnp.bfloat16
I32 = jnp.int32

EPS = 1e-6
LOG2E = 1.4426950408889634
LANE = 128
VMEM_LIMIT = 48 * 1024 * 1024

D_MODEL = 1024
N_MEM = 256
M_HEADS, M_HEAD_DIM = 4, 64
M_W = M_HEADS * M_HEAD_DIM
A_HEADS, A_QK_DIM, A_V_DIM = 4, 96, 192
A_QK_PAD, A_V_PAD = 128, 256
B_HEADS, Q_LORA, KV_LORA = 6, 384, 256
QK_NOPE, QK_ROPE, V_HEAD = 128, 64, 128
B_QK_HEAD = QK_NOPE + QK_ROPE
B_QK_PAD = 256
VT_HEAD_ROWS = V_HEAD + 16
ROPE_THETA = 10000.0
D_FF = 3584
N_EXPERTS, TOP_K = 8, 2
GATE_TERMS = 3

MLSTM_CHUNK = 256
ROW_TILE = 512
FFN_ROW_TILE = 1024
FFN_FF_TILE = 1792
FFN_CHUNK = 256
MOE_ROW_TILE = 512
ATT_Q_TILE = 512
ATT_K_TILE = 512
ATT_HEADS_PER_STEP = 6
RANK_TILE = 512
RUN_ALIGN = 16
RUN_PIECES = tuple(RANK_TILE >> s for s in range((RANK_TILE // RUN_ALIGN).bit_length()))
LOCAL_ROWS = TOP_K * RANK_TILE + N_EXPERTS * RUN_ALIGN
XS_WIDTH = D_MODEL + LANE

HIGHEST = lax.Precision.HIGHEST


def _cparams(sem):
    return pltpu.CompilerParams(dimension_semantics=sem, vmem_limit_bytes=VMEM_LIMIT)


def _rms(x, g):
    return x * lax.rsqrt(jnp.mean(x * x, axis=-1, keepdims=True) + EPS) * g


def _dot(a, b):
    return jnp.dot(a, b, preferred_element_type=F32)


def _dot_nt(a, b):
    return lax.dot_general(a, b, (((1,), (1,)), ((), ())), preferred_element_type=F32)


def _dot_tn(a, b):
    return lax.dot_general(a, b, (((0,), (0,)), ((), ())), preferred_element_type=F32)


def _group_mean_sq(x, gmat):
    sq = x * x
    hi = sq.astype(BF16)
    lo = (sq - hi.astype(F32)).astype(BF16)
    return _dot(hi, gmat) + _dot(lo, gmat)


def _memkv_kernel(mem_ref, g_ref, w_ref, kg_ref, gmat_ref, kbd_ref, vbd_ref):
    hn = _rms(mem_ref[0], g_ref[...]).astype(BF16)
    kv = _dot(hn, w_ref[...])
    k, v = kv[:, :M_W], kv[:, M_W:]
    kn = k * lax.rsqrt(_group_mean_sq(k, gmat_ref[...]) + EPS) * kg_ref[...]
    lane_head = lax.broadcasted_iota(I32, (1, M_W), 1) // M_HEAD_DIM
    for h in range(M_HEADS):
        keep = lane_head == h
        kbd_ref[0, h * N_MEM:(h + 1) * N_MEM, :] = jnp.where(keep, kn, 0.0).astype(BF16)
        vbd_ref[0, h * N_MEM:(h + 1) * N_MEM, :] = jnp.where(keep, v, 0.0).astype(BF16)


def _memory_kv(mem, g, w_kv, k_g, gmat):
    nb = mem.shape[0]
    out = jax.ShapeDtypeStruct((nb, M_HEADS * N_MEM, M_W), BF16)
    return pl.pallas_call(
        _memkv_kernel,
        out_shape=(out, out),
        grid=(nb,),
        in_specs=[
            pl.BlockSpec((1, N_MEM, D_MODEL), lambda b: (b, 0, 0)),
            pl.BlockSpec((1, D_MODEL), lambda b: (0, 0)),
            pl.BlockSpec((D_MODEL, 2 * M_W), lambda b: (0, 0)),
            pl.BlockSpec((1, M_W), lambda b: (0, 0)),
            pl.BlockSpec((M_W, M_W), lambda b: (0, 0)),
        ],
        out_specs=(pl.BlockSpec((1, M_HEADS * N_MEM, M_W), lambda b: (b, 0, 0)),) * 2,
        compiler_params=_cparams(("parallel",)),
        name="memory_kv",
    )(mem, g.reshape(1, -1), w_kv.astype(BF16), jnp.tile(k_g, M_HEADS).reshape(1, -1), gmat)


def _memory_attention(mq, kbd, vbd, gmat, qg):
    qn = mq * lax.rsqrt(_group_mean_sq(mq, gmat) + EPS) * (qg * (M_HEAD_DIM ** -0.5 * LOG2E))
    s = _dot_nt(qn.astype(BF16), kbd)
    ps = []
    for h in range(M_HEADS):
        sh = s[:, h * N_MEM:(h + 1) * N_MEM]
        e = jnp.exp2(sh - jnp.max(sh, axis=-1, keepdims=True))
        ps.append((e / jnp.sum(e, axis=-1, keepdims=True)).astype(BF16))
    return _dot(jnp.concatenate(ps, axis=-1), vbd)


def _a_proj_kernel(x_ref, g_ref, w_ref, wvot_ref, wif_ref, wift_ref, bc_ref, br_ref,
                   q_ref, k_ref, vt_ref, ot_ref, mq_ref, gc_ref, gr_ref):
    hn = _rms(x_ref[...], g_ref[...]).astype(BF16)
    nq = A_HEADS * A_QK_PAD
    nv = A_HEADS * A_V_PAD
    q_ref[...] = _dot(hn, w_ref[:, :nq]).astype(BF16)
    k_ref[...] = (_dot(hn, w_ref[:, nq:2 * nq]) * (A_QK_DIM ** -0.5)).astype(BF16)
    mq_ref[...] = _dot(hn, w_ref[:, 2 * nq:])
    vt = _dot_nt(wvot_ref[:nv, :], hn)
    ones_row = lax.broadcasted_iota(I32, (nv, 1), 0) % A_V_PAD == A_V_DIM
    vt_ref[...] = jnp.where(ones_row, 1.0, vt).astype(BF16)
    ot_ref[...] = _dot_nt(wvot_ref[nv:, :], hn)
    gc_ref[...] = _dot(hn, wif_ref[...])[:, :2 * A_HEADS] + bc_ref[...]
    gr_ref[...] = _dot_nt(wift_ref[...], hn) + br_ref[...]


def _a_projection(x2, g, w_main, w_vo_t, w_if, gate_bias):
    t = x2.shape[0]
    nq, nv = A_HEADS * A_QK_PAD, A_HEADS * A_V_PAD
    ng = 2 * A_HEADS
    wif_pad = jnp.pad(w_if, ((0, 0), (0, LANE - ng))).astype(BF16)
    row = lambda n: pl.BlockSpec((ROW_TILE, n), lambda i: (i, 0))
    col = lambda n: pl.BlockSpec((n, ROW_TILE), lambda i: (0, i))
    full = lambda a: pl.BlockSpec(a.shape, lambda i: (0,) * a.ndim)
    args = (x2, g.reshape(1, -1), w_main, w_vo_t, wif_pad, w_if.T.astype(BF16),
            gate_bias.reshape(1, ng), gate_bias.reshape(ng, 1))
    return pl.pallas_call(
        _a_proj_kernel,
        out_shape=(jax.ShapeDtypeStruct((t, nq), BF16), jax.ShapeDtypeStruct((t, nq), BF16),
                   jax.ShapeDtypeStruct((nv, t), BF16), jax.ShapeDtypeStruct((nv, t), F32),
                   jax.ShapeDtypeStruct((t, M_W), F32), jax.ShapeDtypeStruct((t, ng), F32),
                   jax.ShapeDtypeStruct((ng, t), F32)),
        grid=(t // ROW_TILE,),
        in_specs=[row(D_MODEL)] + [full(a) for a in args[1:]],
        out_specs=(row(nq), row(nq), col(nv), col(nv), row(M_W), row(ng), col(ng)),
        compiler_params=_cparams(("parallel",)),
        name="mlstm_in_proj",
    )(*args)


def _log_sigmoid(f):
    return jnp.minimum(f, 0.0) - jnp.log(1.0 + jnp.exp(-jnp.abs(f)))


def _dot_nt_highest(a, b):
    return lax.dot_general(a, b, (((1,), (1,)), ((), ())), precision=HIGHEST,
                           preferred_element_type=F32)


def _chunk_gates(gc, gr):
    L = gc.shape[0]
    r = lax.broadcasted_iota(I32, (L, L), 0)
    c = lax.broadcasted_iota(I32, (L, L), 1)
    lower = (c <= r).astype(F32)
    is_f_col = lax.broadcasted_iota(I32, gc.shape, 1) >= A_HEADS
    is_f_row = lax.broadcasted_iota(I32, gr.shape, 0) >= A_HEADS
    lf_c = jnp.where(is_f_col, _log_sigmoid(gc), 0.0)
    lf_r = jnp.where(is_f_row, _log_sigmoid(gr), 0.0)
    lf_c = jnp.concatenate([lf_c, jnp.zeros((L, LANE - gc.shape[1]), F32)], axis=1)
    cum_c = jnp.dot(lower, lf_c, precision=HIGHEST, preferred_element_type=F32)[:, :gc.shape[1]]
    cum_r = _dot_nt_highest(lf_r, lower)
    return jnp.where(is_f_col, cum_c, gc), jnp.where(is_f_row, cum_r, gr)


def _prefix_max_lanes(x):
    lane = lax.broadcasted_iota(I32, x.shape, 1)
    shift = 1
    while shift < x.shape[1]:
        x = jnp.maximum(x, jnp.where(lane >= shift, pltpu.roll(x, shift, axis=1), -jnp.inf))
        shift *= 2
    return x


def _mlstm_kernel(q_ref, k_ref, vt_ref, ot_ref, gc_ref, gr_ref, hg_ref, out_ref, c_ref, m_ref):
    L = MLSTM_CHUNK

    @pl.when(pl.program_id(1) == 0)
    def _():
        c_ref[...] = jnp.zeros_like(c_ref)
        m_ref[...] = jnp.zeros_like(m_ref)

    src = lax.broadcasted_iota(I32, (L, L), 0)
    tgt = lax.broadcasted_iota(I32, (L, L), 1)
    real = lax.broadcasted_iota(I32, (A_V_PAD, 1), 0) < A_V_DIM
    gcol, grow = _chunk_gates(gc_ref[...], gr_ref[...])
    for hd in range(A_HEADS):
        qk = slice(hd * A_QK_PAD, (hd + 1) * A_QK_PAD)
        vv = slice(hd * A_V_PAD, (hd + 1) * A_V_PAD)
        q, k, vt = q_ref[0, :, qk], k_ref[0, :, qk], vt_ref[vv, :]
        u_c = gcol[:, hd:hd + 1] - gcol[:, A_HEADS + hd:A_HEADS + hd + 1]
        g_r = grow[A_HEADS + hd:A_HEADS + hd + 1, :]
        u_r = grow[hd:hd + 1, :] - g_r
        g_last = g_r[:, L - 1:L]
        m_prev = m_ref[hd, 0:1, 0:1]
        c_prev = c_ref[hd]

        run_max = jnp.maximum(_prefix_max_lanes(jnp.broadcast_to(u_r, (8, L)))[0:1, :], m_prev)
        m_t = g_r + run_max
        inter = jnp.exp(m_prev - run_max)
        decay_t = jnp.where(src <= tgt, jnp.exp(u_c - run_max), 0.0)
        p_t = (decay_t * _dot_nt(k, q)).astype(BF16)
        num_t = inter * _dot_nt(c_prev.astype(BF16), q) + _dot(vt, p_t)
        den = num_t[A_V_DIM:A_V_DIM + 1, :]
        h_t = jnp.where(real, num_t / jnp.maximum(jnp.abs(den), jnp.exp(-m_t)), 0.0)
        scale = lax.rsqrt(jnp.sum(h_t * h_t, axis=0, keepdims=True) * (1.0 / A_V_DIM) + EPS)
        out_t = h_t * scale * hg_ref[vv, :] * jax.nn.sigmoid(ot_ref[vv, :])
        out_ref[0, :, vv] = out_t.T.astype(BF16)

        w_r = g_last + u_r
        m_new = jnp.maximum(g_last + m_prev, jnp.max(w_r, axis=-1, keepdims=True))
        ev_t = (jnp.exp(w_r - m_new) * vt.astype(F32)).astype(BF16)
        c_ref[hd] = jnp.exp(g_last + m_prev - m_new) * c_prev + _dot(ev_t, k)
        m_ref[hd] = jnp.broadcast_to(m_new, m_ref.shape[1:])


def _mlstm(q, k, vt, ot, gcol, grow, head_g):
    nb, s, _ = q.shape
    L = MLSTM_CHUNK
    nv = A_HEADS * A_V_PAD
    blk = lambda w: pl.BlockSpec((1, L, w), lambda b, c: (b, c, 0))
    col = lambda n: pl.BlockSpec((n, L), lambda b, c: (0, b * (s // L) + c))
    return pl.pallas_call(
        _mlstm_kernel,
        out_shape=jax.ShapeDtypeStruct((nb, s, nv), BF16),
        grid=(nb, s // L),
        in_specs=[blk(A_HEADS * A_QK_PAD), blk(A_HEADS * A_QK_PAD), col(nv), col(nv),
                  pl.BlockSpec((L, 2 * A_HEADS), lambda b, c: (b * (s // L) + c, 0)), col(2 * A_HEADS),
                  pl.BlockSpec((nv, L), lambda b, c: (0, 0))],
        out_specs=blk(nv),
        scratch_shapes=[pltpu.VMEM((A_HEADS, A_V_PAD, A_QK_PAD), F32), pltpu.VMEM((A_HEADS, 8, LANE), F32)],
        compiler_params=_cparams(("parallel", "arbitrary")),
        name="mlstm_chunkwise",
    )(q, k, vt, ot, gcol, grow, head_g)


def _mix_out_kernel(x_ref, h_ref, mq_ref, kbd_ref, vbd_ref, gmat_ref, qg_ref, w1_ref, w2_ref, out_ref):
    mo = _memory_attention(mq_ref[...], kbd_ref[0], vbd_ref[0], gmat_ref[...], qg_ref[...])
    out_ref[...] = x_ref[...] + _dot(h_ref[...], w1_ref[...]) + _dot(mo.astype(BF16), w2_ref[...])


def _mix_out(x2, h2, mq, kbd, vbd, gmat, qg, w_main, w_mem, seq):
    t = x2.shape[0]
    tm = ROW_TILE
    row = lambda n: pl.BlockSpec((tm, n), lambda i: (i, 0))
    full = lambda a: pl.BlockSpec(a.shape, lambda i: (0,) * a.ndim)
    per_batch = pl.BlockSpec((1,) + kbd.shape[1:], lambda i: ((i * tm) // seq, 0, 0))
    qg_t = jnp.tile(qg, M_HEADS).reshape(1, -1)
    return pl.pallas_call(
        _mix_out_kernel,
        out_shape=jax.ShapeDtypeStruct((t, D_MODEL), F32),
        grid=(t // tm,),
        in_specs=[row(D_MODEL), row(h2.shape[1]), row(M_W), per_batch, per_batch,
                  full(gmat), full(qg_t), full(w_main), full(w_mem)],
        out_specs=row(D_MODEL),
        compiler_params=_cparams(("parallel",)),
        name="mixer_out_proj",
    )(x2, h2, mq, kbd, vbd, gmat, qg_t, w_main, w_mem)


def _swiglu_accumulate(acc_ref, h_ref, x, wg_ref, wu_ref, wd_ref):
    for c in range(wg_ref.shape[-1] // FFN_CHUNK):
        cols = slice(c * FFN_CHUNK, (c + 1) * FFN_CHUNK)
        gate = _dot(x, wg_ref[:, cols])
        up = _dot(x, wu_ref[:, cols])
        h_ref[:, cols] = (jax.nn.silu(gate) * up).astype(BF16)
    acc_ref[...] += _dot(h_ref[...], wd_ref[...])


def _ffn_kernel(x_ref, g_ref, wg_ref, wu_ref, wd_ref, out_ref, hn_ref, acc_ref, h_ref):
    j = pl.program_id(1)

    @pl.when(j == 0)
    def _():
        hn_ref[...] = _rms(x_ref[...], g_ref[...]).astype(BF16)
        acc_ref[...] = jnp.zeros_like(acc_ref)

    _swiglu_accumulate(acc_ref, h_ref, hn_ref[...], wg_ref, wu_ref, wd_ref)

    @pl.when(j == pl.num_programs(1) - 1)
    def _():
        out_ref[...] = x_ref[...] + acc_ref[...]


def _dense_ffn(x2, g, w_gate, w_up, w_down):
    t = x2.shape[0]
    tm, tf = FFN_ROW_TILE, FFN_FF_TILE
    return pl.pallas_call(
        _ffn_kernel,
        out_shape=jax.ShapeDtypeStruct((t, D_MODEL), F32),
        grid=(t // tm, D_FF // tf),
        in_specs=[pl.BlockSpec((tm, D_MODEL), lambda i, j: (i, 0)),
                  pl.BlockSpec((1, D_MODEL), lambda i, j: (0, 0)),
                  pl.BlockSpec((D_MODEL, tf), lambda i, j: (0, j)),
                  pl.BlockSpec((D_MODEL, tf), lambda i, j: (0, j)),
                  pl.BlockSpec((tf, D_MODEL), lambda i, j: (j, 0))],
        out_specs=pl.BlockSpec((tm, D_MODEL), lambda i, j: (i, 0)),
        scratch_shapes=[pltpu.VMEM((tm, D_MODEL), BF16), pltpu.VMEM((tm, D_MODEL), F32),
                        pltpu.VMEM((tm, tf), BF16)],
        compiler_params=_cparams(("parallel", "arbitrary")),
        name="dense_swiglu",
    )(x2, g.reshape(1, -1), w_gate, w_up, w_down)


def _rope_kernel(pos_ref, inv_ref, sign_ref, cs_ref, sn_ref):
    ang = pos_ref[...].astype(F32) * inv_ref[...]
    cs_ref[...] = jnp.cos(ang)
    sn_ref[...] = jnp.sin(ang) * sign_ref[...]


def _rope_tables(positions):
    t = positions.size
    half = QK_ROPE // 2
    inv = 1.0 / (ROPE_THETA ** (jnp.arange(0, QK_ROPE, 2, dtype=F32) / QK_ROPE))
    pad = jnp.zeros((LANE - QK_ROPE,), F32)
    inv_l = jnp.concatenate([inv, inv, pad]).reshape(1, LANE)
    sign = jnp.concatenate([-jnp.ones((half,), F32), jnp.ones((half,), F32), pad]).reshape(1, LANE)
    out = jax.ShapeDtypeStruct((t, LANE), F32)
    return pl.pallas_call(
        _rope_kernel,
        out_shape=(out, out),
        grid=(t // ROW_TILE,),
        in_specs=[pl.BlockSpec((ROW_TILE, 1), lambda i: (i, 0)),
                  pl.BlockSpec((1, LANE), lambda i: (0, 0)),
                  pl.BlockSpec((1, LANE), lambda i: (0, 0))],
        out_specs=(pl.BlockSpec((ROW_TILE, LANE), lambda i: (i, 0)),) * 2,
        compiler_params=_cparams(("parallel",)),
        name="rope_tables",
    )(positions.reshape(t, 1), inv_l, sign)


def _head_qk_norm_rope(nope, rope, rope_sw, g_nope, g_rope, g_rope_sw, cs, sn, scale):
    ss = jnp.sum(nope * nope, axis=-1, keepdims=True) + jnp.sum(rope * rope, axis=-1, keepdims=True)
    r = lax.rsqrt(ss * (1.0 / B_QK_HEAD) + EPS) * scale
    return nope * r * g_nope, (rope * g_rope * cs + rope_sw * g_rope_sw * sn) * r


def _latent_kv_body(hn, wd_ref, ga_ref, wuk_ref, wuvt_ref, kg_ref, cs_ref, sn_ref, k_ref, vt_ref):
    z = _dot(hn, wd_ref[...])
    c_kv = z[:, :KV_LORA]
    rope, rope_sw = z[:, KV_LORA:KV_LORA + LANE], z[:, KV_LORA + LANE:]
    cn = _rms(c_kv, ga_ref[...]).astype(BF16)
    kv = _dot(cn, wuk_ref[...])
    vt = _dot_nt(wuvt_ref[...], cn)
    tm = vt.shape[1]
    ones_row = (lax.broadcasted_iota(I32, (VT_HEAD_ROWS - V_HEAD, tm), 0) == 0).astype(BF16)
    for h in range(B_HEADS):
        vt_ref[h * VT_HEAD_ROWS:h * VT_HEAD_ROWS + V_HEAD, :] = vt[h * V_HEAD:(h + 1) * V_HEAD].astype(BF16)
        vt_ref[h * VT_HEAD_ROWS + V_HEAD:(h + 1) * VT_HEAD_ROWS, :] = ones_row
    kg = kg_ref[...]
    for h in range(B_HEADS):
        kn, kr = _head_qk_norm_rope(kv[:, h * QK_NOPE:(h + 1) * QK_NOPE], rope, rope_sw,
                                    kg[:, :LANE], kg[:, LANE:2 * LANE], kg[:, 2 * LANE:],
                                    cs_ref[...], sn_ref[...], 1.0)
        k_ref[:, h * B_QK_PAD:h * B_QK_PAD + QK_NOPE] = kn.astype(BF16)
        k_ref[:, h * B_QK_PAD + QK_NOPE:(h + 1) * B_QK_PAD] = kr.astype(BF16)


def _rope_swap(w):
    half = QK_ROPE // 2
    return jnp.concatenate([w[..., half:], w[..., :half]], axis=-1)


def _pad_lanes(w, n=LANE):
    return jnp.pad(w, [(0, 0)] * (w.ndim - 1) + [(0, n - w.shape[-1])])


def _head_gain(g):
    g_rope = g[QK_NOPE:]
    return jnp.concatenate([g[:QK_NOPE], _pad_lanes(g_rope), _pad_lanes(_rope_swap(g_rope))]).reshape(1, -1)


def _query_body(hn, win_ref, ga_ref, wuq_ref, qg_ref, cs_ref, sn_ref, q_ref, mq_ref):
    proj = _dot(hn, win_ref[...])
    mq_ref[...] = proj[:, Q_LORA:]
    qall = _dot(_rms(proj[:, :Q_LORA], ga_ref[...]).astype(BF16), wuq_ref[...])
    qg = qg_ref[...]
    per_head = QK_NOPE + 2 * LANE
    for h in range(B_HEADS):
        base = h * per_head
        qn, qr = _head_qk_norm_rope(qall[:, base:base + QK_NOPE],
                                    qall[:, base + QK_NOPE:base + QK_NOPE + LANE],
                                    qall[:, base + QK_NOPE + LANE:base + per_head],
                                    qg[:, :LANE], qg[:, LANE:2 * LANE], qg[:, 2 * LANE:],
                                    cs_ref[...], sn_ref[...], B_QK_HEAD ** -0.5)
        q_ref[:, h * B_QK_PAD:h * B_QK_PAD + QK_NOPE] = qn.astype(BF16)
        q_ref[:, h * B_QK_PAD + QK_NOPE:(h + 1) * B_QK_PAD] = qr.astype(BF16)


def _mla_proj_kernel(x_ref, gkv_ref, wd_ref, ga_ref, wuk_ref, wuvt_ref, kg_ref,
                     gq_ref, win_ref, gqa_ref, wuq_ref, qg_ref, cs_ref, sn_ref,
                     k_ref, vt_ref, q_ref, mq_ref):
    x = x_ref[...]
    xn = x * lax.rsqrt(jnp.mean(x * x, axis=-1, keepdims=True) + EPS)
    _latent_kv_body((xn * gkv_ref[...]).astype(BF16), wd_ref, ga_ref, wuk_ref, wuvt_ref, kg_ref,
                    cs_ref, sn_ref, k_ref, vt_ref)
    _query_body((xn * gq_ref[...]).astype(BF16), win_ref, gqa_ref, wuq_ref, qg_ref, cs_ref, sn_ref, q_ref, mq_ref)


def _mla_projection(x2, kv_norm, w_dkv, kv_a_norm, w_ukv, k_head_norm,
                    q_norm, w_in, q_a_g, w_uq, q_head_g, cs, sn):
    t = x2.shape[0]
    w_rope = w_dkv[:, KV_LORA:]
    wd = jnp.concatenate([w_dkv[:, :KV_LORA], _pad_lanes(w_rope), _pad_lanes(_rope_swap(w_rope))],
                         axis=1).astype(BF16)
    wu = w_ukv.reshape(KV_LORA, B_HEADS, QK_NOPE + V_HEAD)
    wuk = wu[:, :, :QK_NOPE].reshape(KV_LORA, -1).astype(BF16)
    wuvt = wu[:, :, QK_NOPE:].reshape(KV_LORA, -1).T.astype(BF16)
    wq = w_uq.reshape(Q_LORA, B_HEADS, B_QK_HEAD)
    wq_rope = wq[:, :, QK_NOPE:]
    wq = jnp.concatenate([wq[:, :, :QK_NOPE], _pad_lanes(wq_rope), _pad_lanes(_rope_swap(wq_rope))],
                         axis=-1).reshape(Q_LORA, -1).astype(BF16)
    consts = (kv_norm.reshape(1, -1), wd, kv_a_norm.reshape(1, -1), wuk, wuvt, _head_gain(k_head_norm),
              q_norm.reshape(1, -1), w_in.astype(BF16), q_a_g.reshape(1, -1), wq, _head_gain(q_head_g))
    row = lambda n: pl.BlockSpec((ROW_TILE, n), lambda i: (i, 0))
    full = lambda a: pl.BlockSpec(a.shape, lambda i: (0,) * a.ndim)
    return pl.pallas_call(
        _mla_proj_kernel,
        out_shape=(jax.ShapeDtypeStruct((t, B_HEADS * B_QK_PAD), BF16),
                   jax.ShapeDtypeStruct((B_HEADS * VT_HEAD_ROWS, t), BF16),
                   jax.ShapeDtypeStruct((t, B_HEADS * B_QK_PAD), BF16),
                   jax.ShapeDtypeStruct((t, M_W), F32)),
        grid=(t // ROW_TILE,),
        in_specs=[row(D_MODEL)] + [full(a) for a in consts] + [row(LANE), row(LANE)],
        out_specs=(row(B_HEADS * B_QK_PAD), pl.BlockSpec((B_HEADS * VT_HEAD_ROWS, ROW_TILE), lambda i: (0, i)),
                   row(B_HEADS * B_QK_PAD), row(M_W)),
        compiler_params=_cparams(("parallel",)),
        name="mla_proj",
    )(x2, *consts, cs, sn)


def _attn_kernel(q_ref, k_ref, vt_ref, out_ref):
    tq, tk = ATT_Q_TILE, ATT_K_TILE
    i = pl.program_id(2)

    def block(h, j, carry, masked):
        m, acc = carry
        off = pl.multiple_of(j * tk, tk)
        q = q_ref[0, :, h * B_QK_PAD:(h + 1) * B_QK_PAD]
        st = _dot_nt(k_ref[0, pl.ds(off, tk), h * B_QK_PAD:(h + 1) * B_QK_PAD], q)
        if masked:
            key = lax.broadcasted_iota(I32, (tk, tq), 0)
            qry = lax.broadcasted_iota(I32, (tk, tq), 1)
            st = jnp.where(key <= qry, st, -jnp.inf)
        m_new = jnp.maximum(m, jnp.max(st, axis=0, keepdims=True))
        p = jnp.exp(st - m_new).astype(BF16)
        vt = vt_ref[h * VT_HEAD_ROWS:(h + 1) * VT_HEAD_ROWS, pl.ds(off, tk)]
        return m_new, jnp.exp(m - m_new) * acc + _dot(vt, p)

    heads = range(ATT_HEADS_PER_STEP)
    init = tuple((jnp.full((1, tq), -jnp.inf, F32), jnp.zeros((VT_HEAD_ROWS, tq), F32)) for _ in heads)
    carry = lax.fori_loop(0, i, lambda j, c: tuple(block(h, j, c[h], False) for h in heads), init)
    for h in heads:
        _, acc = block(h, i, carry[h], True)
        out_t = acc[:V_HEAD] / acc[V_HEAD:V_HEAD + 1]
        out_ref[0, :, h * V_HEAD:(h + 1) * V_HEAD] = out_t.T.astype(BF16)


def _causal_attention(q, k, vt, seq):
    nb = q.shape[0]
    g = ATT_HEADS_PER_STEP
    assert ATT_Q_TILE == ATT_K_TILE and B_HEADS % g == 0
    return pl.pallas_call(
        _attn_kernel,
        out_shape=jax.ShapeDtypeStruct((nb, seq, B_HEADS * V_HEAD), BF16),
        grid=(nb, B_HEADS // g, seq // ATT_Q_TILE),
        in_specs=[pl.BlockSpec((1, ATT_Q_TILE, g * B_QK_PAD), lambda b, h, i: (b, i, h)),
                  pl.BlockSpec((1, seq, g * B_QK_PAD), lambda b, h, i: (b, 0, h)),
                  pl.BlockSpec((g * VT_HEAD_ROWS, seq), lambda b, h, i: (h, b))],
        out_specs=pl.BlockSpec((1, ATT_Q_TILE, g * V_HEAD), lambda b, h, i: (b, i, h)),
        compiler_params=_cparams(("parallel", "parallel", "arbitrary")),
        name="causal_attention",
    )(q, k, vt)


def _router_kernel(x_ref, g_ref, wr_ref, hn_ref, idx_ref, gate_ref, count_ref):
    hn = _rms(x_ref[...], g_ref[...])
    hn_ref[...] = hn.astype(BF16)
    logits = jnp.dot(hn, wr_ref[...], precision=HIGHEST, preferred_element_type=F32)
    lane = lax.broadcasted_iota(I32, logits.shape, 1)
    logits = jnp.where(lane < N_EXPERTS, logits, -jnp.inf)
    v1 = jnp.max(logits, axis=-1, keepdims=True)
    i1 = jnp.min(jnp.where(logits == v1, lane, LANE), axis=-1, keepdims=True)
    rest = jnp.where(lane == i1, -jnp.inf, logits)
    v2 = jnp.max(rest, axis=-1, keepdims=True)
    i2 = jnp.min(jnp.where(rest == v2, lane, LANE), axis=-1, keepdims=True)
    e2 = jnp.exp(v2 - v1)
    den = 1.0 + e2
    idx_ref[...] = jnp.where(lane == 0, i1, jnp.where(lane == 1, i2, 0))
    record = jnp.zeros(logits.shape, F32)
    for k, gate in enumerate((1.0 / den, e2 / den)):
        hi = gate.astype(BF16).astype(F32)
        mid = (gate - hi).astype(BF16).astype(F32)
        for part, term in enumerate((hi, mid, gate - hi - mid)):
            record = jnp.where(lane == GATE_TERMS * k + part, term, record)
    record = jnp.where(lane == 2 * GATE_TERMS, i1.astype(F32), record)
    record = jnp.where(lane == 2 * GATE_TERMS + 1, i2.astype(F32), record)
    gate_ref[...] = record.astype(BF16)
    pairs = (lane == i1).astype(F32) + (lane == i2).astype(F32)
    count_ref[...] = jnp.broadcast_to(jnp.sum(pairs, axis=0, keepdims=True), count_ref.shape)


def _router(x2, g, w_router):
    t = x2.shape[0]
    row = lambda n: pl.BlockSpec((RANK_TILE, n), lambda i: (i, 0))
    wr = _pad_lanes(w_router)
    return pl.pallas_call(
        _router_kernel,
        out_shape=(jax.ShapeDtypeStruct((t, D_MODEL), BF16), jax.ShapeDtypeStruct((t, LANE), I32),
                   jax.ShapeDtypeStruct((t, LANE), BF16), jax.ShapeDtypeStruct((8 * (t // RANK_TILE), LANE), F32)),
        grid=(t // RANK_TILE,),
        in_specs=[row(D_MODEL), pl.BlockSpec((1, D_MODEL), lambda i: (0, 0)),
                  pl.BlockSpec(wr.shape, lambda i: (0, 0))],
        out_specs=(row(D_MODEL), row(LANE), row(LANE), pl.BlockSpec((8, LANE), lambda i: (i, 0))),
        compiler_params=_cparams(("parallel",)),
        name="moe_router",
    )(x2, g.reshape(1, -1), wr)


def _aligned(count):
    return jnp.ceil(count * (1.0 / RUN_ALIGN)) * RUN_ALIGN


def _rank_kernel(idx_ref, counts_ref, lpos_ref, lposr_ref, tile_ref, runs_ref, run_ref, start_ref):
    blk = pl.program_id(0)
    tb = RANK_TILE
    lane = lax.broadcasted_iota(I32, (tb, LANE), 1)
    idx = idx_ref[...]
    oh0 = (lane == idx[:, 0:1]).astype(F32)
    oh1 = (lane == idx[:, 1:2]).astype(F32)
    both = oh0 + oh1
    run_len = _aligned(jnp.sum(both, axis=0, keepdims=True))
    r = lax.broadcasted_iota(I32, (LANE, LANE), 0)
    c = lax.broadcasted_iota(I32, (LANE, LANE), 1)
    before = (r < c).astype(F32)

    @pl.when(blk == 0)
    def _():
        sizes = jnp.sum(_aligned(counts_ref[...]), axis=0, keepdims=True) * 0.125
        tiles = jnp.ceil(sizes * (1.0 / MOE_ROW_TILE))
        tile_start = jnp.dot(tiles, before, precision=HIGHEST, preferred_element_type=F32)
        start_ref[...] = tile_start * MOE_ROW_TILE
        tile_end = tile_start + tiles
        n_col = r.astype(F32)
        ended = ((n_col >= tile_end) & (c < N_EXPERTS)).astype(F32)
        expert = jnp.sum(ended, axis=-1, keepdims=True)
        total = jnp.max(tile_end, axis=-1, keepdims=True)
        as_column = lambda row: jnp.sum(jnp.where(r == c, row, 0.0), axis=-1, keepdims=True)
        tail_start = as_column(start_ref[...] + sizes)
        tail_len = as_column(tiles * MOE_ROW_TILE - sizes)
        col = lax.broadcasted_iota(I32, tile_ref.shape, 1)
        tile_ref[...] = jnp.where(col == 0, expert, jnp.where(col == 1, total, jnp.where(
            col == 2, tail_start, tail_len))).astype(I32)
        run_ref[...] = jnp.zeros_like(run_ref)
        runs_ref[...] = jnp.zeros_like(runs_ref)

    rr = lax.broadcasted_iota(I32, (tb, tb), 0)
    cc = lax.broadcasted_iota(I32, (tb, tb), 1)
    strict = (cc < rr).astype(BF16)
    local = jnp.dot(run_len, before, precision=HIGHEST, preferred_element_type=F32)
    base = _dot(strict, both.astype(BF16)) + local
    p0 = jnp.sum(oh0 * base, axis=-1, keepdims=True)
    p1 = jnp.sum(oh1 * base, axis=-1, keepdims=True)
    posf = jnp.where(lane == 0, p0, jnp.where(lane == 1, p1, 0.0))
    lpos_ref[...] = posf.astype(I32)
    pick = (lax.broadcasted_iota(I32, (8, LANE), 0) == lax.broadcasted_iota(I32, (8, LANE), 1)).astype(F32)
    lposr_ref[...] = _dot_nt_highest(pick, posf).astype(I32)
    n = runs_ref.shape[0] // 3
    mine = lax.broadcasted_iota(I32, (n, LANE), 0) == blk
    for k, value in enumerate((local, run_ref[...] + start_ref[...], run_len)):
        runs_ref[k * n:(k + 1) * n, :] = jnp.where(mine, value.astype(I32), runs_ref[k * n:(k + 1) * n, :])
    run_ref[...] += run_len


def _rank(idx, counts):
    t = idx.shape[0]
    nblk = t // RANK_TILE
    const = lambda shape: pl.BlockSpec(shape, lambda i: (0, 0))
    return pl.pallas_call(
        _rank_kernel,
        out_shape=(jax.ShapeDtypeStruct((t, LANE), I32), jax.ShapeDtypeStruct((8, t), I32),
                   jax.ShapeDtypeStruct((LANE, 4), I32), jax.ShapeDtypeStruct((3 * nblk, LANE), I32)),
        grid=(nblk,),
        in_specs=[pl.BlockSpec((RANK_TILE, LANE), lambda i: (i, 0)), const(counts.shape)],
        out_specs=(pl.BlockSpec((RANK_TILE, LANE), lambda i: (i, 0)),
                   pl.BlockSpec((8, RANK_TILE), lambda i: (0, i)),
                   const((LANE, 4)), const((3 * nblk, LANE))),
        scratch_shapes=[pltpu.VMEM((1, LANE), F32), pltpu.VMEM((1, LANE), F32)],
        compiler_params=_cparams(("arbitrary",)),
        name="moe_rank",
    )(idx, counts)


def _for_each_piece(length, fn):
    for size in RUN_PIECES:
        @pl.when((length & size) != 0)
        def _(size=size):
            fn(pl.multiple_of(length & (-2 * size), RUN_ALIGN), size)


def _for_each_run_piece(blk, loc_ref, dst_ref, len_ref, fn):
    for e in range(N_EXPERTS):
        k = blk * N_EXPERTS + e
        loc, dst = loc_ref[k], dst_ref[k]
        _for_each_piece(len_ref[k], lambda done, size, loc=loc, dst=dst: fn(
            pl.multiple_of(loc + done, RUN_ALIGN), pl.multiple_of(dst + done, RUN_ALIGN), size))


def _dispatch_kernel(loc_ref, dst_ref, len_ref, tail_ref, tail_len_ref, ntiles_ref, hn_ref, gate_ref, lposr_ref,
                     xs_ref, local_ref, zero_ref, sem):
    blk = pl.program_id(0)

    @pl.when(blk == 0)
    def _():
        zero_ref[...] = jnp.zeros_like(zero_ref)

        def fill(e):
            start = tail_ref[e]
            return lambda done, size: pltpu.make_async_copy(
                zero_ref.at[pl.ds(0, size), :], xs_ref.at[pl.ds(pl.multiple_of(start + done, RUN_ALIGN), size), :], sem)

        def fill_tile(i):
            return pltpu.make_async_copy(
                zero_ref, xs_ref.at[pl.ds(pl.multiple_of(i * RANK_TILE, RANK_TILE), RANK_TILE), :], sem)

        def unused_tiles(action):
            def body(i, carry):
                action(fill_tile(i))
                return carry
            lax.fori_loop(ntiles_ref[0] * (MOE_ROW_TILE // RANK_TILE), xs_ref.shape[0] // RANK_TILE, body, 0)

        for e in range(N_EXPERTS):
            _for_each_piece(tail_len_ref[e], lambda *a, e=e: fill(e)(*a).start())
        unused_tiles(lambda copy: copy.start())
        for e in range(N_EXPERTS):
            _for_each_piece(tail_len_ref[e], lambda *a, e=e: fill(e)(*a).wait())
        unused_tiles(lambda copy: copy.wait())

    rows = lax.broadcasted_iota(I32, (LOCAL_ROWS, 1), 0)
    sel = jnp.where(lposr_ref[0:1, :] == rows, 1.0, jnp.where(lposr_ref[1:2, :] == rows, 1.0, 0.0)).astype(BF16)
    local_ref[:, :D_MODEL] = _dot(sel, hn_ref[...]).astype(BF16)
    local_ref[:, D_MODEL:] = _dot(sel, gate_ref[...]).astype(BF16)

    def copy(loc, dst, size):
        return pltpu.make_async_copy(local_ref.at[pl.ds(loc, size), :], xs_ref.at[pl.ds(dst, size), :], sem)

    _for_each_run_piece(blk, loc_ref, dst_ref, len_ref, lambda *a: copy(*a).start())
    _for_each_run_piece(blk, loc_ref, dst_ref, len_ref, lambda *a: copy(*a).wait())


def _dispatch(runs, tails, hn, gates, lpos_rows, rows):
    nblk = hn.shape[0] // RANK_TILE
    tok = lambda n: pl.BlockSpec((RANK_TILE, n), lambda b, *_: (b, 0))
    return pl.pallas_call(
        _dispatch_kernel,
        out_shape=jax.ShapeDtypeStruct((rows, XS_WIDTH), BF16),
        grid_spec=pltpu.PrefetchScalarGridSpec(
            num_scalar_prefetch=6,
            grid=(nblk,),
            in_specs=[tok(D_MODEL), tok(LANE), pl.BlockSpec((8, RANK_TILE), lambda b, *_: (0, b))],
            out_specs=pl.BlockSpec(memory_space=pl.ANY),
            scratch_shapes=[pltpu.VMEM((LOCAL_ROWS, XS_WIDTH), BF16), pltpu.VMEM((RANK_TILE, XS_WIDTH), BF16),
                            pltpu.SemaphoreType.DMA],
        ),
        compiler_params=_cparams(("arbitrary",)),
        name="moe_dispatch",
    )(*runs, *tails, hn, gates, lpos_rows)


def _moe_kernel(expert_ref, ntiles_ref, x_ref, wg_ref, wu_ref, wd_ref, out_ref, acc_ref, h_ref):
    i, j = pl.program_id(0), pl.program_id(1)
    active = i < ntiles_ref[0]

    @pl.when(j == 0)
    def _():
        acc_ref[...] = jnp.zeros_like(acc_ref)

    @pl.when(active)
    def _():
        _swiglu_accumulate(acc_ref, h_ref, x_ref[:, :D_MODEL], wg_ref.at[0], wu_ref.at[0], wd_ref.at[0])

    @pl.when(j == pl.num_programs(1) - 1)
    def _():
        gs = x_ref[:, D_MODEL:].astype(F32)
        gate = [sum(gs[:, GATE_TERMS * k + n:GATE_TERMS * k + n + 1] for n in range(GATE_TERMS)) for k in range(TOP_K)]
        first = gs[:, TOP_K * GATE_TERMS:TOP_K * GATE_TERMS + 1] == expert_ref[i].astype(F32)
        row_gate = jnp.where(first, gate[0], gate[1])
        out_ref[...] = jnp.where(active, acc_ref[...] * row_gate, 0.0).astype(BF16)


def _moe_experts(tile_expert, n_tiles, xs, w_gate, w_up, w_down):
    rows = xs.shape[0]
    tm, tf = MOE_ROW_TILE, FFN_FF_TILE
    nj = D_FF // tf

    def x_map(i, j, e_ref, n_ref):
        return jnp.clip(i, 0, jnp.maximum(n_ref[0] - 1, 0)), 0

    def w_idx(i, j, expert_ref, ntiles_ref):
        e = jnp.minimum(expert_ref[i], N_EXPERTS - 1)
        return e, jnp.where(i < ntiles_ref[0], j, nj - 1)

    def w_up_map(i, j, e_ref, n_ref):
        e, jj = w_idx(i, j, e_ref, n_ref)
        return e, 0, jj

    def w_down_map(i, j, e_ref, n_ref):
        e, jj = w_idx(i, j, e_ref, n_ref)
        return e, jj, 0

    return pl.pallas_call(
        _moe_kernel,
        out_shape=jax.ShapeDtypeStruct((rows, D_MODEL), BF16),
        grid_spec=pltpu.PrefetchScalarGridSpec(
            num_scalar_prefetch=2,
            grid=(rows // tm, nj),
            in_specs=[pl.BlockSpec((tm, XS_WIDTH), x_map),
                      pl.BlockSpec((1, D_MODEL, tf), w_up_map),
                      pl.BlockSpec((1, D_MODEL, tf), w_up_map),
                      pl.BlockSpec((1, tf, D_MODEL), w_down_map)],
            out_specs=pl.BlockSpec((tm, D_MODEL), lambda i, j, e, n: (i, 0)),
            scratch_shapes=[pltpu.VMEM((tm, D_MODEL), F32), pltpu.VMEM((tm, tf), BF16)],
        ),
        compiler_params=_cparams(("arbitrary", "arbitrary")),
        name="moe_experts",
    )(tile_expert, n_tiles, xs, w_gate, w_up, w_down)


def _combine_kernel(loc_ref, dst_ref, len_ref, x_ref, lpos_ref, ys_ref, out_ref, local_ref, sem):
    blk = pl.program_id(0)
    local_ref[...] = jnp.zeros_like(local_ref)

    def copy(loc, dst, size):
        return pltpu.make_async_copy(ys_ref.at[pl.ds(dst, size), :], local_ref.at[pl.ds(loc, size), :], sem)

    _for_each_run_piece(blk, loc_ref, dst_ref, len_ref, lambda *a: copy(*a).start())
    _for_each_run_piece(blk, loc_ref, dst_ref, len_ref, lambda *a: copy(*a).wait())
    rows = lax.broadcasted_iota(I32, (1, LOCAL_ROWS), 1)
    lpos = lpos_ref[...]
    sel = jnp.where(lpos[:, 0:1] == rows, 1.0, jnp.where(lpos[:, 1:2] == rows, 1.0, 0.0)).astype(BF16)
    out_ref[...] = x_ref[...] + _dot(sel, local_ref[...])


def _combine(runs, x2, lpos, ys):
    tok = lambda n: pl.BlockSpec((RANK_TILE, n), lambda b, *_: (b, 0))
    return pl.pallas_call(
        _combine_kernel,
        out_shape=jax.ShapeDtypeStruct(x2.shape, F32),
        grid_spec=pltpu.PrefetchScalarGridSpec(
            num_scalar_prefetch=3,
            grid=(x2.shape[0] // RANK_TILE,),
            in_specs=[tok(D_MODEL), tok(LANE), pl.BlockSpec(memory_space=pl.ANY)],
            out_specs=tok(D_MODEL),
            scratch_shapes=[pltpu.VMEM((LOCAL_ROWS, D_MODEL), BF16), pltpu.SemaphoreType.DMA],
        ),
        compiler_params=_cparams(("arbitrary",)),
        name="moe_combine",
    )(*runs, x2, lpos, ys)


def _moe_ffn(x2, g, w_router, w_gate_up, w_down):
    t = x2.shape[0]
    nblk = t // RANK_TILE
    hn, idx, gates, counts = _router(x2, g, w_router)
    lpos, lpos_rows, tile_info, run_tab = _rank(idx, counts)
    runs = tuple(run_tab.reshape(3, nblk, LANE)[:, :, :N_EXPERTS].reshape(3, -1))
    rows = t * TOP_K + nblk * N_EXPERTS * (RUN_ALIGN - 1) + N_EXPERTS * (MOE_ROW_TILE - 1)
    rows = -(-rows // MOE_ROW_TILE) * MOE_ROW_TILE
    assert rows // MOE_ROW_TILE <= LANE
    assert MOE_ROW_TILE % RANK_TILE == 0 and rows % RANK_TILE == 0
    xs = _dispatch(runs, (tile_info[:, 2], tile_info[:, 3], tile_info[:1, 1]), hn, gates, lpos_rows, rows)
    ys = _moe_experts(tile_info[:, 0], tile_info[:1, 1], xs,
                      w_gate_up[..., :D_FF].astype(BF16), w_gate_up[..., D_FF:].astype(BF16),
                      w_down.astype(BF16))
    return _combine(runs, x2, lpos, ys)


def _pad_heads(w, heads, dim, pad):
    w = w.reshape(w.shape[:-1] + (heads, dim))
    return _pad_lanes(w, pad).reshape(w.shape[:-2] + (heads * pad,))


def kernel(x, mem, positions, a_norm, a_w_in, a_gate_bias, a_head_norm, a_w_out, b_norm, b_w_in, b_q_a_norm, b_w_uq, b_q_head_norm, b_w_out, kv_norm, w_dkv, kv_a_norm, w_ukv, k_head_norm, mem_norm, mem_w_kv, mem_q_norm, mem_k_norm, ffn_norm, dense_w_gate_up, dense_w_down, moe_router, moe_w_gate_up, moe_w_down):
    nb, seq, _ = x.shape
    t = nb * seq
    x2 = x.reshape(t, D_MODEL)
    gmat = jnp.kron(jnp.eye(M_HEADS, dtype=F32), jnp.full((M_HEAD_DIM, M_HEAD_DIM), 1.0 / M_HEAD_DIM, F32)).astype(BF16)

    kbd0, vbd0 = _memory_kv(mem, mem_norm[0], mem_w_kv[0], mem_k_norm[0], gmat)
    w_in = a_w_in[0]
    qk_w, v_w = A_HEADS * A_QK_DIM, A_HEADS * A_V_DIM
    o0, o1, o2, o3, o4 = qk_w, 2 * qk_w, 2 * qk_w + v_w, 2 * qk_w + 2 * v_w, 2 * qk_w + 2 * v_w + 2 * A_HEADS
    w_main = jnp.concatenate([
        _pad_heads(w_in[:, :o0], A_HEADS, A_QK_DIM, A_QK_PAD),
        _pad_heads(w_in[:, o0:o1], A_HEADS, A_QK_DIM, A_QK_PAD),
        w_in[:, o4:]], axis=1).astype(BF16)
    w_vo_t = jnp.concatenate([
        _pad_heads(w_in[:, o1:o2], A_HEADS, A_V_DIM, A_V_PAD),
        _pad_heads(w_in[:, o2:o3], A_HEADS, A_V_DIM, A_V_PAD)], axis=1).T.astype(BF16)
    q, k, vt, ot, mq, gc, gr = _a_projection(x2, a_norm[0], w_main, w_vo_t, w_in[:, o3:o4], a_gate_bias[0])
    three = lambda a: a.reshape(nb, seq, a.shape[-1])
    head_g = _pad_heads(a_head_norm[0].reshape(1, -1), A_HEADS, A_V_DIM, A_V_PAD)
    hm = _mlstm(three(q), three(k), vt, ot, gc, gr,
                jnp.broadcast_to(head_g.reshape(-1, 1), (A_HEADS * A_V_PAD, MLSTM_CHUNK)))
    w_out = a_w_out[0]
    w_out_h = jnp.pad(w_out[:v_w].reshape(A_HEADS, A_V_DIM, D_MODEL), ((0, 0), (0, A_V_PAD - A_V_DIM), (0, 0)))
    w_out_h = w_out_h.reshape(A_HEADS * A_V_PAD, D_MODEL).astype(BF16)
    x2 = _mix_out(x2, hm.reshape(t, -1), mq, kbd0, vbd0, gmat, mem_q_norm[0],
                  w_out_h, w_out[v_w:].astype(BF16), seq)
    wgu = dense_w_gate_up[0]
    x2 = _dense_ffn(x2, ffn_norm[0], wgu[:, :D_FF].astype(BF16), wgu[:, D_FF:].astype(BF16),
                    dense_w_down[0].astype(BF16))

    cs, sn = _rope_tables(positions)
    k_sh, vt_sh, qh, mq1 = _mla_projection(x2, kv_norm, w_dkv, kv_a_norm, w_ukv, k_head_norm,
                                           b_norm[0], b_w_in[0], b_q_a_norm[0], b_w_uq[0], b_q_head_norm[0], cs, sn)

    kbd1, vbd1 = _memory_kv(mem, mem_norm[1], mem_w_kv[1], mem_k_norm[1], gmat)
    att = _causal_attention(three(qh), three(k_sh), vt_sh, seq)
    w_out = b_w_out[0]
    n_att = B_HEADS * V_HEAD
    x2 = _mix_out(x2, att.reshape(t, -1), mq1, kbd1, vbd1, gmat, mem_q_norm[1],
                  w_out[:n_att].astype(BF16), w_out[n_att:].astype(BF16), seq)
    x2 = _moe_ffn(x2, ffn_norm[1], moe_router[0], moe_w_gate_up[0], moe_w_down[0])
    return x2.reshape(nb, seq, D_MODEL)
```

```python
import functools

import jax
import jax.numpy as jnp
from jax import lax
from jax.experimental import pallas as pl
from jax.experimental.pallas import tpu as pltpu

F32 = jnp.float32
BF16 = jnp.bfloat16
I32 = jnp.int32

EPS = 1e-6
LOG2E = 1.4426950408889634
LANE = 128
VMEM_LIMIT = 48 * 1024 * 1024

D_MODEL = 1024
N_MEM = 256
M_HEADS, M_HEAD_DIM = 4, 64
M_W = M_HEADS * M_HEAD_DIM
A_HEADS, A_QK_DIM, A_V_DIM = 4, 96, 192
A_QK_PAD, A_V_PAD = 128, 256
B_HEADS, Q_LORA, KV_LORA = 6, 384, 256
QK_NOPE, QK_ROPE, V_HEAD = 128, 64, 128
B_QK_HEAD = QK_NOPE + QK_ROPE
B_QK_PAD = 256
VT_HEAD_ROWS = V_HEAD + 16
ROPE_THETA = 10000.0
D_FF = 3584
N_EXPERTS, TOP_K = 8, 2
GATE_TERMS = 3

MLSTM_CHUNK = 256
ROW_TILE = 512
FFN_ROW_TILE = 1024
FFN_FF_TILE = 1792
FFN_CHUNK = 256
MOE_ROW_TILE = 512
ATT_Q_TILE = 512
ATT_K_TILE = 512
ATT_HEADS_PER_STEP = 6
RANK_TILE = 512
RUN_ALIGN = 16
RUN_PIECES = tuple(RANK_TILE >> s for s in range((RANK_TILE // RUN_ALIGN).bit_length()))
LOCAL_ROWS = TOP_K * RANK_TILE + N_EXPERTS * RUN_ALIGN
XS_WIDTH = D_MODEL + LANE

HIGHEST = lax.Precision.HIGHEST


def _cparams(sem):
    return pltpu.CompilerParams(dimension_semantics=sem, vmem_limit_bytes=VMEM_LIMIT)


def _rms(x, g):
    return x * lax.rsqrt(jnp.mean(x * x, axis=-1, keepdims=True) + EPS) * g


def _dot(a, b):
    return jnp.dot(a, b, preferred_element_type=F32)


def _dot_nt(a, b):
    return lax.dot_general(a, b, (((1,), (1,)), ((), ())), preferred_element_type=F32)


def _dot_tn(a, b):
    return lax.dot_general(a, b, (((0,), (0,)), ((), ())), preferred_element_type=F32)


def _group_mean_sq(x, gmat):
    sq = x * x
    hi = sq.astype(BF16)
    lo = (sq - hi.astype(F32)).astype(BF16)
    return _dot(hi, gmat) + _dot(lo, gmat)


def _memkv_kernel(mem_ref, g_ref, w_ref, kg_ref, gmat_ref, kbd_ref, vbd_ref):
    hn = _rms(mem_ref[0], g_ref[...]).astype(BF16)
    kv = _dot(hn, w_ref[...])
    k, v = kv[:, :M_W], kv[:, M_W:]
    kn = k * lax.rsqrt(_group_mean_sq(k, gmat_ref[...]) + EPS) * kg_ref[...]
    lane_head = lax.broadcasted_iota(I32, (1, M_W), 1) // M_HEAD_DIM
    for h in range(M_HEADS):
        keep = lane_head == h
        kbd_ref[0, h * N_MEM:(h + 1) * N_MEM, :] = jnp.where(keep, kn, 0.0).astype(BF16)
        vbd_ref[0, h * N_MEM:(h + 1) * N_MEM, :] = jnp.where(keep, v, 0.0).astype(BF16)


def _memory_kv(mem, g, w_kv, k_g, gmat):
    nb = mem.shape[0]
    out = jax.ShapeDtypeStruct((nb, M_HEADS * N_MEM, M_W), BF16)
    return pl.pallas_call(
        _memkv_kernel,
        out_shape=(out, out),
        grid=(nb,),
        in_specs=[
            pl.BlockSpec((1, N_MEM, D_MODEL), lambda b: (b, 0, 0)),
            pl.BlockSpec((1, D_MODEL), lambda b: (0, 0)),
            pl.BlockSpec((D_MODEL, 2 * M_W), lambda b: (0, 0)),
            pl.BlockSpec((1, M_W), lambda b: (0, 0)),
            pl.BlockSpec((M_W, M_W), lambda b: (0, 0)),
        ],
        out_specs=(pl.BlockSpec((1, M_HEADS * N_MEM, M_W), lambda b: (b, 0, 0)),) * 2,
        compiler_params=_cparams(("parallel",)),
        name="memory_kv",
    )(mem, g.reshape(1, -1), w_kv.astype(BF16), jnp.tile(k_g, M_HEADS).reshape(1, -1), gmat)


def _memory_attention(mq, kbd, vbd, gmat, qg):
    qn = mq * lax.rsqrt(_group_mean_sq(mq, gmat) + EPS) * (qg * (M_HEAD_DIM ** -0.5 * LOG2E))
    s = _dot_nt(qn.astype(BF16), kbd)
    ps = []
    for h in range(M_HEADS):
        sh = s[:, h * N_MEM:(h + 1) * N_MEM]
        e = jnp.exp2(sh - jnp.max(sh, axis=-1, keepdims=True))
        ps.append((e / jnp.sum(e, axis=-1, keepdims=True)).astype(BF16))
    return _dot(jnp.concatenate(ps, axis=-1), vbd)


def _a_proj_kernel(x_ref, g_ref, w_ref, wvot_ref, wif_ref, wift_ref, bc_ref, br_ref,
                   q_ref, k_ref, vt_ref, ot_ref, mq_ref, gc_ref, gr_ref):
    hn = _rms(x_ref[...], g_ref[...]).astype(BF16)
    nq = A_HEADS * A_QK_PAD
    nv = A_HEADS * A_V_PAD
    q_ref[...] = _dot(hn, w_ref[:, :nq]).astype(BF16)
    k_ref[...] = (_dot(hn, w_ref[:, nq:2 * nq]) * (A_QK_DIM ** -0.5)).astype(BF16)
    mq_ref[...] = _dot(hn, w_ref[:, 2 * nq:])
    vt = _dot_nt(wvot_ref[:nv, :], hn)
    ones_row = lax.broadcasted_iota(I32, (nv, 1), 0) % A_V_PAD == A_V_DIM
    vt_ref[...] = jnp.where(ones_row, 1.0, vt).astype(BF16)
    ot_ref[...] = _dot_nt(wvot_ref[nv:, :], hn)
    gc_ref[...] = _dot(hn, wif_ref[...])[:, :2 * A_HEADS] + bc_ref[...]
    gr_ref[...] = _dot_nt(wift_ref[...], hn) + br_ref[...]


def _a_projection(x2, g, w_main, w_vo_t, w_if, gate_bias):
    t = x2.shape[0]
    nq, nv = A_HEADS * A_QK_PAD, A_HEADS * A_V_PAD
    ng = 2 * A_HEADS
    wif_pad = jnp.pad(w_if, ((0, 0), (0, LANE - ng))).astype(BF16)
    row = lambda n: pl.BlockSpec((ROW_TILE, n), lambda i: (i, 0))
    col = lambda n: pl.BlockSpec((n, ROW_TILE), lambda i: (0, i))
    full = lambda a: pl.BlockSpec(a.shape, lambda i: (0,) * a.ndim)
    args = (x2, g.reshape(1, -1), w_main, w_vo_t, wif_pad, w_if.T.astype(BF16),
            gate_bias.reshape(1, ng), gate_bias.reshape(ng, 1))
    return pl.pallas_call(
        _a_proj_kernel,
        out_shape=(jax.ShapeDtypeStruct((t, nq), BF16), jax.ShapeDtypeStruct((t, nq), BF16),
                   jax.ShapeDtypeStruct((nv, t), BF16), jax.ShapeDtypeStruct((nv, t), F32),
                   jax.ShapeDtypeStruct((t, M_W), F32), jax.ShapeDtypeStruct((t, ng), F32),
                   jax.ShapeDtypeStruct((ng, t), F32)),
        grid=(t // ROW_TILE,),
        in_specs=[row(D_MODEL)] + [full(a) for a in args[1:]],
        out_specs=(row(nq), row(nq), col(nv), col(nv), row(M_W), row(ng), col(ng)),
        compiler_params=_cparams(("parallel",)),
        name="mlstm_in_proj",
    )(*args)


def _log_sigmoid(f):
    return jnp.minimum(f, 0.0) - jnp.log(1.0 + jnp.exp(-jnp.abs(f)))


def _dot_nt_highest(a, b):
    return lax.dot_general(a, b, (((1,), (1,)), ((), ())), precision=HIGHEST,
                           preferred_element_type=F32)


def _chunk_gates(gc, gr):
    L = gc.shape[0]
    r = lax.broadcasted_iota(I32, (L, L), 0)
    c = lax.broadcasted_iota(I32, (L, L), 1)
    lower = (c <= r).astype(F32)
    is_f_col = lax.broadcasted_iota(I32, gc.shape, 1) >= A_HEADS
    is_f_row = lax.broadcasted_iota(I32, gr.shape, 0) >= A_HEADS
    lf_c = jnp.where(is_f_col, _log_sigmoid(gc), 0.0)
    lf_r = jnp.where(is_f_row, _log_sigmoid(gr), 0.0)
    lf_c = jnp.concatenate([lf_c, jnp.zeros((L, LANE - gc.shape[1]), F32)], axis=1)
    cum_c = jnp.dot(lower, lf_c, precision=HIGHEST, preferred_element_type=F32)[:, :gc.shape[1]]
    cum_r = _dot_nt_highest(lf_r, lower)
    return jnp.where(is_f_col, cum_c, gc), jnp.where(is_f_row, cum_r, gr)


def _prefix_max_lanes(x):
    lane = lax.broadcasted_iota(I32, x.shape, 1)
    shift = 1
    while shift < x.shape[1]:
        x = jnp.maximum(x, jnp.where(lane >= shift, pltpu.roll(x, shift, axis=1), -jnp.inf))
        shift *= 2
    return x


def _mlstm_kernel(q_ref, k_ref, vt_ref, ot_ref, gc_ref, gr_ref, hg_ref, out_ref, c_ref, m_ref):
    L = MLSTM_CHUNK

    @pl.when(pl.program_id(1) == 0)
    def _():
        c_ref[...] = jnp.zeros_like(c_ref)
        m_ref[...] = jnp.zeros_like(m_ref)

    src = lax.broadcasted_iota(I32, (L, L), 0)
    tgt = lax.broadcasted_iota(I32, (L, L), 1)
    real = lax.broadcasted_iota(I32, (A_V_PAD, 1), 0) < A_V_DIM
    gcol, grow = _chunk_gates(gc_ref[...], gr_ref[...])
    for hd in range(A_HEADS):
        qk = slice(hd * A_QK_PAD, (hd + 1) * A_QK_PAD)
        vv = slice(hd * A_V_PAD, (hd + 1) * A_V_PAD)
        q, k, vt = q_ref[0, :, qk], k_ref[0, :, qk], vt_ref[vv, :]
        u_c = gcol[:, hd:hd + 1] - gcol[:, A_HEADS + hd:A_HEADS + hd + 1]
        g_r = grow[A_HEADS + hd:A_HEADS + hd + 1, :]
        u_r = grow[hd:hd + 1, :] - g_r
        g_last = g_r[:, L - 1:L]
        m_prev = m_ref[hd, 0:1, 0:1]
        c_prev = c_ref[hd]

        run_max = jnp.maximum(_prefix_max_lanes(jnp.broadcast_to(u_r, (8, L)))[0:1, :], m_prev)
        m_t = g_r + run_max
        inter = jnp.exp(m_prev - run_max)
        decay_t = jnp.where(src <= tgt, jnp.exp(u_c - run_max), 0.0)
        p_t = (decay_t * _dot_nt(k, q)).astype(BF16)
        num_t = inter * _dot_nt(c_prev.astype(BF16), q) + _dot(vt, p_t)
        den = num_t[A_V_DIM:A_V_DIM + 1, :]
        h_t = jnp.where(real, num_t / jnp.maximum(jnp.abs(den), jnp.exp(-m_t)), 0.0)
        scale = lax.rsqrt(jnp.sum(h_t * h_t, axis=0, keepdims=True) * (1.0 / A_V_DIM) + EPS)
        out_t = h_t * scale * hg_ref[vv, :] * jax.nn.sigmoid(ot_ref[vv, :])
        out_ref[0, :, vv] = out_t.T.astype(BF16)

        w_r = g_last + u_r
        m_new = jnp.maximum(g_last + m_prev, jnp.max(w_r, axis=-1, keepdims=True))
        ev_t = (jnp.exp(w_r - m_new) * vt.astype(F32)).astype(BF16)
        c_ref[hd] = jnp.exp(g_last + m_prev - m_new) * c_prev + _dot(ev_t, k)
        m_ref[hd] = jnp.broadcast_to(m_new, m_ref.shape[1:])


def _mlstm(q, k, vt, ot, gcol, grow, head_g):
    nb, s, _ = q.shape
    L = MLSTM_CHUNK
    nv = A_HEADS * A_V_PAD
    blk = lambda w: pl.BlockSpec((1, L, w), lambda b, c: (b, c, 0))
    col = lambda n: pl.BlockSpec((n, L), lambda b, c: (0, b * (s // L) + c))
    return pl.pallas_call(
        _mlstm_kernel,
        out_shape=jax.ShapeDtypeStruct((nb, s, nv), BF16),
        grid=(nb, s // L),
        in_specs=[blk(A_HEADS * A_QK_PAD), blk(A_HEADS * A_QK_PAD), col(nv), col(nv),
                  pl.BlockSpec((L, 2 * A_HEADS), lambda b, c: (b * (s // L) + c, 0)), col(2 * A_HEADS),
                  pl.BlockSpec((nv, L), lambda b, c: (0, 0))],
        out_specs=blk(nv),
        scratch_shapes=[pltpu.VMEM((A_HEADS, A_V_PAD, A_QK_PAD), F32), pltpu.VMEM((A_HEADS, 8, LANE), F32)],
        compiler_params=_cparams(("parallel", "arbitrary")),
        name="mlstm_chunkwise",
    )(q, k, vt, ot, gcol, grow, head_g)


def _mix_out_kernel(x_ref, h_ref, mq_ref, kbd_ref, vbd_ref, gmat_ref, qg_ref, w1_ref, w2_ref, out_ref):
    mo = _memory_attention(mq_ref[...], kbd_ref[0], vbd_ref[0], gmat_ref[...], qg_ref[...])
    out_ref[...] = x_ref[...] + _dot(h_ref[...], w1_ref[...]) + _dot(mo.astype(BF16), w2_ref[...])


def _mix_out(x2, h2, mq, kbd, vbd, gmat, qg, w_main, w_mem, seq):
    t = x2.shape[0]
    tm = ROW_TILE
    row = lambda n: pl.BlockSpec((tm, n), lambda i: (i, 0))
    full = lambda a: pl.BlockSpec(a.shape, lambda i: (0,) * a.ndim)
    per_batch = pl.BlockSpec((1,) + kbd.shape[1:], lambda i: ((i * tm) // seq, 0, 0))
    qg_t = jnp.tile(qg, M_HEADS).reshape(1, -1)
    return pl.pallas_call(
        _mix_out_kernel,
        out_shape=jax.ShapeDtypeStruct((t, D_MODEL), F32),
        grid=(t // tm,),
        in_specs=[row(D_MODEL), row(h2.shape[1]), row(M_W), per_batch, per_batch,
                  full(gmat), full(qg_t), full(w_main), full(w_mem)],
        out_specs=row(D_MODEL),
        compiler_params=_cparams(("parallel",)),
        name="mixer_out_proj",
    )(x2, h2, mq, kbd, vbd, gmat, qg_t, w_main, w_mem)


def _swiglu_accumulate(acc_ref, h_ref, x, wg_ref, wu_ref, wd_ref):
    for c in range(wg_ref.shape[-1] // FFN_CHUNK):
        cols = slice(c * FFN_CHUNK, (c + 1) * FFN_CHUNK)
        gate = _dot(x, wg_ref[:, cols])
        up = _dot(x, wu_ref[:, cols])
        h_ref[:, cols] = (jax.nn.silu(gate) * up).astype(BF16)
    acc_ref[...] += _dot(h_ref[...], wd_ref[...])


def _ffn_kernel(x_ref, g_ref, wg_ref, wu_ref, wd_ref, out_ref, hn_ref, acc_ref, h_ref):
    j = pl.program_id(1)

    @pl.when(j == 0)
    def _():
        hn_ref[...] = _rms(x_ref[...], g_ref[...]).astype(BF16)
        acc_ref[...] = jnp.zeros_like(acc_ref)

    _swiglu_accumulate(acc_ref, h_ref, hn_ref[...], wg_ref, wu_ref, wd_ref)

    @pl.when(j == pl.num_programs(1) - 1)
    def _():
        out_ref[...] = x_ref[...] + acc_ref[...]


def _dense_ffn(x2, g, w_gate, w_up, w_down):
    t = x2.shape[0]
    tm, tf = FFN_ROW_TILE, FFN_FF_TILE
    return pl.pallas_call(
        _ffn_kernel,
        out_shape=jax.ShapeDtypeStruct((t, D_MODEL), F32),
        grid=(t // tm, D_FF // tf),
        in_specs=[pl.BlockSpec((tm, D_MODEL), lambda i, j: (i, 0)),
                  pl.BlockSpec((1, D_MODEL), lambda i, j: (0, 0)),
                  pl.BlockSpec((D_MODEL, tf), lambda i, j: (0, j)),
                  pl.BlockSpec((D_MODEL, tf), lambda i, j: (0, j)),
                  pl.BlockSpec((tf, D_MODEL), lambda i, j: (j, 0))],
        out_specs=pl.BlockSpec((tm, D_MODEL), lambda i, j: (i, 0)),
        scratch_shapes=[pltpu.VMEM((tm, D_MODEL), BF16), pltpu.VMEM((tm, D_MODEL), F32),
                        pltpu.VMEM((tm, tf), BF16)],
        compiler_params=_cparams(("parallel", "arbitrary")),
        name="dense_swiglu",
    )(x2, g.reshape(1, -1), w_gate, w_up, w_down)


def _rope_kernel(pos_ref, inv_ref, sign_ref, cs_ref, sn_ref):
    ang = pos_ref[...].astype(F32) * inv_ref[...]
    cs_ref[...] = jnp.cos(ang)
    sn_ref[...] = jnp.sin(ang) * sign_ref[...]


def _rope_tables(positions):
    t = positions.size
    half = QK_ROPE // 2
    inv = 1.0 / (ROPE_THETA ** (jnp.arange(0, QK_ROPE, 2, dtype=F32) / QK_ROPE))
    pad = jnp.zeros((LANE - QK_ROPE,), F32)
    inv_l = jnp.concatenate([inv, inv, pad]).reshape(1, LANE)
    sign = jnp.concatenate([-jnp.ones((half,), F32), jnp.ones((half,), F32), pad]).reshape(1, LANE)
    out = jax.ShapeDtypeStruct((t, LANE), F32)
    return pl.pallas_call(
        _rope_kernel,
        out_shape=(out, out),
        grid=(t // ROW_TILE,),
        in_specs=[pl.BlockSpec((ROW_TILE, 1), lambda i: (i, 0)),
                  pl.BlockSpec((1, LANE), lambda i: (0, 0)),
                  pl.BlockSpec((1, LANE), lambda i: (0, 0))],
        out_specs=(pl.BlockSpec((ROW_TILE, LANE), lambda i: (i, 0)),) * 2,
        compiler_params=_cparams(("parallel",)),
        name="rope_tables",
    )(positions.reshape(t, 1), inv_l, sign)


def _head_qk_norm_rope(nope, rope, rope_sw, g_nope, g_rope, g_rope_sw, cs, sn, scale):
    ss = jnp.sum(nope * nope, axis=-1, keepdims=True) + jnp.sum(rope * rope, axis=-1, keepdims=True)
    r = lax.rsqrt(ss * (1.0 / B_QK_HEAD) + EPS) * scale
    return nope * r * g_nope, (rope * g_rope * cs + rope_sw * g_rope_sw * sn) * r


def _latent_kv_body(hn, wd_ref, ga_ref, wuk_ref, wuvt_ref, kg_ref, cs_ref, sn_ref, k_ref, vt_ref):
    z = _dot(hn, wd_ref[...])
    c_kv = z[:, :KV_LORA]
    rope, rope_sw = z[:, KV_LORA:KV_LORA + LANE], z[:, KV_LORA + LANE:]
    cn = _rms(c_kv, ga_ref[...]).astype(BF16)
    kv = _dot(cn, wuk_ref[...])
    vt = _dot_nt(wuvt_ref[...], cn)
    tm = vt.shape[1]
    ones_row = (lax.broadcasted_iota(I32, (VT_HEAD_ROWS - V_HEAD, tm), 0) == 0).astype(BF16)
    for h in range(B_HEADS):
        vt_ref[h * VT_HEAD_ROWS:h * VT_HEAD_ROWS + V_HEAD, :] = vt[h * V_HEAD:(h + 1) * V_HEAD].astype(BF16)
        vt_ref[h * VT_HEAD_ROWS + V_HEAD:(h + 1) * VT_HEAD_ROWS, :] = ones_row
    kg = kg_ref[...]
    for h in range(B_HEADS):
        kn, kr = _head_qk_norm_rope(kv[:, h * QK_NOPE:(h + 1) * QK_NOPE], rope, rope_sw,
                                    kg[:, :LANE], kg[:, LANE:2 * LANE], kg[:, 2 * LANE:],
                                    cs_ref[...], sn_ref[...], 1.0)
        k_ref[:, h * B_QK_PAD:h * B_QK_PAD + QK_NOPE] = kn.astype(BF16)
        k_ref[:, h * B_QK_PAD + QK_NOPE:(h + 1) * B_QK_PAD] = kr.astype(BF16)


def _rope_swap(w):
    half = QK_ROPE // 2
    return jnp.concatenate([w[..., half:], w[..., :half]], axis=-1)


def _pad_lanes(w, n=LANE):
    return jnp.pad(w, [(0, 0)] * (w.ndim - 1) + [(0, n - w.shape[-1])])


def _head_gain(g):
    g_rope = g[QK_NOPE:]
    return jnp.concatenate([g[:QK_NOPE], _pad_lanes(g_rope), _pad_lanes(_rope_swap(g_rope))]).reshape(1, -1)


def _query_body(hn, win_ref, ga_ref, wuq_ref, qg_ref, cs_ref, sn_ref, q_ref, mq_ref):
    proj = _dot(hn, win_ref[...])
    mq_ref[...] = proj[:, Q_LORA:]
    qall = _dot(_rms(proj[:, :Q_LORA], ga_ref[...]).astype(BF16), wuq_ref[...])
    qg = qg_ref[...]
    per_head = QK_NOPE + 2 * LANE
    for h in range(B_HEADS):
        base = h * per_head
        qn, qr = _head_qk_norm_rope(qall[:, base:base + QK_NOPE],
                                    qall[:, base + QK_NOPE:base + QK_NOPE + LANE],
                                    qall[:, base + QK_NOPE + LANE:base + per_head],
                                    qg[:, :LANE], qg[:, LANE:2 * LANE], qg[:, 2 * LANE:],
                                    cs_ref[...], sn_ref[...], B_QK_HEAD ** -0.5)
        q_ref[:, h * B_QK_PAD:h * B_QK_PAD + QK_NOPE] = qn.astype(BF16)
        q_ref[:, h * B_QK_PAD + QK_NOPE:(h + 1) * B_QK_PAD] = qr.astype(BF16)


def _mla_proj_kernel(x_ref, gkv_ref, wd_ref, ga_ref, wuk_ref, wuvt_ref, kg_ref,
                     gq_ref, win_ref, gqa_ref, wuq_ref, qg_ref, cs_ref, sn_ref,
                     k_ref, vt_ref, q_ref, mq_ref):
    x = x_ref[...]
    xn = x * lax.rsqrt(jnp.mean(x * x, axis=-1, keepdims=True) + EPS)
    _latent_kv_body((xn * gkv_ref[...]).astype(BF16), wd_ref, ga_ref, wuk_ref, wuvt_ref, kg_ref,
                    cs_ref, sn_ref, k_ref, vt_ref)
    _query_body((xn * gq_ref[...]).astype(BF16), win_ref, gqa_ref, wuq_ref, qg_ref, cs_ref, sn_ref, q_ref, mq_ref)


def _mla_projection(x2, kv_norm, w_dkv, kv_a_norm, w_ukv, k_head_norm,
                    q_norm, w_in, q_a_g, w_uq, q_head_g, cs, sn):
    t = x2.shape[0]
    w_rope = w_dkv[:, KV_LORA:]
    wd = jnp.concatenate([w_dkv[:, :KV_LORA], _pad_lanes(w_rope), _pad_lanes(_rope_swap(w_rope))],
                         axis=1).astype(BF16)
    wu = w_ukv.reshape(KV_LORA, B_HEADS, QK_NOPE + V_HEAD)
    wuk = wu[:, :, :QK_NOPE].reshape(KV_LORA, -1).astype(BF16)
    wuvt = wu[:, :, QK_NOPE:].reshape(KV_LORA, -1).T.astype(BF16)
    wq = w_uq.reshape(Q_LORA, B_HEADS, B_QK_HEAD)
    wq_rope = wq[:, :, QK_NOPE:]
    wq = jnp.concatenate([wq[:, :, :QK_NOPE], _pad_lanes(wq_rope), _pad_lanes(_rope_swap(wq_rope))],
                         axis=-1).reshape(Q_LORA, -1).astype(BF16)
    consts = (kv_norm.reshape(1, -1), wd, kv_a_norm.reshape(1, -1), wuk, wuvt, _head_gain(k_head_norm),
              q_norm.reshape(1, -1), w_in.astype(BF16), q_a_g.reshape(1, -1), wq, _head_gain(q_head_g))
    row = lambda n: pl.BlockSpec((ROW_TILE, n), lambda i: (i, 0))
    full = lambda a: pl.BlockSpec(a.shape, lambda i: (0,) * a.ndim)
    return pl.pallas_call(
        _mla_proj_kernel,
        out_shape=(jax.ShapeDtypeStruct((t, B_HEADS * B_QK_PAD), BF16),
                   jax.ShapeDtypeStruct((B_HEADS * VT_HEAD_ROWS, t), BF16),
                   jax.ShapeDtypeStruct((t, B_HEADS * B_QK_PAD), BF16),
                   jax.ShapeDtypeStruct((t, M_W), F32)),
        grid=(t // ROW_TILE,),
        in_specs=[row(D_MODEL)] + [full(a) for a in consts] + [row(LANE), row(LANE)],
        out_specs=(row(B_HEADS * B_QK_PAD), pl.BlockSpec((B_HEADS * VT_HEAD_ROWS, ROW_TILE), lambda i: (0, i)),
                   row(B_HEADS * B_QK_PAD), row(M_W)),
        compiler_params=_cparams(("parallel",)),
        name="mla_proj",
    )(x2, *consts, cs, sn)


def _attn_kernel(q_ref, k_ref, vt_ref, out_ref):
    tq, tk = ATT_Q_TILE, ATT_K_TILE
    i = pl.program_id(2)

    def block(h, j, carry, masked):
        m, acc = carry
        off = pl.multiple_of(j * tk, tk)
        q = q_ref[0, :, h * B_QK_PAD:(h + 1) * B_QK_PAD]
        st = _dot_nt(k_ref[0, pl.ds(off, tk), h * B_QK_PAD:(h + 1) * B_QK_PAD], q)
        if masked:
            key = lax.broadcasted_iota(I32, (tk, tq), 0)
            qry = lax.broadcasted_iota(I32, (tk, tq), 1)
            st = jnp.where(key <= qry, st, -jnp.inf)
        m_new = jnp.maximum(m, jnp.max(st, axis=0, keepdims=True))
        p = jnp.exp(st - m_new).astype(BF16)
        vt = vt_ref[h * VT_HEAD_ROWS:(h + 1) * VT_HEAD_ROWS, pl.ds(off, tk)]
        return m_new, jnp.exp(m - m_new) * acc + _dot(vt, p)

    heads = range(ATT_HEADS_PER_STEP)
    init = tuple((jnp.full((1, tq), -jnp.inf, F32), jnp.zeros((VT_HEAD_ROWS, tq), F32)) for _ in heads)
    carry = lax.fori_loop(0, i, lambda j, c: tuple(block(h, j, c[h], False) for h in heads), init)
    for h in heads:
        _, acc = block(h, i, carry[h], True)
        out_t = acc[:V_HEAD] / acc[V_HEAD:V_HEAD + 1]
        out_ref[0, :, h * V_HEAD:(h + 1) * V_HEAD] = out_t.T.astype(BF16)


def _causal_attention(q, k, vt, seq):
    nb = q.shape[0]
    g = ATT_HEADS_PER_STEP
    assert ATT_Q_TILE == ATT_K_TILE and B_HEADS % g == 0
    return pl.pallas_call(
        _attn_kernel,
        out_shape=jax.ShapeDtypeStruct((nb, seq, B_HEADS * V_HEAD), BF16),
        grid=(nb, B_HEADS // g, seq // ATT_Q_TILE),
        in_specs=[pl.BlockSpec((1, ATT_Q_TILE, g * B_QK_PAD), lambda b, h, i: (b, i, h)),
                  pl.BlockSpec((1, seq, g * B_QK_PAD), lambda b, h, i: (b, 0, h)),
                  pl.BlockSpec((g * VT_HEAD_ROWS, seq), lambda b, h, i: (h, b))],
        out_specs=pl.BlockSpec((1, ATT_Q_TILE, g * V_HEAD), lambda b, h, i: (b, i, h)),
        compiler_params=_cparams(("parallel", "parallel", "arbitrary")),
        name="causal_attention",
    )(q, k, vt)


def _router_kernel(x_ref, g_ref, wr_ref, hn_ref, idx_ref, gate_ref, count_ref):
    hn = _rms(x_ref[...], g_ref[...])
    hn_ref[...] = hn.astype(BF16)
    logits = jnp.dot(hn, wr_ref[...], precision=HIGHEST, preferred_element_type=F32)
    lane = lax.broadcasted_iota(I32, logits.shape, 1)
    logits = jnp.where(lane < N_EXPERTS, logits, -jnp.inf)
    v1 = jnp.max(logits, axis=-1, keepdims=True)
    i1 = jnp.min(jnp.where(logits == v1, lane, LANE), axis=-1, keepdims=True)
    rest = jnp.where(lane == i1, -jnp.inf, logits)
    v2 = jnp.max(rest, axis=-1, keepdims=True)
    i2 = jnp.min(jnp.where(rest == v2, lane, LANE), axis=-1, keepdims=True)
    e2 = jnp.exp(v2 - v1)
    den = 1.0 + e2
    idx_ref[...] = jnp.where(lane == 0, i1, jnp.where(lane == 1, i2, 0))
    record = jnp.zeros(logits.shape, F32)
    for k, gate in enumerate((1.0 / den, e2 / den)):
        hi = gate.astype(BF16).astype(F32)
        mid = (gate - hi).astype(BF16).astype(F32)
        for part, term in enumerate((hi, mid, gate - hi - mid)):
            record = jnp.where(lane == GATE_TERMS * k + part, term, record)
    record = jnp.where(lane == 2 * GATE_TERMS, i1.astype(F32), record)
    record = jnp.where(lane == 2 * GATE_TERMS + 1, i2.astype(F32), record)
    gate_ref[...] = record.astype(BF16)
    pairs = (lane == i1).astype(F32) + (lane == i2).astype(F32)
    count_ref[...] = jnp.broadcast_to(jnp.sum(pairs, axis=0, keepdims=True), count_ref.shape)


def _router(x2, g, w_router):
    t = x2.shape[0]
    row = lambda n: pl.BlockSpec((RANK_TILE, n), lambda i: (i, 0))
    wr = _pad_lanes(w_router)
    return pl.pallas_call(
        _router_kernel,
        out_shape=(jax.ShapeDtypeStruct((t, D_MODEL), BF16), jax.ShapeDtypeStruct((t, LANE), I32),
                   jax.ShapeDtypeStruct((t, LANE), BF16), jax.ShapeDtypeStruct((8 * (t // RANK_TILE), LANE), F32)),
        grid=(t // RANK_TILE,),
        in_specs=[row(D_MODEL), pl.BlockSpec((1, D_MODEL), lambda i: (0, 0)),
                  pl.BlockSpec(wr.shape, lambda i: (0, 0))],
        out_specs=(row(D_MODEL), row(LANE), row(LANE), pl.BlockSpec((8, LANE), lambda i: (i, 0))),
        compiler_params=_cparams(("parallel",)),
        name="moe_router",
    )(x2, g.reshape(1, -1), wr)


def _aligned(count):
    return jnp.ceil(count * (1.0 / RUN_ALIGN)) * RUN_ALIGN


def _rank_kernel(idx_ref, counts_ref, lpos_ref, lposr_ref, tile_ref, runs_ref, run_ref, start_ref):
    blk = pl.program_id(0)
    tb = RANK_TILE
    lane = lax.broadcasted_iota(I32, (tb, LANE), 1)
    idx = idx_ref[...]
    oh0 = (lane == idx[:, 0:1]).astype(F32)
    oh1 = (lane == idx[:, 1:2]).astype(F32)
    both = oh0 + oh1
    run_len = _aligned(jnp.sum(both, axis=0, keepdims=True))
    r = lax.broadcasted_iota(I32, (LANE, LANE), 0)
    c = lax.broadcasted_iota(I32, (LANE, LANE), 1)
    before = (r < c).astype(F32)

    @pl.when(blk == 0)
    def _():
        sizes = jnp.sum(_aligned(counts_ref[...]), axis=0, keepdims=True) * 0.125
        tiles = jnp.ceil(sizes * (1.0 / MOE_ROW_TILE))
        tile_start = jnp.dot(tiles, before, precision=HIGHEST, preferred_element_type=F32)
        start_ref[...] = tile_start * MOE_ROW_TILE
        tile_end = tile_start + tiles
        n_col = r.astype(F32)
        ended = ((n_col >= tile_end) & (c < N_EXPERTS)).astype(F32)
        expert = jnp.sum(ended, axis=-1, keepdims=True)
        total = jnp.max(tile_end, axis=-1, keepdims=True)
        as_column = lambda row: jnp.sum(jnp.where(r == c, row, 0.0), axis=-1, keepdims=True)
        tail_start = as_column(start_ref[...] + sizes)
        tail_len = as_column(tiles * MOE_ROW_TILE - sizes)
        mine = expert == c.astype(F32)
        group_rows = jnp.sum(jnp.where(mine, sizes, 0.0), axis=-1, keepdims=True)
        group_tile = jnp.sum(jnp.where(mine, tile_start, 0.0), axis=-1, keepdims=True)
        used_rows = jnp.clip(group_rows - (n_col[:, 0:1] - group_tile) * MOE_ROW_TILE, 0.0, MOE_ROW_TILE)
        col = lax.broadcasted_iota(I32, tile_ref.shape, 1)
        tile_ref[...] = jnp.where(col == 0, expert, jnp.where(col == 1, total, jnp.where(
            col == 2, tail_start, jnp.where(col == 3, tail_len, used_rows)))).astype(I32)
        run_ref[...] = jnp.zeros_like(run_ref)
        runs_ref[...] = jnp.zeros_like(runs_ref)

    rr = lax.broadcasted_iota(I32, (tb, tb), 0)
    cc = lax.broadcasted_iota(I32, (tb, tb), 1)
    strict = (cc < rr).astype(BF16)
    local = jnp.dot(run_len, before, precision=HIGHEST, preferred_element_type=F32)
    base = _dot(strict, both.astype(BF16)) + local
    p0 = jnp.sum(oh0 * base, axis=-1, keepdims=True)
    p1 = jnp.sum(oh1 * base, axis=-1, keepdims=True)
    posf = jnp.where(lane == 0, p0, jnp.where(lane == 1, p1, 0.0))
    lpos_ref[...] = posf.astype(I32)
    pick = (lax.broadcasted_iota(I32, (8, LANE), 0) == lax.broadcasted_iota(I32, (8, LANE), 1)).astype(F32)
    lposr_ref[...] = _dot_nt_highest(pick, posf).astype(I32)
    n = runs_ref.shape[0] // 3
    mine = lax.broadcasted_iota(I32, (n, LANE), 0) == blk
    for k, value in enumerate((local, run_ref[...] + start_ref[...], run_len)):
        runs_ref[k * n:(k + 1) * n, :] = jnp.where(mine, value.astype(I32), runs_ref[k * n:(k + 1) * n, :])
    run_ref[...] += run_len


def _rank(idx, counts):
    t = idx.shape[0]
    nblk = t // RANK_TILE
    const = lambda shape: pl.BlockSpec(shape, lambda i: (0, 0))
    return pl.pallas_call(
        _rank_kernel,
        out_shape=(jax.ShapeDtypeStruct((t, LANE), I32), jax.ShapeDtypeStruct((8, t), I32),
                   jax.ShapeDtypeStruct((LANE, 8), I32), jax.ShapeDtypeStruct((3 * nblk, LANE), I32)),
        grid=(nblk,),
        in_specs=[pl.BlockSpec((RANK_TILE, LANE), lambda i: (i, 0)), const(counts.shape)],
        out_specs=(pl.BlockSpec((RANK_TILE, LANE), lambda i: (i, 0)),
                   pl.BlockSpec((8, RANK_TILE), lambda i: (0, i)),
                   const((LANE, 8)), const((3 * nblk, LANE))),
        scratch_shapes=[pltpu.VMEM((1, LANE), F32), pltpu.VMEM((1, LANE), F32)],
        compiler_params=_cparams(("arbitrary",)),
        name="moe_rank",
    )(idx, counts)


def _for_each_piece(length, fn):
    for size in RUN_PIECES:
        @pl.when((length & size) != 0)
        def _(size=size):
            fn(pl.multiple_of(length & (-2 * size), RUN_ALIGN), size)


def _for_each_run_piece(blk, loc_ref, dst_ref, len_ref, fn):
    for e in range(N_EXPERTS):
        k = blk * N_EXPERTS + e
        loc, dst = loc_ref[k], dst_ref[k]
        _for_each_piece(len_ref[k], lambda done, size, loc=loc, dst=dst: fn(
            pl.multiple_of(loc + done, RUN_ALIGN), pl.multiple_of(dst + done, RUN_ALIGN), size))


def _dispatch_kernel(loc_ref, dst_ref, len_ref, tail_ref, tail_len_ref, ntiles_ref, hn_ref, gate_ref, lposr_ref,
                     xs_ref, local_ref, zero_ref, sems):
    blk = pl.program_id(0)

    @pl.when(blk == 0)
    def _():
        sem = sems.at[2]
        zero_ref[...] = jnp.zeros_like(zero_ref)

        def fill(e):
            start = tail_ref[e]
            return lambda done, size: pltpu.make_async_copy(
                zero_ref.at[pl.ds(0, size), :], xs_ref.at[pl.ds(pl.multiple_of(start + done, RUN_ALIGN), size), :], sem)

        def fill_tile(i):
            return pltpu.make_async_copy(
                zero_ref, xs_ref.at[pl.ds(pl.multiple_of(i * RANK_TILE, RANK_TILE), RANK_TILE), :], sem)

        def unused_tiles(action):
            def body(i, carry):
                action(fill_tile(i))
                return carry
            lax.fori_loop(ntiles_ref[0] * (MOE_ROW_TILE // RANK_TILE), xs_ref.shape[0] // RANK_TILE, body, 0)

        for e in range(N_EXPERTS):
            _for_each_piece(tail_len_ref[e], lambda *a, e=e: fill(e)(*a).start())
        unused_tiles(lambda copy: copy.start())
        for e in range(N_EXPERTS):
            _for_each_piece(tail_len_ref[e], lambda *a, e=e: fill(e)(*a).wait())
        unused_tiles(lambda copy: copy.wait())

    rows = lax.broadcasted_iota(I32, (LOCAL_ROWS, 1), 0)
    sel = jnp.where(lposr_ref[0:1, :] == rows, 1.0, jnp.where(lposr_ref[1:2, :] == rows, 1.0, 0.0)).astype(BF16)

    def step(local_ref, sem, other_ref, other_sem):
        local_ref[:, :D_MODEL] = _dot(sel, hn_ref[...]).astype(BF16)
        local_ref[:, D_MODEL:] = _dot(sel, gate_ref[...]).astype(BF16)

        def copy(buf, buf_sem):
            return lambda loc, dst, size: pltpu.make_async_copy(
                buf.at[pl.ds(loc, size), :], xs_ref.at[pl.ds(dst, size), :], buf_sem)

        _for_each_run_piece(blk, loc_ref, dst_ref, len_ref, lambda *a: copy(local_ref, sem)(*a).start())

        @pl.when(blk > 0)
        def _():
            _for_each_run_piece(blk - 1, loc_ref, dst_ref, len_ref, lambda *a: copy(other_ref, other_sem)(*a).wait())

        @pl.when(blk == pl.num_programs(0) - 1)
        def _():
            _for_each_run_piece(blk, loc_ref, dst_ref, len_ref, lambda *a: copy(local_ref, sem)(*a).wait())

    @pl.when(blk % 2 == 0)
    def _():
        step(local_ref.at[0], sems.at[0], local_ref.at[1], sems.at[1])

    @pl.when(blk % 2 == 1)
    def _():
        step(local_ref.at[1], sems.at[1], local_ref.at[0], sems.at[0])


def _dispatch(runs, tails, hn, gates, lpos_rows, rows):
    nblk = hn.shape[0] // RANK_TILE
    tok = lambda n: pl.BlockSpec((RANK_TILE, n), lambda b, *_: (b, 0))
    return pl.pallas_call(
        _dispatch_kernel,
        out_shape=jax.ShapeDtypeStruct((rows, XS_WIDTH), BF16),
        grid_spec=pltpu.PrefetchScalarGridSpec(
            num_scalar_prefetch=6,
            grid=(nblk,),
            in_specs=[tok(D_MODEL), tok(LANE), pl.BlockSpec((8, RANK_TILE), lambda b, *_: (0, b))],
            out_specs=pl.BlockSpec(memory_space=pl.ANY),
            scratch_shapes=[pltpu.VMEM((2, LOCAL_ROWS, XS_WIDTH), BF16), pltpu.VMEM((RANK_TILE, XS_WIDTH), BF16),
                            pltpu.SemaphoreType.DMA((3,))],
        ),
        compiler_params=_cparams(("arbitrary",)),
        name="moe_dispatch",
    )(*runs, *tails, hn, gates, lpos_rows)


def _moe_kernel(expert_ref, ntiles_ref, used_ref, x_ref, wg_ref, wu_ref, wd_ref, out_ref, acc_ref, h_ref):
    i, j = pl.program_id(0), pl.program_id(1)
    active = i < ntiles_ref[0]
    half = MOE_ROW_TILE // 2
    upper_used = used_ref[i] > half

    @pl.when(j == 0)
    def _():
        acc_ref[...] = jnp.zeros_like(acc_ref)

    @pl.when(active & upper_used)
    def _():
        _swiglu_accumulate(acc_ref, h_ref, x_ref[:, :D_MODEL], wg_ref.at[0], wu_ref.at[0], wd_ref.at[0])

    @pl.when(active & jnp.logical_not(upper_used))
    def _():
        lower = pl.ds(0, half)
        _swiglu_accumulate(acc_ref.at[lower, :], h_ref.at[lower, :], x_ref[:half, :D_MODEL],
                           wg_ref.at[0], wu_ref.at[0], wd_ref.at[0])

    @pl.when(j == pl.num_programs(1) - 1)
    def _():
        gs = x_ref[:, D_MODEL:].astype(F32)
        gate = [sum(gs[:, GATE_TERMS * k + n:GATE_TERMS * k + n + 1] for n in range(GATE_TERMS)) for k in range(TOP_K)]
        first = gs[:, TOP_K * GATE_TERMS:TOP_K * GATE_TERMS + 1] == expert_ref[i].astype(F32)
        row_gate = jnp.where(first, gate[0], gate[1])
        out_ref[...] = jnp.where(active, acc_ref[...] * row_gate, 0.0).astype(BF16)


def _moe_experts(tile_expert, n_tiles, tile_used, xs, w_gate, w_up, w_down):
    rows = xs.shape[0]
    tm, tf = MOE_ROW_TILE, FFN_FF_TILE
    nj = D_FF // tf

    def x_map(i, j, e_ref, n_ref, u_ref):
        return jnp.clip(i, 0, jnp.maximum(n_ref[0] - 1, 0)), 0

    def w_idx(i, j, expert_ref, ntiles_ref):
        e = jnp.minimum(expert_ref[i], N_EXPERTS - 1)
        return e, jnp.where(i < ntiles_ref[0], j, nj - 1)

    def w_up_map(i, j, e_ref, n_ref, u_ref):
        e, jj = w_idx(i, j, e_ref, n_ref)
        return e, 0, jj

    def w_down_map(i, j, e_ref, n_ref, u_ref):
        e, jj = w_idx(i, j, e_ref, n_ref)
        return e, jj, 0

    return pl.pallas_call(
        _moe_kernel,
        out_shape=jax.ShapeDtypeStruct((rows, D_MODEL), BF16),
        grid_spec=pltpu.PrefetchScalarGridSpec(
            num_scalar_prefetch=3,
            grid=(rows // tm, nj),
            in_specs=[pl.BlockSpec((tm, XS_WIDTH), x_map),
                      pl.BlockSpec((1, D_MODEL, tf), w_up_map),
                      pl.BlockSpec((1, D_MODEL, tf), w_up_map),
                      pl.BlockSpec((1, tf, D_MODEL), w_down_map)],
            out_specs=pl.BlockSpec((tm, D_MODEL), lambda i, j, *_: (i, 0)),
            scratch_shapes=[pltpu.VMEM((tm, D_MODEL), F32), pltpu.VMEM((tm, tf), BF16)],
        ),
        compiler_params=_cparams(("arbitrary", "arbitrary")),
        name="moe_experts",
    )(tile_expert, n_tiles, tile_used, xs, w_gate, w_up, w_down)


def _combine_kernel(loc_ref, dst_ref, len_ref, x_ref, lpos_ref, ys_ref, out_ref, local_ref, sems):
    blk = pl.program_id(0)
    rows = lax.broadcasted_iota(I32, (1, LOCAL_ROWS), 1)
    lpos = lpos_ref[...]
    sel = jnp.where(lpos[:, 0:1] == rows, 1.0, jnp.where(lpos[:, 1:2] == rows, 1.0, 0.0)).astype(BF16)

    def fetch(block, buf, buf_sem, action):
        _for_each_run_piece(block, loc_ref, dst_ref, len_ref, lambda loc, dst, size: action(pltpu.make_async_copy(
            ys_ref.at[pl.ds(dst, size), :], buf.at[pl.ds(loc, size), :], buf_sem)))

    def start_fetch(block, buf, buf_sem):
        buf[...] = jnp.zeros_like(buf)
        fetch(block, buf, buf_sem, lambda copy: copy.start())

    def step(buf, buf_sem, other, other_sem):
        @pl.when(blk == 0)
        def _():
            start_fetch(blk, buf, buf_sem)

        @pl.when(blk + 1 < pl.num_programs(0))
        def _():
            start_fetch(blk + 1, other, other_sem)

        fetch(blk, buf, buf_sem, lambda copy: copy.wait())
        out_ref[...] = x_ref[...] + _dot(sel, buf[...])

    @pl.when(blk % 2 == 0)
    def _():
        step(local_ref.at[0], sems.at[0], local_ref.at[1], sems.at[1])

    @pl.when(blk % 2 == 1)
    def _():
        step(local_ref.at[1], sems.at[1], local_ref.at[0], sems.at[0])


def _combine(runs, x2, lpos, ys):
    tok = lambda n: pl.BlockSpec((RANK_TILE, n), lambda b, *_: (b, 0))
    return pl.pallas_call(
        _combine_kernel,
        out_shape=jax.ShapeDtypeStruct(x2.shape, F32),
        grid_spec=pltpu.PrefetchScalarGridSpec(
            num_scalar_prefetch=3,
            grid=(x2.shape[0] // RANK_TILE,),
            in_specs=[tok(D_MODEL), tok(LANE), pl.BlockSpec(memory_space=pl.ANY)],
            out_specs=tok(D_MODEL),
            scratch_shapes=[pltpu.VMEM((2, LOCAL_ROWS, D_MODEL), BF16), pltpu.SemaphoreType.DMA((2,))],
        ),
        compiler_params=_cparams(("arbitrary",)),
        name="moe_combine",
    )(*runs, x2, lpos, ys)


def _moe_ffn(x2, g, w_router, w_gate_up, w_down):
    t = x2.shape[0]
    nblk = t // RANK_TILE
    hn, idx, gates, counts = _router(x2, g, w_router)
    lpos, lpos_rows, tile_info, run_tab = _rank(idx, counts)
    runs = tuple(run_tab.reshape(3, nblk, LANE)[:, :, :N_EXPERTS].reshape(3, -1))
    rows = t * TOP_K + nblk * N_EXPERTS * (RUN_ALIGN - 1) + N_EXPERTS * (MOE_ROW_TILE - 1)
    rows = -(-rows // MOE_ROW_TILE) * MOE_ROW_TILE
    assert rows // MOE_ROW_TILE <= LANE
    assert MOE_ROW_TILE % RANK_TILE == 0 and rows % RANK_TILE == 0
    xs = _dispatch(runs, (tile_info[:, 2], tile_info[:, 3], tile_info[:1, 1]), hn, gates, lpos_rows, rows)
    ys = _moe_experts(tile_info[:, 0], tile_info[:1, 1], tile_info[:, 4], xs,
                      w_gate_up[..., :D_FF].astype(BF16), w_gate_up[..., D_FF:].astype(BF16),
                      w_down.astype(BF16))
    return _combine(runs, x2, lpos, ys)


def _pad_heads(w, heads, dim, pad):
    w = w.reshape(w.shape[:-1] + (heads, dim))
    return _pad_lanes(w, pad).reshape(w.shape[:-2] + (heads * pad,))


def kernel(x, mem, positions, a_norm, a_w_in, a_gate_bias, a_head_norm, a_w_out, b_norm, b_w_in, b_q_a_norm, b_w_uq, b_q_head_norm, b_w_out, kv_norm, w_dkv, kv_a_norm, w_ukv, k_head_norm, mem_norm, mem_w_kv, mem_q_norm, mem_k_norm, ffn_norm, dense_w_gate_up, dense_w_down, moe_router, moe_w_gate_up, moe_w_down):
    nb, seq, _ = x.shape
    t = nb * seq
    x2 = x.reshape(t, D_MODEL)
    gmat = jnp.kron(jnp.eye(M_HEADS, dtype=F32), jnp.full((M_HEAD_DIM, M_HEAD_DIM), 1.0 / M_HEAD_DIM, F32)).astype(BF16)

    kbd0, vbd0 = _memory_kv(mem, mem_norm[0], mem_w_kv[0], mem_k_norm[0], gmat)
    w_in = a_w_in[0]
    qk_w, v_w = A_HEADS * A_QK_DIM, A_HEADS * A_V_DIM
    o0, o1, o2, o3, o4 = qk_w, 2 * qk_w, 2 * qk_w + v_w, 2 * qk_w + 2 * v_w, 2 * qk_w + 2 * v_w + 2 * A_HEADS
    w_main = jnp.concatenate([
        _pad_heads(w_in[:, :o0], A_HEADS, A_QK_DIM, A_QK_PAD),
        _pad_heads(w_in[:, o0:o1], A_HEADS, A_QK_DIM, A_QK_PAD),
        w_in[:, o4:]], axis=1).astype(BF16)
    w_vo_t = jnp.concatenate([
        _pad_heads(w_in[:, o1:o2], A_HEADS, A_V_DIM, A_V_PAD),
        _pad_heads(w_in[:, o2:o3], A_HEADS, A_V_DIM, A_V_PAD)], axis=1).T.astype(BF16)
    q, k, vt, ot, mq, gc, gr = _a_projection(x2, a_norm[0], w_main, w_vo_t, w_in[:, o3:o4], a_gate_bias[0])
    three = lambda a: a.reshape(nb, seq, a.shape[-1])
    head_g = _pad_heads(a_head_norm[0].reshape(1, -1), A_HEADS, A_V_DIM, A_V_PAD)
    hm = _mlstm(three(q), three(k), vt, ot, gc, gr,
                jnp.broadcast_to(head_g.reshape(-1, 1), (A_HEADS * A_V_PAD, MLSTM_CHUNK)))
    w_out = a_w_out[0]
    w_out_h = jnp.pad(w_out[:v_w].reshape(A_HEADS, A_V_DIM, D_MODEL), ((0, 0), (0, A_V_PAD - A_V_DIM), (0, 0)))
    w_out_h = w_out_h.reshape(A_HEADS * A_V_PAD, D_MODEL).astype(BF16)
    x2 = _mix_out(x2, hm.reshape(t, -1), mq, kbd0, vbd0, gmat, mem_q_norm[0],
                  w_out_h, w_out[v_w:].astype(BF16), seq)
    wgu = dense_w_gate_up[0]
    x2 = _dense_ffn(x2, ffn_norm[0], wgu[:, :D_FF].astype(BF16), wgu[:, D_FF:].astype(BF16),
                    dense_w_down[0].astype(BF16))

    cs, sn = _rope_tables(positions)
    k_sh, vt_sh, qh, mq1 = _mla_projection(x2, kv_norm, w_dkv, kv_a_norm, w_ukv, k_head_norm,
                                           b_norm[0], b_w_in[0], b_q_a_norm[0], b_w_uq[0], b_q_head_norm[0], cs, sn)

    kbd1, vbd1 = _memory_kv(mem, mem_norm[1], mem_w_kv[1], mem_k_norm[1], gmat)
    att = _causal_attention(three(qh), three(k_sh), vt_sh, seq)
    w_out = b_w_out[0]
    n_att = B_HEADS * V_HEAD
    x2 = _mix_out(x2, att.reshape(t, -1), mq1, kbd1, vbd1, gmat, mem_q_norm[1],
                  w_out[:n_att].astype(BF16), w_out[n_att:].astype(BF16), seq)
    x2 = _moe_ffn(x2, ffn_norm[1], moe_router[0], moe_w_gate_up[0], moe_w_down[0])
    return x2.reshape(nb, seq, D_MODEL)
```

```python
import functools

import jax
import jax.numpy as jnp
from jax import lax
from jax.experimental import pallas as pl
from jax.experimental.pallas import tpu as pltpu

F32 = jnp.float32
BF16 = jnp.bfloat16
I32 = jnp.int32

EPS = 1e-6
LOG2E = 1.4426950408889634
LANE = 128
VMEM_LIMIT = 48 * 1024 * 1024

D_MODEL = 1024
N_MEM = 256
M_HEADS, M_HEAD_DIM = 4, 64
M_W = M_HEADS * M_HEAD_DIM
A_HEADS, A_QK_DIM, A_V_DIM = 4, 96, 192
A_QK_PAD, A_V_PAD = 128, 256
B_HEADS, Q_LORA, KV_LORA = 6, 384, 256
QK_NOPE, QK_ROPE, V_HEAD = 128, 64, 128
B_QK_HEAD = QK_NOPE + QK_ROPE
B_QK_PAD = 256
VT_HEAD_ROWS = V_HEAD + 16
ROPE_THETA = 10000.0
D_FF = 3584
N_EXPERTS, TOP_K = 8, 2
GATE_TERMS = 3

MLSTM_CHUNK = 512
ROW_TILE = 512
FFN_ROW_TILE = 1024
FFN_FF_TILE = 1792
FFN_CHUNK = 256
MOE_ROW_TILE = 1024
MOE_SKIP_ROWS = 256
ATT_Q_TILE = 512
ATT_K_TILE = 512
ATT_HEADS_PER_STEP = 6
RANK_TILE = 512
RUN_ALIGN = 16
RUN_PIECES = tuple(RANK_TILE >> s for s in range((RANK_TILE // RUN_ALIGN).bit_length()))
LOCAL_ROWS = TOP_K * RANK_TILE + N_EXPERTS * RUN_ALIGN
XS_WIDTH = D_MODEL + LANE

HIGHEST = lax.Precision.HIGHEST


def _cparams(sem):
    return pltpu.CompilerParams(dimension_semantics=sem, vmem_limit_bytes=VMEM_LIMIT)


def _rms(x, g):
    return x * lax.rsqrt(jnp.mean(x * x, axis=-1, keepdims=True) + EPS) * g


def _dot(a, b):
    return jnp.dot(a, b, preferred_element_type=F32)


def _dot_nt(a, b):
    return lax.dot_general(a, b, (((1,), (1,)), ((), ())), preferred_element_type=F32)


def _dot_tn(a, b):
    return lax.dot_general(a, b, (((0,), (0,)), ((), ())), preferred_element_type=F32)


def _group_mean_sq(x, gmat):
    sq = x * x
    hi = sq.astype(BF16)
    lo = (sq - hi.astype(F32)).astype(BF16)
    return _dot(hi, gmat) + _dot(lo, gmat)


def _memkv_kernel(mem_ref, g_ref, w_ref, kg_ref, gmat_ref, kbd_ref, vbd_ref):
    hn = _rms(mem_ref[0], g_ref[...]).astype(BF16)
    kv = _dot(hn, w_ref[...])
    k, v = kv[:, :M_W], kv[:, M_W:]
    kn = k * lax.rsqrt(_group_mean_sq(k, gmat_ref[...]) + EPS) * kg_ref[...]
    lane_head = lax.broadcasted_iota(I32, (1, M_W), 1) // M_HEAD_DIM
    for h in range(M_HEADS):
        keep = lane_head == h
        kbd_ref[0, h * N_MEM:(h + 1) * N_MEM, :] = jnp.where(keep, kn, 0.0).astype(BF16)
        vbd_ref[0, h * N_MEM:(h + 1) * N_MEM, :] = jnp.where(keep, v, 0.0).astype(BF16)


def _memory_kv(mem, g, w_kv, k_g, gmat):
    nb = mem.shape[0]
    out = jax.ShapeDtypeStruct((nb, M_HEADS * N_MEM, M_W), BF16)
    return pl.pallas_call(
        _memkv_kernel,
        out_shape=(out, out),
        grid=(nb,),
        in_specs=[
            pl.BlockSpec((1, N_MEM, D_MODEL), lambda b: (b, 0, 0)),
            pl.BlockSpec((1, D_MODEL), lambda b: (0, 0)),
            pl.BlockSpec((D_MODEL, 2 * M_W), lambda b: (0, 0)),
            pl.BlockSpec((1, M_W), lambda b: (0, 0)),
            pl.BlockSpec((M_W, M_W), lambda b: (0, 0)),
        ],
        out_specs=(pl.BlockSpec((1, M_HEADS * N_MEM, M_W), lambda b: (b, 0, 0)),) * 2,
        compiler_params=_cparams(("parallel",)),
        name="memory_kv",
    )(mem, g.reshape(1, -1), w_kv.astype(BF16), jnp.tile(k_g, M_HEADS).reshape(1, -1), gmat)


def _memory_attention(mq, kbd, vbd, gmat, qg):
    qn = mq * lax.rsqrt(_group_mean_sq(mq, gmat) + EPS) * (qg * (M_HEAD_DIM ** -0.5 * LOG2E))
    s = _dot_nt(qn.astype(BF16), kbd)
    ps = []
    for h in range(M_HEADS):
        sh = s[:, h * N_MEM:(h + 1) * N_MEM]
        e = jnp.exp2(sh - jnp.max(sh, axis=-1, keepdims=True))
        ps.append((e / jnp.sum(e, axis=-1, keepdims=True)).astype(BF16))
    return _dot(jnp.concatenate(ps, axis=-1), vbd)


def _a_proj_kernel(x_ref, g_ref, w_ref, wvot_ref, wif_ref, wift_ref, bc_ref, br_ref,
                   q_ref, k_ref, vt_ref, ot_ref, mq_ref, gc_ref, gr_ref):
    hn = _rms(x_ref[...], g_ref[...]).astype(BF16)
    nq = A_HEADS * A_QK_PAD
    nv = A_HEADS * A_V_PAD
    q_ref[...] = _dot(hn, w_ref[:, :nq]).astype(BF16)
    k_ref[...] = (_dot(hn, w_ref[:, nq:2 * nq]) * (A_QK_DIM ** -0.5)).astype(BF16)
    mq_ref[...] = _dot(hn, w_ref[:, 2 * nq:])
    vt = _dot_nt(wvot_ref[:nv, :], hn)
    ones_row = lax.broadcasted_iota(I32, (nv, 1), 0) % A_V_PAD == A_V_DIM
    vt_ref[...] = jnp.where(ones_row, 1.0, vt).astype(BF16)
    ot_ref[...] = _dot_nt(wvot_ref[nv:, :], hn)
    gc_ref[...] = _dot(hn, wif_ref[...])[:, :2 * A_HEADS] + bc_ref[...]
    gr_ref[...] = _dot_nt(wift_ref[...], hn) + br_ref[...]


def _a_projection(x2, g, w_main, w_vo_t, w_if, gate_bias):
    t = x2.shape[0]
    nq, nv = A_HEADS * A_QK_PAD, A_HEADS * A_V_PAD
    ng = 2 * A_HEADS
    wif_pad = jnp.pad(w_if, ((0, 0), (0, LANE - ng))).astype(BF16)
    row = lambda n: pl.BlockSpec((ROW_TILE, n), lambda i: (i, 0))
    col = lambda n: pl.BlockSpec((n, ROW_TILE), lambda i: (0, i))
    full = lambda a: pl.BlockSpec(a.shape, lambda i: (0,) * a.ndim)
    args = (x2, g.reshape(1, -1), w_main, w_vo_t, wif_pad, w_if.T.astype(BF16),
            gate_bias.reshape(1, ng), gate_bias.reshape(ng, 1))
    return pl.pallas_call(
        _a_proj_kernel,
        out_shape=(jax.ShapeDtypeStruct((t, nq), BF16), jax.ShapeDtypeStruct((t, nq), BF16),
                   jax.ShapeDtypeStruct((nv, t), BF16), jax.ShapeDtypeStruct((nv, t), F32),
                   jax.ShapeDtypeStruct((t, M_W), F32), jax.ShapeDtypeStruct((t, ng), F32),
                   jax.ShapeDtypeStruct((ng, t), F32)),
        grid=(t // ROW_TILE,),
        in_specs=[row(D_MODEL)] + [full(a) for a in args[1:]],
        out_specs=(row(nq), row(nq), col(nv), col(nv), row(M_W), row(ng), col(ng)),
        compiler_params=_cparams(("parallel",)),
        name="mlstm_in_proj",
    )(*args)


def _log_sigmoid(f):
    return jnp.minimum(f, 0.0) - jnp.log(1.0 + jnp.exp(-jnp.abs(f)))


def _dot_nt_highest(a, b):
    return lax.dot_general(a, b, (((1,), (1,)), ((), ())), precision=HIGHEST,
                           preferred_element_type=F32)


def _chunk_gates(gc, gr):
    L = gc.shape[0]
    r = lax.broadcasted_iota(I32, (L, L), 0)
    c = lax.broadcasted_iota(I32, (L, L), 1)
    lower = (c <= r).astype(F32)
    is_f_col = lax.broadcasted_iota(I32, gc.shape, 1) >= A_HEADS
    is_f_row = lax.broadcasted_iota(I32, gr.shape, 0) >= A_HEADS
    lf_c = jnp.where(is_f_col, _log_sigmoid(gc), 0.0)
    lf_r = jnp.where(is_f_row, _log_sigmoid(gr), 0.0)
    lf_c = jnp.concatenate([lf_c, jnp.zeros((L, LANE - gc.shape[1]), F32)], axis=1)
    cum_c = jnp.dot(lower, lf_c, precision=HIGHEST, preferred_element_type=F32)[:, :gc.shape[1]]
    cum_r = _dot_nt_highest(lf_r, lower)
    return jnp.where(is_f_col, cum_c, gc), jnp.where(is_f_row, cum_r, gr)


def _prefix_max_lanes(x):
    lane = lax.broadcasted_iota(I32, x.shape, 1)
    shift = 1
    while shift < x.shape[1]:
        x = jnp.maximum(x, jnp.where(lane >= shift, pltpu.roll(x, shift, axis=1), -jnp.inf))
        shift *= 2
    return x


def _mlstm_kernel(q_ref, k_ref, vt_ref, ot_ref, gc_ref, gr_ref, hg_ref, out_ref, c_ref, m_ref):
    L = MLSTM_CHUNK

    @pl.when(pl.program_id(1) == 0)
    def _():
        c_ref[...] = jnp.zeros_like(c_ref)
        m_ref[...] = jnp.zeros_like(m_ref)

    src = lax.broadcasted_iota(I32, (L, L), 0)
    tgt = lax.broadcasted_iota(I32, (L, L), 1)
    real = lax.broadcasted_iota(I32, (A_V_PAD, 1), 0) < A_V_DIM
    gcol, grow = _chunk_gates(gc_ref[...], gr_ref[...])
    for hd in range(A_HEADS):
        qk = slice(hd * A_QK_PAD, (hd + 1) * A_QK_PAD)
        vv = slice(hd * A_V_PAD, (hd + 1) * A_V_PAD)
        q, k, vt = q_ref[0, :, qk], k_ref[0, :, qk], vt_ref[vv, :]
        u_c = gcol[:, hd:hd + 1] - gcol[:, A_HEADS + hd:A_HEADS + hd + 1]
        g_r = grow[A_HEADS + hd:A_HEADS + hd + 1, :]
        u_r = grow[hd:hd + 1, :] - g_r
        g_last = g_r[:, L - 1:L]
        m_prev = m_ref[hd, 0:1, 0:1]
        c_prev = c_ref[hd]

        run_max = jnp.maximum(_prefix_max_lanes(jnp.broadcast_to(u_r, (8, L)))[0:1, :], m_prev)
        m_t = g_r + run_max
        inter = jnp.exp(m_prev - run_max)
        decay_t = jnp.where(src <= tgt, jnp.exp(u_c - run_max), 0.0)
        p_t = (decay_t * _dot_nt(k, q)).astype(BF16)
        num_t = inter * _dot_nt(c_prev.astype(BF16), q) + _dot(vt, p_t)
        den = num_t[A_V_DIM:A_V_DIM + 1, :]
        h_t = jnp.where(real, num_t / jnp.maximum(jnp.abs(den), jnp.exp(-m_t)), 0.0)
        scale = lax.rsqrt(jnp.sum(h_t * h_t, axis=0, keepdims=True) * (1.0 / A_V_DIM) + EPS)
        out_t = h_t * scale * hg_ref[vv, :] * jax.nn.sigmoid(ot_ref[vv, :])
        out_ref[0, :, vv] = out_t.T.astype(BF16)

        w_r = g_last + u_r
        m_new = jnp.maximum(g_last + m_prev, jnp.max(w_r, axis=-1, keepdims=True))
        ev_t = (jnp.exp(w_r - m_new) * vt.astype(F32)).astype(BF16)
        c_ref[hd] = jnp.exp(g_last + m_prev - m_new) * c_prev + _dot(ev_t, k)
        m_ref[hd] = jnp.broadcast_to(m_new, m_ref.shape[1:])


def _mlstm(q, k, vt, ot, gcol, grow, head_g):
    nb, s, _ = q.shape
    L = MLSTM_CHUNK
    nv = A_HEADS * A_V_PAD
    blk = lambda w: pl.BlockSpec((1, L, w), lambda b, c: (b, c, 0))
    col = lambda n: pl.BlockSpec((n, L), lambda b, c: (0, b * (s // L) + c))
    return pl.pallas_call(
        _mlstm_kernel,
        out_shape=jax.ShapeDtypeStruct((nb, s, nv), BF16),
        grid=(nb, s // L),
        in_specs=[blk(A_HEADS * A_QK_PAD), blk(A_HEADS * A_QK_PAD), col(nv), col(nv),
                  pl.BlockSpec((L, 2 * A_HEADS), lambda b, c: (b * (s // L) + c, 0)), col(2 * A_HEADS),
                  pl.BlockSpec((nv, L), lambda b, c: (0, 0))],
        out_specs=blk(nv),
        scratch_shapes=[pltpu.VMEM((A_HEADS, A_V_PAD, A_QK_PAD), F32), pltpu.VMEM((A_HEADS, 8, LANE), F32)],
        compiler_params=_cparams(("parallel", "arbitrary")),
        name="mlstm_chunkwise",
    )(q, k, vt, ot, gcol, grow, head_g)


def _mix_out_kernel(x_ref, h_ref, mq_ref, kbd_ref, vbd_ref, gmat_ref, qg_ref, w1_ref, w2_ref, out_ref):
    mo = _memory_attention(mq_ref[...], kbd_ref[0], vbd_ref[0], gmat_ref[...], qg_ref[...])
    out_ref[...] = x_ref[...] + _dot(h_ref[...], w1_ref[...]) + _dot(mo.astype(BF16), w2_ref[...])


def _mix_out(x2, h2, mq, kbd, vbd, gmat, qg, w_main, w_mem, seq):
    t = x2.shape[0]
    tm = ROW_TILE
    row = lambda n: pl.BlockSpec((tm, n), lambda i: (i, 0))
    full = lambda a: pl.BlockSpec(a.shape, lambda i: (0,) * a.ndim)
    per_batch = pl.BlockSpec((1,) + kbd.shape[1:], lambda i: ((i * tm) // seq, 0, 0))
    qg_t = jnp.tile(qg, M_HEADS).reshape(1, -1)
    return pl.pallas_call(
        _mix_out_kernel,
        out_shape=jax.ShapeDtypeStruct((t, D_MODEL), F32),
        grid=(t // tm,),
        in_specs=[row(D_MODEL), row(h2.shape[1]), row(M_W), per_batch, per_batch,
                  full(gmat), full(qg_t), full(w_main), full(w_mem)],
        out_specs=row(D_MODEL),
        compiler_params=_cparams(("parallel",)),
        name="mixer_out_proj",
    )(x2, h2, mq, kbd, vbd, gmat, qg_t, w_main, w_mem)


def _swiglu_accumulate(acc_ref, h_ref, x, wg_ref, wu_ref, wd_ref):
    for c in range(wg_ref.shape[-1] // FFN_CHUNK):
        cols = slice(c * FFN_CHUNK, (c + 1) * FFN_CHUNK)
        gate = _dot(x, wg_ref[:, cols])
        up = _dot(x, wu_ref[:, cols])
        h_ref[:, cols] = (jax.nn.silu(gate) * up).astype(BF16)
    acc_ref[...] += _dot(h_ref[...], wd_ref[...])


def _ffn_kernel(x_ref, g_ref, wg_ref, wu_ref, wd_ref, out_ref, hn_ref, acc_ref, h_ref):
    j = pl.program_id(1)

    @pl.when(j == 0)
    def _():
        hn_ref[...] = _rms(x_ref[...], g_ref[...]).astype(BF16)
        acc_ref[...] = jnp.zeros_like(acc_ref)

    _swiglu_accumulate(acc_ref, h_ref, hn_ref[...], wg_ref, wu_ref, wd_ref)

    @pl.when(j == pl.num_programs(1) - 1)
    def _():
        out_ref[...] = x_ref[...] + acc_ref[...]


def _dense_ffn(x2, g, w_gate_up, w_down):
    t = x2.shape[0]
    tm, tf = FFN_ROW_TILE, FFN_FF_TILE
    nj = D_FF // tf
    return pl.pallas_call(
        _ffn_kernel,
        out_shape=jax.ShapeDtypeStruct((t, D_MODEL), F32),
        grid=(t // tm, nj),
        in_specs=[pl.BlockSpec((tm, D_MODEL), lambda i, j: (i, 0)),
                  pl.BlockSpec((1, D_MODEL), lambda i, j: (0, 0)),
                  pl.BlockSpec((D_MODEL, tf), lambda i, j: (0, j)),
                  pl.BlockSpec((D_MODEL, tf), lambda i, j: (0, nj + j)),
                  pl.BlockSpec((tf, D_MODEL), lambda i, j: (j, 0))],
        out_specs=pl.BlockSpec((tm, D_MODEL), lambda i, j: (i, 0)),
        scratch_shapes=[pltpu.VMEM((tm, D_MODEL), BF16), pltpu.VMEM((tm, D_MODEL), F32),
                        pltpu.VMEM((tm, tf), BF16)],
        compiler_params=_cparams(("parallel", "arbitrary")),
        name="dense_swiglu",
    )(x2, g.reshape(1, -1), w_gate_up, w_gate_up, w_down)


def _rope_kernel(pos_ref, inv_ref, sign_ref, cs_ref, sn_ref):
    ang = pos_ref[...].astype(F32) * inv_ref[...]
    cs_ref[...] = jnp.cos(ang)
    sn_ref[...] = jnp.sin(ang) * sign_ref[...]


def _rope_tables(positions):
    t = positions.size
    half = QK_ROPE // 2
    inv = 1.0 / (ROPE_THETA ** (jnp.arange(0, QK_ROPE, 2, dtype=F32) / QK_ROPE))
    pad = jnp.zeros((LANE - QK_ROPE,), F32)
    inv_l = jnp.concatenate([inv, inv, pad]).reshape(1, LANE)
    sign = jnp.concatenate([-jnp.ones((half,), F32), jnp.ones((half,), F32), pad]).reshape(1, LANE)
    out = jax.ShapeDtypeStruct((t, LANE), F32)
    return pl.pallas_call(
        _rope_kernel,
        out_shape=(out, out),
        grid=(t // ROW_TILE,),
        in_specs=[pl.BlockSpec((ROW_TILE, 1), lambda i: (i, 0)),
                  pl.BlockSpec((1, LANE), lambda i: (0, 0)),
                  pl.BlockSpec((1, LANE), lambda i: (0, 0))],
        out_specs=(pl.BlockSpec((ROW_TILE, LANE), lambda i: (i, 0)),) * 2,
        compiler_params=_cparams(("parallel",)),
        name="rope_tables",
    )(positions.reshape(t, 1), inv_l, sign)


def _head_qk_norm_rope(nope, rope, rope_sw, g_nope, g_rope, g_rope_sw, cs, sn, scale):
    ss = jnp.sum(nope * nope, axis=-1, keepdims=True) + jnp.sum(rope * rope, axis=-1, keepdims=True)
    r = lax.rsqrt(ss * (1.0 / B_QK_HEAD) + EPS) * scale
    return nope * r * g_nope, (rope * g_rope * cs + rope_sw * g_rope_sw * sn) * r


def _latent_kv_body(hn, wd_ref, ga_ref, wuk_ref, wuvt_ref, kg_ref, cs_ref, sn_ref, k_ref, vt_ref):
    z = _dot(hn, wd_ref[...])
    c_kv = z[:, :KV_LORA]
    rope, rope_sw = z[:, KV_LORA:KV_LORA + LANE], z[:, KV_LORA + LANE:]
    cn = _rms(c_kv, ga_ref[...]).astype(BF16)
    kv = _dot(cn, wuk_ref[...])
    vt = _dot_nt(wuvt_ref[...], cn)
    tm = vt.shape[1]
    ones_row = (lax.broadcasted_iota(I32, (VT_HEAD_ROWS - V_HEAD, tm), 0) == 0).astype(BF16)
    for h in range(B_HEADS):
        vt_ref[h * VT_HEAD_ROWS:h * VT_HEAD_ROWS + V_HEAD, :] = vt[h * V_HEAD:(h + 1) * V_HEAD].astype(BF16)
        vt_ref[h * VT_HEAD_ROWS + V_HEAD:(h + 1) * VT_HEAD_ROWS, :] = ones_row
    kg = kg_ref[...]
    for h in range(B_HEADS):
        kn, kr = _head_qk_norm_rope(kv[:, h * QK_NOPE:(h + 1) * QK_NOPE], rope, rope_sw,
                                    kg[:, :LANE], kg[:, LANE:2 * LANE], kg[:, 2 * LANE:],
                                    cs_ref[...], sn_ref[...], 1.0)
        k_ref[:, h * B_QK_PAD:h * B_QK_PAD + QK_NOPE] = kn.astype(BF16)
        k_ref[:, h * B_QK_PAD + QK_NOPE:(h + 1) * B_QK_PAD] = kr.astype(BF16)


def _rope_swap(w):
    half = QK_ROPE // 2
    return jnp.concatenate([w[..., half:], w[..., :half]], axis=-1)


def _pad_lanes(w, n=LANE):
    return jnp.pad(w, [(0, 0)] * (w.ndim - 1) + [(0, n - w.shape[-1])])


def _head_gain(g):
    g_rope = g[QK_NOPE:]
    return jnp.concatenate([g[:QK_NOPE], _pad_lanes(g_rope), _pad_lanes(_rope_swap(g_rope))]).reshape(1, -1)


def _query_body(hn, win_ref, ga_ref, wuq_ref, qg_ref, cs_ref, sn_ref, q_ref, mq_ref):
    proj = _dot(hn, win_ref[...])
    mq_ref[...] = proj[:, Q_LORA:]
    qall = _dot(_rms(proj[:, :Q_LORA], ga_ref[...]).astype(BF16), wuq_ref[...])
    qg = qg_ref[...]
    per_head = QK_NOPE + 2 * LANE
    for h in range(B_HEADS):
        base = h * per_head
        qn, qr = _head_qk_norm_rope(qall[:, base:base + QK_NOPE],
                                    qall[:, base + QK_NOPE:base + QK_NOPE + LANE],
                                    qall[:, base + QK_NOPE + LANE:base + per_head],
                                    qg[:, :LANE], qg[:, LANE:2 * LANE], qg[:, 2 * LANE:],
                                    cs_ref[...], sn_ref[...], B_QK_HEAD ** -0.5)
        q_ref[:, h * B_QK_PAD:h * B_QK_PAD + QK_NOPE] = qn.astype(BF16)
        q_ref[:, h * B_QK_PAD + QK_NOPE:(h + 1) * B_QK_PAD] = qr.astype(BF16)


def _mla_proj_kernel(x_ref, gkv_ref, wd_ref, ga_ref, wuk_ref, wuvt_ref, kg_ref,
                     gq_ref, win_ref, gqa_ref, wuq_ref, qg_ref, cs_ref, sn_ref,
                     k_ref, vt_ref, q_ref, mq_ref):
    x = x_ref[...]
    xn = x * lax.rsqrt(jnp.mean(x * x, axis=-1, keepdims=True) + EPS)
    _latent_kv_body((xn * gkv_ref[...]).astype(BF16), wd_ref, ga_ref, wuk_ref, wuvt_ref, kg_ref,
                    cs_ref, sn_ref, k_ref, vt_ref)
    _query_body((xn * gq_ref[...]).astype(BF16), win_ref, gqa_ref, wuq_ref, qg_ref, cs_ref, sn_ref, q_ref, mq_ref)


def _mla_projection(x2, kv_norm, w_dkv, kv_a_norm, w_ukv, k_head_norm,
                    q_norm, w_in, q_a_g, w_uq, q_head_g, cs, sn):
    t = x2.shape[0]
    w_rope = w_dkv[:, KV_LORA:]
    wd = jnp.concatenate([w_dkv[:, :KV_LORA], _pad_lanes(w_rope), _pad_lanes(_rope_swap(w_rope))],
                         axis=1).astype(BF16)
    wu = w_ukv.reshape(KV_LORA, B_HEADS, QK_NOPE + V_HEAD)
    wuk = wu[:, :, :QK_NOPE].reshape(KV_LORA, -1).astype(BF16)
    wuvt = wu[:, :, QK_NOPE:].reshape(KV_LORA, -1).T.astype(BF16)
    wq = w_uq.reshape(Q_LORA, B_HEADS, B_QK_HEAD)
    wq_rope = wq[:, :, QK_NOPE:]
    wq = jnp.concatenate([wq[:, :, :QK_NOPE], _pad_lanes(wq_rope), _pad_lanes(_rope_swap(wq_rope))],
                         axis=-1).reshape(Q_LORA, -1).astype(BF16)
    consts = (kv_norm.reshape(1, -1), wd, kv_a_norm.reshape(1, -1), wuk, wuvt, _head_gain(k_head_norm),
              q_norm.reshape(1, -1), w_in.astype(BF16), q_a_g.reshape(1, -1), wq, _head_gain(q_head_g))
    row = lambda n: pl.BlockSpec((ROW_TILE, n), lambda i: (i, 0))
    full = lambda a: pl.BlockSpec(a.shape, lambda i: (0,) * a.ndim)
    return pl.pallas_call(
        _mla_proj_kernel,
        out_shape=(jax.ShapeDtypeStruct((t, B_HEADS * B_QK_PAD), BF16),
                   jax.ShapeDtypeStruct((B_HEADS * VT_HEAD_ROWS, t), BF16),
                   jax.ShapeDtypeStruct((t, B_HEADS * B_QK_PAD), BF16),
                   jax.ShapeDtypeStruct((t, M_W), F32)),
        grid=(t // ROW_TILE,),
        in_specs=[row(D_MODEL)] + [full(a) for a in consts] + [row(LANE), row(LANE)],
        out_specs=(row(B_HEADS * B_QK_PAD), pl.BlockSpec((B_HEADS * VT_HEAD_ROWS, ROW_TILE), lambda i: (0, i)),
                   row(B_HEADS * B_QK_PAD), row(M_W)),
        compiler_params=_cparams(("parallel",)),
        name="mla_proj",
    )(x2, *consts, cs, sn)


def _attn_kernel(q_ref, k_ref, vt_ref, out_ref):
    tq, tk = ATT_Q_TILE, ATT_K_TILE
    i = pl.program_id(2)

    def block(h, j, carry, masked):
        m, acc = carry
        off = pl.multiple_of(j * tk, tk)
        q = q_ref[0, :, h * B_QK_PAD:(h + 1) * B_QK_PAD]
        st = _dot_nt(k_ref[0, pl.ds(off, tk), h * B_QK_PAD:(h + 1) * B_QK_PAD], q)
        if masked:
            key = lax.broadcasted_iota(I32, (tk, tq), 0)
            qry = lax.broadcasted_iota(I32, (tk, tq), 1)
            st = jnp.where(key <= qry, st, -jnp.inf)
        m_new = jnp.maximum(m, jnp.max(st, axis=0, keepdims=True))
        p = jnp.exp(st - m_new).astype(BF16)
        vt = vt_ref[h * VT_HEAD_ROWS:(h + 1) * VT_HEAD_ROWS, pl.ds(off, tk)]
        return m_new, jnp.exp(m - m_new) * acc + _dot(vt, p)

    heads = range(ATT_HEADS_PER_STEP)
    init = tuple((jnp.full((1, tq), -jnp.inf, F32), jnp.zeros((VT_HEAD_ROWS, tq), F32)) for _ in heads)
    carry = lax.fori_loop(0, i, lambda j, c: tuple(block(h, j, c[h], False) for h in heads), init)
    for h in heads:
        _, acc = block(h, i, carry[h], True)
        out_t = acc[:V_HEAD] / acc[V_HEAD:V_HEAD + 1]
        out_ref[0, :, h * V_HEAD:(h + 1) * V_HEAD] = out_t.T.astype(BF16)


def _causal_attention(q, k, vt, seq):
    nb = q.shape[0]
    g = ATT_HEADS_PER_STEP
    assert ATT_Q_TILE == ATT_K_TILE and B_HEADS % g == 0
    return pl.pallas_call(
        _attn_kernel,
        out_shape=jax.ShapeDtypeStruct((nb, seq, B_HEADS * V_HEAD), BF16),
        grid=(nb, B_HEADS // g, seq // ATT_Q_TILE),
        in_specs=[pl.BlockSpec((1, ATT_Q_TILE, g * B_QK_PAD), lambda b, h, i: (b, i, h)),
                  pl.BlockSpec((1, seq, g * B_QK_PAD), lambda b, h, i: (b, 0, h)),
                  pl.BlockSpec((g * VT_HEAD_ROWS, seq), lambda b, h, i: (h, b))],
        out_specs=pl.BlockSpec((1, ATT_Q_TILE, g * V_HEAD), lambda b, h, i: (b, i, h)),
        compiler_params=_cparams(("parallel", "parallel", "arbitrary")),
        name="causal_attention",
    )(q, k, vt)


def _router_kernel(x_ref, g_ref, wr_hi_ref, wr_lo_ref, hn_ref, idx_ref, gate_ref, count_ref):
    hn = _rms(x_ref[...], g_ref[...])
    hn_hi = hn.astype(BF16)
    hn_ref[...] = hn_hi
    hn_lo = (hn - hn_hi.astype(F32)).astype(BF16)
    logits = _dot(hn_hi, wr_hi_ref[...]) + _dot(hn_hi, wr_lo_ref[...]) + _dot(hn_lo, wr_hi_ref[...])
    lane = lax.broadcasted_iota(I32, logits.shape, 1)
    logits = jnp.where(lane < N_EXPERTS, logits, -jnp.inf)
    v1 = jnp.max(logits, axis=-1, keepdims=True)
    i1 = jnp.min(jnp.where(logits == v1, lane, LANE), axis=-1, keepdims=True)
    rest = jnp.where(lane == i1, -jnp.inf, logits)
    v2 = jnp.max(rest, axis=-1, keepdims=True)
    i2 = jnp.min(jnp.where(rest == v2, lane, LANE), axis=-1, keepdims=True)
    e2 = jnp.exp(v2 - v1)
    den = 1.0 + e2
    idx_ref[...] = jnp.where(lane == 0, i1, jnp.where(lane == 1, i2, 0))
    record = jnp.zeros(logits.shape, F32)
    for k, gate in enumerate((1.0 / den, e2 / den)):
        hi = gate.astype(BF16).astype(F32)
        mid = (gate - hi).astype(BF16).astype(F32)
        for part, term in enumerate((hi, mid, gate - hi - mid)):
            record = jnp.where(lane == GATE_TERMS * k + part, term, record)
    record = jnp.where(lane == 2 * GATE_TERMS, i1.astype(F32), record)
    record = jnp.where(lane == 2 * GATE_TERMS + 1, i2.astype(F32), record)
    gate_ref[...] = record.astype(BF16)
    pairs = (lane == i1).astype(F32) + (lane == i2).astype(F32)
    count_ref[...] = jnp.broadcast_to(jnp.sum(pairs, axis=0, keepdims=True), count_ref.shape)


def _router(x2, g, w_router):
    t = x2.shape[0]
    row = lambda n: pl.BlockSpec((RANK_TILE, n), lambda i: (i, 0))
    wr = _pad_lanes(w_router)
    wr_hi = wr.astype(BF16)
    wr_lo = (wr - wr_hi.astype(F32)).astype(BF16)
    return pl.pallas_call(
        _router_kernel,
        out_shape=(jax.ShapeDtypeStruct((t, D_MODEL), BF16), jax.ShapeDtypeStruct((t, LANE), I32),
                   jax.ShapeDtypeStruct((t, LANE), BF16), jax.ShapeDtypeStruct((8 * (t // RANK_TILE), LANE), F32)),
        grid=(t // RANK_TILE,),
        in_specs=[row(D_MODEL), pl.BlockSpec((1, D_MODEL), lambda i: (0, 0)),
                  pl.BlockSpec(wr.shape, lambda i: (0, 0)), pl.BlockSpec(wr.shape, lambda i: (0, 0))],
        out_specs=(row(D_MODEL), row(LANE), row(LANE), pl.BlockSpec((8, LANE), lambda i: (i, 0))),
        compiler_params=_cparams(("parallel",)),
        name="moe_router",
    )(x2, g.reshape(1, -1), wr_hi, wr_lo)


def _aligned(count):
    return jnp.ceil(count * (1.0 / RUN_ALIGN)) * RUN_ALIGN


def _rank_kernel(idx_ref, counts_ref, lpos_ref, lposr_ref, tile_ref, runs_ref, run_ref, start_ref):
    blk = pl.program_id(0)
    tb = RANK_TILE
    lane = lax.broadcasted_iota(I32, (tb, LANE), 1)
    idx = idx_ref[...]
    oh0 = (lane == idx[:, 0:1]).astype(F32)
    oh1 = (lane == idx[:, 1:2]).astype(F32)
    both = oh0 + oh1
    run_len = _aligned(jnp.sum(both, axis=0, keepdims=True))
    r = lax.broadcasted_iota(I32, (LANE, LANE), 0)
    c = lax.broadcasted_iota(I32, (LANE, LANE), 1)
    before = (r < c).astype(F32)

    @pl.when(blk == 0)
    def _():
        sizes = jnp.sum(_aligned(counts_ref[...]), axis=0, keepdims=True) * 0.125
        tiles = jnp.ceil(sizes * (1.0 / MOE_ROW_TILE))
        tile_start = jnp.dot(tiles, before, precision=HIGHEST, preferred_element_type=F32)
        start_ref[...] = tile_start * MOE_ROW_TILE
        tile_end = tile_start + tiles
        n_col = r.astype(F32)
        ended = ((n_col >= tile_end) & (c < N_EXPERTS)).astype(F32)
        expert = jnp.sum(ended, axis=-1, keepdims=True)
        total = jnp.max(tile_end, axis=-1, keepdims=True)
        as_column = lambda row: jnp.sum(jnp.where(r == c, row, 0.0), axis=-1, keepdims=True)
        tail_start = as_column(start_ref[...] + sizes)
        tail_len = as_column(tiles * MOE_ROW_TILE - sizes)
        mine = expert == c.astype(F32)
        group_rows = jnp.sum(jnp.where(mine, sizes, 0.0), axis=-1, keepdims=True)
        group_tile = jnp.sum(jnp.where(mine, tile_start, 0.0), axis=-1, keepdims=True)
        used_rows = jnp.clip(group_rows - (n_col[:, 0:1] - group_tile) * MOE_ROW_TILE, 0.0, MOE_ROW_TILE)
        col = lax.broadcasted_iota(I32, tile_ref.shape, 1)
        tile_ref[...] = jnp.where(col == 0, expert, jnp.where(col == 1, total, jnp.where(
            col == 2, tail_start, jnp.where(col == 3, tail_len, used_rows)))).astype(I32)
        run_ref[...] = jnp.zeros_like(run_ref)
        runs_ref[...] = jnp.zeros_like(runs_ref)

    rr = lax.broadcasted_iota(I32, (tb, tb), 0)
    cc = lax.broadcasted_iota(I32, (tb, tb), 1)
    strict = (cc < rr).astype(BF16)
    local = jnp.dot(run_len, before, precision=HIGHEST, preferred_element_type=F32)
    base = _dot(strict, both.astype(BF16)) + local
    p0 = jnp.sum(oh0 * base, axis=-1, keepdims=True)
    p1 = jnp.sum(oh1 * base, axis=-1, keepdims=True)
    posf = jnp.where(lane == 0, p0, jnp.where(lane == 1, p1, 0.0))
    lpos_ref[...] = posf.astype(I32)
    pick = (lax.broadcasted_iota(I32, (8, LANE), 0) == lax.broadcasted_iota(I32, (8, LANE), 1)).astype(F32)
    lposr_ref[...] = _dot_nt_highest(pick, posf).astype(I32)
    n = runs_ref.shape[0] // 3
    mine = lax.broadcasted_iota(I32, (n, LANE), 0) == blk
    for k, value in enumerate((local, run_ref[...] + start_ref[...], run_len)):
        runs_ref[k * n:(k + 1) * n, :] = jnp.where(mine, value.astype(I32), runs_ref[k * n:(k + 1) * n, :])
    run_ref[...] += run_len


def _rank(idx, counts):
    t = idx.shape[0]
    nblk = t // RANK_TILE
    const = lambda shape: pl.BlockSpec(shape, lambda i: (0, 0))
    return pl.pallas_call(
        _rank_kernel,
        out_shape=(jax.ShapeDtypeStruct((t, LANE), I32), jax.ShapeDtypeStruct((8, t), I32),
                   jax.ShapeDtypeStruct((LANE, 8), I32), jax.ShapeDtypeStruct((3 * nblk, LANE), I32)),
        grid=(nblk,),
        in_specs=[pl.BlockSpec((RANK_TILE, LANE), lambda i: (i, 0)), const(counts.shape)],
        out_specs=(pl.BlockSpec((RANK_TILE, LANE), lambda i: (i, 0)),
                   pl.BlockSpec((8, RANK_TILE), lambda i: (0, i)),
                   const((LANE, 8)), const((3 * nblk, LANE))),
        scratch_shapes=[pltpu.VMEM((1, LANE), F32), pltpu.VMEM((1, LANE), F32)],
        compiler_params=_cparams(("arbitrary",)),
        name="moe_rank",
    )(idx, counts)


def _for_each_piece(length, fn):
    for size in RUN_PIECES:
        @pl.when((length & size) != 0)
        def _(size=size):
            fn(pl.multiple_of(length & (-2 * size), RUN_ALIGN), size)


def _for_each_run_piece(blk, loc_ref, dst_ref, len_ref, fn):
    for e in range(N_EXPERTS):
        k = blk * N_EXPERTS + e
        loc, dst = loc_ref[k], dst_ref[k]
        _for_each_piece(len_ref[k], lambda done, size, loc=loc, dst=dst: fn(
            pl.multiple_of(loc + done, RUN_ALIGN), pl.multiple_of(dst + done, RUN_ALIGN), size))


def _dispatch_kernel(loc_ref, dst_ref, len_ref, tail_ref, tail_len_ref, ntiles_ref, hn_ref, gate_ref, lposr_ref,
                     xs_ref, local_ref, zero_ref, sems):
    blk = pl.program_id(0)

    @pl.when(blk == 0)
    def _():
        sem = sems.at[2]
        zero_ref[...] = jnp.zeros_like(zero_ref)

        def fill(e):
            start = tail_ref[e]
            return lambda done, size: pltpu.make_async_copy(
                zero_ref.at[pl.ds(0, size), :], xs_ref.at[pl.ds(pl.multiple_of(start + done, RUN_ALIGN), size), :], sem)

        def fill_tile(i):
            return pltpu.make_async_copy(
                zero_ref, xs_ref.at[pl.ds(pl.multiple_of(i * RANK_TILE, RANK_TILE), RANK_TILE), :], sem)

        def unused_tiles(action):
            def body(i, carry):
                action(fill_tile(i))
                return carry
            lax.fori_loop(ntiles_ref[0] * (MOE_ROW_TILE // RANK_TILE), xs_ref.shape[0] // RANK_TILE, body, 0)

        for e in range(N_EXPERTS):
            _for_each_piece(tail_len_ref[e], lambda *a, e=e: fill(e)(*a).start())
        unused_tiles(lambda copy: copy.start())
        for e in range(N_EXPERTS):
            _for_each_piece(tail_len_ref[e], lambda *a, e=e: fill(e)(*a).wait())
        unused_tiles(lambda copy: copy.wait())

    rows = lax.broadcasted_iota(I32, (LOCAL_ROWS, 1), 0)
    sel = jnp.where(lposr_ref[0:1, :] == rows, 1.0, jnp.where(lposr_ref[1:2, :] == rows, 1.0, 0.0)).astype(BF16)

    def step(local_ref, sem, other_ref, other_sem):
        local_ref[:, :D_MODEL] = _dot(sel, hn_ref[...]).astype(BF16)
        local_ref[:, D_MODEL:] = _dot(sel, gate_ref[...]).astype(BF16)

        def copy(buf, buf_sem):
            return lambda loc, dst, size: pltpu.make_async_copy(
                buf.at[pl.ds(loc, size), :], xs_ref.at[pl.ds(dst, size), :], buf_sem)

        _for_each_run_piece(blk, loc_ref, dst_ref, len_ref, lambda *a: copy(local_ref, sem)(*a).start())

        @pl.when(blk > 0)
        def _():
            _for_each_run_piece(blk - 1, loc_ref, dst_ref, len_ref, lambda *a: copy(other_ref, other_sem)(*a).wait())

        @pl.when(blk == pl.num_programs(0) - 1)
        def _():
            _for_each_run_piece(blk, loc_ref, dst_ref, len_ref, lambda *a: copy(local_ref, sem)(*a).wait())

    @pl.when(blk % 2 == 0)
    def _():
        step(local_ref.at[0], sems.at[0], local_ref.at[1], sems.at[1])

    @pl.when(blk % 2 == 1)
    def _():
        step(local_ref.at[1], sems.at[1], local_ref.at[0], sems.at[0])


def _dispatch(runs, tails, hn, gates, lpos_rows, rows):
    nblk = hn.shape[0] // RANK_TILE
    tok = lambda n: pl.BlockSpec((RANK_TILE, n), lambda b, *_: (b, 0))
    return pl.pallas_call(
        _dispatch_kernel,
        out_shape=jax.ShapeDtypeStruct((rows, XS_WIDTH), BF16),
        grid_spec=pltpu.PrefetchScalarGridSpec(
            num_scalar_prefetch=6,
            grid=(nblk,),
            in_specs=[tok(D_MODEL), tok(LANE), pl.BlockSpec((8, RANK_TILE), lambda b, *_: (0, b))],
            out_specs=pl.BlockSpec(memory_space=pl.ANY),
            scratch_shapes=[pltpu.VMEM((2, LOCAL_ROWS, XS_WIDTH), BF16), pltpu.VMEM((RANK_TILE, XS_WIDTH), BF16),
                            pltpu.SemaphoreType.DMA((3,))],
        ),
        compiler_params=_cparams(("arbitrary",)),
        name="moe_dispatch",
    )(*runs, *tails, hn, gates, lpos_rows)


def _moe_kernel(expert_ref, ntiles_ref, used_ref, x_ref, wg_ref, wu_ref, wd_ref, out_ref, acc_ref, h_ref):
    i, j = pl.program_id(0), pl.program_id(1)
    active = i < ntiles_ref[0]
    parts = -(-used_ref[i] // MOE_SKIP_ROWS)

    @pl.when(j == 0)
    def _():
        acc_ref[...] = jnp.zeros_like(acc_ref)

    for n in range(1, MOE_ROW_TILE // MOE_SKIP_ROWS + 1):
        @pl.when(active & (parts == n))
        def _(n=n):
            lead = pl.ds(0, n * MOE_SKIP_ROWS)
            _swiglu_accumulate(acc_ref.at[lead, :], h_ref.at[lead, :], x_ref[:n * MOE_SKIP_ROWS, :D_MODEL],
                               wg_ref.at[0], wu_ref.at[0], wd_ref.at[0])

    @pl.when(j == pl.num_programs(1) - 1)
    def _():
        gs = x_ref[:, D_MODEL:].astype(F32)
        gate = [sum(gs[:, GATE_TERMS * k + n:GATE_TERMS * k + n + 1] for n in range(GATE_TERMS)) for k in range(TOP_K)]
        first = gs[:, TOP_K * GATE_TERMS:TOP_K * GATE_TERMS + 1] == expert_ref[i].astype(F32)
        row_gate = jnp.where(first, gate[0], gate[1])
        out_ref[...] = jnp.where(active, acc_ref[...] * row_gate, 0.0).astype(BF16)


def _moe_experts(tile_expert, n_tiles, tile_used, xs, w_gate_up, w_down):
    rows = xs.shape[0]
    tm, tf = MOE_ROW_TILE, FFN_FF_TILE
    nj = D_FF // tf

    def x_map(i, j, e_ref, n_ref, u_ref):
        return jnp.clip(i, 0, jnp.maximum(n_ref[0] - 1, 0)), 0

    def w_idx(i, j, expert_ref, ntiles_ref):
        e = jnp.minimum(expert_ref[i], N_EXPERTS - 1)
        return e, jnp.where(i < ntiles_ref[0], j, nj - 1)

    def w_gate_map(i, j, e_ref, n_ref, u_ref):
        e, jj = w_idx(i, j, e_ref, n_ref)
        return e, 0, jj

    def w_up_map(i, j, e_ref, n_ref, u_ref):
        e, jj = w_idx(i, j, e_ref, n_ref)
        return e, 0, nj + jj

    def w_down_map(i, j, e_ref, n_ref, u_ref):
        e, jj = w_idx(i, j, e_ref, n_ref)
        return e, jj, 0

    return pl.pallas_call(
        _moe_kernel,
        out_shape=jax.ShapeDtypeStruct((rows, D_MODEL), BF16),
        grid_spec=pltpu.PrefetchScalarGridSpec(
            num_scalar_prefetch=3,
            grid=(rows // tm, nj),
            in_specs=[pl.BlockSpec((tm, XS_WIDTH), x_map),
                      pl.BlockSpec((1, D_MODEL, tf), w_gate_map),
                      pl.BlockSpec((1, D_MODEL, tf), w_up_map),
                      pl.BlockSpec((1, tf, D_MODEL), w_down_map)],
            out_specs=pl.BlockSpec((tm, D_MODEL), lambda i, j, *_: (i, 0)),
            scratch_shapes=[pltpu.VMEM((tm, D_MODEL), F32), pltpu.VMEM((tm, tf), BF16)],
        ),
        compiler_params=_cparams(("arbitrary", "arbitrary")),
        name="moe_experts",
    )(tile_expert, n_tiles, tile_used, xs, w_gate_up, w_gate_up, w_down)


def _combine_kernel(loc_ref, dst_ref, len_ref, x_ref, lpos_ref, ys_ref, out_ref, local_ref, sems):
    blk = pl.program_id(0)
    rows = lax.broadcasted_iota(I32, (1, LOCAL_ROWS), 1)
    lpos = lpos_ref[...]
    sel = jnp.where(lpos[:, 0:1] == rows, 1.0, jnp.where(lpos[:, 1:2] == rows, 1.0, 0.0)).astype(BF16)

    def fetch(block, buf, buf_sem, action):
        _for_each_run_piece(block, loc_ref, dst_ref, len_ref, lambda loc, dst, size: action(pltpu.make_async_copy(
            ys_ref.at[pl.ds(dst, size), :], buf.at[pl.ds(loc, size), :], buf_sem)))

    def start_fetch(block, buf, buf_sem):
        buf[...] = jnp.zeros_like(buf)
        fetch(block, buf, buf_sem, lambda copy: copy.start())

    def step(buf, buf_sem, other, other_sem):
        @pl.when(blk == 0)
        def _():
            start_fetch(blk, buf, buf_sem)

        @pl.when(blk + 1 < pl.num_programs(0))
        def _():
            start_fetch(blk + 1, other, other_sem)

        fetch(blk, buf, buf_sem, lambda copy: copy.wait())
        out_ref[...] = x_ref[...] + _dot(sel, buf[...])

    @pl.when(blk % 2 == 0)
    def _():
        step(local_ref.at[0], sems.at[0], local_ref.at[1], sems.at[1])

    @pl.when(blk % 2 == 1)
    def _():
        step(local_ref.at[1], sems.at[1], local_ref.at[0], sems.at[0])


def _combine(runs, x2, lpos, ys):
    tok = lambda n: pl.BlockSpec((RANK_TILE, n), lambda b, *_: (b, 0))
    return pl.pallas_call(
        _combine_kernel,
        out_shape=jax.ShapeDtypeStruct(x2.shape, F32),
        grid_spec=pltpu.PrefetchScalarGridSpec(
            num_scalar_prefetch=3,
            grid=(x2.shape[0] // RANK_TILE,),
            in_specs=[tok(D_MODEL), tok(LANE), pl.BlockSpec(memory_space=pl.ANY)],
            out_specs=tok(D_MODEL),
            scratch_shapes=[pltpu.VMEM((2, LOCAL_ROWS, D_MODEL), BF16), pltpu.SemaphoreType.DMA((2,))],
        ),
        compiler_params=_cparams(("arbitrary",)),
        name="moe_combine",
    )(*runs, x2, lpos, ys)


def _moe_ffn(x2, g, w_router, w_gate_up, w_down):
    t = x2.shape[0]
    nblk = t // RANK_TILE
    hn, idx, gates, counts = _router(x2, g, w_router)
    lpos, lpos_rows, tile_info, run_tab = _rank(idx, counts)
    runs = tuple(run_tab.reshape(3, nblk, LANE)[:, :, :N_EXPERTS].reshape(3, -1))
    rows = t * TOP_K + nblk * N_EXPERTS * (RUN_ALIGN - 1) + N_EXPERTS * (MOE_ROW_TILE - 1)
    rows = -(-rows // MOE_ROW_TILE) * MOE_ROW_TILE
    assert rows // MOE_ROW_TILE <= LANE
    assert MOE_ROW_TILE % RANK_TILE == 0 and rows % RANK_TILE == 0
    xs = _dispatch(runs, (tile_info[:, 2], tile_info[:, 3], tile_info[:1, 1]), hn, gates, lpos_rows, rows)
    ys = _moe_experts(tile_info[:, 0], tile_info[:1, 1], tile_info[:, 4], xs,
                      w_gate_up.astype(BF16), w_down.astype(BF16))
    return _combine(runs, x2, lpos, ys)


def _pad_heads(w, heads, dim, pad):
    w = w.reshape(w.shape[:-1] + (heads, dim))
    return _pad_lanes(w, pad).reshape(w.shape[:-2] + (heads * pad,))


def kernel(x, mem, positions, a_norm, a_w_in, a_gate_bias, a_head_norm, a_w_out, b_norm, b_w_in, b_q_a_norm, b_w_uq, b_q_head_norm, b_w_out, kv_norm, w_dkv, kv_a_norm, w_ukv, k_head_norm, mem_norm, mem_w_kv, mem_q_norm, mem_k_norm, ffn_norm, dense_w_gate_up, dense_w_down, moe_router, moe_w_gate_up, moe_w_down):
    nb, seq, _ = x.shape
    t = nb * seq
    x2 = x.reshape(t, D_MODEL)
    gmat = jnp.kron(jnp.eye(M_HEADS, dtype=F32), jnp.full((M_HEAD_DIM, M_HEAD_DIM), 1.0 / M_HEAD_DIM, F32)).astype(BF16)

    kbd0, vbd0 = _memory_kv(mem, mem_norm[0], mem_w_kv[0], mem_k_norm[0], gmat)
    w_in = a_w_in[0]
    qk_w, v_w = A_HEADS * A_QK_DIM, A_HEADS * A_V_DIM
    o0, o1, o2, o3, o4 = qk_w, 2 * qk_w, 2 * qk_w + v_w, 2 * qk_w + 2 * v_w, 2 * qk_w + 2 * v_w + 2 * A_HEADS
    w_main = jnp.concatenate([
        _pad_heads(w_in[:, :o0], A_HEADS, A_QK_DIM, A_QK_PAD),
        _pad_heads(w_in[:, o0:o1], A_HEADS, A_QK_DIM, A_QK_PAD),
        w_in[:, o4:]], axis=1).astype(BF16)
    w_vo_t = jnp.concatenate([
        _pad_heads(w_in[:, o1:o2], A_HEADS, A_V_DIM, A_V_PAD),
        _pad_heads(w_in[:, o2:o3], A_HEADS, A_V_DIM, A_V_PAD)], axis=1).T.astype(BF16)
    q, k, vt, ot, mq, gc, gr = _a_projection(x2, a_norm[0], w_main, w_vo_t, w_in[:, o3:o4], a_gate_bias[0])
    three = lambda a: a.reshape(nb, seq, a.shape[-1])
    head_g = _pad_heads(a_head_norm[0].reshape(1, -1), A_HEADS, A_V_DIM, A_V_PAD)
    hm = _mlstm(three(q), three(k), vt, ot, gc, gr,
                jnp.broadcast_to(head_g.reshape(-1, 1), (A_HEADS * A_V_PAD, MLSTM_CHUNK)))
    w_out = a_w_out[0]
    w_out_h = jnp.pad(w_out[:v_w].reshape(A_HEADS, A_V_DIM, D_MODEL), ((0, 0), (0, A_V_PAD - A_V_DIM), (0, 0)))
    w_out_h = w_out_h.reshape(A_HEADS * A_V_PAD, D_MODEL).astype(BF16)
    x2 = _mix_out(x2, hm.reshape(t, -1), mq, kbd0, vbd0, gmat, mem_q_norm[0],
                  w_out_h, w_out[v_w:].astype(BF16), seq)
    x2 = _dense_ffn(x2, ffn_norm[0], dense_w_gate_up[0].astype(BF16), dense_w_down[0].astype(BF16))

    cs, sn = _rope_tables(positions)
    k_sh, vt_sh, qh, mq1 = _mla_projection(x2, kv_norm, w_dkv, kv_a_norm, w_ukv, k_head_norm,
                                           b_norm[0], b_w_in[0], b_q_a_norm[0], b_w_uq[0], b_q_head_norm[0], cs, sn)

    kbd1, vbd1 = _memory_kv(mem, mem_norm[1], mem_w_kv[1], mem_k_norm[1], gmat)
    att = _causal_attention(three(qh), three(k_sh), vt_sh, seq)
    w_out = b_w_out[0]
    n_att = B_HEADS * V_HEAD
    x2 = _mix_out(x2, att.reshape(t, -1), mq1, kbd1, vbd1, gmat, mem_q_norm[1],
                  w_out[:n_att].astype(BF16), w_out[n_att:].astype(BF16), seq)
    x2 = _moe_ffn(x2, ffn_norm[1], moe_router[0], moe_w_gate_up[0], moe_w_down[0])
    return x2.reshape(nb, seq, D_MODEL)
```

```python
import functools

import jax
import jax.numpy as jnp
from jax import lax
from jax.experimental import pallas as pl
from jax.experimental.pallas import tpu as pltpu

F32 = jnp.float32
BF16 = jnp.bfloat16
I32 = jnp.int32

EPS = 1e-6
LOG2E = 1.4426950408889634
LANE = 128
VMEM_LIMIT = 48 * 1024 * 1024

D_MODEL = 1024
N_MEM = 256
M_HEADS, M_HEAD_DIM = 4, 64
M_W = M_HEADS * M_HEAD_DIM
A_HEADS, A_QK_DIM, A_V_DIM = 4, 96, 192
A_QK_PAD, A_V_PAD = 128, 256
B_HEADS, Q_LORA, KV_LORA = 6, 384, 256
QK_NOPE, QK_ROPE, V_HEAD = 128, 64, 128
B_QK_HEAD = QK_NOPE + QK_ROPE
B_QK_PAD = 256
VT_HEAD_ROWS = V_HEAD + 16
ROPE_THETA = 10000.0
D_FF = 3584
N_EXPERTS, TOP_K = 8, 2
GATE_TERMS = 3

MLSTM_CHUNK = 512
ROW_TILE = 512
FFN_ROW_TILE = 1024
FFN_FF_TILE = 1792
FFN_CHUNK = 256
MOE_FF_TILE = 896
MOE_ROW_TILE = 1024
MOE_SKIP_ROWS = 256
ATT_Q_TILE = 512
ATT_K_TILE = 512
ATT_HEADS_PER_STEP = 6
RANK_TILE = 512
RUN_ALIGN = 16
RUN_PIECES = tuple(RANK_TILE >> s for s in range((RANK_TILE // RUN_ALIGN).bit_length()))
LOCAL_ROWS = TOP_K * RANK_TILE + N_EXPERTS * RUN_ALIGN
XS_WIDTH = D_MODEL + LANE

HIGHEST = lax.Precision.HIGHEST


def _cparams(sem):
    return pltpu.CompilerParams(dimension_semantics=sem, vmem_limit_bytes=VMEM_LIMIT)


def _rms(x, g):
    return x * lax.rsqrt(jnp.mean(x * x, axis=-1, keepdims=True) + EPS) * g


def _dot(a, b):
    return jnp.dot(a, b, preferred_element_type=F32)


def _dot_nt(a, b):
    return lax.dot_general(a, b, (((1,), (1,)), ((), ())), preferred_element_type=F32)


def _dot_tn(a, b):
    return lax.dot_general(a, b, (((0,), (0,)), ((), ())), preferred_element_type=F32)


def _group_mean_sq(x, gmat):
    sq = x * x
    hi = sq.astype(BF16)
    lo = (sq - hi.astype(F32)).astype(BF16)
    return _dot(hi, gmat) + _dot(lo, gmat)


def _memkv_kernel(mem_ref, g_ref, w_ref, kg_ref, gmat_ref, kbd_ref, vbd_ref):
    hn = _rms(mem_ref[0], g_ref[...]).astype(BF16)
    kv = _dot(hn, w_ref[...])
    k, v = kv[:, :M_W], kv[:, M_W:]
    kn = k * lax.rsqrt(_group_mean_sq(k, gmat_ref[...]) + EPS) * kg_ref[...]
    lane_head = lax.broadcasted_iota(I32, (1, M_W), 1) // M_HEAD_DIM
    for h in range(M_HEADS):
        keep = lane_head == h
        kbd_ref[0, h * N_MEM:(h + 1) * N_MEM, :] = jnp.where(keep, kn, 0.0).astype(BF16)
        vbd_ref[0, h * N_MEM:(h + 1) * N_MEM, :] = jnp.where(keep, v, 0.0).astype(BF16)


def _memory_kv(mem, g, w_kv, k_g, gmat):
    nb = mem.shape[0]
    out = jax.ShapeDtypeStruct((nb, M_HEADS * N_MEM, M_W), BF16)
    return pl.pallas_call(
        _memkv_kernel,
        out_shape=(out, out),
        grid=(nb,),
        in_specs=[
            pl.BlockSpec((1, N_MEM, D_MODEL), lambda b: (b, 0, 0)),
            pl.BlockSpec((1, D_MODEL), lambda b: (0, 0)),
            pl.BlockSpec((D_MODEL, 2 * M_W), lambda b: (0, 0)),
            pl.BlockSpec((1, M_W), lambda b: (0, 0)),
            pl.BlockSpec((M_W, M_W), lambda b: (0, 0)),
        ],
        out_specs=(pl.BlockSpec((1, M_HEADS * N_MEM, M_W), lambda b: (b, 0, 0)),) * 2,
        compiler_params=_cparams(("parallel",)),
        name="memory_kv",
    )(mem, g.reshape(1, -1), w_kv.astype(BF16), jnp.tile(k_g, M_HEADS).reshape(1, -1), gmat)


def _memory_attention(mq, kbd, vbd, gmat, qg):
    qn = mq * lax.rsqrt(_group_mean_sq(mq, gmat) + EPS) * (qg * (M_HEAD_DIM ** -0.5 * LOG2E))
    s = _dot_nt(qn.astype(BF16), kbd)
    ps = []
    for h in range(M_HEADS):
        sh = s[:, h * N_MEM:(h + 1) * N_MEM]
        e = jnp.exp2(sh - jnp.max(sh, axis=-1, keepdims=True))
        ps.append((e / jnp.sum(e, axis=-1, keepdims=True)).astype(BF16))
    return _dot(jnp.concatenate(ps, axis=-1), vbd)


def _a_proj_kernel(x_ref, g_ref, w_ref, wvot_ref, wif_ref, wift_ref, bc_ref, br_ref,
                   q_ref, k_ref, vt_ref, ot_ref, mq_ref, gc_ref, gr_ref):
    hn = _rms(x_ref[...], g_ref[...]).astype(BF16)
    nq = A_HEADS * A_QK_PAD
    nv = A_HEADS * A_V_PAD
    q_ref[...] = _dot(hn, w_ref[:, :nq]).astype(BF16)
    k_ref[...] = (_dot(hn, w_ref[:, nq:2 * nq]) * (A_QK_DIM ** -0.5)).astype(BF16)
    mq_ref[...] = _dot(hn, w_ref[:, 2 * nq:])
    vt = _dot_nt(wvot_ref[:nv, :], hn)
    ones_row = lax.broadcasted_iota(I32, (nv, 1), 0) % A_V_PAD == A_V_DIM
    vt_ref[...] = jnp.where(ones_row, 1.0, vt).astype(BF16)
    ot_ref[...] = _dot_nt(wvot_ref[nv:, :], hn)
    gc_ref[...] = _dot(hn, wif_ref[...])[:, :2 * A_HEADS] + bc_ref[...]
    gr_ref[...] = _dot_nt(wift_ref[...], hn) + br_ref[...]


def _a_projection(x2, g, w_main, w_vo_t, w_if, gate_bias):
    t = x2.shape[0]
    nq, nv = A_HEADS * A_QK_PAD, A_HEADS * A_V_PAD
    ng = 2 * A_HEADS
    wif_pad = jnp.pad(w_if, ((0, 0), (0, LANE - ng))).astype(BF16)
    row = lambda n: pl.BlockSpec((ROW_TILE, n), lambda i: (i, 0))
    col = lambda n: pl.BlockSpec((n, ROW_TILE), lambda i: (0, i))
    full = lambda a: pl.BlockSpec(a.shape, lambda i: (0,) * a.ndim)
    args = (x2, g.reshape(1, -1), w_main, w_vo_t, wif_pad, w_if.T.astype(BF16),
            gate_bias.reshape(1, ng), gate_bias.reshape(ng, 1))
    return pl.pallas_call(
        _a_proj_kernel,
        out_shape=(jax.ShapeDtypeStruct((t, nq), BF16), jax.ShapeDtypeStruct((t, nq), BF16),
                   jax.ShapeDtypeStruct((nv, t), BF16), jax.ShapeDtypeStruct((nv, t), F32),
                   jax.ShapeDtypeStruct((t, M_W), F32), jax.ShapeDtypeStruct((t, ng), F32),
                   jax.ShapeDtypeStruct((ng, t), F32)),
        grid=(t // ROW_TILE,),
        in_specs=[row(D_MODEL)] + [full(a) for a in args[1:]],
        out_specs=(row(nq), row(nq), col(nv), col(nv), row(M_W), row(ng), col(ng)),
        compiler_params=_cparams(("parallel",)),
        name="mlstm_in_proj",
    )(*args)


def _log_sigmoid(f):
    return jnp.minimum(f, 0.0) - jnp.log(1.0 + jnp.exp(-jnp.abs(f)))


def _dot_nt_highest(a, b):
    return lax.dot_general(a, b, (((1,), (1,)), ((), ())), precision=HIGHEST,
                           preferred_element_type=F32)


def _chunk_gates(gc, gr):
    L = gc.shape[0]
    r = lax.broadcasted_iota(I32, (L, L), 0)
    c = lax.broadcasted_iota(I32, (L, L), 1)
    lower = (c <= r).astype(F32)
    is_f_col = lax.broadcasted_iota(I32, gc.shape, 1) >= A_HEADS
    is_f_row = lax.broadcasted_iota(I32, gr.shape, 0) >= A_HEADS
    lf_c = jnp.where(is_f_col, _log_sigmoid(gc), 0.0)
    lf_r = jnp.where(is_f_row, _log_sigmoid(gr), 0.0)
    lf_c = jnp.concatenate([lf_c, jnp.zeros((L, LANE - gc.shape[1]), F32)], axis=1)
    cum_c = jnp.dot(lower, lf_c, precision=HIGHEST, preferred_element_type=F32)[:, :gc.shape[1]]
    cum_r = _dot_nt_highest(lf_r, lower)
    return jnp.where(is_f_col, cum_c, gc), jnp.where(is_f_row, cum_r, gr)


def _prefix_max_lanes(x):
    lane = lax.broadcasted_iota(I32, x.shape, 1)
    shift = 1
    while shift < x.shape[1]:
        x = jnp.maximum(x, jnp.where(lane >= shift, pltpu.roll(x, shift, axis=1), -jnp.inf))
        shift *= 2
    return x


def _mlstm_kernel(q_ref, k_ref, vt_ref, ot_ref, gc_ref, gr_ref, hg_ref, out_ref, c_ref, m_ref):
    L = MLSTM_CHUNK

    @pl.when(pl.program_id(1) == 0)
    def _():
        c_ref[...] = jnp.zeros_like(c_ref)
        m_ref[...] = jnp.zeros_like(m_ref)

    src = lax.broadcasted_iota(I32, (L, L), 0)
    tgt = lax.broadcasted_iota(I32, (L, L), 1)
    real = lax.broadcasted_iota(I32, (A_V_PAD, 1), 0) < A_V_DIM
    gcol, grow = _chunk_gates(gc_ref[...], gr_ref[...])
    for hd in range(A_HEADS):
        qk = slice(hd * A_QK_PAD, (hd + 1) * A_QK_PAD)
        vv = slice(hd * A_V_PAD, (hd + 1) * A_V_PAD)
        q, k, vt = q_ref[0, :, qk], k_ref[0, :, qk], vt_ref[vv, :]
        u_c = gcol[:, hd:hd + 1] - gcol[:, A_HEADS + hd:A_HEADS + hd + 1]
        g_r = grow[A_HEADS + hd:A_HEADS + hd + 1, :]
        u_r = grow[hd:hd + 1, :] - g_r
        g_last = g_r[:, L - 1:L]
        m_prev = m_ref[hd, 0:1, 0:1]
        c_prev = c_ref[hd]

        run_max = jnp.maximum(_prefix_max_lanes(jnp.broadcast_to(u_r, (8, L)))[0:1, :], m_prev)
        m_t = g_r + run_max
        inter = jnp.exp(m_prev - run_max)
        decay_t = jnp.where(src <= tgt, jnp.exp(u_c - run_max), 0.0)
        p_t = (decay_t * _dot_nt(k, q)).astype(BF16)
        num_t = inter * _dot_nt(c_prev.astype(BF16), q) + _dot(vt, p_t)
        den = num_t[A_V_DIM:A_V_DIM + 1, :]
        h_t = jnp.where(real, num_t / jnp.maximum(jnp.abs(den), jnp.exp(-m_t)), 0.0)
        scale = lax.rsqrt(jnp.sum(h_t * h_t, axis=0, keepdims=True) * (1.0 / A_V_DIM) + EPS)
        out_t = h_t * scale * hg_ref[vv, :] * jax.nn.sigmoid(ot_ref[vv, :])
        out_ref[0, :, vv] = out_t.T.astype(BF16)

        w_r = g_last + u_r
        m_new = jnp.maximum(g_last + m_prev, jnp.max(w_r, axis=-1, keepdims=True))
        ev_t = (jnp.exp(w_r - m_new) * vt.astype(F32)).astype(BF16)
        c_ref[hd] = jnp.exp(g_last + m_prev - m_new) * c_prev + _dot(ev_t, k)
        m_ref[hd] = jnp.broadcast_to(m_new, m_ref.shape[1:])


def _mlstm(q, k, vt, ot, gcol, grow, head_g):
    nb, s, _ = q.shape
    L = MLSTM_CHUNK
    nv = A_HEADS * A_V_PAD
    blk = lambda w: pl.BlockSpec((1, L, w), lambda b, c: (b, c, 0))
    col = lambda n: pl.BlockSpec((n, L), lambda b, c: (0, b * (s // L) + c))
    return pl.pallas_call(
        _mlstm_kernel,
        out_shape=jax.ShapeDtypeStruct((nb, s, nv), BF16),
        grid=(nb, s // L),
        in_specs=[blk(A_HEADS * A_QK_PAD), blk(A_HEADS * A_QK_PAD), col(nv), col(nv),
                  pl.BlockSpec((L, 2 * A_HEADS), lambda b, c: (b * (s // L) + c, 0)), col(2 * A_HEADS),
                  pl.BlockSpec((nv, L), lambda b, c: (0, 0))],
        out_specs=blk(nv),
        scratch_shapes=[pltpu.VMEM((A_HEADS, A_V_PAD, A_QK_PAD), F32), pltpu.VMEM((A_HEADS, 8, LANE), F32)],
        compiler_params=_cparams(("parallel", "arbitrary")),
        name="mlstm_chunkwise",
    )(q, k, vt, ot, gcol, grow, head_g)


def _mix_out_kernel(x_ref, h_ref, mq_ref, kbd_ref, vbd_ref, gmat_ref, qg_ref, w1_ref, w2_ref, out_ref):
    mo = _memory_attention(mq_ref[...], kbd_ref[0], vbd_ref[0], gmat_ref[...], qg_ref[...])
    out_ref[...] = x_ref[...] + _dot(h_ref[...], w1_ref[...]) + _dot(mo.astype(BF16), w2_ref[...])


def _mix_out(x2, h2, mq, kbd, vbd, gmat, qg, w_main, w_mem, seq):
    t = x2.shape[0]
    tm = ROW_TILE
    row = lambda n: pl.BlockSpec((tm, n), lambda i: (i, 0))
    full = lambda a: pl.BlockSpec(a.shape, lambda i: (0,) * a.ndim)
    per_batch = pl.BlockSpec((1,) + kbd.shape[1:], lambda i: ((i * tm) // seq, 0, 0))
    qg_t = jnp.tile(qg, M_HEADS).reshape(1, -1)
    return pl.pallas_call(
        _mix_out_kernel,
        out_shape=jax.ShapeDtypeStruct((t, D_MODEL), F32),
        grid=(t // tm,),
        in_specs=[row(D_MODEL), row(h2.shape[1]), row(M_W), per_batch, per_batch,
                  full(gmat), full(qg_t), full(w_main), full(w_mem)],
        out_specs=row(D_MODEL),
        compiler_params=_cparams(("parallel",)),
        name="mixer_out_proj",
    )(x2, h2, mq, kbd, vbd, gmat, qg_t, w_main, w_mem)


def _swiglu_accumulate(acc_ref, h_ref, x, wg_ref, wu_ref, wd_ref):
    width = wg_ref.shape[-1]
    for start in range(0, width, FFN_CHUNK):
        cols = slice(start, min(start + FFN_CHUNK, width))
        gate = _dot(x, wg_ref[:, cols].astype(BF16))
        up = _dot(x, wu_ref[:, cols].astype(BF16))
        h_ref[:, cols] = (jax.nn.silu(gate) * up).astype(BF16)
    acc_ref[...] += _dot(h_ref[...], wd_ref[...].astype(BF16))


def _ffn_kernel(x_ref, g_ref, wg_ref, wu_ref, wd_ref, out_ref, hn_ref, acc_ref, h_ref):
    j = pl.program_id(1)

    @pl.when(j == 0)
    def _():
        hn_ref[...] = _rms(x_ref[...], g_ref[...]).astype(BF16)
        acc_ref[...] = jnp.zeros_like(acc_ref)

    _swiglu_accumulate(acc_ref, h_ref, hn_ref[...], wg_ref, wu_ref, wd_ref)

    @pl.when(j == pl.num_programs(1) - 1)
    def _():
        out_ref[...] = x_ref[...] + acc_ref[...]


def _dense_ffn(x2, g, w_gate_up, w_down):
    t = x2.shape[0]
    tm, tf = FFN_ROW_TILE, FFN_FF_TILE
    nj = D_FF // tf
    return pl.pallas_call(
        _ffn_kernel,
        out_shape=jax.ShapeDtypeStruct((t, D_MODEL), F32),
        grid=(t // tm, nj),
        in_specs=[pl.BlockSpec((tm, D_MODEL), lambda i, j: (i, 0)),
                  pl.BlockSpec((1, D_MODEL), lambda i, j: (0, 0)),
                  pl.BlockSpec((D_MODEL, tf), lambda i, j: (0, j)),
                  pl.BlockSpec((D_MODEL, tf), lambda i, j: (0, nj + j)),
                  pl.BlockSpec((tf, D_MODEL), lambda i, j: (j, 0))],
        out_specs=pl.BlockSpec((tm, D_MODEL), lambda i, j: (i, 0)),
        scratch_shapes=[pltpu.VMEM((tm, D_MODEL), BF16), pltpu.VMEM((tm, D_MODEL), F32),
                        pltpu.VMEM((tm, tf), BF16)],
        compiler_params=_cparams(("parallel", "arbitrary")),
        name="dense_swiglu",
    )(x2, g.reshape(1, -1), w_gate_up, w_gate_up, w_down)


def _rope_kernel(pos_ref, inv_ref, sign_ref, cs_ref, sn_ref):
    ang = pos_ref[...].astype(F32) * inv_ref[...]
    cs_ref[...] = jnp.cos(ang)
    sn_ref[...] = jnp.sin(ang) * sign_ref[...]


def _rope_tables(positions):
    t = positions.size
    half = QK_ROPE // 2
    inv = 1.0 / (ROPE_THETA ** (jnp.arange(0, QK_ROPE, 2, dtype=F32) / QK_ROPE))
    pad = jnp.zeros((LANE - QK_ROPE,), F32)
    inv_l = jnp.concatenate([inv, inv, pad]).reshape(1, LANE)
    sign = jnp.concatenate([-jnp.ones((half,), F32), jnp.ones((half,), F32), pad]).reshape(1, LANE)
    out = jax.ShapeDtypeStruct((t, LANE), F32)
    return pl.pallas_call(
        _rope_kernel,
        out_shape=(out, out),
        grid=(t // ROW_TILE,),
        in_specs=[pl.BlockSpec((ROW_TILE, 1), lambda i: (i, 0)),
                  pl.BlockSpec((1, LANE), lambda i: (0, 0)),
                  pl.BlockSpec((1, LANE), lambda i: (0, 0))],
        out_specs=(pl.BlockSpec((ROW_TILE, LANE), lambda i: (i, 0)),) * 2,
        compiler_params=_cparams(("parallel",)),
        name="rope_tables",
    )(positions.reshape(t, 1), inv_l, sign)


def _head_qk_norm_rope(nope, rope, rope_sw, g_nope, g_rope, g_rope_sw, cs, sn, scale):
    ss = jnp.sum(nope * nope, axis=-1, keepdims=True) + jnp.sum(rope * rope, axis=-1, keepdims=True)
    r = lax.rsqrt(ss * (1.0 / B_QK_HEAD) + EPS) * scale
    return nope * r * g_nope, (rope * g_rope * cs + rope_sw * g_rope_sw * sn) * r


def _latent_kv_body(hn, wd_ref, ga_ref, wuk_ref, wuvt_ref, kg_ref, cs_ref, sn_ref, k_ref, vt_ref):
    z = _dot(hn, wd_ref[...])
    c_kv = z[:, :KV_LORA]
    rope, rope_sw = z[:, KV_LORA:KV_LORA + LANE], z[:, KV_LORA + LANE:]
    cn = _rms(c_kv, ga_ref[...]).astype(BF16)
    kv = _dot(cn, wuk_ref[...])
    vt = _dot_nt(wuvt_ref[...], cn)
    tm = vt.shape[1]
    ones_row = (lax.broadcasted_iota(I32, (VT_HEAD_ROWS - V_HEAD, tm), 0) == 0).astype(BF16)
    for h in range(B_HEADS):
        vt_ref[h * VT_HEAD_ROWS:h * VT_HEAD_ROWS + V_HEAD, :] = vt[h * V_HEAD:(h + 1) * V_HEAD].astype(BF16)
        vt_ref[h * VT_HEAD_ROWS + V_HEAD:(h + 1) * VT_HEAD_ROWS, :] = ones_row
    kg = kg_ref[...]
    for h in range(B_HEADS):
        kn, kr = _head_qk_norm_rope(kv[:, h * QK_NOPE:(h + 1) * QK_NOPE], rope, rope_sw,
                                    kg[:, :LANE], kg[:, LANE:2 * LANE], kg[:, 2 * LANE:],
                                    cs_ref[...], sn_ref[...], 1.0)
        k_ref[:, h * B_QK_PAD:h * B_QK_PAD + QK_NOPE] = kn.astype(BF16)
        k_ref[:, h * B_QK_PAD + QK_NOPE:(h + 1) * B_QK_PAD] = kr.astype(BF16)


def _rope_swap(w):
    half = QK_ROPE // 2
    return jnp.concatenate([w[..., half:], w[..., :half]], axis=-1)


def _pad_lanes(w, n=LANE):
    return jnp.pad(w, [(0, 0)] * (w.ndim - 1) + [(0, n - w.shape[-1])])


def _head_gain(g):
    g_rope = g[QK_NOPE:]
    return jnp.concatenate([g[:QK_NOPE], _pad_lanes(g_rope), _pad_lanes(_rope_swap(g_rope))]).reshape(1, -1)


def _query_body(hn, win_ref, ga_ref, wuq_ref, qg_ref, cs_ref, sn_ref, q_ref, mq_ref):
    proj = _dot(hn, win_ref[...])
    mq_ref[...] = proj[:, Q_LORA:]
    qall = _dot(_rms(proj[:, :Q_LORA], ga_ref[...]).astype(BF16), wuq_ref[...])
    qg = qg_ref[...]
    per_head = QK_NOPE + 2 * LANE
    for h in range(B_HEADS):
        base = h * per_head
        qn, qr = _head_qk_norm_rope(qall[:, base:base + QK_NOPE],
                                    qall[:, base + QK_NOPE:base + QK_NOPE + LANE],
                                    qall[:, base + QK_NOPE + LANE:base + per_head],
                                    qg[:, :LANE], qg[:, LANE:2 * LANE], qg[:, 2 * LANE:],
                                    cs_ref[...], sn_ref[...], B_QK_HEAD ** -0.5)
        q_ref[:, h * B_QK_PAD:h * B_QK_PAD + QK_NOPE] = qn.astype(BF16)
        q_ref[:, h * B_QK_PAD + QK_NOPE:(h + 1) * B_QK_PAD] = qr.astype(BF16)


def _mla_proj_kernel(x_ref, gkv_ref, wd_ref, ga_ref, wuk_ref, wuvt_ref, kg_ref,
                     gq_ref, win_ref, gqa_ref, wuq_ref, qg_ref, cs_ref, sn_ref,
                     k_ref, vt_ref, q_ref, mq_ref):
    x = x_ref[...]
    xn = x * lax.rsqrt(jnp.mean(x * x, axis=-1, keepdims=True) + EPS)
    _latent_kv_body((xn * gkv_ref[...]).astype(BF16), wd_ref, ga_ref, wuk_ref, wuvt_ref, kg_ref,
                    cs_ref, sn_ref, k_ref, vt_ref)
    _query_body((xn * gq_ref[...]).astype(BF16), win_ref, gqa_ref, wuq_ref, qg_ref, cs_ref, sn_ref, q_ref, mq_ref)


def _mla_projection(x2, kv_norm, w_dkv, kv_a_norm, w_ukv, k_head_norm,
                    q_norm, w_in, q_a_g, w_uq, q_head_g, cs, sn):
    t = x2.shape[0]
    w_rope = w_dkv[:, KV_LORA:]
    wd = jnp.concatenate([w_dkv[:, :KV_LORA], _pad_lanes(w_rope), _pad_lanes(_rope_swap(w_rope))],
                         axis=1).astype(BF16)
    wu = w_ukv.reshape(KV_LORA, B_HEADS, QK_NOPE + V_HEAD)
    wuk = wu[:, :, :QK_NOPE].reshape(KV_LORA, -1).astype(BF16)
    wuvt = wu[:, :, QK_NOPE:].reshape(KV_LORA, -1).T.astype(BF16)
    wq = w_uq.reshape(Q_LORA, B_HEADS, B_QK_HEAD)
    wq_rope = wq[:, :, QK_NOPE:]
    wq = jnp.concatenate([wq[:, :, :QK_NOPE], _pad_lanes(wq_rope), _pad_lanes(_rope_swap(wq_rope))],
                         axis=-1).reshape(Q_LORA, -1).astype(BF16)
    consts = (kv_norm.reshape(1, -1), wd, kv_a_norm.reshape(1, -1), wuk, wuvt, _head_gain(k_head_norm),
              q_norm.reshape(1, -1), w_in.astype(BF16), q_a_g.reshape(1, -1), wq, _head_gain(q_head_g))
    row = lambda n: pl.BlockSpec((ROW_TILE, n), lambda i: (i, 0))
    full = lambda a: pl.BlockSpec(a.shape, lambda i: (0,) * a.ndim)
    return pl.pallas_call(
        _mla_proj_kernel,
        out_shape=(jax.ShapeDtypeStruct((t, B_HEADS * B_QK_PAD), BF16),
                   jax.ShapeDtypeStruct((B_HEADS * VT_HEAD_ROWS, t), BF16),
                   jax.ShapeDtypeStruct((t, B_HEADS * B_QK_PAD), BF16),
                   jax.ShapeDtypeStruct((t, M_W), F32)),
        grid=(t // ROW_TILE,),
        in_specs=[row(D_MODEL)] + [full(a) for a in consts] + [row(LANE), row(LANE)],
        out_specs=(row(B_HEADS * B_QK_PAD), pl.BlockSpec((B_HEADS * VT_HEAD_ROWS, ROW_TILE), lambda i: (0, i)),
                   row(B_HEADS * B_QK_PAD), row(M_W)),
        compiler_params=_cparams(("parallel",)),
        name="mla_proj",
    )(x2, *consts, cs, sn)


def _attn_kernel(q_ref, k_ref, vt_ref, out_ref):
    tq, tk = ATT_Q_TILE, ATT_K_TILE
    i = pl.program_id(2)

    def block(h, j, carry, masked):
        m, acc = carry
        off = pl.multiple_of(j * tk, tk)
        q = q_ref[0, :, h * B_QK_PAD:(h + 1) * B_QK_PAD]
        st = _dot_nt(k_ref[0, pl.ds(off, tk), h * B_QK_PAD:(h + 1) * B_QK_PAD], q)
        if masked:
            key = lax.broadcasted_iota(I32, (tk, tq), 0)
            qry = lax.broadcasted_iota(I32, (tk, tq), 1)
            st = jnp.where(key <= qry, st, -jnp.inf)
        m_new = jnp.maximum(m, jnp.max(st, axis=0, keepdims=True))
        p = jnp.exp(st - m_new).astype(BF16)
        vt = vt_ref[h * VT_HEAD_ROWS:(h + 1) * VT_HEAD_ROWS, pl.ds(off, tk)]
        return m_new, jnp.exp(m - m_new) * acc + _dot(vt, p)

    heads = range(ATT_HEADS_PER_STEP)
    init = tuple((jnp.full((1, tq), -jnp.inf, F32), jnp.zeros((VT_HEAD_ROWS, tq), F32)) for _ in heads)
    carry = lax.fori_loop(0, i, lambda j, c: tuple(block(h, j, c[h], False) for h in heads), init)
    for h in heads:
        _, acc = block(h, i, carry[h], True)
        out_t = acc[:V_HEAD] / acc[V_HEAD:V_HEAD + 1]
        out_ref[0, :, h * V_HEAD:(h + 1) * V_HEAD] = out_t.T.astype(BF16)


def _causal_attention(q, k, vt, seq):
    nb = q.shape[0]
    g = ATT_HEADS_PER_STEP
    assert ATT_Q_TILE == ATT_K_TILE and B_HEADS % g == 0
    return pl.pallas_call(
        _attn_kernel,
        out_shape=jax.ShapeDtypeStruct((nb, seq, B_HEADS * V_HEAD), BF16),
        grid=(nb, B_HEADS // g, seq // ATT_Q_TILE),
        in_specs=[pl.BlockSpec((1, ATT_Q_TILE, g * B_QK_PAD), lambda b, h, i: (b, i, h)),
                  pl.BlockSpec((1, seq, g * B_QK_PAD), lambda b, h, i: (b, 0, h)),
                  pl.BlockSpec((g * VT_HEAD_ROWS, seq), lambda b, h, i: (h, b))],
        out_specs=pl.BlockSpec((1, ATT_Q_TILE, g * V_HEAD), lambda b, h, i: (b, i, h)),
        compiler_params=_cparams(("parallel", "parallel", "arbitrary")),
        name="causal_attention",
    )(q, k, vt)


def _router_kernel(x_ref, g_ref, wr_hi_ref, wr_lo_ref, hn_ref, idx_ref, gate_ref, count_ref):
    hn = _rms(x_ref[...], g_ref[...])
    hn_hi = hn.astype(BF16)
    hn_ref[...] = hn_hi
    hn_lo = (hn - hn_hi.astype(F32)).astype(BF16)
    logits = _dot(hn_hi, wr_hi_ref[...]) + _dot(hn_hi, wr_lo_ref[...]) + _dot(hn_lo, wr_hi_ref[...])
    lane = lax.broadcasted_iota(I32, logits.shape, 1)
    logits = jnp.where(lane < N_EXPERTS, logits, -jnp.inf)
    v1 = jnp.max(logits, axis=-1, keepdims=True)
    i1 = jnp.min(jnp.where(logits == v1, lane, LANE), axis=-1, keepdims=True)
    rest = jnp.where(lane == i1, -jnp.inf, logits)
    v2 = jnp.max(rest, axis=-1, keepdims=True)
    i2 = jnp.min(jnp.where(rest == v2, lane, LANE), axis=-1, keepdims=True)
    e2 = jnp.exp(v2 - v1)
    den = 1.0 + e2
    idx_ref[...] = jnp.where(lane == 0, i1, jnp.where(lane == 1, i2, 0))
    record = jnp.zeros(logits.shape, F32)
    for k, gate in enumerate((1.0 / den, e2 / den)):
        hi = gate.astype(BF16).astype(F32)
        mid = (gate - hi).astype(BF16).astype(F32)
        for part, term in enumerate((hi, mid, gate - hi - mid)):
            record = jnp.where(lane == GATE_TERMS * k + part, term, record)
    record = jnp.where(lane == 2 * GATE_TERMS, i1.astype(F32), record)
    record = jnp.where(lane == 2 * GATE_TERMS + 1, i2.astype(F32), record)
    gate_ref[...] = record.astype(BF16)
    pairs = (lane == i1).astype(F32) + (lane == i2).astype(F32)
    count_ref[...] = jnp.broadcast_to(jnp.sum(pairs, axis=0, keepdims=True), count_ref.shape)


def _router(x2, g, w_router):
    t = x2.shape[0]
    row = lambda n: pl.BlockSpec((RANK_TILE, n), lambda i: (i, 0))
    wr = _pad_lanes(w_router)
    wr_hi = wr.astype(BF16)
    wr_lo = (wr - wr_hi.astype(F32)).astype(BF16)
    return pl.pallas_call(
        _router_kernel,
        out_shape=(jax.ShapeDtypeStruct((t, D_MODEL), BF16), jax.ShapeDtypeStruct((t, LANE), I32),
                   jax.ShapeDtypeStruct((t, LANE), BF16), jax.ShapeDtypeStruct((8 * (t // RANK_TILE), LANE), F32)),
        grid=(t // RANK_TILE,),
        in_specs=[row(D_MODEL), pl.BlockSpec((1, D_MODEL), lambda i: (0, 0)),
                  pl.BlockSpec(wr.shape, lambda i: (0, 0)), pl.BlockSpec(wr.shape, lambda i: (0, 0))],
        out_specs=(row(D_MODEL), row(LANE), row(LANE), pl.BlockSpec((8, LANE), lambda i: (i, 0))),
        compiler_params=_cparams(("parallel",)),
        name="moe_router",
    )(x2, g.reshape(1, -1), wr_hi, wr_lo)


def _aligned(count):
    return jnp.ceil(count * (1.0 / RUN_ALIGN)) * RUN_ALIGN


def _rank_kernel(idx_ref, counts_ref, lpos_ref, lposr_ref, tile_ref, runs_ref, run_ref, start_ref):
    blk = pl.program_id(0)
    tb = RANK_TILE
    lane = lax.broadcasted_iota(I32, (tb, LANE), 1)
    idx = idx_ref[...]
    oh0 = (lane == idx[:, 0:1]).astype(F32)
    oh1 = (lane == idx[:, 1:2]).astype(F32)
    both = oh0 + oh1
    run_len = _aligned(jnp.sum(both, axis=0, keepdims=True))
    r = lax.broadcasted_iota(I32, (LANE, LANE), 0)
    c = lax.broadcasted_iota(I32, (LANE, LANE), 1)
    before = (r < c).astype(F32)

    @pl.when(blk == 0)
    def _():
        sizes = jnp.sum(_aligned(counts_ref[...]), axis=0, keepdims=True) * 0.125
        tiles = jnp.ceil(sizes * (1.0 / MOE_ROW_TILE))
        tile_start = jnp.dot(tiles, before, precision=HIGHEST, preferred_element_type=F32)
        start_ref[...] = tile_start * MOE_ROW_TILE
        tile_end = tile_start + tiles
        n_col = r.astype(F32)
        ended = ((n_col >= tile_end) & (c < N_EXPERTS)).astype(F32)
        expert = jnp.sum(ended, axis=-1, keepdims=True)
        total = jnp.max(tile_end, axis=-1, keepdims=True)
        as_column = lambda row: jnp.sum(jnp.where(r == c, row, 0.0), axis=-1, keepdims=True)
        tail_start = as_column(start_ref[...] + sizes)
        tail_len = as_column(tiles * MOE_ROW_TILE - sizes)
        mine = expert == c.astype(F32)
        group_rows = jnp.sum(jnp.where(mine, sizes, 0.0), axis=-1, keepdims=True)
        group_tile = jnp.sum(jnp.where(mine, tile_start, 0.0), axis=-1, keepdims=True)
        used_rows = jnp.clip(group_rows - (n_col[:, 0:1] - group_tile) * MOE_ROW_TILE, 0.0, MOE_ROW_TILE)
        col = lax.broadcasted_iota(I32, tile_ref.shape, 1)
        tile_ref[...] = jnp.where(col == 0, expert, jnp.where(col == 1, total, jnp.where(
            col == 2, tail_start, jnp.where(col == 3, tail_len, used_rows)))).astype(I32)
        run_ref[...] = jnp.zeros_like(run_ref)
        runs_ref[...] = jnp.zeros_like(runs_ref)

    rr = lax.broadcasted_iota(I32, (tb, tb), 0)
    cc = lax.broadcasted_iota(I32, (tb, tb), 1)
    strict = (cc < rr).astype(BF16)
    local = jnp.dot(run_len, before, precision=HIGHEST, preferred_element_type=F32)
    base = _dot(strict, both.astype(BF16)) + local
    p0 = jnp.sum(oh0 * base, axis=-1, keepdims=True)
    p1 = jnp.sum(oh1 * base, axis=-1, keepdims=True)
    posf = jnp.where(lane == 0, p0, jnp.where(lane == 1, p1, 0.0))
    lpos_ref[...] = posf.astype(I32)
    pick = (lax.broadcasted_iota(I32, (8, LANE), 0) == lax.broadcasted_iota(I32, (8, LANE), 1)).astype(F32)
    lposr_ref[...] = _dot_nt_highest(pick, posf).astype(I32)
    n = runs_ref.shape[0] // 3
    mine = lax.broadcasted_iota(I32, (n, LANE), 0) == blk
    for k, value in enumerate((local, run_ref[...] + start_ref[...], run_len)):
        runs_ref[k * n:(k + 1) * n, :] = jnp.where(mine, value.astype(I32), runs_ref[k * n:(k + 1) * n, :])
    run_ref[...] += run_len


def _rank(idx, counts):
    t = idx.shape[0]
    nblk = t // RANK_TILE
    const = lambda shape: pl.BlockSpec(shape, lambda i: (0, 0))
    return pl.pallas_call(
        _rank_kernel,
        out_shape=(jax.ShapeDtypeStruct((t, LANE), I32), jax.ShapeDtypeStruct((8, t), I32),
                   jax.ShapeDtypeStruct((LANE, 8), I32), jax.ShapeDtypeStruct((3 * nblk, LANE), I32)),
        grid=(nblk,),
        in_specs=[pl.BlockSpec((RANK_TILE, LANE), lambda i: (i, 0)), const(counts.shape)],
        out_specs=(pl.BlockSpec((RANK_TILE, LANE), lambda i: (i, 0)),
                   pl.BlockSpec((8, RANK_TILE), lambda i: (0, i)),
                   const((LANE, 8)), const((3 * nblk, LANE))),
        scratch_shapes=[pltpu.VMEM((1, LANE), F32), pltpu.VMEM((1, LANE), F32)],
        compiler_params=_cparams(("arbitrary",)),
        name="moe_rank",
    )(idx, counts)


def _for_each_piece(length, fn):
    for size in RUN_PIECES:
        @pl.when((length & size) != 0)
        def _(size=size):
            fn(pl.multiple_of(length & (-2 * size), RUN_ALIGN), size)


def _for_each_run_piece(blk, loc_ref, dst_ref, len_ref, fn):
    for e in range(N_EXPERTS):
        k = blk * N_EXPERTS + e
        loc, dst = loc_ref[k], dst_ref[k]
        _for_each_piece(len_ref[k], lambda done, size, loc=loc, dst=dst: fn(
            pl.multiple_of(loc + done, RUN_ALIGN), pl.multiple_of(dst + done, RUN_ALIGN), size))


def _dispatch_kernel(loc_ref, dst_ref, len_ref, tail_ref, tail_len_ref, ntiles_ref, hn_ref, gate_ref, lposr_ref,
                     xs_ref, local_ref, zero_ref, sems):
    blk = pl.program_id(0)

    @pl.when(blk == 0)
    def _():
        sem = sems.at[2]
        zero_ref[...] = jnp.zeros_like(zero_ref)

        def fill(e):
            start = tail_ref[e]
            return lambda done, size: pltpu.make_async_copy(
                zero_ref.at[pl.ds(0, size), :], xs_ref.at[pl.ds(pl.multiple_of(start + done, RUN_ALIGN), size), :], sem)

        def fill_tile(i):
            return pltpu.make_async_copy(
                zero_ref, xs_ref.at[pl.ds(pl.multiple_of(i * RANK_TILE, RANK_TILE), RANK_TILE), :], sem)

        def unused_tiles(action):
            def body(i, carry):
                action(fill_tile(i))
                return carry
            lax.fori_loop(ntiles_ref[0] * (MOE_ROW_TILE // RANK_TILE), xs_ref.shape[0] // RANK_TILE, body, 0)

        for e in range(N_EXPERTS):
            _for_each_piece(tail_len_ref[e], lambda *a, e=e: fill(e)(*a).start())
        unused_tiles(lambda copy: copy.start())
        for e in range(N_EXPERTS):
            _for_each_piece(tail_len_ref[e], lambda *a, e=e: fill(e)(*a).wait())
        unused_tiles(lambda copy: copy.wait())

    rows = lax.broadcasted_iota(I32, (LOCAL_ROWS, 1), 0)
    sel = jnp.where(lposr_ref[0:1, :] == rows, 1.0, jnp.where(lposr_ref[1:2, :] == rows, 1.0, 0.0)).astype(BF16)

    def step(local_ref, sem, other_ref, other_sem):
        local_ref[:, :D_MODEL] = _dot(sel, hn_ref[...]).astype(BF16)
        local_ref[:, D_MODEL:] = _dot(sel, gate_ref[...]).astype(BF16)

        def copy(buf, buf_sem):
            return lambda loc, dst, size: pltpu.make_async_copy(
                buf.at[pl.ds(loc, size), :], xs_ref.at[pl.ds(dst, size), :], buf_sem)

        _for_each_run_piece(blk, loc_ref, dst_ref, len_ref, lambda *a: copy(local_ref, sem)(*a).start())

        @pl.when(blk > 0)
        def _():
            _for_each_run_piece(blk - 1, loc_ref, dst_ref, len_ref, lambda *a: copy(other_ref, other_sem)(*a).wait())

        @pl.when(blk == pl.num_programs(0) - 1)
        def _():
            _for_each_run_piece(blk, loc_ref, dst_ref, len_ref, lambda *a: copy(local_ref, sem)(*a).wait())

    @pl.when(blk % 2 == 0)
    def _():
        step(local_ref.at[0], sems.at[0], local_ref.at[1], sems.at[1])

    @pl.when(blk % 2 == 1)
    def _():
        step(local_ref.at[1], sems.at[1], local_ref.at[0], sems.at[0])


def _dispatch(runs, tails, hn, gates, lpos_rows, rows):
    nblk = hn.shape[0] // RANK_TILE
    tok = lambda n: pl.BlockSpec((RANK_TILE, n), lambda b, *_: (b, 0))
    return pl.pallas_call(
        _dispatch_kernel,
        out_shape=jax.ShapeDtypeStruct((rows, XS_WIDTH), BF16),
        grid_spec=pltpu.PrefetchScalarGridSpec(
            num_scalar_prefetch=6,
            grid=(nblk,),
            in_specs=[tok(D_MODEL), tok(LANE), pl.BlockSpec((8, RANK_TILE), lambda b, *_: (0, b))],
            out_specs=pl.BlockSpec(memory_space=pl.ANY),
            scratch_shapes=[pltpu.VMEM((2, LOCAL_ROWS, XS_WIDTH), BF16), pltpu.VMEM((RANK_TILE, XS_WIDTH), BF16),
                            pltpu.SemaphoreType.DMA((3,))],
        ),
        compiler_params=_cparams(("arbitrary",)),
        name="moe_dispatch",
    )(*runs, *tails, hn, gates, lpos_rows)


def _moe_kernel(expert_ref, ntiles_ref, used_ref, x_ref, wg_ref, wu_ref, wd_ref, out_ref, acc_ref, h_ref):
    i, j = pl.program_id(0), pl.program_id(1)
    active = i < ntiles_ref[0]
    parts = -(-used_ref[i] // MOE_SKIP_ROWS)

    @pl.when(j == 0)
    def _():
        acc_ref[...] = jnp.zeros_like(acc_ref)

    for n in range(1, MOE_ROW_TILE // MOE_SKIP_ROWS + 1):
        @pl.when(active & (parts == n))
        def _(n=n):
            lead = pl.ds(0, n * MOE_SKIP_ROWS)
            _swiglu_accumulate(acc_ref.at[lead, :], h_ref.at[lead, :], x_ref[:n * MOE_SKIP_ROWS, :D_MODEL],
                               wg_ref.at[0], wu_ref.at[0], wd_ref.at[0])

    @pl.when(j == pl.num_programs(1) - 1)
    def _():
        gs = x_ref[:, D_MODEL:].astype(F32)
        gate = [sum(gs[:, GATE_TERMS * k + n:GATE_TERMS * k + n + 1] for n in range(GATE_TERMS)) for k in range(TOP_K)]
        first = gs[:, TOP_K * GATE_TERMS:TOP_K * GATE_TERMS + 1] == expert_ref[i].astype(F32)
        row_gate = jnp.where(first, gate[0], gate[1])
        out_ref[...] = jnp.where(active, acc_ref[...] * row_gate, 0.0).astype(BF16)


def _moe_experts(tile_expert, n_tiles, tile_used, xs, w_gate_up, w_down):
    rows = xs.shape[0]
    tm, tf = MOE_ROW_TILE, MOE_FF_TILE
    nj = D_FF // tf

    def x_map(i, j, e_ref, n_ref, u_ref):
        return jnp.clip(i, 0, jnp.maximum(n_ref[0] - 1, 0)), 0

    def w_idx(i, j, expert_ref, ntiles_ref):
        e = jnp.minimum(expert_ref[i], N_EXPERTS - 1)
        return e, jnp.where(i < ntiles_ref[0], j, nj - 1)

    def w_gate_map(i, j, e_ref, n_ref, u_ref):
        e, jj = w_idx(i, j, e_ref, n_ref)
        return e, 0, jj

    def w_up_map(i, j, e_ref, n_ref, u_ref):
        e, jj = w_idx(i, j, e_ref, n_ref)
        return e, 0, nj + jj

    def w_down_map(i, j, e_ref, n_ref, u_ref):
        e, jj = w_idx(i, j, e_ref, n_ref)
        return e, jj, 0

    return pl.pallas_call(
        _moe_kernel,
        out_shape=jax.ShapeDtypeStruct((rows, D_MODEL), BF16),
        grid_spec=pltpu.PrefetchScalarGridSpec(
            num_scalar_prefetch=3,
            grid=(rows // tm, nj),
            in_specs=[pl.BlockSpec((tm, XS_WIDTH), x_map),
                      pl.BlockSpec((1, D_MODEL, tf), w_gate_map),
                      pl.BlockSpec((1, D_MODEL, tf), w_up_map),
                      pl.BlockSpec((1, tf, D_MODEL), w_down_map)],
            out_specs=pl.BlockSpec((tm, D_MODEL), lambda i, j, *_: (i, 0)),
            scratch_shapes=[pltpu.VMEM((tm, D_MODEL), F32), pltpu.VMEM((tm, tf), BF16)],
        ),
        compiler_params=_cparams(("arbitrary", "arbitrary")),
        name="moe_experts",
    )(tile_expert, n_tiles, tile_used, xs, w_gate_up, w_gate_up, w_down)


def _combine_kernel(loc_ref, dst_ref, len_ref, x_ref, lpos_ref, ys_ref, out_ref, local_ref, sems):
    blk = pl.program_id(0)
    rows = lax.broadcasted_iota(I32, (1, LOCAL_ROWS), 1)
    lpos = lpos_ref[...]
    sel = jnp.where(lpos[:, 0:1] == rows, 1.0, jnp.where(lpos[:, 1:2] == rows, 1.0, 0.0)).astype(BF16)

    def fetch(block, buf, buf_sem, action):
        _for_each_run_piece(block, loc_ref, dst_ref, len_ref, lambda loc, dst, size: action(pltpu.make_async_copy(
            ys_ref.at[pl.ds(dst, size), :], buf.at[pl.ds(loc, size), :], buf_sem)))

    def start_fetch(block, buf, buf_sem):
        buf[...] = jnp.zeros_like(buf)
        fetch(block, buf, buf_sem, lambda copy: copy.start())

    def step(buf, buf_sem, other, other_sem):
        @pl.when(blk == 0)
        def _():
            start_fetch(blk, buf, buf_sem)

        @pl.when(blk + 1 < pl.num_programs(0))
        def _():
            start_fetch(blk + 1, other, other_sem)

        fetch(blk, buf, buf_sem, lambda copy: copy.wait())
        out_ref[...] = x_ref[...] + _dot(sel, buf[...])

    @pl.when(blk % 2 == 0)
    def _():
        step(local_ref.at[0], sems.at[0], local_ref.at[1], sems.at[1])

    @pl.when(blk % 2 == 1)
    def _():
        step(local_ref.at[1], sems.at[1], local_ref.at[0], sems.at[0])


def _combine(runs, x2, lpos, ys):
    tok = lambda n: pl.BlockSpec((RANK_TILE, n), lambda b, *_: (b, 0))
    return pl.pallas_call(
        _combine_kernel,
        out_shape=jax.ShapeDtypeStruct(x2.shape, F32),
        grid_spec=pltpu.PrefetchScalarGridSpec(
            num_scalar_prefetch=3,
            grid=(x2.shape[0] // RANK_TILE,),
            in_specs=[tok(D_MODEL), tok(LANE), pl.BlockSpec(memory_space=pl.ANY)],
            out_specs=tok(D_MODEL),
            scratch_shapes=[pltpu.VMEM((2, LOCAL_ROWS, D_MODEL), BF16), pltpu.SemaphoreType.DMA((2,))],
        ),
        compiler_params=_cparams(("arbitrary",)),
        name="moe_combine",
    )(*runs, x2, lpos, ys)


def _moe_ffn(x2, g, w_router, w_gate_up, w_down):
    t = x2.shape[0]
    nblk = t // RANK_TILE
    hn, idx, gates, counts = _router(x2, g, w_router)
    lpos, lpos_rows, tile_info, run_tab = _rank(idx, counts)
    runs = tuple(run_tab.reshape(3, nblk, LANE)[:, :, :N_EXPERTS].reshape(3, -1))
    rows = t * TOP_K + nblk * N_EXPERTS * (RUN_ALIGN - 1) + N_EXPERTS * (MOE_ROW_TILE - 1)
    rows = -(-rows // MOE_ROW_TILE) * MOE_ROW_TILE
    assert rows // MOE_ROW_TILE <= LANE
    assert MOE_ROW_TILE % RANK_TILE == 0 and rows % RANK_TILE == 0
    xs = _dispatch(runs, (tile_info[:, 2], tile_info[:, 3], tile_info[:1, 1]), hn, gates, lpos_rows, rows)
    ys = _moe_experts(tile_info[:, 0], tile_info[:1, 1], tile_info[:, 4], xs,
                      w_gate_up, w_down)
    return _combine(runs, x2, lpos, ys)


def _pad_heads(w, heads, dim, pad):
    w = w.reshape(w.shape[:-1] + (heads, dim))
    return _pad_lanes(w, pad).reshape(w.shape[:-2] + (heads * pad,))


def kernel(x, mem, positions, a_norm, a_w_in, a_gate_bias, a_head_norm, a_w_out, b_norm, b_w_in, b_q_a_norm, b_w_uq, b_q_head_norm, b_w_out, kv_norm, w_dkv, kv_a_norm, w_ukv, k_head_norm, mem_norm, mem_w_kv, mem_q_norm, mem_k_norm, ffn_norm, dense_w_gate_up, dense_w_down, moe_router, moe_w_gate_up, moe_w_down):
    nb, seq, _ = x.shape
    t = nb * seq
    x2 = x.reshape(t, D_MODEL)
    gmat = jnp.kron(jnp.eye(M_HEADS, dtype=F32), jnp.full((M_HEAD_DIM, M_HEAD_DIM), 1.0 / M_HEAD_DIM, F32)).astype(BF16)

    kbd0, vbd0 = _memory_kv(mem, mem_norm[0], mem_w_kv[0], mem_k_norm[0], gmat)
    w_in = a_w_in[0]
    qk_w, v_w = A_HEADS * A_QK_DIM, A_HEADS * A_V_DIM
    o0, o1, o2, o3, o4 = qk_w, 2 * qk_w, 2 * qk_w + v_w, 2 * qk_w + 2 * v_w, 2 * qk_w + 2 * v_w + 2 * A_HEADS
    w_main = jnp.concatenate([
        _pad_heads(w_in[:, :o0], A_HEADS, A_QK_DIM, A_QK_PAD),
        _pad_heads(w_in[:, o0:o1], A_HEADS, A_QK_DIM, A_QK_PAD),
        w_in[:, o4:]], axis=1).astype(BF16)
    w_vo_t = jnp.concatenate([
        _pad_heads(w_in[:, o1:o2], A_HEADS, A_V_DIM, A_V_PAD),
        _pad_heads(w_in[:, o2:o3], A_HEADS, A_V_DIM, A_V_PAD)], axis=1).T.astype(BF16)
    q, k, vt, ot, mq, gc, gr = _a_projection(x2, a_norm[0], w_main, w_vo_t, w_in[:, o3:o4], a_gate_bias[0])
    three = lambda a: a.reshape(nb, seq, a.shape[-1])
    head_g = _pad_heads(a_head_norm[0].reshape(1, -1), A_HEADS, A_V_DIM, A_V_PAD)
    hm = _mlstm(three(q), three(k), vt, ot, gc, gr,
                jnp.broadcast_to(head_g.reshape(-1, 1), (A_HEADS * A_V_PAD, MLSTM_CHUNK)))
    w_out = a_w_out[0]
    w_out_h = jnp.pad(w_out[:v_w].reshape(A_HEADS, A_V_DIM, D_MODEL), ((0, 0), (0, A_V_PAD - A_V_DIM), (0, 0)))
    w_out_h = w_out_h.reshape(A_HEADS * A_V_PAD, D_MODEL).astype(BF16)
    x2 = _mix_out(x2, hm.reshape(t, -1), mq, kbd0, vbd0, gmat, mem_q_norm[0],
                  w_out_h, w_out[v_w:].astype(BF16), seq)
    x2 = _dense_ffn(x2, ffn_norm[0], dense_w_gate_up[0].astype(BF16), dense_w_down[0].astype(BF16))

    cs, sn = _rope_tables(positions)
    k_sh, vt_sh, qh, mq1 = _mla_projection(x2, kv_norm, w_dkv, kv_a_norm, w_ukv, k_head_norm,
                                           b_norm[0], b_w_in[0], b_q_a_norm[0], b_w_uq[0], b_q_head_norm[0], cs, sn)

    kbd1, vbd1 = _memory_kv(mem, mem_norm[1], mem_w_kv[1], mem_k_norm[1], gmat)
    att = _causal_attention(three(qh), three(k_sh), vt_sh, seq)
    w_out = b_w_out[0]
    n_att = B_HEADS * V_HEAD
    x2 = _mix_out(x2, att.reshape(t, -1), mq1, kbd1, vbd1, gmat, mem_q_norm[1],
                  w_out[:n_att].astype(BF16), w_out[n_att:].astype(BF16), seq)
    x2 = _moe_ffn(x2, ffn_norm[1], moe_router[0], moe_w_gate_up[0], moe_w_down[0])
    return x2.reshape(nb, seq, D_MODEL)
```

```python
import functools

import jax
import jax.numpy as jnp
from jax import lax
from jax.experimental import pallas as pl
from jax.experimental.pallas import tpu as pltpu

F32 = jnp.float32
BF16 = jnp.bfloat16
I32 = jnp.int32

EPS = 1e-6
LOG2E = 1.4426950408889634
LANE = 128
VMEM_LIMIT = 48 * 1024 * 1024

D_MODEL = 1024
N_MEM = 256
M_HEADS, M_HEAD_DIM = 4, 64
M_W = M_HEADS * M_HEAD_DIM
A_HEADS, A_QK_DIM, A_V_DIM = 4, 96, 192
A_QK_PAD, A_V_PAD = 128, 256
B_HEADS, Q_LORA, KV_LORA = 6, 384, 256
QK_NOPE, QK_ROPE, V_HEAD = 128, 64, 128
B_QK_HEAD = QK_NOPE + QK_ROPE
B_QK_PAD = 256
VT_HEAD_ROWS = V_HEAD + 16
ROPE_THETA = 10000.0
D_FF = 3584
N_EXPERTS, TOP_K = 8, 2
GATE_TERMS = 3

MLSTM_CHUNK = 512
ROW_TILE = 1024
FFN_ROW_TILE = 1024
FFN_FF_TILE = 1792
FFN_CHUNK = 256
MOE_ROW_TILE = 1024
MOE_SKIP_ROWS = 256
ATT_Q_TILE = 512
ATT_K_TILE = 512
ATT_HEADS_PER_STEP = 6
RANK_TILE = 512
RUN_ALIGN = 16
RUN_PIECES = tuple(RANK_TILE >> s for s in range((RANK_TILE // RUN_ALIGN).bit_length()))
LOCAL_ROWS = TOP_K * RANK_TILE + N_EXPERTS * RUN_ALIGN
XS_WIDTH = D_MODEL + LANE

HIGHEST = lax.Precision.HIGHEST


def _cparams(sem):
    return pltpu.CompilerParams(dimension_semantics=sem, vmem_limit_bytes=VMEM_LIMIT)


def _rms(x, g):
    return x * lax.rsqrt(jnp.mean(x * x, axis=-1, keepdims=True) + EPS) * g


def _dot(a, b):
    return jnp.dot(a, b, preferred_element_type=F32)


def _dot_nt(a, b):
    return lax.dot_general(a, b, (((1,), (1,)), ((), ())), preferred_element_type=F32)


def _dot_tn(a, b):
    return lax.dot_general(a, b, (((0,), (0,)), ((), ())), preferred_element_type=F32)


def _group_mean_sq(x, gmat):
    sq = x * x
    hi = sq.astype(BF16)
    lo = (sq - hi.astype(F32)).astype(BF16)
    return _dot(hi, gmat) + _dot(lo, gmat)


def _memkv_kernel(mem_ref, g_ref, w_ref, kg_ref, gmat_ref, kbd_ref, vbd_ref):
    hn = _rms(mem_ref[0], g_ref[...]).astype(BF16)
    kv = _dot(hn, w_ref[...])
    k, v = kv[:, :M_W], kv[:, M_W:]
    kn = k * lax.rsqrt(_group_mean_sq(k, gmat_ref[...]) + EPS) * kg_ref[...]
    lane_head = lax.broadcasted_iota(I32, (1, M_W), 1) // M_HEAD_DIM
    for h in range(M_HEADS):
        keep = lane_head == h
        kbd_ref[0, h * N_MEM:(h + 1) * N_MEM, :] = jnp.where(keep, kn, 0.0).astype(BF16)
        vbd_ref[0, h * N_MEM:(h + 1) * N_MEM, :] = jnp.where(keep, v, 0.0).astype(BF16)


def _memory_kv(mem, g, w_kv, k_g, gmat):
    nb = mem.shape[0]
    out = jax.ShapeDtypeStruct((nb, M_HEADS * N_MEM, M_W), BF16)
    return pl.pallas_call(
        _memkv_kernel,
        out_shape=(out, out),
        grid=(nb,),
        in_specs=[
            pl.BlockSpec((1, N_MEM, D_MODEL), lambda b: (b, 0, 0)),
            pl.BlockSpec((1, D_MODEL), lambda b: (0, 0)),
            pl.BlockSpec((D_MODEL, 2 * M_W), lambda b: (0, 0)),
            pl.BlockSpec((1, M_W), lambda b: (0, 0)),
            pl.BlockSpec((M_W, M_W), lambda b: (0, 0)),
        ],
        out_specs=(pl.BlockSpec((1, M_HEADS * N_MEM, M_W), lambda b: (b, 0, 0)),) * 2,
        compiler_params=_cparams(("parallel",)),
        name="memory_kv",
    )(mem, g.reshape(1, -1), w_kv.astype(BF16), jnp.tile(k_g, M_HEADS).reshape(1, -1), gmat)


def _memory_attention(mq, kbd, vbd, gmat, qg):
    qn = mq * lax.rsqrt(_group_mean_sq(mq, gmat) + EPS) * (qg * (M_HEAD_DIM ** -0.5 * LOG2E))
    s = _dot_nt(qn.astype(BF16), kbd)
    ps = []
    for h in range(M_HEADS):
        sh = s[:, h * N_MEM:(h + 1) * N_MEM]
        e = jnp.exp2(sh - jnp.max(sh, axis=-1, keepdims=True))
        ps.append((e / jnp.sum(e, axis=-1, keepdims=True)).astype(BF16))
    return _dot(jnp.concatenate(ps, axis=-1), vbd)


def _a_proj_kernel(x_ref, g_ref, w_ref, wvot_ref, wif_ref, wift_ref, bc_ref, br_ref,
                   q_ref, k_ref, vt_ref, ot_ref, mq_ref, gc_ref, gr_ref):
    hn = _rms(x_ref[...], g_ref[...]).astype(BF16)
    nq = A_HEADS * A_QK_PAD
    nv = A_HEADS * A_V_PAD
    q_ref[...] = _dot(hn, w_ref[:, :nq]).astype(BF16)
    k_ref[...] = (_dot(hn, w_ref[:, nq:2 * nq]) * (A_QK_DIM ** -0.5)).astype(BF16)
    mq_ref[...] = _dot(hn, w_ref[:, 2 * nq:])
    vt = _dot_nt(wvot_ref[:nv, :], hn)
    ones_row = lax.broadcasted_iota(I32, (nv, 1), 0) % A_V_PAD == A_V_DIM
    vt_ref[...] = jnp.where(ones_row, 1.0, vt).astype(BF16)
    ot_ref[...] = _dot_nt(wvot_ref[nv:, :], hn)
    gc_ref[...] = _dot(hn, wif_ref[...])[:, :2 * A_HEADS] + bc_ref[...]
    gr_ref[...] = _dot_nt(wift_ref[...], hn) + br_ref[...]


def _a_projection(x2, g, w_main, w_vo_t, w_if, gate_bias):
    t = x2.shape[0]
    nq, nv = A_HEADS * A_QK_PAD, A_HEADS * A_V_PAD
    ng = 2 * A_HEADS
    wif_pad = jnp.pad(w_if, ((0, 0), (0, LANE - ng))).astype(BF16)
    row = lambda n: pl.BlockSpec((ROW_TILE, n), lambda i: (i, 0))
    col = lambda n: pl.BlockSpec((n, ROW_TILE), lambda i: (0, i))
    full = lambda a: pl.BlockSpec(a.shape, lambda i: (0,) * a.ndim)
    args = (x2, g.reshape(1, -1), w_main, w_vo_t, wif_pad, w_if.T.astype(BF16),
            gate_bias.reshape(1, ng), gate_bias.reshape(ng, 1))
    return pl.pallas_call(
        _a_proj_kernel,
        out_shape=(jax.ShapeDtypeStruct((t, nq), BF16), jax.ShapeDtypeStruct((t, nq), BF16),
                   jax.ShapeDtypeStruct((nv, t), BF16), jax.ShapeDtypeStruct((nv, t), F32),
                   jax.ShapeDtypeStruct((t, M_W), F32), jax.ShapeDtypeStruct((t, ng), F32),
                   jax.ShapeDtypeStruct((ng, t), F32)),
        grid=(t // ROW_TILE,),
        in_specs=[row(D_MODEL)] + [full(a) for a in args[1:]],
        out_specs=(row(nq), row(nq), col(nv), col(nv), row(M_W), row(ng), col(ng)),
        compiler_params=_cparams(("parallel",)),
        name="mlstm_in_proj",
    )(*args)


def _log_sigmoid(f):
    return jnp.minimum(f, 0.0) - jnp.log(1.0 + jnp.exp(-jnp.abs(f)))


def _dot_nt_highest(a, b):
    return lax.dot_general(a, b, (((1,), (1,)), ((), ())), precision=HIGHEST,
                           preferred_element_type=F32)


def _chunk_gates(gc, gr):
    L = gc.shape[0]
    r = lax.broadcasted_iota(I32, (L, L), 0)
    c = lax.broadcasted_iota(I32, (L, L), 1)
    lower = (c <= r).astype(F32)
    is_f_col = lax.broadcasted_iota(I32, gc.shape, 1) >= A_HEADS
    is_f_row = lax.broadcasted_iota(I32, gr.shape, 0) >= A_HEADS
    lf_c = jnp.where(is_f_col, _log_sigmoid(gc), 0.0)
    lf_r = jnp.where(is_f_row, _log_sigmoid(gr), 0.0)
    lf_c = jnp.concatenate([lf_c, jnp.zeros((L, LANE - gc.shape[1]), F32)], axis=1)
    cum_c = jnp.dot(lower, lf_c, precision=HIGHEST, preferred_element_type=F32)[:, :gc.shape[1]]
    cum_r = _dot_nt_highest(lf_r, lower)
    return jnp.where(is_f_col, cum_c, gc), jnp.where(is_f_row, cum_r, gr)


def _prefix_max_lanes(x):
    lane = lax.broadcasted_iota(I32, x.shape, 1)
    shift = 1
    while shift < x.shape[1]:
        x = jnp.maximum(x, jnp.where(lane >= shift, pltpu.roll(x, shift, axis=1), -jnp.inf))
        shift *= 2
    return x


def _mlstm_kernel(q_ref, k_ref, vt_ref, ot_ref, gc_ref, gr_ref, hg_ref, out_ref, c_ref, m_ref):
    L = MLSTM_CHUNK

    @pl.when(pl.program_id(1) == 0)
    def _():
        c_ref[...] = jnp.zeros_like(c_ref)
        m_ref[...] = jnp.zeros_like(m_ref)

    src = lax.broadcasted_iota(I32, (L, L), 0)
    tgt = lax.broadcasted_iota(I32, (L, L), 1)
    real = lax.broadcasted_iota(I32, (A_V_PAD, 1), 0) < A_V_DIM
    gcol, grow = _chunk_gates(gc_ref[...], gr_ref[...])
    for hd in range(A_HEADS):
        qk = slice(hd * A_QK_PAD, (hd + 1) * A_QK_PAD)
        vv = slice(hd * A_V_PAD, (hd + 1) * A_V_PAD)
        q, k, vt = q_ref[0, :, qk], k_ref[0, :, qk], vt_ref[vv, :]
        u_c = gcol[:, hd:hd + 1] - gcol[:, A_HEADS + hd:A_HEADS + hd + 1]
        g_r = grow[A_HEADS + hd:A_HEADS + hd + 1, :]
        u_r = grow[hd:hd + 1, :] - g_r
        g_last = g_r[:, L - 1:L]
        m_prev = m_ref[hd, 0:1, 0:1]
        c_prev = c_ref[hd]

        run_max = jnp.maximum(_prefix_max_lanes(jnp.broadcast_to(u_r, (8, L)))[0:1, :], m_prev)
        m_t = g_r + run_max
        inter = jnp.exp(m_prev - run_max)
        decay_t = jnp.where(src <= tgt, jnp.exp(u_c - run_max), 0.0)
        p_t = (decay_t * _dot_nt(k, q)).astype(BF16)
        num_t = inter * _dot_nt(c_prev.astype(BF16), q) + _dot(vt, p_t)
        den = num_t[A_V_DIM:A_V_DIM + 1, :]
        h_t = jnp.where(real, num_t / jnp.maximum(jnp.abs(den), jnp.exp(-m_t)), 0.0)
        scale = lax.rsqrt(jnp.sum(h_t * h_t, axis=0, keepdims=True) * (1.0 / A_V_DIM) + EPS)
        out_t = h_t * scale * hg_ref[vv, :] * jax.nn.sigmoid(ot_ref[vv, :])
        out_ref[0, :, vv] = out_t.T.astype(BF16)

        w_r = g_last + u_r
        m_new = jnp.maximum(g_last + m_prev, jnp.max(w_r, axis=-1, keepdims=True))
        ev_t = (jnp.exp(w_r - m_new) * vt.astype(F32)).astype(BF16)
        c_ref[hd] = jnp.exp(g_last + m_prev - m_new) * c_prev + _dot(ev_t, k)
        m_ref[hd] = jnp.broadcast_to(m_new, m_ref.shape[1:])


def _mlstm(q, k, vt, ot, gcol, grow, head_g):
    nb, s, _ = q.shape
    L = MLSTM_CHUNK
    nv = A_HEADS * A_V_PAD
    blk = lambda w: pl.BlockSpec((1, L, w), lambda b, c: (b, c, 0))
    col = lambda n: pl.BlockSpec((n, L), lambda b, c: (0, b * (s // L) + c))
    return pl.pallas_call(
        _mlstm_kernel,
        out_shape=jax.ShapeDtypeStruct((nb, s, nv), BF16),
        grid=(nb, s // L),
        in_specs=[blk(A_HEADS * A_QK_PAD), blk(A_HEADS * A_QK_PAD), col(nv), col(nv),
                  pl.BlockSpec((L, 2 * A_HEADS), lambda b, c: (b * (s // L) + c, 0)), col(2 * A_HEADS),
                  pl.BlockSpec((nv, L), lambda b, c: (0, 0))],
        out_specs=blk(nv),
        scratch_shapes=[pltpu.VMEM((A_HEADS, A_V_PAD, A_QK_PAD), F32), pltpu.VMEM((A_HEADS, 8, LANE), F32)],
        compiler_params=_cparams(("parallel", "arbitrary")),
        name="mlstm_chunkwise",
    )(q, k, vt, ot, gcol, grow, head_g)


def _mix_out_kernel(x_ref, h_ref, mq_ref, kbd_ref, vbd_ref, gmat_ref, qg_ref, w1_ref, w2_ref, out_ref):
    mo = _memory_attention(mq_ref[...], kbd_ref[0], vbd_ref[0], gmat_ref[...], qg_ref[...])
    out_ref[...] = x_ref[...] + _dot(h_ref[...], w1_ref[...]) + _dot(mo.astype(BF16), w2_ref[...])


def _mix_out(x2, h2, mq, kbd, vbd, gmat, qg, w_main, w_mem, seq):
    t = x2.shape[0]
    tm = ROW_TILE
    row = lambda n: pl.BlockSpec((tm, n), lambda i: (i, 0))
    full = lambda a: pl.BlockSpec(a.shape, lambda i: (0,) * a.ndim)
    per_batch = pl.BlockSpec((1,) + kbd.shape[1:], lambda i: ((i * tm) // seq, 0, 0))
    qg_t = jnp.tile(qg, M_HEADS).reshape(1, -1)
    return pl.pallas_call(
        _mix_out_kernel,
        out_shape=jax.ShapeDtypeStruct((t, D_MODEL), F32),
        grid=(t // tm,),
        in_specs=[row(D_MODEL), row(h2.shape[1]), row(M_W), per_batch, per_batch,
                  full(gmat), full(qg_t), full(w_main), full(w_mem)],
        out_specs=row(D_MODEL),
        compiler_params=_cparams(("parallel",)),
        name="mixer_out_proj",
    )(x2, h2, mq, kbd, vbd, gmat, qg_t, w_main, w_mem)


def _swiglu_accumulate(acc_ref, h_ref, x, wg_ref, wu_ref, wd_ref):
    for c in range(wg_ref.shape[-1] // FFN_CHUNK):
        cols = slice(c * FFN_CHUNK, (c + 1) * FFN_CHUNK)
        gate = _dot(x, wg_ref[:, cols])
        up = _dot(x, wu_ref[:, cols])
        h_ref[:, cols] = (jax.nn.silu(gate) * up).astype(BF16)
    acc_ref[...] += _dot(h_ref[...], wd_ref[...])


def _ffn_kernel(x_ref, g_ref, wg_ref, wu_ref, wd_ref, out_ref, hn_ref, acc_ref, h_ref):
    j = pl.program_id(1)

    @pl.when(j == 0)
    def _():
        hn_ref[...] = _rms(x_ref[...], g_ref[...]).astype(BF16)
        acc_ref[...] = jnp.zeros_like(acc_ref)

    _swiglu_accumulate(acc_ref, h_ref, hn_ref[...], wg_ref, wu_ref, wd_ref)

    @pl.when(j == pl.num_programs(1) - 1)
    def _():
        out_ref[...] = x_ref[...] + acc_ref[...]


def _dense_ffn(x2, g, w_gate_up, w_down):
    t = x2.shape[0]
    tm, tf = FFN_ROW_TILE, FFN_FF_TILE
    nj = D_FF // tf
    return pl.pallas_call(
        _ffn_kernel,
        out_shape=jax.ShapeDtypeStruct((t, D_MODEL), F32),
        grid=(t // tm, nj),
        in_specs=[pl.BlockSpec((tm, D_MODEL), lambda i, j: (i, 0)),
                  pl.BlockSpec((1, D_MODEL), lambda i, j: (0, 0)),
                  pl.BlockSpec((D_MODEL, tf), lambda i, j: (0, j)),
                  pl.BlockSpec((D_MODEL, tf), lambda i, j: (0, nj + j)),
                  pl.BlockSpec((tf, D_MODEL), lambda i, j: (j, 0))],
        out_specs=pl.BlockSpec((tm, D_MODEL), lambda i, j: (i, 0)),
        scratch_shapes=[pltpu.VMEM((tm, D_MODEL), BF16), pltpu.VMEM((tm, D_MODEL), F32),
                        pltpu.VMEM((tm, tf), BF16)],
        compiler_params=_cparams(("parallel", "arbitrary")),
        name="dense_swiglu",
    )(x2, g.reshape(1, -1), w_gate_up, w_gate_up, w_down)


def _rope_kernel(pos_ref, inv_ref, sign_ref, cs_ref, sn_ref):
    ang = pos_ref[...].astype(F32) * inv_ref[...]
    cs_ref[...] = jnp.cos(ang)
    sn_ref[...] = jnp.sin(ang) * sign_ref[...]


def _rope_tables(positions):
    t = positions.size
    half = QK_ROPE // 2
    inv = 1.0 / (ROPE_THETA ** (jnp.arange(0, QK_ROPE, 2, dtype=F32) / QK_ROPE))
    pad = jnp.zeros((LANE - QK_ROPE,), F32)
    inv_l = jnp.concatenate([inv, inv, pad]).reshape(1, LANE)
    sign = jnp.concatenate([-jnp.ones((half,), F32), jnp.ones((half,), F32), pad]).reshape(1, LANE)
    out = jax.ShapeDtypeStruct((t, LANE), F32)
    return pl.pallas_call(
        _rope_kernel,
        out_shape=(out, out),
        grid=(t // ROW_TILE,),
        in_specs=[pl.BlockSpec((ROW_TILE, 1), lambda i: (i, 0)),
                  pl.BlockSpec((1, LANE), lambda i: (0, 0)),
                  pl.BlockSpec((1, LANE), lambda i: (0, 0))],
        out_specs=(pl.BlockSpec((ROW_TILE, LANE), lambda i: (i, 0)),) * 2,
        compiler_params=_cparams(("parallel",)),
        name="rope_tables",
    )(positions.reshape(t, 1), inv_l, sign)


def _head_qk_norm_rope(nope, rope, rope_sw, g_nope, g_rope, g_rope_sw, cs, sn, scale):
    ss = jnp.sum(nope * nope, axis=-1, keepdims=True) + jnp.sum(rope * rope, axis=-1, keepdims=True)
    r = lax.rsqrt(ss * (1.0 / B_QK_HEAD) + EPS) * scale
    return nope * r * g_nope, (rope * g_rope * cs + rope_sw * g_rope_sw * sn) * r


def _latent_kv_body(hn, wd_ref, ga_ref, wuk_ref, wuvt_ref, kg_ref, cs_ref, sn_ref, k_ref, vt_ref):
    z = _dot(hn, wd_ref[...])
    c_kv = z[:, :KV_LORA]
    rope, rope_sw = z[:, KV_LORA:KV_LORA + LANE], z[:, KV_LORA + LANE:]
    cn = _rms(c_kv, ga_ref[...]).astype(BF16)
    kv = _dot(cn, wuk_ref[...])
    vt = _dot_nt(wuvt_ref[...], cn)
    tm = vt.shape[1]
    ones_row = (lax.broadcasted_iota(I32, (VT_HEAD_ROWS - V_HEAD, tm), 0) == 0).astype(BF16)
    for h in range(B_HEADS):
        vt_ref[h * VT_HEAD_ROWS:h * VT_HEAD_ROWS + V_HEAD, :] = vt[h * V_HEAD:(h + 1) * V_HEAD].astype(BF16)
        vt_ref[h * VT_HEAD_ROWS + V_HEAD:(h + 1) * VT_HEAD_ROWS, :] = ones_row
    kg = kg_ref[...]
    for h in range(B_HEADS):
        kn, kr = _head_qk_norm_rope(kv[:, h * QK_NOPE:(h + 1) * QK_NOPE], rope, rope_sw,
                                    kg[:, :LANE], kg[:, LANE:2 * LANE], kg[:, 2 * LANE:],
                                    cs_ref[...], sn_ref[...], 1.0)
        k_ref[:, h * B_QK_PAD:h * B_QK_PAD + QK_NOPE] = kn.astype(BF16)
        k_ref[:, h * B_QK_PAD + QK_NOPE:(h + 1) * B_QK_PAD] = kr.astype(BF16)


def _rope_swap(w):
    half = QK_ROPE // 2
    return jnp.concatenate([w[..., half:], w[..., :half]], axis=-1)


def _pad_lanes(w, n=LANE):
    return jnp.pad(w, [(0, 0)] * (w.ndim - 1) + [(0, n - w.shape[-1])])


def _head_gain(g):
    g_rope = g[QK_NOPE:]
    return jnp.concatenate([g[:QK_NOPE], _pad_lanes(g_rope), _pad_lanes(_rope_swap(g_rope))]).reshape(1, -1)


def _query_body(hn, win_ref, ga_ref, wuq_ref, qg_ref, cs_ref, sn_ref, q_ref, mq_ref):
    proj = _dot(hn, win_ref[...])
    mq_ref[...] = proj[:, Q_LORA:]
    qall = _dot(_rms(proj[:, :Q_LORA], ga_ref[...]).astype(BF16), wuq_ref[...])
    qg = qg_ref[...]
    per_head = QK_NOPE + 2 * LANE
    for h in range(B_HEADS):
        base = h * per_head
        qn, qr = _head_qk_norm_rope(qall[:, base:base + QK_NOPE],
                                    qall[:, base + QK_NOPE:base + QK_NOPE + LANE],
                                    qall[:, base + QK_NOPE + LANE:base + per_head],
                                    qg[:, :LANE], qg[:, LANE:2 * LANE], qg[:, 2 * LANE:],
                                    cs_ref[...], sn_ref[...], B_QK_HEAD ** -0.5)
        q_ref[:, h * B_QK_PAD:h * B_QK_PAD + QK_NOPE] = qn.astype(BF16)
        q_ref[:, h * B_QK_PAD + QK_NOPE:(h + 1) * B_QK_PAD] = qr.astype(BF16)


def _mla_proj_kernel(x_ref, gkv_ref, wd_ref, ga_ref, wuk_ref, wuvt_ref, kg_ref,
                     gq_ref, win_ref, gqa_ref, wuq_ref, qg_ref, cs_ref, sn_ref,
                     k_ref, vt_ref, q_ref, mq_ref):
    x = x_ref[...]
    xn = x * lax.rsqrt(jnp.mean(x * x, axis=-1, keepdims=True) + EPS)
    _latent_kv_body((xn * gkv_ref[...]).astype(BF16), wd_ref, ga_ref, wuk_ref, wuvt_ref, kg_ref,
                    cs_ref, sn_ref, k_ref, vt_ref)
    _query_body((xn * gq_ref[...]).astype(BF16), win_ref, gqa_ref, wuq_ref, qg_ref, cs_ref, sn_ref, q_ref, mq_ref)


def _mla_projection(x2, kv_norm, w_dkv, kv_a_norm, w_ukv, k_head_norm,
                    q_norm, w_in, q_a_g, w_uq, q_head_g, cs, sn):
    t = x2.shape[0]
    w_rope = w_dkv[:, KV_LORA:]
    wd = jnp.concatenate([w_dkv[:, :KV_LORA], _pad_lanes(w_rope), _pad_lanes(_rope_swap(w_rope))],
                         axis=1).astype(BF16)
    wu = w_ukv.reshape(KV_LORA, B_HEADS, QK_NOPE + V_HEAD)
    wuk = wu[:, :, :QK_NOPE].reshape(KV_LORA, -1).astype(BF16)
    wuvt = wu[:, :, QK_NOPE:].reshape(KV_LORA, -1).T.astype(BF16)
    wq = w_uq.reshape(Q_LORA, B_HEADS, B_QK_HEAD)
    wq_rope = wq[:, :, QK_NOPE:]
    wq = jnp.concatenate([wq[:, :, :QK_NOPE], _pad_lanes(wq_rope), _pad_lanes(_rope_swap(wq_rope))],
                         axis=-1).reshape(Q_LORA, -1).astype(BF16)
    consts = (kv_norm.reshape(1, -1), wd, kv_a_norm.reshape(1, -1), wuk, wuvt, _head_gain(k_head_norm),
              q_norm.reshape(1, -1), w_in.astype(BF16), q_a_g.reshape(1, -1), wq, _head_gain(q_head_g))
    row = lambda n: pl.BlockSpec((ROW_TILE, n), lambda i: (i, 0))
    full = lambda a: pl.BlockSpec(a.shape, lambda i: (0,) * a.ndim)
    return pl.pallas_call(
        _mla_proj_kernel,
        out_shape=(jax.ShapeDtypeStruct((t, B_HEADS * B_QK_PAD), BF16),
                   jax.ShapeDtypeStruct((B_HEADS * VT_HEAD_ROWS, t), BF16),
                   jax.ShapeDtypeStruct((t, B_HEADS * B_QK_PAD), BF16),
                   jax.ShapeDtypeStruct((t, M_W), F32)),
        grid=(t // ROW_TILE,),
        in_specs=[row(D_MODEL)] + [full(a) for a in consts] + [row(LANE), row(LANE)],
        out_specs=(row(B_HEADS * B_QK_PAD), pl.BlockSpec((B_HEADS * VT_HEAD_ROWS, ROW_TILE), lambda i: (0, i)),
                   row(B_HEADS * B_QK_PAD), row(M_W)),
        compiler_params=_cparams(("parallel",)),
        name="mla_proj",
    )(x2, *consts, cs, sn)


def _attn_kernel(q_ref, k_ref, vt_ref, out_ref):
    tq, tk = ATT_Q_TILE, ATT_K_TILE
    i = pl.program_id(2)

    def block(h, j, carry, masked):
        m, acc = carry
        off = pl.multiple_of(j * tk, tk)
        q = q_ref[0, :, h * B_QK_PAD:(h + 1) * B_QK_PAD]
        st = _dot_nt(k_ref[0, pl.ds(off, tk), h * B_QK_PAD:(h + 1) * B_QK_PAD], q)
        if masked:
            key = lax.broadcasted_iota(I32, (tk, tq), 0)
            qry = lax.broadcasted_iota(I32, (tk, tq), 1)
            st = jnp.where(key <= qry, st, -jnp.inf)
        m_new = jnp.maximum(m, jnp.max(st, axis=0, keepdims=True))
        p = jnp.exp(st - m_new).astype(BF16)
        vt = vt_ref[h * VT_HEAD_ROWS:(h + 1) * VT_HEAD_ROWS, pl.ds(off, tk)]
        return m_new, jnp.exp(m - m_new) * acc + _dot(vt, p)

    heads = range(ATT_HEADS_PER_STEP)
    init = tuple((jnp.full((1, tq), -jnp.inf, F32), jnp.zeros((VT_HEAD_ROWS, tq), F32)) for _ in heads)
    carry = lax.fori_loop(0, i, lambda j, c: tuple(block(h, j, c[h], False) for h in heads), init)
    for h in heads:
        _, acc = block(h, i, carry[h], True)
        out_t = acc[:V_HEAD] / acc[V_HEAD:V_HEAD + 1]
        out_ref[0, :, h * V_HEAD:(h + 1) * V_HEAD] = out_t.T.astype(BF16)


def _causal_attention(q, k, vt, seq):
    nb = q.shape[0]
    g = ATT_HEADS_PER_STEP
    assert ATT_Q_TILE == ATT_K_TILE and B_HEADS % g == 0
    return pl.pallas_call(
        _attn_kernel,
        out_shape=jax.ShapeDtypeStruct((nb, seq, B_HEADS * V_HEAD), BF16),
        grid=(nb, B_HEADS // g, seq // ATT_Q_TILE),
        in_specs=[pl.BlockSpec((1, ATT_Q_TILE, g * B_QK_PAD), lambda b, h, i: (b, i, h)),
                  pl.BlockSpec((1, seq, g * B_QK_PAD), lambda b, h, i: (b, 0, h)),
                  pl.BlockSpec((g * VT_HEAD_ROWS, seq), lambda b, h, i: (h, b))],
        out_specs=pl.BlockSpec((1, ATT_Q_TILE, g * V_HEAD), lambda b, h, i: (b, i, h)),
        compiler_params=_cparams(("parallel", "parallel", "arbitrary")),
        name="causal_attention",
    )(q, k, vt)


def _router_kernel(x_ref, g_ref, wr_hi_ref, wr_lo_ref, hn_ref, idx_ref, gate_ref, count_ref):
    hn = _rms(x_ref[...], g_ref[...])
    hn_hi = hn.astype(BF16)
    hn_ref[...] = hn_hi
    hn_lo = (hn - hn_hi.astype(F32)).astype(BF16)
    logits = _dot(hn_hi, wr_hi_ref[...]) + _dot(hn_hi, wr_lo_ref[...]) + _dot(hn_lo, wr_hi_ref[...])
    lane = lax.broadcasted_iota(I32, logits.shape, 1)
    logits = jnp.where(lane < N_EXPERTS, logits, -jnp.inf)
    v1 = jnp.max(logits, axis=-1, keepdims=True)
    i1 = jnp.min(jnp.where(logits == v1, lane, LANE), axis=-1, keepdims=True)
    rest = jnp.where(lane == i1, -jnp.inf, logits)
    v2 = jnp.max(rest, axis=-1, keepdims=True)
    i2 = jnp.min(jnp.where(rest == v2, lane, LANE), axis=-1, keepdims=True)
    e2 = jnp.exp(v2 - v1)
    den = 1.0 + e2
    idx_ref[...] = jnp.where(lane == 0, i1, jnp.where(lane == 1, i2, 0))
    record = jnp.zeros(logits.shape, F32)
    for k, gate in enumerate((1.0 / den, e2 / den)):
        hi = gate.astype(BF16).astype(F32)
        mid = (gate - hi).astype(BF16).astype(F32)
        for part, term in enumerate((hi, mid, gate - hi - mid)):
            record = jnp.where(lane == GATE_TERMS * k + part, term, record)
    record = jnp.where(lane == 2 * GATE_TERMS, i1.astype(F32), record)
    record = jnp.where(lane == 2 * GATE_TERMS + 1, i2.astype(F32), record)
    gate_ref[...] = record.astype(BF16)
    pairs = (lane == i1).astype(F32) + (lane == i2).astype(F32)
    count_ref[...] = jnp.broadcast_to(jnp.sum(pairs, axis=0, keepdims=True), count_ref.shape)


def _router(x2, g, w_router):
    t = x2.shape[0]
    row = lambda n: pl.BlockSpec((RANK_TILE, n), lambda i: (i, 0))
    wr = _pad_lanes(w_router)
    wr_hi = wr.astype(BF16)
    wr_lo = (wr - wr_hi.astype(F32)).astype(BF16)
    return pl.pallas_call(
        _router_kernel,
        out_shape=(jax.ShapeDtypeStruct((t, D_MODEL), BF16), jax.ShapeDtypeStruct((t, LANE), I32),
                   jax.ShapeDtypeStruct((t, LANE), BF16), jax.ShapeDtypeStruct((8 * (t // RANK_TILE), LANE), F32)),
        grid=(t // RANK_TILE,),
        in_specs=[row(D_MODEL), pl.BlockSpec((1, D_MODEL), lambda i: (0, 0)),
                  pl.BlockSpec(wr.shape, lambda i: (0, 0)), pl.BlockSpec(wr.shape, lambda i: (0, 0))],
        out_specs=(row(D_MODEL), row(LANE), row(LANE), pl.BlockSpec((8, LANE), lambda i: (i, 0))),
        compiler_params=_cparams(("parallel",)),
        name="moe_router",
    )(x2, g.reshape(1, -1), wr_hi, wr_lo)


def _aligned(count):
    return jnp.ceil(count * (1.0 / RUN_ALIGN)) * RUN_ALIGN


def _rank_kernel(idx_ref, counts_ref, lpos_ref, lposr_ref, tile_ref, runs_ref, run_ref, start_ref):
    blk = pl.program_id(0)
    tb = RANK_TILE
    lane = lax.broadcasted_iota(I32, (tb, LANE), 1)
    idx = idx_ref[...]
    oh0 = (lane == idx[:, 0:1]).astype(F32)
    oh1 = (lane == idx[:, 1:2]).astype(F32)
    both = oh0 + oh1
    run_len = _aligned(jnp.sum(both, axis=0, keepdims=True))
    r = lax.broadcasted_iota(I32, (LANE, LANE), 0)
    c = lax.broadcasted_iota(I32, (LANE, LANE), 1)
    before = (r < c).astype(F32)

    @pl.when(blk == 0)
    def _():
        sizes = jnp.sum(_aligned(counts_ref[...]), axis=0, keepdims=True) * 0.125
        tiles = jnp.ceil(sizes * (1.0 / MOE_ROW_TILE))
        tile_start = jnp.dot(tiles, before, precision=HIGHEST, preferred_element_type=F32)
        start_ref[...] = tile_start * MOE_ROW_TILE
        tile_end = tile_start + tiles
        n_col = r.astype(F32)
        ended = ((n_col >= tile_end) & (c < N_EXPERTS)).astype(F32)
        expert = jnp.sum(ended, axis=-1, keepdims=True)
        total = jnp.max(tile_end, axis=-1, keepdims=True)
        as_column = lambda row: jnp.sum(jnp.where(r == c, row, 0.0), axis=-1, keepdims=True)
        tail_start = as_column(start_ref[...] + sizes)
        tail_len = as_column(tiles * MOE_ROW_TILE - sizes)
        mine = expert == c.astype(F32)
        group_rows = jnp.sum(jnp.where(mine, sizes, 0.0), axis=-1, keepdims=True)
        group_tile = jnp.sum(jnp.where(mine, tile_start, 0.0), axis=-1, keepdims=True)
        used_rows = jnp.clip(group_rows - (n_col[:, 0:1] - group_tile) * MOE_ROW_TILE, 0.0, MOE_ROW_TILE)
        col = lax.broadcasted_iota(I32, tile_ref.shape, 1)
        tile_ref[...] = jnp.where(col == 0, expert, jnp.where(col == 1, total, jnp.where(
            col == 2, tail_start, jnp.where(col == 3, tail_len, used_rows)))).astype(I32)
        run_ref[...] = jnp.zeros_like(run_ref)
        runs_ref[...] = jnp.zeros_like(runs_ref)

    rr = lax.broadcasted_iota(I32, (tb, tb), 0)
    cc = lax.broadcasted_iota(I32, (tb, tb), 1)
    strict = (cc < rr).astype(BF16)
    local = jnp.dot(run_len, before, precision=HIGHEST, preferred_element_type=F32)
    base = _dot(strict, both.astype(BF16)) + local
    p0 = jnp.sum(oh0 * base, axis=-1, keepdims=True)
    p1 = jnp.sum(oh1 * base, axis=-1, keepdims=True)
    posf = jnp.where(lane == 0, p0, jnp.where(lane == 1, p1, 0.0))
    lpos_ref[...] = posf.astype(I32)
    pick = (lax.broadcasted_iota(I32, (8, LANE), 0) == lax.broadcasted_iota(I32, (8, LANE), 1)).astype(F32)
    lposr_ref[...] = _dot_nt_highest(pick, posf).astype(I32)
    n = runs_ref.shape[0] // 3
    mine = lax.broadcasted_iota(I32, (n, LANE), 0) == blk
    for k, value in enumerate((local, run_ref[...] + start_ref[...], run_len)):
        runs_ref[k * n:(k + 1) * n, :] = jnp.where(mine, value.astype(I32), runs_ref[k * n:(k + 1) * n, :])
    run_ref[...] += run_len


def _rank(idx, counts):
    t = idx.shape[0]
    nblk = t // RANK_TILE
    const = lambda shape: pl.BlockSpec(shape, lambda i: (0, 0))
    return pl.pallas_call(
        _rank_kernel,
        out_shape=(jax.ShapeDtypeStruct((t, LANE), I32), jax.ShapeDtypeStruct((8, t), I32),
                   jax.ShapeDtypeStruct((LANE, 8), I32), jax.ShapeDtypeStruct((3 * nblk, LANE), I32)),
        grid=(nblk,),
        in_specs=[pl.BlockSpec((RANK_TILE, LANE), lambda i: (i, 0)), const(counts.shape)],
        out_specs=(pl.BlockSpec((RANK_TILE, LANE), lambda i: (i, 0)),
                   pl.BlockSpec((8, RANK_TILE), lambda i: (0, i)),
                   const((LANE, 8)), const((3 * nblk, LANE))),
        scratch_shapes=[pltpu.VMEM((1, LANE), F32), pltpu.VMEM((1, LANE), F32)],
        compiler_params=_cparams(("arbitrary",)),
        name="moe_rank",
    )(idx, counts)


def _for_each_piece(length, fn):
    for size in RUN_PIECES:
        @pl.when((length & size) != 0)
        def _(size=size):
            fn(pl.multiple_of(length & (-2 * size), RUN_ALIGN), size)


def _for_each_run_piece(blk, loc_ref, dst_ref, len_ref, fn):
    for e in range(N_EXPERTS):
        k = blk * N_EXPERTS + e
        loc, dst = loc_ref[k], dst_ref[k]
        _for_each_piece(len_ref[k], lambda done, size, loc=loc, dst=dst: fn(
            pl.multiple_of(loc + done, RUN_ALIGN), pl.multiple_of(dst + done, RUN_ALIGN), size))


def _dispatch_kernel(loc_ref, dst_ref, len_ref, tail_ref, tail_len_ref, ntiles_ref, hn_ref, gate_ref, lposr_ref,
                     xs_ref, local_ref, zero_ref, sems):
    blk = pl.program_id(0)

    @pl.when(blk == 0)
    def _():
        sem = sems.at[2]
        zero_ref[...] = jnp.zeros_like(zero_ref)

        def fill(e):
            start = tail_ref[e]
            return lambda done, size: pltpu.make_async_copy(
                zero_ref.at[pl.ds(0, size), :], xs_ref.at[pl.ds(pl.multiple_of(start + done, RUN_ALIGN), size), :], sem)

        def fill_tile(i):
            return pltpu.make_async_copy(
                zero_ref, xs_ref.at[pl.ds(pl.multiple_of(i * RANK_TILE, RANK_TILE), RANK_TILE), :], sem)

        def unused_tiles(action):
            def body(i, carry):
                action(fill_tile(i))
                return carry
            lax.fori_loop(ntiles_ref[0] * (MOE_ROW_TILE // RANK_TILE), xs_ref.shape[0] // RANK_TILE, body, 0)

        for e in range(N_EXPERTS):
            _for_each_piece(tail_len_ref[e], lambda *a, e=e: fill(e)(*a).start())
        unused_tiles(lambda copy: copy.start())
        for e in range(N_EXPERTS):
            _for_each_piece(tail_len_ref[e], lambda *a, e=e: fill(e)(*a).wait())
        unused_tiles(lambda copy: copy.wait())

    rows = lax.broadcasted_iota(I32, (LOCAL_ROWS, 1), 0)
    sel = jnp.where(lposr_ref[0:1, :] == rows, 1.0, jnp.where(lposr_ref[1:2, :] == rows, 1.0, 0.0)).astype(BF16)

    def step(local_ref, sem, other_ref, other_sem):
        local_ref[:, :D_MODEL] = _dot(sel, hn_ref[...]).astype(BF16)
        local_ref[:, D_MODEL:] = _dot(sel, gate_ref[...]).astype(BF16)

        def copy(buf, buf_sem):
            return lambda loc, dst, size: pltpu.make_async_copy(
                buf.at[pl.ds(loc, size), :], xs_ref.at[pl.ds(dst, size), :], buf_sem)

        _for_each_run_piece(blk, loc_ref, dst_ref, len_ref, lambda *a: copy(local_ref, sem)(*a).start())

        @pl.when(blk > 0)
        def _():
            _for_each_run_piece(blk - 1, loc_ref, dst_ref, len_ref, lambda *a: copy(other_ref, other_sem)(*a).wait())

        @pl.when(blk == pl.num_programs(0) - 1)
        def _():
            _for_each_run_piece(blk, loc_ref, dst_ref, len_ref, lambda *a: copy(local_ref, sem)(*a).wait())

    @pl.when(blk % 2 == 0)
    def _():
        step(local_ref.at[0], sems.at[0], local_ref.at[1], sems.at[1])

    @pl.when(blk % 2 == 1)
    def _():
        step(local_ref.at[1], sems.at[1], local_ref.at[0], sems.at[0])


def _dispatch(runs, tails, hn, gates, lpos_rows, rows):
    nblk = hn.shape[0] // RANK_TILE
    tok = lambda n: pl.BlockSpec((RANK_TILE, n), lambda b, *_: (b, 0))
    return pl.pallas_call(
        _dispatch_kernel,
        out_shape=jax.ShapeDtypeStruct((rows, XS_WIDTH), BF16),
        grid_spec=pltpu.PrefetchScalarGridSpec(
            num_scalar_prefetch=6,
            grid=(nblk,),
            in_specs=[tok(D_MODEL), tok(LANE), pl.BlockSpec((8, RANK_TILE), lambda b, *_: (0, b))],
            out_specs=pl.BlockSpec(memory_space=pl.ANY),
            scratch_shapes=[pltpu.VMEM((2, LOCAL_ROWS, XS_WIDTH), BF16), pltpu.VMEM((RANK_TILE, XS_WIDTH), BF16),
                            pltpu.SemaphoreType.DMA((3,))],
        ),
        compiler_params=_cparams(("arbitrary",)),
        name="moe_dispatch",
    )(*runs, *tails, hn, gates, lpos_rows)


def _moe_kernel(expert_ref, ntiles_ref, used_ref, x_ref, wg_ref, wu_ref, wd_ref, out_ref, acc_ref, h_ref):
    i, j = pl.program_id(0), pl.program_id(1)
    active = i < ntiles_ref[0]
    parts = -(-used_ref[i] // MOE_SKIP_ROWS)

    @pl.when(j == 0)
    def _():
        acc_ref[...] = jnp.zeros_like(acc_ref)

    for n in range(1, MOE_ROW_TILE // MOE_SKIP_ROWS + 1):
        @pl.when(active & (parts == n))
        def _(n=n):
            lead = pl.ds(0, n * MOE_SKIP_ROWS)
            _swiglu_accumulate(acc_ref.at[lead, :], h_ref.at[lead, :], x_ref[:n * MOE_SKIP_ROWS, :D_MODEL],
                               wg_ref.at[0], wu_ref.at[0], wd_ref.at[0])

    @pl.when(j == pl.num_programs(1) - 1)
    def _():
        gs = x_ref[:, D_MODEL:].astype(F32)
        gate = [sum(gs[:, GATE_TERMS * k + n:GATE_TERMS * k + n + 1] for n in range(GATE_TERMS)) for k in range(TOP_K)]
        first = gs[:, TOP_K * GATE_TERMS:TOP_K * GATE_TERMS + 1] == expert_ref[i].astype(F32)
        row_gate = jnp.where(first, gate[0], gate[1])
        out_ref[...] = jnp.where(active, acc_ref[...] * row_gate, 0.0).astype(BF16)


def _moe_experts(tile_expert, n_tiles, tile_used, xs, w_gate_up, w_down):
    rows = xs.shape[0]
    tm, tf = MOE_ROW_TILE, FFN_FF_TILE
    nj = D_FF // tf

    def x_map(i, j, e_ref, n_ref, u_ref):
        return jnp.clip(i, 0, jnp.maximum(n_ref[0] - 1, 0)), 0

    def w_idx(i, j, expert_ref, ntiles_ref):
        e = jnp.minimum(expert_ref[i], N_EXPERTS - 1)
        return e, jnp.where(i < ntiles_ref[0], j, nj - 1)

    def w_gate_map(i, j, e_ref, n_ref, u_ref):
        e, jj = w_idx(i, j, e_ref, n_ref)
        return e, 0, jj

    def w_up_map(i, j, e_ref, n_ref, u_ref):
        e, jj = w_idx(i, j, e_ref, n_ref)
        return e, 0, nj + jj

    def w_down_map(i, j, e_ref, n_ref, u_ref):
        e, jj = w_idx(i, j, e_ref, n_ref)
        return e, jj, 0

    return pl.pallas_call(
        _moe_kernel,
        out_shape=jax.ShapeDtypeStruct((rows, D_MODEL), BF16),
        grid_spec=pltpu.PrefetchScalarGridSpec(
            num_scalar_prefetch=3,
            grid=(rows // tm, nj),
            in_specs=[pl.BlockSpec((tm, XS_WIDTH), x_map),
                      pl.BlockSpec((1, D_MODEL, tf), w_gate_map),
                      pl.BlockSpec((1, D_MODEL, tf), w_up_map),
                      pl.BlockSpec((1, tf, D_MODEL), w_down_map)],
            out_specs=pl.BlockSpec((tm, D_MODEL), lambda i, j, *_: (i, 0)),
            scratch_shapes=[pltpu.VMEM((tm, D_MODEL), F32), pltpu.VMEM((tm, tf), BF16)],
        ),
        compiler_params=_cparams(("arbitrary", "arbitrary")),
        name="moe_experts",
    )(tile_expert, n_tiles, tile_used, xs, w_gate_up, w_gate_up, w_down)


def _combine_kernel(loc_ref, dst_ref, len_ref, x_ref, lpos_ref, ys_ref, out_ref, local_ref, sems):
    blk = pl.program_id(0)
    rows = lax.broadcasted_iota(I32, (1, LOCAL_ROWS), 1)
    lpos = lpos_ref[...]
    sel = jnp.where(lpos[:, 0:1] == rows, 1.0, jnp.where(lpos[:, 1:2] == rows, 1.0, 0.0)).astype(BF16)

    def fetch(block, buf, buf_sem, action):
        _for_each_run_piece(block, loc_ref, dst_ref, len_ref, lambda loc, dst, size: action(pltpu.make_async_copy(
            ys_ref.at[pl.ds(dst, size), :], buf.at[pl.ds(loc, size), :], buf_sem)))

    def start_fetch(block, buf, buf_sem):
        buf[...] = jnp.zeros_like(buf)
        fetch(block, buf, buf_sem, lambda copy: copy.start())

    def step(buf, buf_sem, other, other_sem):
        @pl.when(blk == 0)
        def _():
            start_fetch(blk, buf, buf_sem)

        @pl.when(blk + 1 < pl.num_programs(0))
        def _():
            start_fetch(blk + 1, other, other_sem)

        fetch(blk, buf, buf_sem, lambda copy: copy.wait())
        out_ref[...] = x_ref[...] + _dot(sel, buf[...])

    @pl.when(blk % 2 == 0)
    def _():
        step(local_ref.at[0], sems.at[0], local_ref.at[1], sems.at[1])

    @pl.when(blk % 2 == 1)
    def _():
        step(local_ref.at[1], sems.at[1], local_ref.at[0], sems.at[0])


def _combine(runs, x2, lpos, ys):
    tok = lambda n: pl.BlockSpec((RANK_TILE, n), lambda b, *_: (b, 0))
    return pl.pallas_call(
        _combine_kernel,
        out_shape=jax.ShapeDtypeStruct(x2.shape, F32),
        grid_spec=pltpu.PrefetchScalarGridSpec(
            num_scalar_prefetch=3,
            grid=(x2.shape[0] // RANK_TILE,),
            in_specs=[tok(D_MODEL), tok(LANE), pl.BlockSpec(memory_space=pl.ANY)],
            out_specs=tok(D_MODEL),
            scratch_shapes=[pltpu.VMEM((2, LOCAL_ROWS, D_MODEL), BF16), pltpu.SemaphoreType.DMA((2,))],
        ),
        compiler_params=_cparams(("arbitrary",)),
        name="moe_combine",
    )(*runs, x2, lpos, ys)


def _moe_ffn(x2, g, w_router, w_gate_up, w_down):
    t = x2.shape[0]
    nblk = t // RANK_TILE
    hn, idx, gates, counts = _router(x2, g, w_router)
    lpos, lpos_rows, tile_info, run_tab = _rank(idx, counts)
    runs = tuple(run_tab.reshape(3, nblk, LANE)[:, :, :N_EXPERTS].reshape(3, -1))
    rows = t * TOP_K + nblk * N_EXPERTS * (RUN_ALIGN - 1) + N_EXPERTS * (MOE_ROW_TILE - 1)
    rows = -(-rows // MOE_ROW_TILE) * MOE_ROW_TILE
    assert rows // MOE_ROW_TILE <= LANE
    assert MOE_ROW_TILE % RANK_TILE == 0 and rows % RANK_TILE == 0
    xs = _dispatch(runs, (tile_info[:, 2], tile_info[:, 3], tile_info[:1, 1]), hn, gates, lpos_rows, rows)
    ys = _moe_experts(tile_info[:, 0], tile_info[:1, 1], tile_info[:, 4], xs,
                      w_gate_up.astype(BF16), w_down.astype(BF16))
    return _combine(runs, x2, lpos, ys)


def _pad_heads(w, heads, dim, pad):
    w = w.reshape(w.shape[:-1] + (heads, dim))
    return _pad_lanes(w, pad).reshape(w.shape[:-2] + (heads * pad,))


def kernel(x, mem, positions, a_norm, a_w_in, a_gate_bias, a_head_norm, a_w_out, b_norm, b_w_in, b_q_a_norm, b_w_uq, b_q_head_norm, b_w_out, kv_norm, w_dkv, kv_a_norm, w_ukv, k_head_norm, mem_norm, mem_w_kv, mem_q_norm, mem_k_norm, ffn_norm, dense_w_gate_up, dense_w_down, moe_router, moe_w_gate_up, moe_w_down):
    nb, seq, _ = x.shape
    t = nb * seq
    x2 = x.reshape(t, D_MODEL)
    gmat = jnp.kron(jnp.eye(M_HEADS, dtype=F32), jnp.full((M_HEAD_DIM, M_HEAD_DIM), 1.0 / M_HEAD_DIM, F32)).astype(BF16)

    kbd0, vbd0 = _memory_kv(mem, mem_norm[0], mem_w_kv[0], mem_k_norm[0], gmat)
    w_in = a_w_in[0]
    qk_w, v_w = A_HEADS * A_QK_DIM, A_HEADS * A_V_DIM
    o0, o1, o2, o3, o4 = qk_w, 2 * qk_w, 2 * qk_w + v_w, 2 * qk_w + 2 * v_w, 2 * qk_w + 2 * v_w + 2 * A_HEADS
    w_main = jnp.concatenate([
        _pad_heads(w_in[:, :o0], A_HEADS, A_QK_DIM, A_QK_PAD),
        _pad_heads(w_in[:, o0:o1], A_HEADS, A_QK_DIM, A_QK_PAD),
        w_in[:, o4:]], axis=1).astype(BF16)
    w_vo_t = jnp.concatenate([
        _pad_heads(w_in[:, o1:o2], A_HEADS, A_V_DIM, A_V_PAD),
        _pad_heads(w_in[:, o2:o3], A_HEADS, A_V_DIM, A_V_PAD)], axis=1).T.astype(BF16)
    q, k, vt, ot, mq, gc, gr = _a_projection(x2, a_norm[0], w_main, w_vo_t, w_in[:, o3:o4], a_gate_bias[0])
    three = lambda a: a.reshape(nb, seq, a.shape[-1])
    head_g = _pad_heads(a_head_norm[0].reshape(1, -1), A_HEADS, A_V_DIM, A_V_PAD)
    hm = _mlstm(three(q), three(k), vt, ot, gc, gr,
                jnp.broadcast_to(head_g.reshape(-1, 1), (A_HEADS * A_V_PAD, MLSTM_CHUNK)))
    w_out = a_w_out[0]
    w_out_h = jnp.pad(w_out[:v_w].reshape(A_HEADS, A_V_DIM, D_MODEL), ((0, 0), (0, A_V_PAD - A_V_DIM), (0, 0)))
    w_out_h = w_out_h.reshape(A_HEADS * A_V_PAD, D_MODEL).astype(BF16)
    x2 = _mix_out(x2, hm.reshape(t, -1), mq, kbd0, vbd0, gmat, mem_q_norm[0],
                  w_out_h, w_out[v_w:].astype(BF16), seq)
    x2 = _dense_ffn(x2, ffn_norm[0], dense_w_gate_up[0].astype(BF16), dense_w_down[0].astype(BF16))

    cs, sn = _rope_tables(positions)
    k_sh, vt_sh, qh, mq1 = _mla_projection(x2, kv_norm, w_dkv, kv_a_norm, w_ukv, k_head_norm,
                                           b_norm[0], b_w_in[0], b_q_a_norm[0], b_w_uq[0], b_q_head_norm[0], cs, sn)

    kbd1, vbd1 = _memory_kv(mem, mem_norm[1], mem_w_kv[1], mem_k_norm[1], gmat)
    att = _causal_attention(three(qh), three(k_sh), vt_sh, seq)
    w_out = b_w_out[0]
    n_att = B_HEADS * V_HEAD
    x2 = _mix_out(x2, att.reshape(t, -1), mq1, kbd1, vbd1, gmat, mem_q_norm[1],
                  w_out[:n_att].astype(BF16), w_out[n_att:].astype(BF16), seq)
    x2 = _moe_ffn(x2, ffn_norm[1], moe_router[0], moe_w_gate_up[0], moe_w_down[0])
    return x2.reshape(nb, seq, D_MODEL)
```

```python
import functools

import jax
import jax.numpy as jnp
from jax import lax
from jax.experimental import pallas as pl
from jax.experimental.pallas import tpu as pltpu

F32 = jnp.float32
BF16 = jnp.bfloat16
I32 = jnp.int32

EPS = 1e-6
LOG2E = 1.4426950408889634
LANE = 128
VMEM_LIMIT = 48 * 1024 * 1024

D_MODEL = 1024
N_MEM = 256
M_HEADS, M_HEAD_DIM = 4, 64
M_W = M_HEADS * M_HEAD_DIM
A_HEADS, A_QK_DIM, A_V_DIM = 4, 96, 192
A_QK_PAD, A_V_PAD = 128, 256
A_V_ROWS = A_V_DIM + 16
B_HEADS, Q_LORA, KV_LORA = 6, 384, 256
QK_NOPE, QK_ROPE, V_HEAD = 128, 64, 128
B_QK_HEAD = QK_NOPE + QK_ROPE
B_QK_PAD = 256
VT_HEAD_ROWS = V_HEAD + 16
ROPE_THETA = 10000.0
D_FF = 3584
N_EXPERTS, TOP_K = 8, 2
GATE_TERMS = 3

MLSTM_CHUNK = 512
ROW_TILE = 1024
FFN_ROW_TILE = 1024
FFN_FF_TILE = 1792
FFN_CHUNK = 256
MOE_ROW_TILE = 1024
MOE_SKIP_ROWS = 256
ATT_Q_TILE = 512
ATT_K_TILE = 512
ATT_HEADS_PER_STEP = 6
RANK_TILE = 512
RUN_ALIGN = 16
RUN_PIECES = tuple(RANK_TILE >> s for s in range((RANK_TILE // RUN_ALIGN).bit_length()))
LOCAL_ROWS = TOP_K * RANK_TILE + N_EXPERTS * RUN_ALIGN
XS_WIDTH = D_MODEL + LANE

HIGHEST = lax.Precision.HIGHEST


def _cparams(sem):
    return pltpu.CompilerParams(dimension_semantics=sem, vmem_limit_bytes=VMEM_LIMIT)


def _rms(x, g):
    return x * lax.rsqrt(jnp.mean(x * x, axis=-1, keepdims=True) + EPS) * g


def _dot(a, b):
    return jnp.dot(a, b, preferred_element_type=F32)


def _dot_nt(a, b):
    return lax.dot_general(a, b, (((1,), (1,)), ((), ())), preferred_element_type=F32)


def _dot_tn(a, b):
    return lax.dot_general(a, b, (((0,), (0,)), ((), ())), preferred_element_type=F32)


def _group_mean_sq(x, gmat):
    sq = x * x
    hi = sq.astype(BF16)
    lo = (sq - hi.astype(F32)).astype(BF16)
    return _dot(hi, gmat) + _dot(lo, gmat)


def _memkv_kernel(mem_ref, g_ref, w_ref, kg_ref, gmat_ref, kbd_ref, vbd_ref):
    hn = _rms(mem_ref[0], g_ref[...]).astype(BF16)
    kv = _dot(hn, w_ref[...])
    k, v = kv[:, :M_W], kv[:, M_W:]
    kn = k * lax.rsqrt(_group_mean_sq(k, gmat_ref[...]) + EPS) * kg_ref[...]
    lane_head = lax.broadcasted_iota(I32, (1, M_W), 1) // M_HEAD_DIM
    for h in range(M_HEADS):
        keep = lane_head == h
        kbd_ref[0, h * N_MEM:(h + 1) * N_MEM, :] = jnp.where(keep, kn, 0.0).astype(BF16)
        vbd_ref[0, h * N_MEM:(h + 1) * N_MEM, :] = jnp.where(keep, v, 0.0).astype(BF16)


def _memory_kv(mem, g, w_kv, k_g, gmat):
    nb = mem.shape[0]
    out = jax.ShapeDtypeStruct((nb, M_HEADS * N_MEM, M_W), BF16)
    return pl.pallas_call(
        _memkv_kernel,
        out_shape=(out, out),
        grid=(nb,),
        in_specs=[
            pl.BlockSpec((1, N_MEM, D_MODEL), lambda b: (b, 0, 0)),
            pl.BlockSpec((1, D_MODEL), lambda b: (0, 0)),
            pl.BlockSpec((D_MODEL, 2 * M_W), lambda b: (0, 0)),
            pl.BlockSpec((1, M_W), lambda b: (0, 0)),
            pl.BlockSpec((M_W, M_W), lambda b: (0, 0)),
        ],
        out_specs=(pl.BlockSpec((1, M_HEADS * N_MEM, M_W), lambda b: (b, 0, 0)),) * 2,
        compiler_params=_cparams(("parallel",)),
        name="memory_kv",
    )(mem, g.reshape(1, -1), w_kv.astype(BF16), jnp.tile(k_g, M_HEADS).reshape(1, -1), gmat)


def _memory_attention(mq, kbd, vbd, gmat, qg):
    qn = mq * lax.rsqrt(_group_mean_sq(mq, gmat) + EPS) * (qg * (M_HEAD_DIM ** -0.5 * LOG2E))
    s = _dot_nt(qn.astype(BF16), kbd)
    ps = []
    for h in range(M_HEADS):
        sh = s[:, h * N_MEM:(h + 1) * N_MEM]
        e = jnp.exp2(sh - jnp.max(sh, axis=-1, keepdims=True))
        ps.append((e / jnp.sum(e, axis=-1, keepdims=True)).astype(BF16))
    return _dot(jnp.concatenate(ps, axis=-1), vbd)


def _a_proj_kernel(x_ref, g_ref, w_ref, wvot_ref, wif_ref, wift_ref, bc_ref, br_ref,
                   q_ref, k_ref, vt_ref, ot_ref, mq_ref, gc_ref, gr_ref):
    hn = _rms(x_ref[...], g_ref[...]).astype(BF16)
    nq = A_HEADS * A_QK_PAD
    nv = A_HEADS * A_V_ROWS
    q_ref[...] = _dot(hn, w_ref[:, :nq]).astype(BF16)
    k_ref[...] = (_dot(hn, w_ref[:, nq:2 * nq]) * (A_QK_DIM ** -0.5)).astype(BF16)
    mq_ref[...] = _dot(hn, w_ref[:, 2 * nq:])
    vt = _dot_nt(wvot_ref[:nv, :], hn)
    ones_row = lax.broadcasted_iota(I32, (nv, 1), 0) % A_V_ROWS == A_V_DIM
    vt_ref[...] = jnp.where(ones_row, 1.0, vt).astype(BF16)
    ot_ref[...] = _dot_nt(wvot_ref[nv:, :], hn)
    gc_ref[...] = _dot(hn, wif_ref[...])[:, :2 * A_HEADS] + bc_ref[...]
    gr_ref[...] = _dot_nt(wift_ref[...], hn) + br_ref[...]


def _a_projection(x2, g, w_main, w_vo_t, w_if, gate_bias):
    t = x2.shape[0]
    nq, nv = A_HEADS * A_QK_PAD, A_HEADS * A_V_ROWS
    ng = 2 * A_HEADS
    wif_pad = jnp.pad(w_if, ((0, 0), (0, LANE - ng))).astype(BF16)
    row = lambda n: pl.BlockSpec((ROW_TILE, n), lambda i: (i, 0))
    col = lambda n: pl.BlockSpec((n, ROW_TILE), lambda i: (0, i))
    full = lambda a: pl.BlockSpec(a.shape, lambda i: (0,) * a.ndim)
    args = (x2, g.reshape(1, -1), w_main, w_vo_t, wif_pad, w_if.T.astype(BF16),
            gate_bias.reshape(1, ng), gate_bias.reshape(ng, 1))
    return pl.pallas_call(
        _a_proj_kernel,
        out_shape=(jax.ShapeDtypeStruct((t, nq), BF16), jax.ShapeDtypeStruct((t, nq), BF16),
                   jax.ShapeDtypeStruct((nv, t), BF16), jax.ShapeDtypeStruct((nv, t), F32),
                   jax.ShapeDtypeStruct((t, M_W), F32), jax.ShapeDtypeStruct((t, ng), F32),
                   jax.ShapeDtypeStruct((ng, t), F32)),
        grid=(t // ROW_TILE,),
        in_specs=[row(D_MODEL)] + [full(a) for a in args[1:]],
        out_specs=(row(nq), row(nq), col(nv), col(nv), row(M_W), row(ng), col(ng)),
        compiler_params=_cparams(("parallel",)),
        name="mlstm_in_proj",
    )(*args)


def _log_sigmoid(f):
    return jnp.minimum(f, 0.0) - jnp.log(1.0 + jnp.exp(-jnp.abs(f)))


def _dot_nt_highest(a, b):
    return lax.dot_general(a, b, (((1,), (1,)), ((), ())), precision=HIGHEST,
                           preferred_element_type=F32)


def _chunk_gates(gc, gr):
    L = gc.shape[0]
    r = lax.broadcasted_iota(I32, (L, L), 0)
    c = lax.broadcasted_iota(I32, (L, L), 1)
    lower = (c <= r).astype(F32)
    is_f_col = lax.broadcasted_iota(I32, gc.shape, 1) >= A_HEADS
    is_f_row = lax.broadcasted_iota(I32, gr.shape, 0) >= A_HEADS
    lf_c = jnp.where(is_f_col, _log_sigmoid(gc), 0.0)
    lf_r = jnp.where(is_f_row, _log_sigmoid(gr), 0.0)
    lf_c = jnp.concatenate([lf_c, jnp.zeros((L, LANE - gc.shape[1]), F32)], axis=1)
    cum_c = jnp.dot(lower, lf_c, precision=HIGHEST, preferred_element_type=F32)[:, :gc.shape[1]]
    cum_r = _dot_nt_highest(lf_r, lower)
    return jnp.where(is_f_col, cum_c, gc), jnp.where(is_f_row, cum_r, gr)


def _prefix_max_lanes(x):
    lane = lax.broadcasted_iota(I32, x.shape, 1)
    shift = 1
    while shift < x.shape[1]:
        x = jnp.maximum(x, jnp.where(lane >= shift, pltpu.roll(x, shift, axis=1), -jnp.inf))
        shift *= 2
    return x


def _mlstm_kernel(q_ref, k_ref, vt_ref, ot_ref, gc_ref, gr_ref, hg_ref, out_ref, c_ref, m_ref):
    L = MLSTM_CHUNK

    @pl.when(pl.program_id(1) == 0)
    def _():
        c_ref[...] = jnp.zeros_like(c_ref)
        m_ref[...] = jnp.zeros_like(m_ref)

    src = lax.broadcasted_iota(I32, (L, L), 0)
    tgt = lax.broadcasted_iota(I32, (L, L), 1)
    real = lax.broadcasted_iota(I32, (A_V_ROWS, 1), 0) < A_V_DIM
    gcol, grow = _chunk_gates(gc_ref[...], gr_ref[...])
    for hd in range(A_HEADS):
        qk = slice(hd * A_QK_PAD, (hd + 1) * A_QK_PAD)
        vv = slice(hd * A_V_ROWS, (hd + 1) * A_V_ROWS)
        q, k, vt = q_ref[0, :, qk], k_ref[0, :, qk], vt_ref[vv, :]
        u_c = gcol[:, hd:hd + 1] - gcol[:, A_HEADS + hd:A_HEADS + hd + 1]
        g_r = grow[A_HEADS + hd:A_HEADS + hd + 1, :]
        u_r = grow[hd:hd + 1, :] - g_r
        g_last = g_r[:, L - 1:L]
        m_prev = m_ref[hd, 0:1, 0:1]
        c_prev = c_ref[hd]

        run_max = jnp.maximum(_prefix_max_lanes(jnp.broadcast_to(u_r, (8, L)))[0:1, :], m_prev)
        m_t = g_r + run_max
        inter = jnp.exp(m_prev - run_max)
        decay_t = jnp.where(src <= tgt, jnp.exp(u_c - run_max), 0.0)
        p_t = (decay_t * _dot_nt(k, q)).astype(BF16)
        num_t = inter * _dot_nt(c_prev.astype(BF16), q) + _dot(vt, p_t)
        den = num_t[A_V_DIM:A_V_DIM + 1, :]
        h_t = jnp.where(real, num_t / jnp.maximum(jnp.abs(den), jnp.exp(-m_t)), 0.0)
        scale = lax.rsqrt(jnp.sum(h_t * h_t, axis=0, keepdims=True) * (1.0 / A_V_DIM) + EPS)
        out_t = h_t * scale * hg_ref[vv, :] * jax.nn.sigmoid(ot_ref[vv, :])
        out_t = jnp.concatenate([out_t, jnp.zeros((A_V_PAD - A_V_ROWS, L), F32)], axis=0)
        out_ref[0, :, hd * A_V_PAD:(hd + 1) * A_V_PAD] = out_t.T.astype(BF16)

        w_r = g_last + u_r
        m_new = jnp.maximum(g_last + m_prev, jnp.max(w_r, axis=-1, keepdims=True))
        ev_t = (jnp.exp(w_r - m_new) * vt.astype(F32)).astype(BF16)
        c_ref[hd] = jnp.exp(g_last + m_prev - m_new) * c_prev + _dot(ev_t, k)
        m_ref[hd] = jnp.broadcast_to(m_new, m_ref.shape[1:])


def _mlstm(q, k, vt, ot, gcol, grow, head_g):
    nb, s, _ = q.shape
    L = MLSTM_CHUNK
    nv, nvt = A_HEADS * A_V_PAD, A_HEADS * A_V_ROWS
    blk = lambda w: pl.BlockSpec((1, L, w), lambda b, c: (b, c, 0))
    col = lambda n: pl.BlockSpec((n, L), lambda b, c: (0, b * (s // L) + c))
    return pl.pallas_call(
        _mlstm_kernel,
        out_shape=jax.ShapeDtypeStruct((nb, s, nv), BF16),
        grid=(nb, s // L),
        in_specs=[blk(A_HEADS * A_QK_PAD), blk(A_HEADS * A_QK_PAD), col(nvt), col(nvt),
                  pl.BlockSpec((L, 2 * A_HEADS), lambda b, c: (b * (s // L) + c, 0)), col(2 * A_HEADS),
                  pl.BlockSpec((nvt, L), lambda b, c: (0, 0))],
        out_specs=blk(nv),
        scratch_shapes=[pltpu.VMEM((A_HEADS, A_V_ROWS, A_QK_PAD), F32), pltpu.VMEM((A_HEADS, 8, LANE), F32)],
        compiler_params=_cparams(("parallel", "arbitrary")),
        name="mlstm_chunkwise",
    )(q, k, vt, ot, gcol, grow, head_g)


def _mix_out_kernel(x_ref, h_ref, mq_ref, kbd_ref, vbd_ref, gmat_ref, qg_ref, w1_ref, w2_ref, out_ref):
    mo = _memory_attention(mq_ref[...], kbd_ref[0], vbd_ref[0], gmat_ref[...], qg_ref[...])
    out_ref[...] = x_ref[...] + _dot(h_ref[...], w1_ref[...]) + _dot(mo.astype(BF16), w2_ref[...])


def _mix_out(x2, h2, mq, kbd, vbd, gmat, qg, w_main, w_mem, seq):
    t = x2.shape[0]
    tm = ROW_TILE
    row = lambda n: pl.BlockSpec((tm, n), lambda i: (i, 0))
    full = lambda a: pl.BlockSpec(a.shape, lambda i: (0,) * a.ndim)
    per_batch = pl.BlockSpec((1,) + kbd.shape[1:], lambda i: ((i * tm) // seq, 0, 0))
    qg_t = jnp.tile(qg, M_HEADS).reshape(1, -1)
    return pl.pallas_call(
        _mix_out_kernel,
        out_shape=jax.ShapeDtypeStruct((t, D_MODEL), F32),
        grid=(t // tm,),
        in_specs=[row(D_MODEL), row(h2.shape[1]), row(M_W), per_batch, per_batch,
                  full(gmat), full(qg_t), full(w_main), full(w_mem)],
        out_specs=row(D_MODEL),
        compiler_params=_cparams(("parallel",)),
        name="mixer_out_proj",
    )(x2, h2, mq, kbd, vbd, gmat, qg_t, w_main, w_mem)


def _swiglu_accumulate(acc_ref, h_ref, x, wg_ref, wu_ref, wd_ref):
    for c in range(wg_ref.shape[-1] // FFN_CHUNK):
        cols = slice(c * FFN_CHUNK, (c + 1) * FFN_CHUNK)
        gate = _dot(x, wg_ref[:, cols])
        up = _dot(x, wu_ref[:, cols])
        h_ref[:, cols] = (jax.nn.silu(gate) * up).astype(BF16)
    acc_ref[...] += _dot(h_ref[...], wd_ref[...])


def _ffn_kernel(x_ref, g_ref, wg_ref, wu_ref, wd_ref, out_ref, hn_ref, acc_ref, h_ref):
    j = pl.program_id(1)

    @pl.when(j == 0)
    def _():
        hn_ref[...] = _rms(x_ref[...], g_ref[...]).astype(BF16)
        acc_ref[...] = jnp.zeros_like(acc_ref)

    _swiglu_accumulate(acc_ref, h_ref, hn_ref[...], wg_ref, wu_ref, wd_ref)

    @pl.when(j == pl.num_programs(1) - 1)
    def _():
        out_ref[...] = x_ref[...] + acc_ref[...]


def _dense_ffn(x2, g, w_gate_up, w_down):
    t = x2.shape[0]
    tm, tf = FFN_ROW_TILE, FFN_FF_TILE
    nj = D_FF // tf
    return pl.pallas_call(
        _ffn_kernel,
        out_shape=jax.ShapeDtypeStruct((t, D_MODEL), F32),
        grid=(t // tm, nj),
        in_specs=[pl.BlockSpec((tm, D_MODEL), lambda i, j: (i, 0)),
                  pl.BlockSpec((1, D_MODEL), lambda i, j: (0, 0)),
                  pl.BlockSpec((D_MODEL, tf), lambda i, j: (0, j)),
                  pl.BlockSpec((D_MODEL, tf), lambda i, j: (0, nj + j)),
                  pl.BlockSpec((tf, D_MODEL), lambda i, j: (j, 0))],
        out_specs=pl.BlockSpec((tm, D_MODEL), lambda i, j: (i, 0)),
        scratch_shapes=[pltpu.VMEM((tm, D_MODEL), BF16), pltpu.VMEM((tm, D_MODEL), F32),
                        pltpu.VMEM((tm, tf), BF16)],
        compiler_params=_cparams(("parallel", "arbitrary")),
        name="dense_swiglu",
    )(x2, g.reshape(1, -1), w_gate_up, w_gate_up, w_down)


def _rope_kernel(pos_ref, inv_ref, sign_ref, cs_ref, sn_ref):
    ang = pos_ref[...].astype(F32) * inv_ref[...]
    cs_ref[...] = jnp.cos(ang)
    sn_ref[...] = jnp.sin(ang) * sign_ref[...]


def _rope_tables(positions):
    t = positions.size
    half = QK_ROPE // 2
    inv = 1.0 / (ROPE_THETA ** (jnp.arange(0, QK_ROPE, 2, dtype=F32) / QK_ROPE))
    pad = jnp.zeros((LANE - QK_ROPE,), F32)
    inv_l = jnp.concatenate([inv, inv, pad]).reshape(1, LANE)
    sign = jnp.concatenate([-jnp.ones((half,), F32), jnp.ones((half,), F32), pad]).reshape(1, LANE)
    out = jax.ShapeDtypeStruct((t, LANE), F32)
    return pl.pallas_call(
        _rope_kernel,
        out_shape=(out, out),
        grid=(t // ROW_TILE,),
        in_specs=[pl.BlockSpec((ROW_TILE, 1), lambda i: (i, 0)),
                  pl.BlockSpec((1, LANE), lambda i: (0, 0)),
                  pl.BlockSpec((1, LANE), lambda i: (0, 0))],
        out_specs=(pl.BlockSpec((ROW_TILE, LANE), lambda i: (i, 0)),) * 2,
        compiler_params=_cparams(("parallel",)),
        name="rope_tables",
    )(positions.reshape(t, 1), inv_l, sign)


def _head_qk_norm_rope(nope, rope, rope_sw, g_nope, g_rope, g_rope_sw, cs, sn, scale):
    ss = jnp.sum(nope * nope, axis=-1, keepdims=True) + jnp.sum(rope * rope, axis=-1, keepdims=True)
    r = lax.rsqrt(ss * (1.0 / B_QK_HEAD) + EPS) * scale
    return nope * r * g_nope, (rope * g_rope * cs + rope_sw * g_rope_sw * sn) * r


def _latent_kv_body(hn, wd_ref, ga_ref, wuk_ref, wuvt_ref, kg_ref, cs_ref, sn_ref, k_ref, vt_ref):
    z = _dot(hn, wd_ref[...])
    c_kv = z[:, :KV_LORA]
    rope, rope_sw = z[:, KV_LORA:KV_LORA + LANE], z[:, KV_LORA + LANE:]
    cn = _rms(c_kv, ga_ref[...]).astype(BF16)
    kv = _dot(cn, wuk_ref[...])
    vt = _dot_nt(wuvt_ref[...], cn)
    tm = vt.shape[1]
    ones_row = (lax.broadcasted_iota(I32, (VT_HEAD_ROWS - V_HEAD, tm), 0) == 0).astype(BF16)
    for h in range(B_HEADS):
        vt_ref[h * VT_HEAD_ROWS:h * VT_HEAD_ROWS + V_HEAD, :] = vt[h * V_HEAD:(h + 1) * V_HEAD].astype(BF16)
        vt_ref[h * VT_HEAD_ROWS + V_HEAD:(h + 1) * VT_HEAD_ROWS, :] = ones_row
    kg = kg_ref[...]
    for h in range(B_HEADS):
        kn, kr = _head_qk_norm_rope(kv[:, h * QK_NOPE:(h + 1) * QK_NOPE], rope, rope_sw,
                                    kg[:, :LANE], kg[:, LANE:2 * LANE], kg[:, 2 * LANE:],
                                    cs_ref[...], sn_ref[...], 1.0)
        k_ref[:, h * B_QK_PAD:h * B_QK_PAD + QK_NOPE] = kn.astype(BF16)
        k_ref[:, h * B_QK_PAD + QK_NOPE:(h + 1) * B_QK_PAD] = kr.astype(BF16)


def _rope_swap(w):
    half = QK_ROPE // 2
    return jnp.concatenate([w[..., half:], w[..., :half]], axis=-1)


def _pad_lanes(w, n=LANE):
    return jnp.pad(w, [(0, 0)] * (w.ndim - 1) + [(0, n - w.shape[-1])])


def _head_gain(g):
    g_rope = g[QK_NOPE:]
    return jnp.concatenate([g[:QK_NOPE], _pad_lanes(g_rope), _pad_lanes(_rope_swap(g_rope))]).reshape(1, -1)


def _query_body(hn, win_ref, ga_ref, wuq_ref, qg_ref, cs_ref, sn_ref, q_ref, mq_ref):
    proj = _dot(hn, win_ref[...])
    mq_ref[...] = proj[:, Q_LORA:]
    qall = _dot(_rms(proj[:, :Q_LORA], ga_ref[...]).astype(BF16), wuq_ref[...])
    qg = qg_ref[...]
    per_head = QK_NOPE + 2 * LANE
    for h in range(B_HEADS):
        base = h * per_head
        qn, qr = _head_qk_norm_rope(qall[:, base:base + QK_NOPE],
                                    qall[:, base + QK_NOPE:base + QK_NOPE + LANE],
                                    qall[:, base + QK_NOPE + LANE:base + per_head],
                                    qg[:, :LANE], qg[:, LANE:2 * LANE], qg[:, 2 * LANE:],
                                    cs_ref[...], sn_ref[...], B_QK_HEAD ** -0.5)
        q_ref[:, h * B_QK_PAD:h * B_QK_PAD + QK_NOPE] = qn.astype(BF16)
        q_ref[:, h * B_QK_PAD + QK_NOPE:(h + 1) * B_QK_PAD] = qr.astype(BF16)


def _mla_proj_kernel(x_ref, gkv_ref, wd_ref, ga_ref, wuk_ref, wuvt_ref, kg_ref,
                     gq_ref, win_ref, gqa_ref, wuq_ref, qg_ref, cs_ref, sn_ref,
                     k_ref, vt_ref, q_ref, mq_ref):
    x = x_ref[...]
    xn = x * lax.rsqrt(jnp.mean(x * x, axis=-1, keepdims=True) + EPS)
    _latent_kv_body((xn * gkv_ref[...]).astype(BF16), wd_ref, ga_ref, wuk_ref, wuvt_ref, kg_ref,
                    cs_ref, sn_ref, k_ref, vt_ref)
    _query_body((xn * gq_ref[...]).astype(BF16), win_ref, gqa_ref, wuq_ref, qg_ref, cs_ref, sn_ref, q_ref, mq_ref)


def _mla_projection(x2, kv_norm, w_dkv, kv_a_norm, w_ukv, k_head_norm,
                    q_norm, w_in, q_a_g, w_uq, q_head_g, cs, sn):
    t = x2.shape[0]
    w_rope = w_dkv[:, KV_LORA:]
    wd = jnp.concatenate([w_dkv[:, :KV_LORA], _pad_lanes(w_rope), _pad_lanes(_rope_swap(w_rope))],
                         axis=1).astype(BF16)
    wu = w_ukv.reshape(KV_LORA, B_HEADS, QK_NOPE + V_HEAD)
    wuk = wu[:, :, :QK_NOPE].reshape(KV_LORA, -1).astype(BF16)
    wuvt = wu[:, :, QK_NOPE:].reshape(KV_LORA, -1).T.astype(BF16)
    wq = w_uq.reshape(Q_LORA, B_HEADS, B_QK_HEAD)
    wq_rope = wq[:, :, QK_NOPE:]
    wq = jnp.concatenate([wq[:, :, :QK_NOPE], _pad_lanes(wq_rope), _pad_lanes(_rope_swap(wq_rope))],
                         axis=-1).reshape(Q_LORA, -1).astype(BF16)
    consts = (kv_norm.reshape(1, -1), wd, kv_a_norm.reshape(1, -1), wuk, wuvt, _head_gain(k_head_norm),
              q_norm.reshape(1, -1), w_in.astype(BF16), q_a_g.reshape(1, -1), wq, _head_gain(q_head_g))
    row = lambda n: pl.BlockSpec((ROW_TILE, n), lambda i: (i, 0))
    full = lambda a: pl.BlockSpec(a.shape, lambda i: (0,) * a.ndim)
    return pl.pallas_call(
        _mla_proj_kernel,
        out_shape=(jax.ShapeDtypeStruct((t, B_HEADS * B_QK_PAD), BF16),
                   jax.ShapeDtypeStruct((B_HEADS * VT_HEAD_ROWS, t), BF16),
                   jax.ShapeDtypeStruct((t, B_HEADS * B_QK_PAD), BF16),
                   jax.ShapeDtypeStruct((t, M_W), F32)),
        grid=(t // ROW_TILE,),
        in_specs=[row(D_MODEL)] + [full(a) for a in consts] + [row(LANE), row(LANE)],
        out_specs=(row(B_HEADS * B_QK_PAD), pl.BlockSpec((B_HEADS * VT_HEAD_ROWS, ROW_TILE), lambda i: (0, i)),
                   row(B_HEADS * B_QK_PAD), row(M_W)),
        compiler_params=_cparams(("parallel",)),
        name="mla_proj",
    )(x2, *consts, cs, sn)


def _attn_kernel(q_ref, k_ref, vt_ref, out_ref):
    tq, tk = ATT_Q_TILE, ATT_K_TILE
    i = pl.program_id(2)

    def block(h, j, carry, masked):
        m, acc = carry
        off = pl.multiple_of(j * tk, tk)
        q = q_ref[0, :, h * B_QK_PAD:(h + 1) * B_QK_PAD]
        st = _dot_nt(k_ref[0, pl.ds(off, tk), h * B_QK_PAD:(h + 1) * B_QK_PAD], q)
        if masked:
            key = lax.broadcasted_iota(I32, (tk, tq), 0)
            qry = lax.broadcasted_iota(I32, (tk, tq), 1)
            st = jnp.where(key <= qry, st, -jnp.inf)
        m_new = jnp.maximum(m, jnp.max(st, axis=0, keepdims=True))
        p = jnp.exp(st - m_new).astype(BF16)
        vt = vt_ref[h * VT_HEAD_ROWS:(h + 1) * VT_HEAD_ROWS, pl.ds(off, tk)]
        return m_new, jnp.exp(m - m_new) * acc + _dot(vt, p)

    heads = range(ATT_HEADS_PER_STEP)
    init = tuple((jnp.full((1, tq), -jnp.inf, F32), jnp.zeros((VT_HEAD_ROWS, tq), F32)) for _ in heads)
    carry = lax.fori_loop(0, i, lambda j, c: tuple(block(h, j, c[h], False) for h in heads), init)
    for h in heads:
        _, acc = block(h, i, carry[h], True)
        out_t = acc[:V_HEAD] / acc[V_HEAD:V_HEAD + 1]
        out_ref[0, :, h * V_HEAD:(h + 1) * V_HEAD] = out_t.T.astype(BF16)


def _causal_attention(q, k, vt, seq):
    nb = q.shape[0]
    g = ATT_HEADS_PER_STEP
    assert ATT_Q_TILE == ATT_K_TILE and B_HEADS % g == 0
    return pl.pallas_call(
        _attn_kernel,
        out_shape=jax.ShapeDtypeStruct((nb, seq, B_HEADS * V_HEAD), BF16),
        grid=(nb, B_HEADS // g, seq // ATT_Q_TILE),
        in_specs=[pl.BlockSpec((1, ATT_Q_TILE, g * B_QK_PAD), lambda b, h, i: (b, i, h)),
                  pl.BlockSpec((1, seq, g * B_QK_PAD), lambda b, h, i: (b, 0, h)),
                  pl.BlockSpec((g * VT_HEAD_ROWS, seq), lambda b, h, i: (h, b))],
        out_specs=pl.BlockSpec((1, ATT_Q_TILE, g * V_HEAD), lambda b, h, i: (b, i, h)),
        compiler_params=_cparams(("parallel", "parallel", "arbitrary")),
        name="causal_attention",
    )(q, k, vt)


def _router_kernel(x_ref, g_ref, wr_hi_ref, wr_lo_ref, hn_ref, idx_ref, gate_ref, count_ref):
    hn = _rms(x_ref[...], g_ref[...])
    hn_hi = hn.astype(BF16)
    hn_ref[...] = hn_hi
    hn_lo = (hn - hn_hi.astype(F32)).astype(BF16)
    logits = _dot(hn_hi, wr_hi_ref[...]) + _dot(hn_hi, wr_lo_ref[...]) + _dot(hn_lo, wr_hi_ref[...])
    lane = lax.broadcasted_iota(I32, logits.shape, 1)
    logits = jnp.where(lane < N_EXPERTS, logits, -jnp.inf)
    v1 = jnp.max(logits, axis=-1, keepdims=True)
    i1 = jnp.min(jnp.where(logits == v1, lane, LANE), axis=-1, keepdims=True)
    rest = jnp.where(lane == i1, -jnp.inf, logits)
    v2 = jnp.max(rest, axis=-1, keepdims=True)
    i2 = jnp.min(jnp.where(rest == v2, lane, LANE), axis=-1, keepdims=True)
    e2 = jnp.exp(v2 - v1)
    den = 1.0 + e2
    idx_ref[...] = jnp.where(lane == 0, i1, jnp.where(lane == 1, i2, 0))
    record = jnp.zeros(logits.shape, F32)
    for k, gate in enumerate((1.0 / den, e2 / den)):
        hi = gate.astype(BF16).astype(F32)
        mid = (gate - hi).astype(BF16).astype(F32)
        for part, term in enumerate((hi, mid, gate - hi - mid)):
            record = jnp.where(lane == GATE_TERMS * k + part, term, record)
    record = jnp.where(lane == 2 * GATE_TERMS, i1.astype(F32), record)
    record = jnp.where(lane == 2 * GATE_TERMS + 1, i2.astype(F32), record)
    gate_ref[...] = record.astype(BF16)
    pairs = (lane == i1).astype(F32) + (lane == i2).astype(F32)
    count_ref[...] = jnp.broadcast_to(jnp.sum(pairs, axis=0, keepdims=True), count_ref.shape)


def _router(x2, g, w_router):
    t = x2.shape[0]
    row = lambda n: pl.BlockSpec((RANK_TILE, n), lambda i: (i, 0))
    wr = _pad_lanes(w_router)
    wr_hi = wr.astype(BF16)
    wr_lo = (wr - wr_hi.astype(F32)).astype(BF16)
    return pl.pallas_call(
        _router_kernel,
        out_shape=(jax.ShapeDtypeStruct((t, D_MODEL), BF16), jax.ShapeDtypeStruct((t, LANE), I32),
                   jax.ShapeDtypeStruct((t, LANE), BF16), jax.ShapeDtypeStruct((8 * (t // RANK_TILE), LANE), F32)),
        grid=(t // RANK_TILE,),
        in_specs=[row(D_MODEL), pl.BlockSpec((1, D_MODEL), lambda i: (0, 0)),
                  pl.BlockSpec(wr.shape, lambda i: (0, 0)), pl.BlockSpec(wr.shape, lambda i: (0, 0))],
        out_specs=(row(D_MODEL), row(LANE), row(LANE), pl.BlockSpec((8, LANE), lambda i: (i, 0))),
        compiler_params=_cparams(("parallel",)),
        name="moe_router",
    )(x2, g.reshape(1, -1), wr_hi, wr_lo)


def _aligned(count):
    return jnp.ceil(count * (1.0 / RUN_ALIGN)) * RUN_ALIGN


def _rank_kernel(idx_ref, counts_ref, lpos_ref, lposr_ref, tile_ref, runs_ref, run_ref, start_ref):
    blk = pl.program_id(0)
    tb = RANK_TILE
    lane = lax.broadcasted_iota(I32, (tb, LANE), 1)
    idx = idx_ref[...]
    oh0 = (lane == idx[:, 0:1]).astype(F32)
    oh1 = (lane == idx[:, 1:2]).astype(F32)
    both = oh0 + oh1
    run_len = _aligned(jnp.sum(both, axis=0, keepdims=True))
    r = lax.broadcasted_iota(I32, (LANE, LANE), 0)
    c = lax.broadcasted_iota(I32, (LANE, LANE), 1)
    before = (r < c).astype(F32)

    @pl.when(blk == 0)
    def _():
        sizes = jnp.sum(_aligned(counts_ref[...]), axis=0, keepdims=True) * 0.125
        tiles = jnp.ceil(sizes * (1.0 / MOE_ROW_TILE))
        tile_start = jnp.dot(tiles, before, precision=HIGHEST, preferred_element_type=F32)
        start_ref[...] = tile_start * MOE_ROW_TILE
        tile_end = tile_start + tiles
        n_col = r.astype(F32)
        ended = ((n_col >= tile_end) & (c < N_EXPERTS)).astype(F32)
        expert = jnp.sum(ended, axis=-1, keepdims=True)
        total = jnp.max(tile_end, axis=-1, keepdims=True)
        as_column = lambda row: jnp.sum(jnp.where(r == c, row, 0.0), axis=-1, keepdims=True)
        tail_start = as_column(start_ref[...] + sizes)
        tail_len = as_column(tiles * MOE_ROW_TILE - sizes)
        mine = expert == c.astype(F32)
        group_rows = jnp.sum(jnp.where(mine, sizes, 0.0), axis=-1, keepdims=True)
        group_tile = jnp.sum(jnp.where(mine, tile_start, 0.0), axis=-1, keepdims=True)
        used_rows = jnp.clip(group_rows - (n_col[:, 0:1] - group_tile) * MOE_ROW_TILE, 0.0, MOE_ROW_TILE)
        col = lax.broadcasted_iota(I32, tile_ref.shape, 1)
        tile_ref[...] = jnp.where(col == 0, expert, jnp.where(col == 1, total, jnp.where(
            col == 2, tail_start, jnp.where(col == 3, tail_len, used_rows)))).astype(I32)
        run_ref[...] = jnp.zeros_like(run_ref)
        runs_ref[...] = jnp.zeros_like(runs_ref)

    rr = lax.broadcasted_iota(I32, (tb, tb), 0)
    cc = lax.broadcasted_iota(I32, (tb, tb), 1)
    strict = (cc < rr).astype(BF16)
    local = jnp.dot(run_len, before, precision=HIGHEST, preferred_element_type=F32)
    base = _dot(strict, both.astype(BF16)) + local
    p0 = jnp.sum(oh0 * base, axis=-1, keepdims=True)
    p1 = jnp.sum(oh1 * base, axis=-1, keepdims=True)
    posf = jnp.where(lane == 0, p0, jnp.where(lane == 1, p1, 0.0))
    lpos_ref[...] = posf.astype(I32)
    pick = (lax.broadcasted_iota(I32, (8, LANE), 0) == lax.broadcasted_iota(I32, (8, LANE), 1)).astype(F32)
    lposr_ref[...] = _dot_nt_highest(pick, posf).astype(I32)
    n = runs_ref.shape[0] // 3
    mine = lax.broadcasted_iota(I32, (n, LANE), 0) == blk
    for k, value in enumerate((local, run_ref[...] + start_ref[...], run_len)):
        runs_ref[k * n:(k + 1) * n, :] = jnp.where(mine, value.astype(I32), runs_ref[k * n:(k + 1) * n, :])
    run_ref[...] += run_len


def _rank(idx, counts):
    t = idx.shape[0]
    nblk = t // RANK_TILE
    const = lambda shape: pl.BlockSpec(shape, lambda i: (0, 0))
    return pl.pallas_call(
        _rank_kernel,
        out_shape=(jax.ShapeDtypeStruct((t, LANE), I32), jax.ShapeDtypeStruct((8, t), I32),
                   jax.ShapeDtypeStruct((LANE, 8), I32), jax.ShapeDtypeStruct((3 * nblk, LANE), I32)),
        grid=(nblk,),
        in_specs=[pl.BlockSpec((RANK_TILE, LANE), lambda i: (i, 0)), const(counts.shape)],
        out_specs=(pl.BlockSpec((RANK_TILE, LANE), lambda i: (i, 0)),
                   pl.BlockSpec((8, RANK_TILE), lambda i: (0, i)),
                   const((LANE, 8)), const((3 * nblk, LANE))),
        scratch_shapes=[pltpu.VMEM((1, LANE), F32), pltpu.VMEM((1, LANE), F32)],
        compiler_params=_cparams(("arbitrary",)),
        name="moe_rank",
    )(idx, counts)


def _for_each_piece(length, fn):
    for size in RUN_PIECES:
        @pl.when((length & size) != 0)
        def _(size=size):
            fn(pl.multiple_of(length & (-2 * size), RUN_ALIGN), size)


def _for_each_run_piece(blk, loc_ref, dst_ref, len_ref, fn):
    for e in range(N_EXPERTS):
        k = blk * N_EXPERTS + e
        loc, dst = loc_ref[k], dst_ref[k]
        _for_each_piece(len_ref[k], lambda done, size, loc=loc, dst=dst: fn(
            pl.multiple_of(loc + done, RUN_ALIGN), pl.multiple_of(dst + done, RUN_ALIGN), size))


def _dispatch_kernel(loc_ref, dst_ref, len_ref, tail_ref, tail_len_ref, ntiles_ref, hn_ref, gate_ref, lposr_ref,
                     xs_ref, local_ref, zero_ref, sems):
    blk = pl.program_id(0)

    @pl.when(blk == 0)
    def _():
        sem = sems.at[2]
        zero_ref[...] = jnp.zeros_like(zero_ref)

        def fill(e):
            start = tail_ref[e]
            return lambda done, size: pltpu.make_async_copy(
                zero_ref.at[pl.ds(0, size), :], xs_ref.at[pl.ds(pl.multiple_of(start + done, RUN_ALIGN), size), :], sem)

        def fill_tile(i):
            return pltpu.make_async_copy(
                zero_ref, xs_ref.at[pl.ds(pl.multiple_of(i * RANK_TILE, RANK_TILE), RANK_TILE), :], sem)

        def unused_tiles(action):
            def body(i, carry):
                action(fill_tile(i))
                return carry
            lax.fori_loop(ntiles_ref[0] * (MOE_ROW_TILE // RANK_TILE), xs_ref.shape[0] // RANK_TILE, body, 0)

        for e in range(N_EXPERTS):
            _for_each_piece(tail_len_ref[e], lambda *a, e=e: fill(e)(*a).start())
        unused_tiles(lambda copy: copy.start())
        for e in range(N_EXPERTS):
            _for_each_piece(tail_len_ref[e], lambda *a, e=e: fill(e)(*a).wait())
        unused_tiles(lambda copy: copy.wait())

    rows = lax.broadcasted_iota(I32, (LOCAL_ROWS, 1), 0)
    sel = jnp.where(lposr_ref[0:1, :] == rows, 1.0, jnp.where(lposr_ref[1:2, :] == rows, 1.0, 0.0)).astype(BF16)

    def step(local_ref, sem, other_ref, other_sem):
        local_ref[:, :D_MODEL] = _dot(sel, hn_ref[...]).astype(BF16)
        local_ref[:, D_MODEL:] = _dot(sel, gate_ref[...]).astype(BF16)

        def copy(buf, buf_sem):
            return lambda loc, dst, size: pltpu.make_async_copy(
                buf.at[pl.ds(loc, size), :], xs_ref.at[pl.ds(dst, size), :], buf_sem)

        _for_each_run_piece(blk, loc_ref, dst_ref, len_ref, lambda *a: copy(local_ref, sem)(*a).start())

        @pl.when(blk > 0)
        def _():
            _for_each_run_piece(blk - 1, loc_ref, dst_ref, len_ref, lambda *a: copy(other_ref, other_sem)(*a).wait())

        @pl.when(blk == pl.num_programs(0) - 1)
        def _():
            _for_each_run_piece(blk, loc_ref, dst_ref, len_ref, lambda *a: copy(local_ref, sem)(*a).wait())

    @pl.when(blk % 2 == 0)
    def _():
        step(local_ref.at[0], sems.at[0], local_ref.at[1], sems.at[1])

    @pl.when(blk % 2 == 1)
    def _():
        step(local_ref.at[1], sems.at[1], local_ref.at[0], sems.at[0])


def _dispatch(runs, tails, hn, gates, lpos_rows, rows):
    nblk = hn.shape[0] // RANK_TILE
    tok = lambda n: pl.BlockSpec((RANK_TILE, n), lambda b, *_: (b, 0))
    return pl.pallas_call(
        _dispatch_kernel,
        out_shape=jax.ShapeDtypeStruct((rows, XS_WIDTH), BF16),
        grid_spec=pltpu.PrefetchScalarGridSpec(
            num_scalar_prefetch=6,
            grid=(nblk,),
            in_specs=[tok(D_MODEL), tok(LANE), pl.BlockSpec((8, RANK_TILE), lambda b, *_: (0, b))],
            out_specs=pl.BlockSpec(memory_space=pl.ANY),
            scratch_shapes=[pltpu.VMEM((2, LOCAL_ROWS, XS_WIDTH), BF16), pltpu.VMEM((RANK_TILE, XS_WIDTH), BF16),
                            pltpu.SemaphoreType.DMA((3,))],
        ),
        compiler_params=_cparams(("arbitrary",)),
        name="moe_dispatch",
    )(*runs, *tails, hn, gates, lpos_rows)


def _moe_kernel(expert_ref, ntiles_ref, used_ref, x_ref, wg_ref, wu_ref, wd_ref, out_ref, acc_ref, h_ref):
    i, j = pl.program_id(0), pl.program_id(1)
    active = i < ntiles_ref[0]
    parts = -(-used_ref[i] // MOE_SKIP_ROWS)

    @pl.when(j == 0)
    def _():
        acc_ref[...] = jnp.zeros_like(acc_ref)

    for n in range(1, MOE_ROW_TILE // MOE_SKIP_ROWS + 1):
        @pl.when(active & (parts == n))
        def _(n=n):
            lead = pl.ds(0, n * MOE_SKIP_ROWS)
            _swiglu_accumulate(acc_ref.at[lead, :], h_ref.at[lead, :], x_ref[:n * MOE_SKIP_ROWS, :D_MODEL],
                               wg_ref.at[0], wu_ref.at[0], wd_ref.at[0])

    @pl.when(j == pl.num_programs(1) - 1)
    def _():
        gs = x_ref[:, D_MODEL:].astype(F32)
        gate = [sum(gs[:, GATE_TERMS * k + n:GATE_TERMS * k + n + 1] for n in range(GATE_TERMS)) for k in range(TOP_K)]
        first = gs[:, TOP_K * GATE_TERMS:TOP_K * GATE_TERMS + 1] == expert_ref[i].astype(F32)
        row_gate = jnp.where(first, gate[0], gate[1])
        out_ref[...] = jnp.where(active, acc_ref[...] * row_gate, 0.0).astype(BF16)


def _moe_experts(tile_expert, n_tiles, tile_used, xs, w_gate_up, w_down):
    rows = xs.shape[0]
    tm, tf = MOE_ROW_TILE, FFN_FF_TILE
    nj = D_FF // tf

    def x_map(i, j, e_ref, n_ref, u_ref):
        return jnp.clip(i, 0, jnp.maximum(n_ref[0] - 1, 0)), 0

    def w_idx(i, j, expert_ref, ntiles_ref):
        e = jnp.minimum(expert_ref[i], N_EXPERTS - 1)
        return e, jnp.where(i < ntiles_ref[0], j, nj - 1)

    def w_gate_map(i, j, e_ref, n_ref, u_ref):
        e, jj = w_idx(i, j, e_ref, n_ref)
        return e, 0, jj

    def w_up_map(i, j, e_ref, n_ref, u_ref):
        e, jj = w_idx(i, j, e_ref, n_ref)
        return e, 0, nj + jj

    def w_down_map(i, j, e_ref, n_ref, u_ref):
        e, jj = w_idx(i, j, e_ref, n_ref)
        return e, jj, 0

    return pl.pallas_call(
        _moe_kernel,
        out_shape=jax.ShapeDtypeStruct((rows, D_MODEL), BF16),
        grid_spec=pltpu.PrefetchScalarGridSpec(
            num_scalar_prefetch=3,
            grid=(rows // tm, nj),
            in_specs=[pl.BlockSpec((tm, XS_WIDTH), x_map),
                      pl.BlockSpec((1, D_MODEL, tf), w_gate_map),
                      pl.BlockSpec((1, D_MODEL, tf), w_up_map),
                      pl.BlockSpec((1, tf, D_MODEL), w_down_map)],
            out_specs=pl.BlockSpec((tm, D_MODEL), lambda i, j, *_: (i, 0)),
            scratch_shapes=[pltpu.VMEM((tm, D_MODEL), F32), pltpu.VMEM((tm, tf), BF16)],
        ),
        compiler_params=_cparams(("arbitrary", "arbitrary")),
        name="moe_experts",
    )(tile_expert, n_tiles, tile_used, xs, w_gate_up, w_gate_up, w_down)


def _combine_kernel(loc_ref, dst_ref, len_ref, x_ref, lpos_ref, ys_ref, out_ref, local_ref, sems):
    blk = pl.program_id(0)
    rows = lax.broadcasted_iota(I32, (1, LOCAL_ROWS), 1)
    lpos = lpos_ref[...]
    sel = jnp.where(lpos[:, 0:1] == rows, 1.0, jnp.where(lpos[:, 1:2] == rows, 1.0, 0.0)).astype(BF16)

    def fetch(block, buf, buf_sem, action):
        _for_each_run_piece(block, loc_ref, dst_ref, len_ref, lambda loc, dst, size: action(pltpu.make_async_copy(
            ys_ref.at[pl.ds(dst, size), :], buf.at[pl.ds(loc, size), :], buf_sem)))

    def start_fetch(block, buf, buf_sem):
        buf[...] = jnp.zeros_like(buf)
        fetch(block, buf, buf_sem, lambda copy: copy.start())

    def step(buf, buf_sem, other, other_sem):
        @pl.when(blk == 0)
        def _():
            start_fetch(blk, buf, buf_sem)

        @pl.when(blk + 1 < pl.num_programs(0))
        def _():
            start_fetch(blk + 1, other, other_sem)

        fetch(blk, buf, buf_sem, lambda copy: copy.wait())
        out_ref[...] = x_ref[...] + _dot(sel, buf[...])

    @pl.when(blk % 2 == 0)
    def _():
        step(local_ref.at[0], sems.at[0], local_ref.at[1], sems.at[1])

    @pl.when(blk % 2 == 1)
    def _():
        step(local_ref.at[1], sems.at[1], local_ref.at[0], sems.at[0])


def _combine(runs, x2, lpos, ys):
    tok = lambda n: pl.BlockSpec((RANK_TILE, n), lambda b, *_: (b, 0))
    return pl.pallas_call(
        _combine_kernel,
        out_shape=jax.ShapeDtypeStruct(x2.shape, F32),
        grid_spec=pltpu.PrefetchScalarGridSpec(
            num_scalar_prefetch=3,
            grid=(x2.shape[0] // RANK_TILE,),
            in_specs=[tok(D_MODEL), tok(LANE), pl.BlockSpec(memory_space=pl.ANY)],
            out_specs=tok(D_MODEL),
            scratch_shapes=[pltpu.VMEM((2, LOCAL_ROWS, D_MODEL), BF16), pltpu.SemaphoreType.DMA((2,))],
        ),
        compiler_params=_cparams(("arbitrary",)),
        name="moe_combine",
    )(*runs, x2, lpos, ys)


def _moe_ffn(x2, g, w_router, w_gate_up, w_down):
    t = x2.shape[0]
    nblk = t // RANK_TILE
    hn, idx, gates, counts = _router(x2, g, w_router)
    lpos, lpos_rows, tile_info, run_tab = _rank(idx, counts)
    runs = tuple(run_tab.reshape(3, nblk, LANE)[:, :, :N_EXPERTS].reshape(3, -1))
    rows = t * TOP_K + nblk * N_EXPERTS * (RUN_ALIGN - 1) + N_EXPERTS * (MOE_ROW_TILE - 1)
    rows = -(-rows // MOE_ROW_TILE) * MOE_ROW_TILE
    assert rows // MOE_ROW_TILE <= LANE
    assert MOE_ROW_TILE % RANK_TILE == 0 and rows % RANK_TILE == 0
    xs = _dispatch(runs, (tile_info[:, 2], tile_info[:, 3], tile_info[:1, 1]), hn, gates, lpos_rows, rows)
    ys = _moe_experts(tile_info[:, 0], tile_info[:1, 1], tile_info[:, 4], xs,
                      w_gate_up.astype(BF16), w_down.astype(BF16))
    return _combine(runs, x2, lpos, ys)


def _pad_heads(w, heads, dim, pad):
    w = w.reshape(w.shape[:-1] + (heads, dim))
    return _pad_lanes(w, pad).reshape(w.shape[:-2] + (heads * pad,))


def kernel(x, mem, positions, a_norm, a_w_in, a_gate_bias, a_head_norm, a_w_out, b_norm, b_w_in, b_q_a_norm, b_w_uq, b_q_head_norm, b_w_out, kv_norm, w_dkv, kv_a_norm, w_ukv, k_head_norm, mem_norm, mem_w_kv, mem_q_norm, mem_k_norm, ffn_norm, dense_w_gate_up, dense_w_down, moe_router, moe_w_gate_up, moe_w_down):
    nb, seq, _ = x.shape
    t = nb * seq
    x2 = x.reshape(t, D_MODEL)
    gmat = jnp.kron(jnp.eye(M_HEADS, dtype=F32), jnp.full((M_HEAD_DIM, M_HEAD_DIM), 1.0 / M_HEAD_DIM, F32)).astype(BF16)

    kbd0, vbd0 = _memory_kv(mem, mem_norm[0], mem_w_kv[0], mem_k_norm[0], gmat)
    w_in = a_w_in[0]
    qk_w, v_w = A_HEADS * A_QK_DIM, A_HEADS * A_V_DIM
    o0, o1, o2, o3, o4 = qk_w, 2 * qk_w, 2 * qk_w + v_w, 2 * qk_w + 2 * v_w, 2 * qk_w + 2 * v_w + 2 * A_HEADS
    w_main = jnp.concatenate([
        _pad_heads(w_in[:, :o0], A_HEADS, A_QK_DIM, A_QK_PAD),
        _pad_heads(w_in[:, o0:o1], A_HEADS, A_QK_DIM, A_QK_PAD),
        w_in[:, o4:]], axis=1).astype(BF16)
    w_vo_t = jnp.concatenate([
        _pad_heads(w_in[:, o1:o2], A_HEADS, A_V_DIM, A_V_ROWS),
        _pad_heads(w_in[:, o2:o3], A_HEADS, A_V_DIM, A_V_ROWS)], axis=1).T.astype(BF16)
    q, k, vt, ot, mq, gc, gr = _a_projection(x2, a_norm[0], w_main, w_vo_t, w_in[:, o3:o4], a_gate_bias[0])
    three = lambda a: a.reshape(nb, seq, a.shape[-1])
    head_g = _pad_heads(a_head_norm[0].reshape(1, -1), A_HEADS, A_V_DIM, A_V_ROWS)
    hm = _mlstm(three(q), three(k), vt, ot, gc, gr,
                jnp.broadcast_to(head_g.reshape(-1, 1), (A_HEADS * A_V_ROWS, MLSTM_CHUNK)))
    w_out = a_w_out[0]
    w_out_h = jnp.pad(w_out[:v_w].reshape(A_HEADS, A_V_DIM, D_MODEL), ((0, 0), (0, A_V_PAD - A_V_DIM), (0, 0)))
    w_out_h = w_out_h.reshape(A_HEADS * A_V_PAD, D_MODEL).astype(BF16)
    x2 = _mix_out(x2, hm.reshape(t, -1), mq, kbd0, vbd0, gmat, mem_q_norm[0],
                  w_out_h, w_out[v_w:].astype(BF16), seq)
    x2 = _dense_ffn(x2, ffn_norm[0], dense_w_gate_up[0].astype(BF16), dense_w_down[0].astype(BF16))

    cs, sn = _rope_tables(positions)
    k_sh, vt_sh, qh, mq1 = _mla_projection(x2, kv_norm, w_dkv, kv_a_norm, w_ukv, k_head_norm,
                                           b_norm[0], b_w_in[0], b_q_a_norm[0], b_w_uq[0], b_q_head_norm[0], cs, sn)

    kbd1, vbd1 = _memory_kv(mem, mem_norm[1], mem_w_kv[1], mem_k_norm[1], gmat)
    att = _causal_attention(three(qh), three(k_sh), vt_sh, seq)
    w_out = b_w_out[0]
    n_att = B_HEADS * V_HEAD
    x2 = _mix_out(x2, att.reshape(t, -1), mq1, kbd1, vbd1, gmat, mem_q_norm[1],
                  w_out[:n_att].astype(BF16), w_out[n_att:].astype(BF16), seq)
    x2 = _moe_ffn(x2, ffn_norm[1], moe_router[0], moe_w_gate_up[0], moe_w_down[0])
    return x2.reshape(nb, seq, D_MODEL)
```

```python
import functools

import jax
import jax.numpy as jnp
from jax import lax
from jax.experimental import pallas as pl
from jax.experimental.pallas import tpu as pltpu

F32 = jnp.float32
BF16 = jnp.bfloat16
I32 = jnp.int32

EPS = 1e-6
LOG2E = 1.4426950408889634
LANE = 128
SUBLANE = 8
VMEM_LIMIT = 48 * 1024 * 1024

D_MODEL = 1024
N_MEM = 256
M_HEADS, M_HEAD_DIM = 4, 64
M_W = M_HEADS * M_HEAD_DIM
A_HEADS, A_QK_DIM, A_V_DIM = 4, 96, 192
A_QK_PAD, A_V_PAD = 128, 256
A_V_ROWS = A_V_DIM + 16
B_HEADS, Q_LORA, KV_LORA = 6, 384, 256
QK_NOPE, QK_ROPE, V_HEAD = 128, 64, 128
B_QK_HEAD = QK_NOPE + QK_ROPE
B_QK_PAD = 256
VT_HEAD_ROWS = V_HEAD + 16
ROPE_THETA = 10000.0
D_FF = 3584
N_EXPERTS, TOP_K = 8, 2
GATE_TERMS = 3

MLSTM_CHUNK = 512
ROW_TILE = 1024
FFN_ROW_TILE = 1024
FFN_FF_TILE = 1792
FFN_CHUNK = 256
MOE_ROW_TILE = 1024
MOE_SKIP_ROWS = 256
ATT_Q_TILE = 512
ATT_K_TILE = 512
ATT_HEADS_PER_STEP = 6
RANK_TILE = 512
RUN_ALIGN = 16
RUN_PIECES = tuple(RANK_TILE >> s for s in range((RANK_TILE // RUN_ALIGN).bit_length()))
LOCAL_ROWS = TOP_K * RANK_TILE + N_EXPERTS * RUN_ALIGN
XS_WIDTH = D_MODEL + LANE
TILE_TABLE_COLS = 8

HIGHEST = lax.Precision.HIGHEST


def _cparams(sem):
    return pltpu.CompilerParams(dimension_semantics=sem, vmem_limit_bytes=VMEM_LIMIT)


def _rms(x, g):
    return x * lax.rsqrt(jnp.mean(x * x, axis=-1, keepdims=True) + EPS) * g


def _dot(a, b):
    return jnp.dot(a, b, preferred_element_type=F32)


def _dot_nt(a, b):
    return lax.dot_general(a, b, (((1,), (1,)), ((), ())), preferred_element_type=F32)


def _dot_tn(a, b):
    return lax.dot_general(a, b, (((0,), (0,)), ((), ())), preferred_element_type=F32)


def _group_mean_sq(x, gmat):
    sq = x * x
    hi = sq.astype(BF16)
    lo = (sq - hi.astype(F32)).astype(BF16)
    return _dot(hi, gmat) + _dot(lo, gmat)


def _memkv_kernel(mem_ref, g_ref, w_ref, kg_ref, gmat_ref, kbd_ref, vbd_ref):
    hn = _rms(mem_ref[0], g_ref[...]).astype(BF16)
    kv = _dot(hn, w_ref[...])
    k, v = kv[:, :M_W], kv[:, M_W:]
    kn = k * lax.rsqrt(_group_mean_sq(k, gmat_ref[...]) + EPS) * kg_ref[...]
    lane_head = lax.broadcasted_iota(I32, (1, M_W), 1) // M_HEAD_DIM
    for h in range(M_HEADS):
        keep = lane_head == h
        kbd_ref[0, h * N_MEM:(h + 1) * N_MEM, :] = jnp.where(keep, kn, 0.0).astype(BF16)
        vbd_ref[0, h * N_MEM:(h + 1) * N_MEM, :] = jnp.where(keep, v, 0.0).astype(BF16)


def _memory_kv(mem, g, w_kv, k_g, gmat):
    nb = mem.shape[0]
    out = jax.ShapeDtypeStruct((nb, M_HEADS * N_MEM, M_W), BF16)
    return pl.pallas_call(
        _memkv_kernel,
        out_shape=(out, out),
        grid=(nb,),
        in_specs=[
            pl.BlockSpec((1, N_MEM, D_MODEL), lambda b: (b, 0, 0)),
            pl.BlockSpec((1, D_MODEL), lambda b: (0, 0)),
            pl.BlockSpec((D_MODEL, 2 * M_W), lambda b: (0, 0)),
            pl.BlockSpec((1, M_W), lambda b: (0, 0)),
            pl.BlockSpec((M_W, M_W), lambda b: (0, 0)),
        ],
        out_specs=(pl.BlockSpec((1, M_HEADS * N_MEM, M_W), lambda b: (b, 0, 0)),) * 2,
        compiler_params=_cparams(("parallel",)),
        name="memory_kv",
    )(mem, g.reshape(1, -1), w_kv.astype(BF16), jnp.tile(k_g, M_HEADS).reshape(1, -1), gmat)


def _memory_attention(mq, kbd, vbd, gmat, qg):
    qn = mq * lax.rsqrt(_group_mean_sq(mq, gmat) + EPS) * (qg * (M_HEAD_DIM ** -0.5 * LOG2E))
    s = _dot_nt(qn.astype(BF16), kbd)
    ps = []
    for h in range(M_HEADS):
        sh = s[:, h * N_MEM:(h + 1) * N_MEM]
        e = jnp.exp2(sh - jnp.max(sh, axis=-1, keepdims=True))
        ps.append((e / jnp.sum(e, axis=-1, keepdims=True)).astype(BF16))
    return _dot(jnp.concatenate(ps, axis=-1), vbd)


def _a_proj_kernel(x_ref, g_ref, w_ref, wvot_ref, wif_ref, wift_ref, bc_ref, br_ref,
                   q_ref, k_ref, vt_ref, ot_ref, mq_ref, gc_ref, gr_ref):
    hn = _rms(x_ref[...], g_ref[...]).astype(BF16)
    nq = A_HEADS * A_QK_PAD
    nv = A_HEADS * A_V_ROWS
    q_ref[...] = _dot(hn, w_ref[:, :nq]).astype(BF16)
    k_ref[...] = (_dot(hn, w_ref[:, nq:2 * nq]) * (A_QK_DIM ** -0.5)).astype(BF16)
    mq_ref[...] = _dot(hn, w_ref[:, 2 * nq:])
    vt = _dot_nt(wvot_ref[:nv, :], hn)
    ones_row = lax.broadcasted_iota(I32, (nv, 1), 0) % A_V_ROWS == A_V_DIM
    vt_ref[...] = jnp.where(ones_row, 1.0, vt).astype(BF16)
    ot_ref[...] = _dot_nt(wvot_ref[nv:, :], hn)
    gc_ref[...] = _dot(hn, wif_ref[...])[:, :2 * A_HEADS] + bc_ref[...]
    gr_ref[...] = _dot_nt(wift_ref[...], hn) + br_ref[...]


def _a_projection(x2, g, w_main, w_vo_t, w_if, gate_bias):
    t = x2.shape[0]
    nq, nv = A_HEADS * A_QK_PAD, A_HEADS * A_V_ROWS
    ng = 2 * A_HEADS
    wif_pad = jnp.pad(w_if, ((0, 0), (0, LANE - ng))).astype(BF16)
    row = lambda n: pl.BlockSpec((ROW_TILE, n), lambda i: (i, 0))
    col = lambda n: pl.BlockSpec((n, ROW_TILE), lambda i: (0, i))
    full = lambda a: pl.BlockSpec(a.shape, lambda i: (0,) * a.ndim)
    args = (x2, g.reshape(1, -1), w_main, w_vo_t, wif_pad, w_if.T.astype(BF16),
            gate_bias.reshape(1, ng), gate_bias.reshape(ng, 1))
    return pl.pallas_call(
        _a_proj_kernel,
        out_shape=(jax.ShapeDtypeStruct((t, nq), BF16), jax.ShapeDtypeStruct((t, nq), BF16),
                   jax.ShapeDtypeStruct((nv, t), BF16), jax.ShapeDtypeStruct((nv, t), F32),
                   jax.ShapeDtypeStruct((t, M_W), F32), jax.ShapeDtypeStruct((t, ng), F32),
                   jax.ShapeDtypeStruct((ng, t), F32)),
        grid=(t // ROW_TILE,),
        in_specs=[row(D_MODEL)] + [full(a) for a in args[1:]],
        out_specs=(row(nq), row(nq), col(nv), col(nv), row(M_W), row(ng), col(ng)),
        compiler_params=_cparams(("parallel",)),
        name="mlstm_in_proj",
    )(*args)


def _log_sigmoid(f):
    return jnp.minimum(f, 0.0) - jnp.log(1.0 + jnp.exp(-jnp.abs(f)))


def _dot_nt_highest(a, b):
    return lax.dot_general(a, b, (((1,), (1,)), ((), ())), precision=HIGHEST,
                           preferred_element_type=F32)


def _chunk_gates(gc, gr):
    L = gc.shape[0]
    r = lax.broadcasted_iota(I32, (L, L), 0)
    c = lax.broadcasted_iota(I32, (L, L), 1)
    lower = (c <= r).astype(F32)
    is_f_col = lax.broadcasted_iota(I32, gc.shape, 1) >= A_HEADS
    is_f_row = lax.broadcasted_iota(I32, gr.shape, 0) >= A_HEADS
    lf_c = jnp.where(is_f_col, _log_sigmoid(gc), 0.0)
    lf_r = jnp.where(is_f_row, _log_sigmoid(gr), 0.0)
    lf_c = jnp.concatenate([lf_c, jnp.zeros((L, LANE - gc.shape[1]), F32)], axis=1)
    cum_c = jnp.dot(lower, lf_c, precision=HIGHEST, preferred_element_type=F32)[:, :gc.shape[1]]
    cum_r = _dot_nt_highest(lf_r, lower)
    return jnp.where(is_f_col, cum_c, gc), jnp.where(is_f_row, cum_r, gr)


def _prefix_max_lanes(x):
    lane = lax.broadcasted_iota(I32, x.shape, 1)
    shift = 1
    while shift < x.shape[1]:
        x = jnp.maximum(x, jnp.where(lane >= shift, pltpu.roll(x, shift, axis=1), -jnp.inf))
        shift *= 2
    return x


def _mlstm_kernel(q_ref, k_ref, vt_ref, ot_ref, gc_ref, gr_ref, hg_ref, out_ref, c_ref, m_ref):
    L = MLSTM_CHUNK

    @pl.when(pl.program_id(1) == 0)
    def _():
        c_ref[...] = jnp.zeros_like(c_ref)
        m_ref[...] = jnp.zeros_like(m_ref)

    src = lax.broadcasted_iota(I32, (L, L), 0)
    tgt = lax.broadcasted_iota(I32, (L, L), 1)
    real = lax.broadcasted_iota(I32, (A_V_ROWS, 1), 0) < A_V_DIM
    gcol, grow = _chunk_gates(gc_ref[...], gr_ref[...])
    for hd in range(A_HEADS):
        qk = slice(hd * A_QK_PAD, (hd + 1) * A_QK_PAD)
        vv = slice(hd * A_V_ROWS, (hd + 1) * A_V_ROWS)
        q, k, vt = q_ref[0, :, qk], k_ref[0, :, qk], vt_ref[vv, :]
        u_c = gcol[:, hd:hd + 1] - gcol[:, A_HEADS + hd:A_HEADS + hd + 1]
        g_r = grow[A_HEADS + hd:A_HEADS + hd + 1, :]
        u_r = grow[hd:hd + 1, :] - g_r
        g_last = g_r[:, L - 1:L]
        m_prev = m_ref[hd, 0:1, 0:1]
        c_prev = c_ref[hd]

        run_max = jnp.maximum(_prefix_max_lanes(jnp.broadcast_to(u_r, (SUBLANE, L)))[0:1, :], m_prev)
        m_t = g_r + run_max
        inter = jnp.exp(m_prev - run_max)
        decay_t = jnp.where(src <= tgt, jnp.exp(u_c - run_max), 0.0)
        p_t = (decay_t * _dot_nt(k, q)).astype(BF16)
        num_t = inter * _dot_nt(c_prev.astype(BF16), q) + _dot(vt, p_t)
        den = num_t[A_V_DIM:A_V_DIM + 1, :]
        h_t = jnp.where(real, num_t / jnp.maximum(jnp.abs(den), jnp.exp(-m_t)), 0.0)
        scale = lax.rsqrt(jnp.sum(h_t * h_t, axis=0, keepdims=True) * (1.0 / A_V_DIM) + EPS)
        out_t = h_t * scale * hg_ref[vv, :] * jax.nn.sigmoid(ot_ref[vv, :])
        out_t = jnp.concatenate([out_t, jnp.zeros((A_V_PAD - A_V_ROWS, L), F32)], axis=0)
        out_ref[0, :, hd * A_V_PAD:(hd + 1) * A_V_PAD] = out_t.T.astype(BF16)

        w_r = g_last + u_r
        m_new = jnp.maximum(g_last + m_prev, jnp.max(w_r, axis=-1, keepdims=True))
        ev_t = (jnp.exp(w_r - m_new) * vt.astype(F32)).astype(BF16)
        c_ref[hd] = jnp.exp(g_last + m_prev - m_new) * c_prev + _dot(ev_t, k)
        m_ref[hd] = jnp.broadcast_to(m_new, m_ref.shape[1:])


def _mlstm(q, k, vt, ot, gcol, grow, head_g):
    nb, s, _ = q.shape
    L = MLSTM_CHUNK
    nv, nvt = A_HEADS * A_V_PAD, A_HEADS * A_V_ROWS
    blk = lambda w: pl.BlockSpec((1, L, w), lambda b, c: (b, c, 0))
    col = lambda n: pl.BlockSpec((n, L), lambda b, c: (0, b * (s // L) + c))
    return pl.pallas_call(
        _mlstm_kernel,
        out_shape=jax.ShapeDtypeStruct((nb, s, nv), BF16),
        grid=(nb, s // L),
        in_specs=[blk(A_HEADS * A_QK_PAD), blk(A_HEADS * A_QK_PAD), col(nvt), col(nvt),
                  pl.BlockSpec((L, 2 * A_HEADS), lambda b, c: (b * (s // L) + c, 0)), col(2 * A_HEADS),
                  pl.BlockSpec((nvt, L), lambda b, c: (0, 0))],
        out_specs=blk(nv),
        scratch_shapes=[pltpu.VMEM((A_HEADS, A_V_ROWS, A_QK_PAD), F32), pltpu.VMEM((A_HEADS, SUBLANE, LANE), F32)],
        compiler_params=_cparams(("parallel", "arbitrary")),
        name="mlstm_chunkwise",
    )(q, k, vt, ot, gcol, grow, head_g)


def _mix_out_kernel(x_ref, h_ref, mq_ref, kbd_ref, vbd_ref, gmat_ref, qg_ref, w1_ref, w2_ref, out_ref):
    mo = _memory_attention(mq_ref[...], kbd_ref[0], vbd_ref[0], gmat_ref[...], qg_ref[...])
    out_ref[...] = x_ref[...] + _dot(h_ref[...], w1_ref[...]) + _dot(mo.astype(BF16), w2_ref[...])


def _mix_out(x2, h2, mq, kbd, vbd, gmat, qg, w_main, w_mem, seq):
    t = x2.shape[0]
    tm = ROW_TILE
    row = lambda n: pl.BlockSpec((tm, n), lambda i: (i, 0))
    full = lambda a: pl.BlockSpec(a.shape, lambda i: (0,) * a.ndim)
    per_batch = pl.BlockSpec((1,) + kbd.shape[1:], lambda i: ((i * tm) // seq, 0, 0))
    qg_t = jnp.tile(qg, M_HEADS).reshape(1, -1)
    return pl.pallas_call(
        _mix_out_kernel,
        out_shape=jax.ShapeDtypeStruct((t, D_MODEL), F32),
        grid=(t // tm,),
        in_specs=[row(D_MODEL), row(h2.shape[1]), row(M_W), per_batch, per_batch,
                  full(gmat), full(qg_t), full(w_main), full(w_mem)],
        out_specs=row(D_MODEL),
        compiler_params=_cparams(("parallel",)),
        name="mixer_out_proj",
    )(x2, h2, mq, kbd, vbd, gmat, qg_t, w_main, w_mem)


def _swiglu_accumulate(acc_ref, h_ref, x, wg_ref, wu_ref, wd_ref):
    for c in range(wg_ref.shape[-1] // FFN_CHUNK):
        cols = slice(c * FFN_CHUNK, (c + 1) * FFN_CHUNK)
        gate = _dot(x, wg_ref[:, cols])
        up = _dot(x, wu_ref[:, cols])
        h_ref[:, cols] = (jax.nn.silu(gate) * up).astype(BF16)
    acc_ref[...] += _dot(h_ref[...], wd_ref[...])


def _ffn_kernel(x_ref, g_ref, wg_ref, wu_ref, wd_ref, out_ref, hn_ref, acc_ref, h_ref):
    j = pl.program_id(1)

    @pl.when(j == 0)
    def _():
        hn_ref[...] = _rms(x_ref[...], g_ref[...]).astype(BF16)
        acc_ref[...] = jnp.zeros_like(acc_ref)

    _swiglu_accumulate(acc_ref, h_ref, hn_ref[...], wg_ref, wu_ref, wd_ref)

    @pl.when(j == pl.num_programs(1) - 1)
    def _():
        out_ref[...] = x_ref[...] + acc_ref[...]


def _dense_ffn(x2, g, w_gate_up, w_down):
    t = x2.shape[0]
    tm, tf = FFN_ROW_TILE, FFN_FF_TILE
    nj = D_FF // tf
    return pl.pallas_call(
        _ffn_kernel,
        out_shape=jax.ShapeDtypeStruct((t, D_MODEL), F32),
        grid=(t // tm, nj),
        in_specs=[pl.BlockSpec((tm, D_MODEL), lambda i, j: (i, 0)),
                  pl.BlockSpec((1, D_MODEL), lambda i, j: (0, 0)),
                  pl.BlockSpec((D_MODEL, tf), lambda i, j: (0, j)),
                  pl.BlockSpec((D_MODEL, tf), lambda i, j: (0, nj + j)),
                  pl.BlockSpec((tf, D_MODEL), lambda i, j: (j, 0))],
        out_specs=pl.BlockSpec((tm, D_MODEL), lambda i, j: (i, 0)),
        scratch_shapes=[pltpu.VMEM((tm, D_MODEL), BF16), pltpu.VMEM((tm, D_MODEL), F32),
                        pltpu.VMEM((tm, tf), BF16)],
        compiler_params=_cparams(("parallel", "arbitrary")),
        name="dense_swiglu",
    )(x2, g.reshape(1, -1), w_gate_up, w_gate_up, w_down)


def _rope_kernel(pos_ref, inv_ref, sign_ref, cs_ref, sn_ref):
    ang = pos_ref[...].astype(F32) * inv_ref[...]
    cs_ref[...] = jnp.cos(ang)
    sn_ref[...] = jnp.sin(ang) * sign_ref[...]


def _rope_tables(positions):
    t = positions.size
    half = QK_ROPE // 2
    inv = 1.0 / (ROPE_THETA ** (jnp.arange(0, QK_ROPE, 2, dtype=F32) / QK_ROPE))
    pad = jnp.zeros((LANE - QK_ROPE,), F32)
    inv_l = jnp.concatenate([inv, inv, pad]).reshape(1, LANE)
    sign = jnp.concatenate([-jnp.ones((half,), F32), jnp.ones((half,), F32), pad]).reshape(1, LANE)
    out = jax.ShapeDtypeStruct((t, LANE), F32)
    return pl.pallas_call(
        _rope_kernel,
        out_shape=(out, out),
        grid=(t // ROW_TILE,),
        in_specs=[pl.BlockSpec((ROW_TILE, 1), lambda i: (i, 0)),
                  pl.BlockSpec((1, LANE), lambda i: (0, 0)),
                  pl.BlockSpec((1, LANE), lambda i: (0, 0))],
        out_specs=(pl.BlockSpec((ROW_TILE, LANE), lambda i: (i, 0)),) * 2,
        compiler_params=_cparams(("parallel",)),
        name="rope_tables",
    )(positions.reshape(t, 1), inv_l, sign)


def _head_qk_norm_rope(nope, rope, rope_sw, g_nope, g_rope, g_rope_sw, cs, sn, scale):
    ss = jnp.sum(nope * nope, axis=-1, keepdims=True) + jnp.sum(rope * rope, axis=-1, keepdims=True)
    r = lax.rsqrt(ss * (1.0 / B_QK_HEAD) + EPS) * scale
    return nope * r * g_nope, (rope * g_rope * cs + rope_sw * g_rope_sw * sn) * r


def _latent_kv_body(hn, wd_ref, ga_ref, wuk_ref, wuvt_ref, kg_ref, cs_ref, sn_ref, k_ref, vt_ref):
    z = _dot(hn, wd_ref[...])
    c_kv = z[:, :KV_LORA]
    rope, rope_sw = z[:, KV_LORA:KV_LORA + LANE], z[:, KV_LORA + LANE:]
    cn = _rms(c_kv, ga_ref[...]).astype(BF16)
    kv = _dot(cn, wuk_ref[...])
    vt = _dot_nt(wuvt_ref[...], cn)
    tm = vt.shape[1]
    ones_row = (lax.broadcasted_iota(I32, (VT_HEAD_ROWS - V_HEAD, tm), 0) == 0).astype(BF16)
    for h in range(B_HEADS):
        vt_ref[h * VT_HEAD_ROWS:h * VT_HEAD_ROWS + V_HEAD, :] = vt[h * V_HEAD:(h + 1) * V_HEAD].astype(BF16)
        vt_ref[h * VT_HEAD_ROWS + V_HEAD:(h + 1) * VT_HEAD_ROWS, :] = ones_row
    kg = kg_ref[...]
    for h in range(B_HEADS):
        kn, kr = _head_qk_norm_rope(kv[:, h * QK_NOPE:(h + 1) * QK_NOPE], rope, rope_sw,
                                    kg[:, :LANE], kg[:, LANE:2 * LANE], kg[:, 2 * LANE:],
                                    cs_ref[...], sn_ref[...], 1.0)
        k_ref[:, h * B_QK_PAD:h * B_QK_PAD + QK_NOPE] = kn.astype(BF16)
        k_ref[:, h * B_QK_PAD + QK_NOPE:(h + 1) * B_QK_PAD] = kr.astype(BF16)


def _rope_swap(w):
    half = QK_ROPE // 2
    return jnp.concatenate([w[..., half:], w[..., :half]], axis=-1)


def _pad_lanes(w, n=LANE):
    return jnp.pad(w, [(0, 0)] * (w.ndim - 1) + [(0, n - w.shape[-1])])


def _head_gain(g):
    g_rope = g[QK_NOPE:]
    return jnp.concatenate([g[:QK_NOPE], _pad_lanes(g_rope), _pad_lanes(_rope_swap(g_rope))]).reshape(1, -1)


def _query_body(hn, win_ref, ga_ref, wuq_ref, qg_ref, cs_ref, sn_ref, q_ref, mq_ref):
    proj = _dot(hn, win_ref[...])
    mq_ref[...] = proj[:, Q_LORA:]
    qall = _dot(_rms(proj[:, :Q_LORA], ga_ref[...]).astype(BF16), wuq_ref[...])
    qg = qg_ref[...]
    per_head = QK_NOPE + 2 * LANE
    for h in range(B_HEADS):
        base = h * per_head
        qn, qr = _head_qk_norm_rope(qall[:, base:base + QK_NOPE],
                                    qall[:, base + QK_NOPE:base + QK_NOPE + LANE],
                                    qall[:, base + QK_NOPE + LANE:base + per_head],
                                    qg[:, :LANE], qg[:, LANE:2 * LANE], qg[:, 2 * LANE:],
                                    cs_ref[...], sn_ref[...], B_QK_HEAD ** -0.5)
        q_ref[:, h * B_QK_PAD:h * B_QK_PAD + QK_NOPE] = qn.astype(BF16)
        q_ref[:, h * B_QK_PAD + QK_NOPE:(h + 1) * B_QK_PAD] = qr.astype(BF16)


def _mla_proj_kernel(x_ref, gkv_ref, wd_ref, ga_ref, wuk_ref, wuvt_ref, kg_ref,
                     gq_ref, win_ref, gqa_ref, wuq_ref, qg_ref, cs_ref, sn_ref,
                     k_ref, vt_ref, q_ref, mq_ref):
    x = x_ref[...]
    xn = x * lax.rsqrt(jnp.mean(x * x, axis=-1, keepdims=True) + EPS)
    _latent_kv_body((xn * gkv_ref[...]).astype(BF16), wd_ref, ga_ref, wuk_ref, wuvt_ref, kg_ref,
                    cs_ref, sn_ref, k_ref, vt_ref)
    _query_body((xn * gq_ref[...]).astype(BF16), win_ref, gqa_ref, wuq_ref, qg_ref, cs_ref, sn_ref, q_ref, mq_ref)


def _mla_projection(x2, kv_norm, w_dkv, kv_a_norm, w_ukv, k_head_norm,
                    q_norm, w_in, q_a_g, w_uq, q_head_g, cs, sn):
    t = x2.shape[0]
    w_rope = w_dkv[:, KV_LORA:]
    wd = jnp.concatenate([w_dkv[:, :KV_LORA], _pad_lanes(w_rope), _pad_lanes(_rope_swap(w_rope))],
                         axis=1).astype(BF16)
    wu = w_ukv.reshape(KV_LORA, B_HEADS, QK_NOPE + V_HEAD)
    wuk = wu[:, :, :QK_NOPE].reshape(KV_LORA, -1).astype(BF16)
    wuvt = wu[:, :, QK_NOPE:].reshape(KV_LORA, -1).T.astype(BF16)
    wq = w_uq.reshape(Q_LORA, B_HEADS, B_QK_HEAD)
    wq_rope = wq[:, :, QK_NOPE:]
    wq = jnp.concatenate([wq[:, :, :QK_NOPE], _pad_lanes(wq_rope), _pad_lanes(_rope_swap(wq_rope))],
                         axis=-1).reshape(Q_LORA, -1).astype(BF16)
    consts = (kv_norm.reshape(1, -1), wd, kv_a_norm.reshape(1, -1), wuk, wuvt, _head_gain(k_head_norm),
              q_norm.reshape(1, -1), w_in.astype(BF16), q_a_g.reshape(1, -1), wq, _head_gain(q_head_g))
    row = lambda n: pl.BlockSpec((ROW_TILE, n), lambda i: (i, 0))
    full = lambda a: pl.BlockSpec(a.shape, lambda i: (0,) * a.ndim)
    return pl.pallas_call(
        _mla_proj_kernel,
        out_shape=(jax.ShapeDtypeStruct((t, B_HEADS * B_QK_PAD), BF16),
                   jax.ShapeDtypeStruct((B_HEADS * VT_HEAD_ROWS, t), BF16),
                   jax.ShapeDtypeStruct((t, B_HEADS * B_QK_PAD), BF16),
                   jax.ShapeDtypeStruct((t, M_W), F32)),
        grid=(t // ROW_TILE,),
        in_specs=[row(D_MODEL)] + [full(a) for a in consts] + [row(LANE), row(LANE)],
        out_specs=(row(B_HEADS * B_QK_PAD), pl.BlockSpec((B_HEADS * VT_HEAD_ROWS, ROW_TILE), lambda i: (0, i)),
                   row(B_HEADS * B_QK_PAD), row(M_W)),
        compiler_params=_cparams(("parallel",)),
        name="mla_proj",
    )(x2, *consts, cs, sn)


def _attn_kernel(q_ref, k_ref, vt_ref, out_ref):
    tq, tk = ATT_Q_TILE, ATT_K_TILE
    i = pl.program_id(2)

    def block(h, j, carry, masked):
        m, acc = carry
        off = pl.multiple_of(j * tk, tk)
        q = q_ref[0, :, h * B_QK_PAD:(h + 1) * B_QK_PAD]
        st = _dot_nt(k_ref[0, pl.ds(off, tk), h * B_QK_PAD:(h + 1) * B_QK_PAD], q)
        if masked:
            key = lax.broadcasted_iota(I32, (tk, tq), 0)
            qry = lax.broadcasted_iota(I32, (tk, tq), 1)
            st = jnp.where(key <= qry, st, -jnp.inf)
        m_new = jnp.maximum(m, jnp.max(st, axis=0, keepdims=True))
        p = jnp.exp(st - m_new).astype(BF16)
        vt = vt_ref[h * VT_HEAD_ROWS:(h + 1) * VT_HEAD_ROWS, pl.ds(off, tk)]
        return m_new, jnp.exp(m - m_new) * acc + _dot(vt, p)

    heads = range(ATT_HEADS_PER_STEP)
    init = tuple((jnp.full((1, tq), -jnp.inf, F32), jnp.zeros((VT_HEAD_ROWS, tq), F32)) for _ in heads)
    carry = lax.fori_loop(0, i, lambda j, c: tuple(block(h, j, c[h], False) for h in heads), init)
    for h in heads:
        _, acc = block(h, i, carry[h], True)
        out_t = acc[:V_HEAD] / acc[V_HEAD:V_HEAD + 1]
        out_ref[0, :, h * V_HEAD:(h + 1) * V_HEAD] = out_t.T.astype(BF16)


def _causal_attention(q, k, vt, seq):
    nb = q.shape[0]
    g = ATT_HEADS_PER_STEP
    assert ATT_Q_TILE == ATT_K_TILE and B_HEADS % g == 0
    return pl.pallas_call(
        _attn_kernel,
        out_shape=jax.ShapeDtypeStruct((nb, seq, B_HEADS * V_HEAD), BF16),
        grid=(nb, B_HEADS // g, seq // ATT_Q_TILE),
        in_specs=[pl.BlockSpec((1, ATT_Q_TILE, g * B_QK_PAD), lambda b, h, i: (b, i, h)),
                  pl.BlockSpec((1, seq, g * B_QK_PAD), lambda b, h, i: (b, 0, h)),
                  pl.BlockSpec((g * VT_HEAD_ROWS, seq), lambda b, h, i: (h, b))],
        out_specs=pl.BlockSpec((1, ATT_Q_TILE, g * V_HEAD), lambda b, h, i: (b, i, h)),
        compiler_params=_cparams(("parallel", "parallel", "arbitrary")),
        name="causal_attention",
    )(q, k, vt)


def _router_kernel(x_ref, g_ref, wr_hi_ref, wr_lo_ref, hn_ref, idx_ref, gate_ref, count_ref):
    hn = _rms(x_ref[...], g_ref[...])
    hn_hi = hn.astype(BF16)
    hn_ref[...] = hn_hi
    hn_lo = (hn - hn_hi.astype(F32)).astype(BF16)
    logits = _dot(hn_hi, wr_hi_ref[...]) + _dot(hn_hi, wr_lo_ref[...]) + _dot(hn_lo, wr_hi_ref[...])
    lane = lax.broadcasted_iota(I32, logits.shape, 1)
    logits = jnp.where(lane < N_EXPERTS, logits, -jnp.inf)
    v1 = jnp.max(logits, axis=-1, keepdims=True)
    i1 = jnp.min(jnp.where(logits == v1, lane, LANE), axis=-1, keepdims=True)
    rest = jnp.where(lane == i1, -jnp.inf, logits)
    v2 = jnp.max(rest, axis=-1, keepdims=True)
    i2 = jnp.min(jnp.where(rest == v2, lane, LANE), axis=-1, keepdims=True)
    e2 = jnp.exp(v2 - v1)
    den = 1.0 + e2
    idx_ref[...] = jnp.where(lane == 0, i1, jnp.where(lane == 1, i2, 0))
    record = jnp.zeros(logits.shape, F32)
    for k, gate in enumerate((1.0 / den, e2 / den)):
        hi = gate.astype(BF16).astype(F32)
        mid = (gate - hi).astype(BF16).astype(F32)
        for part, term in enumerate((hi, mid, gate - hi - mid)):
            record = jnp.where(lane == GATE_TERMS * k + part, term, record)
    record = jnp.where(lane == 2 * GATE_TERMS, i1.astype(F32), record)
    record = jnp.where(lane == 2 * GATE_TERMS + 1, i2.astype(F32), record)
    gate_ref[...] = record.astype(BF16)
    pairs = (lane == i1).astype(F32) + (lane == i2).astype(F32)
    count_ref[...] = jnp.broadcast_to(jnp.sum(pairs, axis=0, keepdims=True), count_ref.shape)


def _router(x2, g, w_router):
    t = x2.shape[0]
    row = lambda n: pl.BlockSpec((RANK_TILE, n), lambda i: (i, 0))
    wr = _pad_lanes(w_router)
    wr_hi = wr.astype(BF16)
    wr_lo = (wr - wr_hi.astype(F32)).astype(BF16)
    return pl.pallas_call(
        _router_kernel,
        out_shape=(jax.ShapeDtypeStruct((t, D_MODEL), BF16), jax.ShapeDtypeStruct((t, LANE), I32),
                   jax.ShapeDtypeStruct((t, LANE), BF16), jax.ShapeDtypeStruct((SUBLANE * (t // RANK_TILE), LANE), F32)),
        grid=(t // RANK_TILE,),
        in_specs=[row(D_MODEL), pl.BlockSpec((1, D_MODEL), lambda i: (0, 0)),
                  pl.BlockSpec(wr.shape, lambda i: (0, 0)), pl.BlockSpec(wr.shape, lambda i: (0, 0))],
        out_specs=(row(D_MODEL), row(LANE), row(LANE), pl.BlockSpec((SUBLANE, LANE), lambda i: (i, 0))),
        compiler_params=_cparams(("parallel",)),
        name="moe_router",
    )(x2, g.reshape(1, -1), wr_hi, wr_lo)


def _aligned(count):
    return jnp.ceil(count * (1.0 / RUN_ALIGN)) * RUN_ALIGN


def _rank_kernel(idx_ref, counts_ref, lpos_ref, lposr_ref, tile_ref, runs_ref, run_ref, start_ref):
    blk = pl.program_id(0)
    tb = RANK_TILE
    lane = lax.broadcasted_iota(I32, (tb, LANE), 1)
    idx = idx_ref[...]
    oh0 = (lane == idx[:, 0:1]).astype(F32)
    oh1 = (lane == idx[:, 1:2]).astype(F32)
    both = oh0 + oh1
    run_len = _aligned(jnp.sum(both, axis=0, keepdims=True))
    r = lax.broadcasted_iota(I32, (LANE, LANE), 0)
    c = lax.broadcasted_iota(I32, (LANE, LANE), 1)
    before = (r < c).astype(F32)

    @pl.when(blk == 0)
    def _():
        sizes = jnp.sum(_aligned(counts_ref[...]), axis=0, keepdims=True) * (1.0 / SUBLANE)
        tiles = jnp.ceil(sizes * (1.0 / MOE_ROW_TILE))
        tile_start = jnp.dot(tiles, before, precision=HIGHEST, preferred_element_type=F32)
        start_ref[...] = tile_start * MOE_ROW_TILE
        tile_end = tile_start + tiles
        n_col = r.astype(F32)
        ended = ((n_col >= tile_end) & (c < N_EXPERTS)).astype(F32)
        expert = jnp.sum(ended, axis=-1, keepdims=True)
        total = jnp.max(tile_end, axis=-1, keepdims=True)
        as_column = lambda row: jnp.sum(jnp.where(r == c, row, 0.0), axis=-1, keepdims=True)
        tail_start = as_column(start_ref[...] + sizes)
        tail_len = as_column(tiles * MOE_ROW_TILE - sizes)
        mine = expert == c.astype(F32)
        group_rows = jnp.sum(jnp.where(mine, sizes, 0.0), axis=-1, keepdims=True)
        group_tile = jnp.sum(jnp.where(mine, tile_start, 0.0), axis=-1, keepdims=True)
        used_rows = jnp.clip(group_rows - (n_col[:, 0:1] - group_tile) * MOE_ROW_TILE, 0.0, MOE_ROW_TILE)
        col = lax.broadcasted_iota(I32, tile_ref.shape, 1)
        tile_ref[...] = jnp.where(col == 0, expert, jnp.where(col == 1, total, jnp.where(
            col == 2, tail_start, jnp.where(col == 3, tail_len, used_rows)))).astype(I32)
        run_ref[...] = jnp.zeros_like(run_ref)
        runs_ref[...] = jnp.zeros_like(runs_ref)

    rr = lax.broadcasted_iota(I32, (tb, tb), 0)
    cc = lax.broadcasted_iota(I32, (tb, tb), 1)
    strict = (cc < rr).astype(BF16)
    local = jnp.dot(run_len, before, precision=HIGHEST, preferred_element_type=F32)
    base = _dot(strict, both.astype(BF16)) + local
    p0 = jnp.sum(oh0 * base, axis=-1, keepdims=True)
    p1 = jnp.sum(oh1 * base, axis=-1, keepdims=True)
    posf = jnp.where(lane == 0, p0, jnp.where(lane == 1, p1, 0.0))
    lpos_ref[...] = posf.astype(I32)
    pick = (lax.broadcasted_iota(I32, (SUBLANE, LANE), 0) == lax.broadcasted_iota(I32, (SUBLANE, LANE), 1)).astype(F32)
    lposr_ref[...] = _dot_nt_highest(pick, posf).astype(I32)
    n = runs_ref.shape[0] // 3
    mine = lax.broadcasted_iota(I32, (n, LANE), 0) == blk
    for k, value in enumerate((local, run_ref[...] + start_ref[...], run_len)):
        runs_ref[k * n:(k + 1) * n, :] = jnp.where(mine, value.astype(I32), runs_ref[k * n:(k + 1) * n, :])
    run_ref[...] += run_len


def _rank(idx, counts):
    t = idx.shape[0]
    nblk = t // RANK_TILE
    const = lambda shape: pl.BlockSpec(shape, lambda i: (0, 0))
    return pl.pallas_call(
        _rank_kernel,
        out_shape=(jax.ShapeDtypeStruct((t, LANE), I32), jax.ShapeDtypeStruct((SUBLANE, t), I32),
                   jax.ShapeDtypeStruct((LANE, TILE_TABLE_COLS), I32), jax.ShapeDtypeStruct((3 * nblk, LANE), I32)),
        grid=(nblk,),
        in_specs=[pl.BlockSpec((RANK_TILE, LANE), lambda i: (i, 0)), const(counts.shape)],
        out_specs=(pl.BlockSpec((RANK_TILE, LANE), lambda i: (i, 0)),
                   pl.BlockSpec((SUBLANE, RANK_TILE), lambda i: (0, i)),
                   const((LANE, TILE_TABLE_COLS)), const((3 * nblk, LANE))),
        scratch_shapes=[pltpu.VMEM((1, LANE), F32), pltpu.VMEM((1, LANE), F32)],
        compiler_params=_cparams(("arbitrary",)),
        name="moe_rank",
    )(idx, counts)


def _for_each_piece(length, fn):
    for size in RUN_PIECES:
        @pl.when((length & size) != 0)
        def _(size=size):
            fn(pl.multiple_of(length & (-2 * size), RUN_ALIGN), size)


def _for_each_run_piece(blk, loc_ref, dst_ref, len_ref, fn):
    for e in range(N_EXPERTS):
        k = blk * N_EXPERTS + e
        loc, dst = loc_ref[k], dst_ref[k]
        _for_each_piece(len_ref[k], lambda done, size, loc=loc, dst=dst: fn(
            pl.multiple_of(loc + done, RUN_ALIGN), pl.multiple_of(dst + done, RUN_ALIGN), size))


def _dispatch_kernel(loc_ref, dst_ref, len_ref, tail_ref, tail_len_ref, ntiles_ref, hn_ref, gate_ref, lposr_ref,
                     xs_ref, local_ref, zero_ref, sems):
    blk = pl.program_id(0)

    @pl.when(blk == 0)
    def _():
        sem = sems.at[2]
        zero_ref[...] = jnp.zeros_like(zero_ref)

        def fill(e):
            start = tail_ref[e]
            return lambda done, size: pltpu.make_async_copy(
                zero_ref.at[pl.ds(0, size), :], xs_ref.at[pl.ds(pl.multiple_of(start + done, RUN_ALIGN), size), :], sem)

        def fill_tile(i):
            return pltpu.make_async_copy(
                zero_ref, xs_ref.at[pl.ds(pl.multiple_of(i * RANK_TILE, RANK_TILE), RANK_TILE), :], sem)

        def unused_tiles(action):
            def body(i, carry):
                action(fill_tile(i))
                return carry
            lax.fori_loop(ntiles_ref[0] * (MOE_ROW_TILE // RANK_TILE), xs_ref.shape[0] // RANK_TILE, body, 0)

        for e in range(N_EXPERTS):
            _for_each_piece(tail_len_ref[e], lambda *a, e=e: fill(e)(*a).start())
        unused_tiles(lambda copy: copy.start())
        for e in range(N_EXPERTS):
            _for_each_piece(tail_len_ref[e], lambda *a, e=e: fill(e)(*a).wait())
        unused_tiles(lambda copy: copy.wait())

    rows = lax.broadcasted_iota(I32, (LOCAL_ROWS, 1), 0)
    sel = jnp.where(lposr_ref[0:1, :] == rows, 1.0, jnp.where(lposr_ref[1:2, :] == rows, 1.0, 0.0)).astype(BF16)

    def step(local_ref, sem, other_ref, other_sem):
        local_ref[:, :D_MODEL] = _dot(sel, hn_ref[...]).astype(BF16)
        local_ref[:, D_MODEL:] = _dot(sel, gate_ref[...]).astype(BF16)

        def copy(buf, buf_sem):
            return lambda loc, dst, size: pltpu.make_async_copy(
                buf.at[pl.ds(loc, size), :], xs_ref.at[pl.ds(dst, size), :], buf_sem)

        _for_each_run_piece(blk, loc_ref, dst_ref, len_ref, lambda *a: copy(local_ref, sem)(*a).start())

        @pl.when(blk > 0)
        def _():
            _for_each_run_piece(blk - 1, loc_ref, dst_ref, len_ref, lambda *a: copy(other_ref, other_sem)(*a).wait())

        @pl.when(blk == pl.num_programs(0) - 1)
        def _():
            _for_each_run_piece(blk, loc_ref, dst_ref, len_ref, lambda *a: copy(local_ref, sem)(*a).wait())

    @pl.when(blk % 2 == 0)
    def _():
        step(local_ref.at[0], sems.at[0], local_ref.at[1], sems.at[1])

    @pl.when(blk % 2 == 1)
    def _():
        step(local_ref.at[1], sems.at[1], local_ref.at[0], sems.at[0])


def _dispatch(runs, tails, hn, gates, lpos_rows, rows):
    nblk = hn.shape[0] // RANK_TILE
    tok = lambda n: pl.BlockSpec((RANK_TILE, n), lambda b, *_: (b, 0))
    return pl.pallas_call(
        _dispatch_kernel,
        out_shape=jax.ShapeDtypeStruct((rows, XS_WIDTH), BF16),
        grid_spec=pltpu.PrefetchScalarGridSpec(
            num_scalar_prefetch=6,
            grid=(nblk,),
            in_specs=[tok(D_MODEL), tok(LANE), pl.BlockSpec((SUBLANE, RANK_TILE), lambda b, *_: (0, b))],
            out_specs=pl.BlockSpec(memory_space=pl.ANY),
            scratch_shapes=[pltpu.VMEM((2, LOCAL_ROWS, XS_WIDTH), BF16), pltpu.VMEM((RANK_TILE, XS_WIDTH), BF16),
                            pltpu.SemaphoreType.DMA((3,))],
        ),
        compiler_params=_cparams(("arbitrary",)),
        name="moe_dispatch",
    )(*runs, *tails, hn, gates, lpos_rows)


def _moe_kernel(expert_ref, ntiles_ref, used_ref, x_ref, wg_ref, wu_ref, wd_ref, out_ref, acc_ref, h_ref):
    i, j = pl.program_id(0), pl.program_id(1)
    active = i < ntiles_ref[0]
    parts = -(-used_ref[i] // MOE_SKIP_ROWS)

    @pl.when(j == 0)
    def _():
        acc_ref[...] = jnp.zeros_like(acc_ref)

    for n in range(1, MOE_ROW_TILE // MOE_SKIP_ROWS + 1):
        @pl.when(active & (parts == n))
        def _(n=n):
            lead = pl.ds(0, n * MOE_SKIP_ROWS)
            _swiglu_accumulate(acc_ref.at[lead, :], h_ref.at[lead, :], x_ref[:n * MOE_SKIP_ROWS, :D_MODEL],
                               wg_ref.at[0], wu_ref.at[0], wd_ref.at[0])

    @pl.when(j == pl.num_programs(1) - 1)
    def _():
        gs = x_ref[:, D_MODEL:].astype(F32)
        gate = [sum(gs[:, GATE_TERMS * k + n:GATE_TERMS * k + n + 1] for n in range(GATE_TERMS)) for k in range(TOP_K)]
        first = gs[:, TOP_K * GATE_TERMS:TOP_K * GATE_TERMS + 1] == expert_ref[i].astype(F32)
        row_gate = jnp.where(first, gate[0], gate[1])
        out_ref[...] = jnp.where(active, acc_ref[...] * row_gate, 0.0).astype(BF16)


def _moe_experts(tile_expert, n_tiles, tile_used, xs, w_gate_up, w_down):
    rows = xs.shape[0]
    tm, tf = MOE_ROW_TILE, FFN_FF_TILE
    nj = D_FF // tf

    def x_map(i, j, e_ref, n_ref, u_ref):
        return jnp.clip(i, 0, jnp.maximum(n_ref[0] - 1, 0)), 0

    def w_idx(i, j, expert_ref, ntiles_ref):
        e = jnp.minimum(expert_ref[i], N_EXPERTS - 1)
        return e, jnp.where(i < ntiles_ref[0], j, nj - 1)

    def w_gate_map(i, j, e_ref, n_ref, u_ref):
        e, jj = w_idx(i, j, e_ref, n_ref)
        return e, 0, jj

    def w_up_map(i, j, e_ref, n_ref, u_ref):
        e, jj = w_idx(i, j, e_ref, n_ref)
        return e, 0, nj + jj

    def w_down_map(i, j, e_ref, n_ref, u_ref):
        e, jj = w_idx(i, j, e_ref, n_ref)
        return e, jj, 0

    return pl.pallas_call(
        _moe_kernel,
        out_shape=jax.ShapeDtypeStruct((rows, D_MODEL), BF16),
        grid_spec=pltpu.PrefetchScalarGridSpec(
            num_scalar_prefetch=3,
            grid=(rows // tm, nj),
            in_specs=[pl.BlockSpec((tm, XS_WIDTH), x_map),
                      pl.BlockSpec((1, D_MODEL, tf), w_gate_map),
                      pl.BlockSpec((1, D_MODEL, tf), w_up_map),
                      pl.BlockSpec((1, tf, D_MODEL), w_down_map)],
            out_specs=pl.BlockSpec((tm, D_MODEL), lambda i, j, *_: (i, 0)),
            scratch_shapes=[pltpu.VMEM((tm, D_MODEL), F32), pltpu.VMEM((tm, tf), BF16)],
        ),
        compiler_params=_cparams(("arbitrary", "arbitrary")),
        name="moe_experts",
    )(tile_expert, n_tiles, tile_used, xs, w_gate_up, w_gate_up, w_down)


def _combine_kernel(loc_ref, dst_ref, len_ref, x_ref, lpos_ref, ys_ref, out_ref, local_ref, sems):
    blk = pl.program_id(0)
    rows = lax.broadcasted_iota(I32, (1, LOCAL_ROWS), 1)
    lpos = lpos_ref[...]
    sel = jnp.where(lpos[:, 0:1] == rows, 1.0, jnp.where(lpos[:, 1:2] == rows, 1.0, 0.0)).astype(BF16)

    def fetch(block, buf, buf_sem, action):
        _for_each_run_piece(block, loc_ref, dst_ref, len_ref, lambda loc, dst, size: action(pltpu.make_async_copy(
            ys_ref.at[pl.ds(dst, size), :], buf.at[pl.ds(loc, size), :], buf_sem)))

    def start_fetch(block, buf, buf_sem):
        buf[...] = jnp.zeros_like(buf)
        fetch(block, buf, buf_sem, lambda copy: copy.start())

    def step(buf, buf_sem, other, other_sem):
        @pl.when(blk == 0)
        def _():
            start_fetch(blk, buf, buf_sem)

        @pl.when(blk + 1 < pl.num_programs(0))
        def _():
            start_fetch(blk + 1, other, other_sem)

        fetch(blk, buf, buf_sem, lambda copy: copy.wait())
        out_ref[...] = x_ref[...] + _dot(sel, buf[...])

    @pl.when(blk % 2 == 0)
    def _():
        step(local_ref.at[0], sems.at[0], local_ref.at[1], sems.at[1])

    @pl.when(blk % 2 == 1)
    def _():
        step(local_ref.at[1], sems.at[1], local_ref.at[0], sems.at[0])


def _combine(runs, x2, lpos, ys):
    tok = lambda n: pl.BlockSpec((RANK_TILE, n), lambda b, *_: (b, 0))
    return pl.pallas_call(
        _combine_kernel,
        out_shape=jax.ShapeDtypeStruct(x2.shape, F32),
        grid_spec=pltpu.PrefetchScalarGridSpec(
            num_scalar_prefetch=3,
            grid=(x2.shape[0] // RANK_TILE,),
            in_specs=[tok(D_MODEL), tok(LANE), pl.BlockSpec(memory_space=pl.ANY)],
            out_specs=tok(D_MODEL),
            scratch_shapes=[pltpu.VMEM((2, LOCAL_ROWS, D_MODEL), BF16), pltpu.SemaphoreType.DMA((2,))],
        ),
        compiler_params=_cparams(("arbitrary",)),
        name="moe_combine",
    )(*runs, x2, lpos, ys)


def _moe_ffn(x2, g, w_router, w_gate_up, w_down):
    t = x2.shape[0]
    nblk = t // RANK_TILE
    hn, idx, gates, counts = _router(x2, g, w_router)
    lpos, lpos_rows, tile_info, run_tab = _rank(idx, counts)
    runs = tuple(run_tab.reshape(3, nblk, LANE)[:, :, :N_EXPERTS].reshape(3, -1))
    rows = t * TOP_K + nblk * N_EXPERTS * (RUN_ALIGN - 1) + N_EXPERTS * (MOE_ROW_TILE - 1)
    rows = -(-rows // MOE_ROW_TILE) * MOE_ROW_TILE
    assert rows // MOE_ROW_TILE <= LANE
    assert MOE_ROW_TILE % RANK_TILE == 0 and rows % RANK_TILE == 0
    xs = _dispatch(runs, (tile_info[:, 2], tile_info[:, 3], tile_info[:1, 1]), hn, gates, lpos_rows, rows)
    ys = _moe_experts(tile_info[:, 0], tile_info[:1, 1], tile_info[:, 4], xs,
                      w_gate_up.astype(BF16), w_down.astype(BF16))
    return _combine(runs, x2, lpos, ys)


def _pad_heads(w, heads, dim, pad):
    w = w.reshape(w.shape[:-1] + (heads, dim))
    return _pad_lanes(w, pad).reshape(w.shape[:-2] + (heads * pad,))


def kernel(x, mem, positions, a_norm, a_w_in, a_gate_bias, a_head_norm, a_w_out, b_norm, b_w_in, b_q_a_norm, b_w_uq, b_q_head_norm, b_w_out, kv_norm, w_dkv, kv_a_norm, w_ukv, k_head_norm, mem_norm, mem_w_kv, mem_q_norm, mem_k_norm, ffn_norm, dense_w_gate_up, dense_w_down, moe_router, moe_w_gate_up, moe_w_down):
    nb, seq, _ = x.shape
    t = nb * seq
    x2 = x.reshape(t, D_MODEL)
    gmat = jnp.kron(jnp.eye(M_HEADS, dtype=F32), jnp.full((M_HEAD_DIM, M_HEAD_DIM), 1.0 / M_HEAD_DIM, F32)).astype(BF16)

    kbd0, vbd0 = _memory_kv(mem, mem_norm[0], mem_w_kv[0], mem_k_norm[0], gmat)
    w_in = a_w_in[0]
    qk_w, v_w = A_HEADS * A_QK_DIM, A_HEADS * A_V_DIM
    o0, o1, o2, o3, o4 = qk_w, 2 * qk_w, 2 * qk_w + v_w, 2 * qk_w + 2 * v_w, 2 * qk_w + 2 * v_w + 2 * A_HEADS
    w_main = jnp.concatenate([
        _pad_heads(w_in[:, :o0], A_HEADS, A_QK_DIM, A_QK_PAD),
        _pad_heads(w_in[:, o0:o1], A_HEADS, A_QK_DIM, A_QK_PAD),
        w_in[:, o4:]], axis=1).astype(BF16)
    w_vo_t = jnp.concatenate([
        _pad_heads(w_in[:, o1:o2], A_HEADS, A_V_DIM, A_V_ROWS),
        _pad_heads(w_in[:, o2:o3], A_HEADS, A_V_DIM, A_V_ROWS)], axis=1).T.astype(BF16)
    q, k, vt, ot, mq, gc, gr = _a_projection(x2, a_norm[0], w_main, w_vo_t, w_in[:, o3:o4], a_gate_bias[0])
    three = lambda a: a.reshape(nb, seq, a.shape[-1])
    head_g = _pad_heads(a_head_norm[0].reshape(1, -1), A_HEADS, A_V_DIM, A_V_ROWS)
    hm = _mlstm(three(q), three(k), vt, ot, gc, gr,
                jnp.broadcast_to(head_g.reshape(-1, 1), (A_HEADS * A_V_ROWS, MLSTM_CHUNK)))
    w_out = a_w_out[0]
    w_out_h = jnp.pad(w_out[:v_w].reshape(A_HEADS, A_V_DIM, D_MODEL), ((0, 0), (0, A_V_PAD - A_V_DIM), (0, 0)))
    w_out_h = w_out_h.reshape(A_HEADS * A_V_PAD, D_MODEL).astype(BF16)
    x2 = _mix_out(x2, hm.reshape(t, -1), mq, kbd0, vbd0, gmat, mem_q_norm[0],
                  w_out_h, w_out[v_w:].astype(BF16), seq)
    x2 = _dense_ffn(x2, ffn_norm[0], dense_w_gate_up[0].astype(BF16), dense_w_down[0].astype(BF16))

    cs, sn = _rope_tables(positions)
    k_sh, vt_sh, qh, mq1 = _mla_projection(x2, kv_norm, w_dkv, kv_a_norm, w_ukv, k_head_norm,
                                           b_norm[0], b_w_in[0], b_q_a_norm[0], b_w_uq[0], b_q_head_norm[0], cs, sn)

    kbd1, vbd1 = _memory_kv(mem, mem_norm[1], mem_w_kv[1], mem_k_norm[1], gmat)
    att = _causal_attention(three(qh), three(k_sh), vt_sh, seq)
    w_out = b_w_out[0]
    n_att = B_HEADS * V_HEAD
    x2 = _mix_out(x2, att.reshape(t, -1), mq1, kbd1, vbd1, gmat, mem_q_norm[1],
                  w_out[:n_att].astype(BF16), w_out[n_att:].astype(BF16), seq)
    x2 = _moe_ffn(x2, ffn_norm[1], moe_router[0], moe_w_gate_up[0], moe_w_down[0])
    return x2.reshape(nb, seq, D_MODEL)
```

```python
import functools

import jax
import jax.numpy as jnp
from jax import lax
from jax.experimental import pallas as pl
from jax.experimental.pallas import tpu as pltpu

F32 = jnp.float32
BF16 = jnp.bfloat16
I32 = jnp.int32

EPS = 1e-6
LOG2E = 1.4426950408889634
LANE = 128
SUBLANE = 8
VMEM_LIMIT = 48 * 1024 * 1024

D_MODEL = 1024
N_MEM = 256
M_HEADS, M_HEAD_DIM = 4, 64
M_W = M_HEADS * M_HEAD_DIM
A_HEADS, A_QK_DIM, A_V_DIM = 4, 96, 192
A_QK_PAD, A_V_PAD = 128, 256
A_V_ROWS = A_V_DIM + 16
B_HEADS, Q_LORA, KV_LORA = 6, 384, 256
QK_NOPE, QK_ROPE, V_HEAD = 128, 64, 128
B_QK_HEAD = QK_NOPE + QK_ROPE
B_QK_PAD = 256
VT_HEAD_ROWS = V_HEAD + 16
ROPE_THETA = 10000.0
D_FF = 3584
N_EXPERTS, TOP_K = 8, 2
GATE_TERMS = 3

MLSTM_CHUNK = 512
ROW_TILE = 1024
FFN_ROW_TILE = 1024
FFN_FF_TILE = 1792
FFN_CHUNK = 256
MOE_ROW_TILE = 1024
MOE_SKIP_ROWS = 256
ATT_Q_TILE = 512
ATT_K_TILE = 512
ATT_HEADS_PER_STEP = 6
RANK_TILE = 512
RUN_ALIGN = 16
RUN_PIECES = tuple(RANK_TILE >> s for s in range((RANK_TILE // RUN_ALIGN).bit_length()))
LOCAL_ROWS = TOP_K * RANK_TILE + N_EXPERTS * RUN_ALIGN
XS_WIDTH = D_MODEL + LANE
TILE_TABLE_COLS = 8


def _cparams(sem):
    return pltpu.CompilerParams(dimension_semantics=sem, vmem_limit_bytes=VMEM_LIMIT)


def _rms(x, g):
    return x * lax.rsqrt(jnp.mean(x * x, axis=-1, keepdims=True) + EPS) * g


def _dot(a, b):
    return jnp.dot(a, b, preferred_element_type=F32)


def _dot_nt(a, b):
    return lax.dot_general(a, b, (((1,), (1,)), ((), ())), preferred_element_type=F32)


def _dot_tn(a, b):
    return lax.dot_general(a, b, (((0,), (0,)), ((), ())), preferred_element_type=F32)


def _group_mean_sq(x, gmat):
    sq = x * x
    hi = sq.astype(BF16)
    lo = (sq - hi.astype(F32)).astype(BF16)
    return _dot(hi, gmat) + _dot(lo, gmat)


def _memkv_kernel(mem_ref, g_ref, w_ref, kg_ref, gmat_ref, kbd_ref, vbd_ref):
    hn = _rms(mem_ref[0], g_ref[...]).astype(BF16)
    kv = _dot(hn, w_ref[...])
    k, v = kv[:, :M_W], kv[:, M_W:]
    kn = k * lax.rsqrt(_group_mean_sq(k, gmat_ref[...]) + EPS) * kg_ref[...]
    lane_head = lax.broadcasted_iota(I32, (1, M_W), 1) // M_HEAD_DIM
    for h in range(M_HEADS):
        keep = lane_head == h
        kbd_ref[0, h * N_MEM:(h + 1) * N_MEM, :] = jnp.where(keep, kn, 0.0).astype(BF16)
        vbd_ref[0, h * N_MEM:(h + 1) * N_MEM, :] = jnp.where(keep, v, 0.0).astype(BF16)


def _memory_kv(mem, g, w_kv, k_g, gmat):
    nb = mem.shape[0]
    out = jax.ShapeDtypeStruct((nb, M_HEADS * N_MEM, M_W), BF16)
    return pl.pallas_call(
        _memkv_kernel,
        out_shape=(out, out),
        grid=(nb,),
        in_specs=[
            pl.BlockSpec((1, N_MEM, D_MODEL), lambda b: (b, 0, 0)),
            pl.BlockSpec((1, D_MODEL), lambda b: (0, 0)),
            pl.BlockSpec((D_MODEL, 2 * M_W), lambda b: (0, 0)),
            pl.BlockSpec((1, M_W), lambda b: (0, 0)),
            pl.BlockSpec((M_W, M_W), lambda b: (0, 0)),
        ],
        out_specs=(pl.BlockSpec((1, M_HEADS * N_MEM, M_W), lambda b: (b, 0, 0)),) * 2,
        compiler_params=_cparams(("parallel",)),
        name="memory_kv",
    )(mem, g.reshape(1, -1), w_kv.astype(BF16), jnp.tile(k_g, M_HEADS).reshape(1, -1), gmat)


def _memory_attention(mq, kbd, vbd, gmat, qg):
    qn = mq * lax.rsqrt(_group_mean_sq(mq, gmat) + EPS) * (qg * (M_HEAD_DIM ** -0.5 * LOG2E))
    s = _dot_nt(qn.astype(BF16), kbd)
    ps = []
    for h in range(M_HEADS):
        sh = s[:, h * N_MEM:(h + 1) * N_MEM]
        e = jnp.exp2(sh - jnp.max(sh, axis=-1, keepdims=True))
        ps.append((e / jnp.sum(e, axis=-1, keepdims=True)).astype(BF16))
    return _dot(jnp.concatenate(ps, axis=-1), vbd)


def _a_proj_kernel(x_ref, g_ref, w_ref, wvot_ref, wif_ref, wift_ref, bc_ref, br_ref,
                   q_ref, k_ref, vt_ref, ot_ref, mq_ref, gc_ref, gr_ref):
    hn = _rms(x_ref[...], g_ref[...]).astype(BF16)
    nq = A_HEADS * A_QK_PAD
    nv = A_HEADS * A_V_ROWS
    q_ref[...] = _dot(hn, w_ref[:, :nq]).astype(BF16)
    k_ref[...] = (_dot(hn, w_ref[:, nq:2 * nq]) * (A_QK_DIM ** -0.5)).astype(BF16)
    mq_ref[...] = _dot(hn, w_ref[:, 2 * nq:])
    vt = _dot_nt(wvot_ref[:nv, :], hn)
    ones_row = lax.broadcasted_iota(I32, (nv, 1), 0) % A_V_ROWS == A_V_DIM
    vt_ref[...] = jnp.where(ones_row, 1.0, vt).astype(BF16)
    ot_ref[...] = _dot_nt(wvot_ref[nv:, :], hn)
    gc_ref[...] = _dot(hn, wif_ref[...])[:, :2 * A_HEADS] + bc_ref[...]
    gr_ref[...] = _dot_nt(wift_ref[...], hn) + br_ref[...]


def _a_projection(x2, g, w_main, w_vo_t, w_if, gate_bias):
    t = x2.shape[0]
    nq, nv = A_HEADS * A_QK_PAD, A_HEADS * A_V_ROWS
    ng = 2 * A_HEADS
    wif_pad = jnp.pad(w_if, ((0, 0), (0, LANE - ng))).astype(BF16)
    row = lambda n: pl.BlockSpec((ROW_TILE, n), lambda i: (i, 0))
    col = lambda n: pl.BlockSpec((n, ROW_TILE), lambda i: (0, i))
    full = lambda a: pl.BlockSpec(a.shape, lambda i: (0,) * a.ndim)
    args = (x2, g.reshape(1, -1), w_main, w_vo_t, wif_pad, w_if.T.astype(BF16),
            gate_bias.reshape(1, ng), gate_bias.reshape(ng, 1))
    return pl.pallas_call(
        _a_proj_kernel,
        out_shape=(jax.ShapeDtypeStruct((t, nq), BF16), jax.ShapeDtypeStruct((t, nq), BF16),
                   jax.ShapeDtypeStruct((nv, t), BF16), jax.ShapeDtypeStruct((nv, t), F32),
                   jax.ShapeDtypeStruct((t, M_W), F32), jax.ShapeDtypeStruct((t, ng), F32),
                   jax.ShapeDtypeStruct((ng, t), F32)),
        grid=(t // ROW_TILE,),
        in_specs=[row(D_MODEL)] + [full(a) for a in args[1:]],
        out_specs=(row(nq), row(nq), col(nv), col(nv), row(M_W), row(ng), col(ng)),
        compiler_params=_cparams(("parallel",)),
        name="mlstm_in_proj",
    )(*args)


def _log_sigmoid(f):
    return jnp.minimum(f, 0.0) - jnp.log(1.0 + jnp.exp(-jnp.abs(f)))


def _bf16_terms(a):
    hi = a.astype(BF16)
    rest = a - hi.astype(F32)
    mid = rest.astype(BF16)
    return hi, mid, (rest - mid.astype(F32)).astype(BF16)


def _select_dot(sel, a):
    sel = sel.astype(BF16)
    return sum(_dot(sel, term) for term in _bf16_terms(a))


def _dot_select(a, sel):
    sel = sel.astype(BF16)
    return sum(_dot(term, sel) for term in _bf16_terms(a))


def _select_dot_nt(a, sel):
    sel = sel.astype(BF16)
    return sum(_dot_nt(term, sel) for term in _bf16_terms(a))


def _chunk_gates(gc, gr):
    L = gc.shape[0]
    r = lax.broadcasted_iota(I32, (L, L), 0)
    c = lax.broadcasted_iota(I32, (L, L), 1)
    lower = (c <= r).astype(F32)
    is_f_col = lax.broadcasted_iota(I32, gc.shape, 1) >= A_HEADS
    is_f_row = lax.broadcasted_iota(I32, gr.shape, 0) >= A_HEADS
    lf_c = jnp.where(is_f_col, _log_sigmoid(gc), 0.0)
    lf_r = jnp.where(is_f_row, _log_sigmoid(gr), 0.0)
    lf_c = jnp.concatenate([lf_c, jnp.zeros((L, LANE - gc.shape[1]), F32)], axis=1)
    cum_c = _select_dot(lower, lf_c)[:, :gc.shape[1]]
    cum_r = _select_dot_nt(lf_r, lower)
    return jnp.where(is_f_col, cum_c, gc), jnp.where(is_f_row, cum_r, gr)


def _prefix_max_lanes(x):
    lane = lax.broadcasted_iota(I32, x.shape, 1)
    shift = 1
    while shift < x.shape[1]:
        x = jnp.maximum(x, jnp.where(lane >= shift, pltpu.roll(x, shift, axis=1), -jnp.inf))
        shift *= 2
    return x


def _mlstm_kernel(q_ref, k_ref, vt_ref, ot_ref, gc_ref, gr_ref, hg_ref, out_ref, c_ref, m_ref):
    L = MLSTM_CHUNK

    @pl.when(pl.program_id(1) == 0)
    def _():
        c_ref[...] = jnp.zeros_like(c_ref)
        m_ref[...] = jnp.zeros_like(m_ref)

    src = lax.broadcasted_iota(I32, (L, L), 0)
    tgt = lax.broadcasted_iota(I32, (L, L), 1)
    real = lax.broadcasted_iota(I32, (A_V_ROWS, 1), 0) < A_V_DIM
    gcol, grow = _chunk_gates(gc_ref[...], gr_ref[...])
    for hd in range(A_HEADS):
        qk = slice(hd * A_QK_PAD, (hd + 1) * A_QK_PAD)
        vv = slice(hd * A_V_ROWS, (hd + 1) * A_V_ROWS)
        q, k, vt = q_ref[0, :, qk], k_ref[0, :, qk], vt_ref[vv, :]
        u_c = gcol[:, hd:hd + 1] - gcol[:, A_HEADS + hd:A_HEADS + hd + 1]
        g_r = grow[A_HEADS + hd:A_HEADS + hd + 1, :]
        u_r = grow[hd:hd + 1, :] - g_r
        g_last = g_r[:, L - 1:L]
        m_prev = m_ref[hd, 0:1, 0:1]
        c_prev = c_ref[hd]

        run_max = jnp.maximum(_prefix_max_lanes(jnp.broadcast_to(u_r, (SUBLANE, L)))[0:1, :], m_prev)
        m_t = g_r + run_max
        inter = jnp.exp(m_prev - run_max)
        decay_t = jnp.where(src <= tgt, jnp.exp(u_c - run_max), 0.0)
        p_t = (decay_t * _dot_nt(k, q)).astype(BF16)
        num_t = inter * _dot_nt(c_prev.astype(BF16), q) + _dot(vt, p_t)
        den = num_t[A_V_DIM:A_V_DIM + 1, :]
        h_t = jnp.where(real, num_t / jnp.maximum(jnp.abs(den), jnp.exp(-m_t)), 0.0)
        scale = lax.rsqrt(jnp.sum(h_t * h_t, axis=0, keepdims=True) * (1.0 / A_V_DIM) + EPS)
        out_t = h_t * scale * hg_ref[vv, :] * jax.nn.sigmoid(ot_ref[vv, :])
        out_t = jnp.concatenate([out_t, jnp.zeros((A_V_PAD - A_V_ROWS, L), F32)], axis=0)
        out_ref[0, :, hd * A_V_PAD:(hd + 1) * A_V_PAD] = out_t.T.astype(BF16)

        w_r = g_last + u_r
        m_new = jnp.maximum(g_last + m_prev, jnp.max(w_r, axis=-1, keepdims=True))
        ev_t = (jnp.exp(w_r - m_new) * vt.astype(F32)).astype(BF16)
        c_ref[hd] = jnp.exp(g_last + m_prev - m_new) * c_prev + _dot(ev_t, k)
        m_ref[hd] = jnp.broadcast_to(m_new, m_ref.shape[1:])


def _mlstm(q, k, vt, ot, gcol, grow, head_g):
    nb, s, _ = q.shape
    L = MLSTM_CHUNK
    nv, nvt = A_HEADS * A_V_PAD, A_HEADS * A_V_ROWS
    blk = lambda w: pl.BlockSpec((1, L, w), lambda b, c: (b, c, 0))
    col = lambda n: pl.BlockSpec((n, L), lambda b, c: (0, b * (s // L) + c))
    return pl.pallas_call(
        _mlstm_kernel,
        out_shape=jax.ShapeDtypeStruct((nb, s, nv), BF16),
        grid=(nb, s // L),
        in_specs=[blk(A_HEADS * A_QK_PAD), blk(A_HEADS * A_QK_PAD), col(nvt), col(nvt),
                  pl.BlockSpec((L, 2 * A_HEADS), lambda b, c: (b * (s // L) + c, 0)), col(2 * A_HEADS),
                  pl.BlockSpec((nvt, L), lambda b, c: (0, 0))],
        out_specs=blk(nv),
        scratch_shapes=[pltpu.VMEM((A_HEADS, A_V_ROWS, A_QK_PAD), F32), pltpu.VMEM((A_HEADS, SUBLANE, LANE), F32)],
        compiler_params=_cparams(("parallel", "arbitrary")),
        name="mlstm_chunkwise",
    )(q, k, vt, ot, gcol, grow, head_g)


def _mix_out_kernel(x_ref, h_ref, mq_ref, kbd_ref, vbd_ref, gmat_ref, qg_ref, w1_ref, w2_ref, out_ref):
    mo = _memory_attention(mq_ref[...], kbd_ref[0], vbd_ref[0], gmat_ref[...], qg_ref[...])
    out_ref[...] = x_ref[...] + _dot(h_ref[...], w1_ref[...]) + _dot(mo.astype(BF16), w2_ref[...])


def _mix_out(x2, h2, mq, kbd, vbd, gmat, qg, w_main, w_mem, seq):
    t = x2.shape[0]
    tm = ROW_TILE
    row = lambda n: pl.BlockSpec((tm, n), lambda i: (i, 0))
    full = lambda a: pl.BlockSpec(a.shape, lambda i: (0,) * a.ndim)
    per_batch = pl.BlockSpec((1,) + kbd.shape[1:], lambda i: ((i * tm) // seq, 0, 0))
    qg_t = jnp.tile(qg, M_HEADS).reshape(1, -1)
    return pl.pallas_call(
        _mix_out_kernel,
        out_shape=jax.ShapeDtypeStruct((t, D_MODEL), F32),
        grid=(t // tm,),
        in_specs=[row(D_MODEL), row(h2.shape[1]), row(M_W), per_batch, per_batch,
                  full(gmat), full(qg_t), full(w_main), full(w_mem)],
        out_specs=row(D_MODEL),
        compiler_params=_cparams(("parallel",)),
        name="mixer_out_proj",
    )(x2, h2, mq, kbd, vbd, gmat, qg_t, w_main, w_mem)


def _swiglu_accumulate(acc_ref, h_ref, x, wg_ref, wu_ref, wd_ref):
    for c in range(wg_ref.shape[-1] // FFN_CHUNK):
        cols = slice(c * FFN_CHUNK, (c + 1) * FFN_CHUNK)
        gate = _dot(x, wg_ref[:, cols])
        up = _dot(x, wu_ref[:, cols])
        h_ref[:, cols] = (jax.nn.silu(gate) * up).astype(BF16)
    acc_ref[...] += _dot(h_ref[...], wd_ref[...])


def _ffn_kernel(x_ref, g_ref, wg_ref, wu_ref, wd_ref, out_ref, hn_ref, acc_ref, h_ref):
    j = pl.program_id(1)

    @pl.when(j == 0)
    def _():
        hn_ref[...] = _rms(x_ref[...], g_ref[...]).astype(BF16)
        acc_ref[...] = jnp.zeros_like(acc_ref)

    _swiglu_accumulate(acc_ref, h_ref, hn_ref[...], wg_ref, wu_ref, wd_ref)

    @pl.when(j == pl.num_programs(1) - 1)
    def _():
        out_ref[...] = x_ref[...] + acc_ref[...]


def _dense_ffn(x2, g, w_gate_up, w_down):
    t = x2.shape[0]
    tm, tf = FFN_ROW_TILE, FFN_FF_TILE
    nj = D_FF // tf
    return pl.pallas_call(
        _ffn_kernel,
        out_shape=jax.ShapeDtypeStruct((t, D_MODEL), F32),
        grid=(t // tm, nj),
        in_specs=[pl.BlockSpec((tm, D_MODEL), lambda i, j: (i, 0)),
                  pl.BlockSpec((1, D_MODEL), lambda i, j: (0, 0)),
                  pl.BlockSpec((D_MODEL, tf), lambda i, j: (0, j)),
                  pl.BlockSpec((D_MODEL, tf), lambda i, j: (0, nj + j)),
                  pl.BlockSpec((tf, D_MODEL), lambda i, j: (j, 0))],
        out_specs=pl.BlockSpec((tm, D_MODEL), lambda i, j: (i, 0)),
        scratch_shapes=[pltpu.VMEM((tm, D_MODEL), BF16), pltpu.VMEM((tm, D_MODEL), F32),
                        pltpu.VMEM((tm, tf), BF16)],
        compiler_params=_cparams(("parallel", "arbitrary")),
        name="dense_swiglu",
    )(x2, g.reshape(1, -1), w_gate_up, w_gate_up, w_down)


def _rope_kernel(pos_ref, inv_ref, sign_ref, cs_ref, sn_ref):
    ang = pos_ref[...].astype(F32) * inv_ref[...]
    cs_ref[...] = jnp.cos(ang)
    sn_ref[...] = jnp.sin(ang) * sign_ref[...]


def _rope_tables(positions):
    t = positions.size
    half = QK_ROPE // 2
    inv = 1.0 / (ROPE_THETA ** (jnp.arange(0, QK_ROPE, 2, dtype=F32) / QK_ROPE))
    pad = jnp.zeros((LANE - QK_ROPE,), F32)
    inv_l = jnp.concatenate([inv, inv, pad]).reshape(1, LANE)
    sign = jnp.concatenate([-jnp.ones((half,), F32), jnp.ones((half,), F32), pad]).reshape(1, LANE)
    out = jax.ShapeDtypeStruct((t, LANE), F32)
    return pl.pallas_call(
        _rope_kernel,
        out_shape=(out, out),
        grid=(t // ROW_TILE,),
        in_specs=[pl.BlockSpec((ROW_TILE, 1), lambda i: (i, 0)),
                  pl.BlockSpec((1, LANE), lambda i: (0, 0)),
                  pl.BlockSpec((1, LANE), lambda i: (0, 0))],
        out_specs=(pl.BlockSpec((ROW_TILE, LANE), lambda i: (i, 0)),) * 2,
        compiler_params=_cparams(("parallel",)),
        name="rope_tables",
    )(positions.reshape(t, 1), inv_l, sign)


def _head_qk_norm_rope(nope, rope, rope_sw, g_nope, g_rope, g_rope_sw, cs, sn, scale):
    ss = jnp.sum(nope * nope, axis=-1, keepdims=True) + jnp.sum(rope * rope, axis=-1, keepdims=True)
    r = lax.rsqrt(ss * (1.0 / B_QK_HEAD) + EPS) * scale
    return nope * r * g_nope, (rope * g_rope * cs + rope_sw * g_rope_sw * sn) * r


def _latent_kv_body(hn, wd_ref, ga_ref, wuk_ref, wuvt_ref, kg_ref, cs_ref, sn_ref, k_ref, vt_ref):
    z = _dot(hn, wd_ref[...])
    c_kv = z[:, :KV_LORA]
    rope, rope_sw = z[:, KV_LORA:KV_LORA + LANE], z[:, KV_LORA + LANE:]
    cn = _rms(c_kv, ga_ref[...]).astype(BF16)
    kv = _dot(cn, wuk_ref[...])
    vt = _dot_nt(wuvt_ref[...], cn)
    tm = vt.shape[1]
    ones_row = (lax.broadcasted_iota(I32, (VT_HEAD_ROWS - V_HEAD, tm), 0) == 0).astype(BF16)
    for h in range(B_HEADS):
        vt_ref[h * VT_HEAD_ROWS:h * VT_HEAD_ROWS + V_HEAD, :] = vt[h * V_HEAD:(h + 1) * V_HEAD].astype(BF16)
        vt_ref[h * VT_HEAD_ROWS + V_HEAD:(h + 1) * VT_HEAD_ROWS, :] = ones_row
    kg = kg_ref[...]
    for h in range(B_HEADS):
        kn, kr = _head_qk_norm_rope(kv[:, h * QK_NOPE:(h + 1) * QK_NOPE], rope, rope_sw,
                                    kg[:, :LANE], kg[:, LANE:2 * LANE], kg[:, 2 * LANE:],
                                    cs_ref[...], sn_ref[...], 1.0)
        k_ref[:, h * B_QK_PAD:h * B_QK_PAD + QK_NOPE] = kn.astype(BF16)
        k_ref[:, h * B_QK_PAD + QK_NOPE:(h + 1) * B_QK_PAD] = kr.astype(BF16)


def _rope_swap(w):
    half = QK_ROPE // 2
    return jnp.concatenate([w[..., half:], w[..., :half]], axis=-1)


def _pad_lanes(w, n=LANE):
    return jnp.pad(w, [(0, 0)] * (w.ndim - 1) + [(0, n - w.shape[-1])])


def _head_gain(g):
    g_rope = g[QK_NOPE:]
    return jnp.concatenate([g[:QK_NOPE], _pad_lanes(g_rope), _pad_lanes(_rope_swap(g_rope))]).reshape(1, -1)


def _query_body(hn, win_ref, ga_ref, wuq_ref, qg_ref, cs_ref, sn_ref, q_ref, mq_ref):
    proj = _dot(hn, win_ref[...])
    mq_ref[...] = proj[:, Q_LORA:]
    qall = _dot(_rms(proj[:, :Q_LORA], ga_ref[...]).astype(BF16), wuq_ref[...])
    qg = qg_ref[...]
    per_head = QK_NOPE + 2 * LANE
    for h in range(B_HEADS):
        base = h * per_head
        qn, qr = _head_qk_norm_rope(qall[:, base:base + QK_NOPE],
                                    qall[:, base + QK_NOPE:base + QK_NOPE + LANE],
                                    qall[:, base + QK_NOPE + LANE:base + per_head],
                                    qg[:, :LANE], qg[:, LANE:2 * LANE], qg[:, 2 * LANE:],
                                    cs_ref[...], sn_ref[...], B_QK_HEAD ** -0.5)
        q_ref[:, h * B_QK_PAD:h * B_QK_PAD + QK_NOPE] = qn.astype(BF16)
        q_ref[:, h * B_QK_PAD + QK_NOPE:(h + 1) * B_QK_PAD] = qr.astype(BF16)


def _mla_proj_kernel(x_ref, gkv_ref, wd_ref, ga_ref, wuk_ref, wuvt_ref, kg_ref,
                     gq_ref, win_ref, gqa_ref, wuq_ref, qg_ref, cs_ref, sn_ref,
                     k_ref, vt_ref, q_ref, mq_ref):
    x = x_ref[...]
    xn = x * lax.rsqrt(jnp.mean(x * x, axis=-1, keepdims=True) + EPS)
    _latent_kv_body((xn * gkv_ref[...]).astype(BF16), wd_ref, ga_ref, wuk_ref, wuvt_ref, kg_ref,
                    cs_ref, sn_ref, k_ref, vt_ref)
    _query_body((xn * gq_ref[...]).astype(BF16), win_ref, gqa_ref, wuq_ref, qg_ref, cs_ref, sn_ref, q_ref, mq_ref)


def _mla_projection(x2, kv_norm, w_dkv, kv_a_norm, w_ukv, k_head_norm,
                    q_norm, w_in, q_a_g, w_uq, q_head_g, cs, sn):
    t = x2.shape[0]
    w_rope = w_dkv[:, KV_LORA:]
    wd = jnp.concatenate([w_dkv[:, :KV_LORA], _pad_lanes(w_rope), _pad_lanes(_rope_swap(w_rope))],
                         axis=1).astype(BF16)
    wu = w_ukv.reshape(KV_LORA, B_HEADS, QK_NOPE + V_HEAD)
    wuk = wu[:, :, :QK_NOPE].reshape(KV_LORA, -1).astype(BF16)
    wuvt = wu[:, :, QK_NOPE:].reshape(KV_LORA, -1).T.astype(BF16)
    wq = w_uq.reshape(Q_LORA, B_HEADS, B_QK_HEAD)
    wq_rope = wq[:, :, QK_NOPE:]
    wq = jnp.concatenate([wq[:, :, :QK_NOPE], _pad_lanes(wq_rope), _pad_lanes(_rope_swap(wq_rope))],
                         axis=-1).reshape(Q_LORA, -1).astype(BF16)
    consts = (kv_norm.reshape(1, -1), wd, kv_a_norm.reshape(1, -1), wuk, wuvt, _head_gain(k_head_norm),
              q_norm.reshape(1, -1), w_in.astype(BF16), q_a_g.reshape(1, -1), wq, _head_gain(q_head_g))
    row = lambda n: pl.BlockSpec((ROW_TILE, n), lambda i: (i, 0))
    full = lambda a: pl.BlockSpec(a.shape, lambda i: (0,) * a.ndim)
    return pl.pallas_call(
        _mla_proj_kernel,
        out_shape=(jax.ShapeDtypeStruct((t, B_HEADS * B_QK_PAD), BF16),
                   jax.ShapeDtypeStruct((B_HEADS * VT_HEAD_ROWS, t), BF16),
                   jax.ShapeDtypeStruct((t, B_HEADS * B_QK_PAD), BF16),
                   jax.ShapeDtypeStruct((t, M_W), F32)),
        grid=(t // ROW_TILE,),
        in_specs=[row(D_MODEL)] + [full(a) for a in consts] + [row(LANE), row(LANE)],
        out_specs=(row(B_HEADS * B_QK_PAD), pl.BlockSpec((B_HEADS * VT_HEAD_ROWS, ROW_TILE), lambda i: (0, i)),
                   row(B_HEADS * B_QK_PAD), row(M_W)),
        compiler_params=_cparams(("parallel",)),
        name="mla_proj",
    )(x2, *consts, cs, sn)


def _attn_kernel(q_ref, k_ref, vt_ref, out_ref):
    tq, tk = ATT_Q_TILE, ATT_K_TILE
    i = pl.program_id(2)

    def block(h, j, carry, masked):
        m, acc = carry
        off = pl.multiple_of(j * tk, tk)
        q = q_ref[0, :, h * B_QK_PAD:(h + 1) * B_QK_PAD]
        st = _dot_nt(k_ref[0, pl.ds(off, tk), h * B_QK_PAD:(h + 1) * B_QK_PAD], q)
        if masked:
            key = lax.broadcasted_iota(I32, (tk, tq), 0)
            qry = lax.broadcasted_iota(I32, (tk, tq), 1)
            st = jnp.where(key <= qry, st, -jnp.inf)
        m_new = jnp.maximum(m, jnp.max(st, axis=0, keepdims=True))
        p = jnp.exp(st - m_new).astype(BF16)
        vt = vt_ref[h * VT_HEAD_ROWS:(h + 1) * VT_HEAD_ROWS, pl.ds(off, tk)]
        return m_new, jnp.exp(m - m_new) * acc + _dot(vt, p)

    heads = range(ATT_HEADS_PER_STEP)
    init = tuple((jnp.full((1, tq), -jnp.inf, F32), jnp.zeros((VT_HEAD_ROWS, tq), F32)) for _ in heads)
    carry = lax.fori_loop(0, i, lambda j, c: tuple(block(h, j, c[h], False) for h in heads), init)
    for h in heads:
        _, acc = block(h, i, carry[h], True)
        out_t = acc[:V_HEAD] / acc[V_HEAD:V_HEAD + 1]
        out_ref[0, :, h * V_HEAD:(h + 1) * V_HEAD] = out_t.T.astype(BF16)


def _causal_attention(q, k, vt, seq):
    nb = q.shape[0]
    g = ATT_HEADS_PER_STEP
    assert ATT_Q_TILE == ATT_K_TILE and B_HEADS % g == 0
    return pl.pallas_call(
        _attn_kernel,
        out_shape=jax.ShapeDtypeStruct((nb, seq, B_HEADS * V_HEAD), BF16),
        grid=(nb, B_HEADS // g, seq // ATT_Q_TILE),
        in_specs=[pl.BlockSpec((1, ATT_Q_TILE, g * B_QK_PAD), lambda b, h, i: (b, i, h)),
                  pl.BlockSpec((1, seq, g * B_QK_PAD), lambda b, h, i: (b, 0, h)),
                  pl.BlockSpec((g * VT_HEAD_ROWS, seq), lambda b, h, i: (h, b))],
        out_specs=pl.BlockSpec((1, ATT_Q_TILE, g * V_HEAD), lambda b, h, i: (b, i, h)),
        compiler_params=_cparams(("parallel", "parallel", "arbitrary")),
        name="causal_attention",
    )(q, k, vt)


def _router_kernel(x_ref, g_ref, wr_hi_ref, wr_lo_ref, hn_ref, idx_ref, gate_ref, count_ref):
    hn = _rms(x_ref[...], g_ref[...])
    hn_hi = hn.astype(BF16)
    hn_ref[...] = hn_hi
    hn_lo = (hn - hn_hi.astype(F32)).astype(BF16)
    logits = _dot(hn_hi, wr_hi_ref[...]) + _dot(hn_hi, wr_lo_ref[...]) + _dot(hn_lo, wr_hi_ref[...])
    lane = lax.broadcasted_iota(I32, logits.shape, 1)
    logits = jnp.where(lane < N_EXPERTS, logits, -jnp.inf)
    v1 = jnp.max(logits, axis=-1, keepdims=True)
    i1 = jnp.min(jnp.where(logits == v1, lane, LANE), axis=-1, keepdims=True)
    rest = jnp.where(lane == i1, -jnp.inf, logits)
    v2 = jnp.max(rest, axis=-1, keepdims=True)
    i2 = jnp.min(jnp.where(rest == v2, lane, LANE), axis=-1, keepdims=True)
    e2 = jnp.exp(v2 - v1)
    den = 1.0 + e2
    idx_ref[...] = jnp.where(lane == 0, i1, jnp.where(lane == 1, i2, 0))
    record = jnp.zeros(logits.shape, F32)
    for k, gate in enumerate((1.0 / den, e2 / den)):
        hi = gate.astype(BF16).astype(F32)
        mid = (gate - hi).astype(BF16).astype(F32)
        for part, term in enumerate((hi, mid, gate - hi - mid)):
            record = jnp.where(lane == GATE_TERMS * k + part, term, record)
    record = jnp.where(lane == 2 * GATE_TERMS, i1.astype(F32), record)
    record = jnp.where(lane == 2 * GATE_TERMS + 1, i2.astype(F32), record)
    gate_ref[...] = record.astype(BF16)
    pairs = (lane == i1).astype(F32) + (lane == i2).astype(F32)
    count_ref[...] = jnp.broadcast_to(jnp.sum(pairs, axis=0, keepdims=True), count_ref.shape)


def _router(x2, g, w_router):
    t = x2.shape[0]
    row = lambda n: pl.BlockSpec((RANK_TILE, n), lambda i: (i, 0))
    wr = _pad_lanes(w_router)
    wr_hi = wr.astype(BF16)
    wr_lo = (wr - wr_hi.astype(F32)).astype(BF16)
    return pl.pallas_call(
        _router_kernel,
        out_shape=(jax.ShapeDtypeStruct((t, D_MODEL), BF16), jax.ShapeDtypeStruct((t, LANE), I32),
                   jax.ShapeDtypeStruct((t, LANE), BF16), jax.ShapeDtypeStruct((SUBLANE * (t // RANK_TILE), LANE), F32)),
        grid=(t // RANK_TILE,),
        in_specs=[row(D_MODEL), pl.BlockSpec((1, D_MODEL), lambda i: (0, 0)),
                  pl.BlockSpec(wr.shape, lambda i: (0, 0)), pl.BlockSpec(wr.shape, lambda i: (0, 0))],
        out_specs=(row(D_MODEL), row(LANE), row(LANE), pl.BlockSpec((SUBLANE, LANE), lambda i: (i, 0))),
        compiler_params=_cparams(("parallel",)),
        name="moe_router",
    )(x2, g.reshape(1, -1), wr_hi, wr_lo)


def _aligned(count):
    return jnp.ceil(count * (1.0 / RUN_ALIGN)) * RUN_ALIGN


def _rank_kernel(idx_ref, counts_ref, lpos_ref, lposr_ref, tile_ref, runs_ref, run_ref, start_ref):
    blk = pl.program_id(0)
    tb = RANK_TILE
    lane = lax.broadcasted_iota(I32, (tb, LANE), 1)
    idx = idx_ref[...]
    oh0 = (lane == idx[:, 0:1]).astype(F32)
    oh1 = (lane == idx[:, 1:2]).astype(F32)
    both = oh0 + oh1
    run_len = _aligned(jnp.sum(both, axis=0, keepdims=True))
    r = lax.broadcasted_iota(I32, (LANE, LANE), 0)
    c = lax.broadcasted_iota(I32, (LANE, LANE), 1)
    before = (r < c).astype(F32)

    @pl.when(blk == 0)
    def _():
        sizes = jnp.sum(_aligned(counts_ref[...]), axis=0, keepdims=True) * (1.0 / SUBLANE)
        tiles = jnp.ceil(sizes * (1.0 / MOE_ROW_TILE))
        tile_start = _dot_select(tiles, before)
        start_ref[...] = tile_start * MOE_ROW_TILE
        tile_end = tile_start + tiles
        n_col = r.astype(F32)
        ended = ((n_col >= tile_end) & (c < N_EXPERTS)).astype(F32)
        expert = jnp.sum(ended, axis=-1, keepdims=True)
        total = jnp.max(tile_end, axis=-1, keepdims=True)
        as_column = lambda row: jnp.sum(jnp.where(r == c, row, 0.0), axis=-1, keepdims=True)
        tail_start = as_column(start_ref[...] + sizes)
        tail_len = as_column(tiles * MOE_ROW_TILE - sizes)
        mine = expert == c.astype(F32)
        group_rows = jnp.sum(jnp.where(mine, sizes, 0.0), axis=-1, keepdims=True)
        group_tile = jnp.sum(jnp.where(mine, tile_start, 0.0), axis=-1, keepdims=True)
        used_rows = jnp.clip(group_rows - (n_col[:, 0:1] - group_tile) * MOE_ROW_TILE, 0.0, MOE_ROW_TILE)
        col = lax.broadcasted_iota(I32, tile_ref.shape, 1)
        tile_ref[...] = jnp.where(col == 0, expert, jnp.where(col == 1, total, jnp.where(
            col == 2, tail_start, jnp.where(col == 3, tail_len, used_rows)))).astype(I32)
        run_ref[...] = jnp.zeros_like(run_ref)
        runs_ref[...] = jnp.zeros_like(runs_ref)

    rr = lax.broadcasted_iota(I32, (tb, tb), 0)
    cc = lax.broadcasted_iota(I32, (tb, tb), 1)
    strict = (cc < rr).astype(BF16)
    local = _dot_select(run_len, before)
    base = _dot(strict, both.astype(BF16)) + local
    p0 = jnp.sum(oh0 * base, axis=-1, keepdims=True)
    p1 = jnp.sum(oh1 * base, axis=-1, keepdims=True)
    posf = jnp.where(lane == 0, p0, jnp.where(lane == 1, p1, 0.0))
    lpos_ref[...] = posf.astype(I32)
    pick = (lax.broadcasted_iota(I32, (SUBLANE, LANE), 0) == lax.broadcasted_iota(I32, (SUBLANE, LANE), 1)).astype(F32)
    lposr_ref[...] = sum(_dot_nt(pick.astype(BF16), term) for term in _bf16_terms(posf)).astype(I32)
    n = runs_ref.shape[0] // 3
    mine = lax.broadcasted_iota(I32, (n, LANE), 0) == blk
    for k, value in enumerate((local, run_ref[...] + start_ref[...], run_len)):
        runs_ref[k * n:(k + 1) * n, :] = jnp.where(mine, value.astype(I32), runs_ref[k * n:(k + 1) * n, :])
    run_ref[...] += run_len


def _rank(idx, counts):
    t = idx.shape[0]
    nblk = t // RANK_TILE
    const = lambda shape: pl.BlockSpec(shape, lambda i: (0, 0))
    return pl.pallas_call(
        _rank_kernel,
        out_shape=(jax.ShapeDtypeStruct((t, LANE), I32), jax.ShapeDtypeStruct((SUBLANE, t), I32),
                   jax.ShapeDtypeStruct((LANE, TILE_TABLE_COLS), I32), jax.ShapeDtypeStruct((3 * nblk, LANE), I32)),
        grid=(nblk,),
        in_specs=[pl.BlockSpec((RANK_TILE, LANE), lambda i: (i, 0)), const(counts.shape)],
        out_specs=(pl.BlockSpec((RANK_TILE, LANE), lambda i: (i, 0)),
                   pl.BlockSpec((SUBLANE, RANK_TILE), lambda i: (0, i)),
                   const((LANE, TILE_TABLE_COLS)), const((3 * nblk, LANE))),
        scratch_shapes=[pltpu.VMEM((1, LANE), F32), pltpu.VMEM((1, LANE), F32)],
        compiler_params=_cparams(("arbitrary",)),
        name="moe_rank",
    )(idx, counts)


def _for_each_piece(length, fn):
    for size in RUN_PIECES:
        @pl.when((length & size) != 0)
        def _(size=size):
            fn(pl.multiple_of(length & (-2 * size), RUN_ALIGN), size)


def _for_each_run_piece(blk, loc_ref, dst_ref, len_ref, fn):
    for e in range(N_EXPERTS):
        k = blk * N_EXPERTS + e
        loc, dst = loc_ref[k], dst_ref[k]
        _for_each_piece(len_ref[k], lambda done, size, loc=loc, dst=dst: fn(
            pl.multiple_of(loc + done, RUN_ALIGN), pl.multiple_of(dst + done, RUN_ALIGN), size))


def _dispatch_kernel(loc_ref, dst_ref, len_ref, tail_ref, tail_len_ref, ntiles_ref, hn_ref, gate_ref, lposr_ref,
                     xs_ref, local_ref, zero_ref, sems):
    blk = pl.program_id(0)

    @pl.when(blk == 0)
    def _():
        sem = sems.at[2]
        zero_ref[...] = jnp.zeros_like(zero_ref)

        def fill(e):
            start = tail_ref[e]
            return lambda done, size: pltpu.make_async_copy(
                zero_ref.at[pl.ds(0, size), :], xs_ref.at[pl.ds(pl.multiple_of(start + done, RUN_ALIGN), size), :], sem)

        def fill_tile(i):
            return pltpu.make_async_copy(
                zero_ref, xs_ref.at[pl.ds(pl.multiple_of(i * RANK_TILE, RANK_TILE), RANK_TILE), :], sem)

        def unused_tiles(action):
            def body(i, carry):
                action(fill_tile(i))
                return carry
            lax.fori_loop(ntiles_ref[0] * (MOE_ROW_TILE // RANK_TILE), xs_ref.shape[0] // RANK_TILE, body, 0)

        for e in range(N_EXPERTS):
            _for_each_piece(tail_len_ref[e], lambda *a, e=e: fill(e)(*a).start())
        unused_tiles(lambda copy: copy.start())
        for e in range(N_EXPERTS):
            _for_each_piece(tail_len_ref[e], lambda *a, e=e: fill(e)(*a).wait())
        unused_tiles(lambda copy: copy.wait())

    rows = lax.broadcasted_iota(I32, (LOCAL_ROWS, 1), 0)
    sel = jnp.where(lposr_ref[0:1, :] == rows, 1.0, jnp.where(lposr_ref[1:2, :] == rows, 1.0, 0.0)).astype(BF16)

    def step(local_ref, sem, other_ref, other_sem):
        local_ref[:, :D_MODEL] = _dot(sel, hn_ref[...]).astype(BF16)
        local_ref[:, D_MODEL:] = _dot(sel, gate_ref[...]).astype(BF16)

        def copy(buf, buf_sem):
            return lambda loc, dst, size: pltpu.make_async_copy(
                buf.at[pl.ds(loc, size), :], xs_ref.at[pl.ds(dst, size), :], buf_sem)

        _for_each_run_piece(blk, loc_ref, dst_ref, len_ref, lambda *a: copy(local_ref, sem)(*a).start())

        @pl.when(blk > 0)
        def _():
            _for_each_run_piece(blk - 1, loc_ref, dst_ref, len_ref, lambda *a: copy(other_ref, other_sem)(*a).wait())

        @pl.when(blk == pl.num_programs(0) - 1)
        def _():
            _for_each_run_piece(blk, loc_ref, dst_ref, len_ref, lambda *a: copy(local_ref, sem)(*a).wait())

    @pl.when(blk % 2 == 0)
    def _():
        step(local_ref.at[0], sems.at[0], local_ref.at[1], sems.at[1])

    @pl.when(blk % 2 == 1)
    def _():
        step(local_ref.at[1], sems.at[1], local_ref.at[0], sems.at[0])


def _dispatch(runs, tails, hn, gates, lpos_rows, rows):
    nblk = hn.shape[0] // RANK_TILE
    tok = lambda n: pl.BlockSpec((RANK_TILE, n), lambda b, *_: (b, 0))
    return pl.pallas_call(
        _dispatch_kernel,
        out_shape=jax.ShapeDtypeStruct((rows, XS_WIDTH), BF16),
        grid_spec=pltpu.PrefetchScalarGridSpec(
            num_scalar_prefetch=6,
            grid=(nblk,),
            in_specs=[tok(D_MODEL), tok(LANE), pl.BlockSpec((SUBLANE, RANK_TILE), lambda b, *_: (0, b))],
            out_specs=pl.BlockSpec(memory_space=pl.ANY),
            scratch_shapes=[pltpu.VMEM((2, LOCAL_ROWS, XS_WIDTH), BF16), pltpu.VMEM((RANK_TILE, XS_WIDTH), BF16),
                            pltpu.SemaphoreType.DMA((3,))],
        ),
        compiler_params=_cparams(("arbitrary",)),
        name="moe_dispatch",
    )(*runs, *tails, hn, gates, lpos_rows)


def _moe_kernel(expert_ref, ntiles_ref, used_ref, x_ref, wg_ref, wu_ref, wd_ref, out_ref, acc_ref, h_ref):
    i, j = pl.program_id(0), pl.program_id(1)
    active = i < ntiles_ref[0]
    parts = -(-used_ref[i] // MOE_SKIP_ROWS)

    @pl.when(j == 0)
    def _():
        acc_ref[...] = jnp.zeros_like(acc_ref)

    for n in range(1, MOE_ROW_TILE // MOE_SKIP_ROWS + 1):
        @pl.when(active & (parts == n))
        def _(n=n):
            lead = pl.ds(0, n * MOE_SKIP_ROWS)
            _swiglu_accumulate(acc_ref.at[lead, :], h_ref.at[lead, :], x_ref[:n * MOE_SKIP_ROWS, :D_MODEL],
                               wg_ref.at[0], wu_ref.at[0], wd_ref.at[0])

    @pl.when(j == pl.num_programs(1) - 1)
    def _():
        gs = x_ref[:, D_MODEL:].astype(F32)
        gate = [sum(gs[:, GATE_TERMS * k + n:GATE_TERMS * k + n + 1] for n in range(GATE_TERMS)) for k in range(TOP_K)]
        first = gs[:, TOP_K * GATE_TERMS:TOP_K * GATE_TERMS + 1] == expert_ref[i].astype(F32)
        row_gate = jnp.where(first, gate[0], gate[1])
        out_ref[...] = jnp.where(active, acc_ref[...] * row_gate, 0.0).astype(BF16)


def _moe_experts(tile_expert, n_tiles, tile_used, xs, w_gate_up, w_down):
    rows = xs.shape[0]
    tm, tf = MOE_ROW_TILE, FFN_FF_TILE
    nj = D_FF // tf

    def x_map(i, j, e_ref, n_ref, u_ref):
        return jnp.clip(i, 0, jnp.maximum(n_ref[0] - 1, 0)), 0

    def w_idx(i, j, expert_ref, ntiles_ref):
        e = jnp.minimum(expert_ref[i], N_EXPERTS - 1)
        return e, jnp.where(i < ntiles_ref[0], j, nj - 1)

    def w_gate_map(i, j, e_ref, n_ref, u_ref):
        e, jj = w_idx(i, j, e_ref, n_ref)
        return e, 0, jj

    def w_up_map(i, j, e_ref, n_ref, u_ref):
        e, jj = w_idx(i, j, e_ref, n_ref)
        return e, 0, nj + jj

    def w_down_map(i, j, e_ref, n_ref, u_ref):
        e, jj = w_idx(i, j, e_ref, n_ref)
        return e, jj, 0

    return pl.pallas_call(
        _moe_kernel,
        out_shape=jax.ShapeDtypeStruct((rows, D_MODEL), BF16),
        grid_spec=pltpu.PrefetchScalarGridSpec(
            num_scalar_prefetch=3,
            grid=(rows // tm, nj),
            in_specs=[pl.BlockSpec((tm, XS_WIDTH), x_map),
                      pl.BlockSpec((1, D_MODEL, tf), w_gate_map),
                      pl.BlockSpec((1, D_MODEL, tf), w_up_map),
                      pl.BlockSpec((1, tf, D_MODEL), w_down_map)],
            out_specs=pl.BlockSpec((tm, D_MODEL), lambda i, j, *_: (i, 0)),
            scratch_shapes=[pltpu.VMEM((tm, D_MODEL), F32), pltpu.VMEM((tm, tf), BF16)],
        ),
        compiler_params=_cparams(("arbitrary", "arbitrary")),
        name="moe_experts",
    )(tile_expert, n_tiles, tile_used, xs, w_gate_up, w_gate_up, w_down)


def _combine_kernel(loc_ref, dst_ref, len_ref, x_ref, lpos_ref, ys_ref, out_ref, local_ref, sems):
    blk = pl.program_id(0)
    rows = lax.broadcasted_iota(I32, (1, LOCAL_ROWS), 1)
    lpos = lpos_ref[...]
    sel = jnp.where(lpos[:, 0:1] == rows, 1.0, jnp.where(lpos[:, 1:2] == rows, 1.0, 0.0)).astype(BF16)

    def fetch(block, buf, buf_sem, action):
        _for_each_run_piece(block, loc_ref, dst_ref, len_ref, lambda loc, dst, size: action(pltpu.make_async_copy(
            ys_ref.at[pl.ds(dst, size), :], buf.at[pl.ds(loc, size), :], buf_sem)))

    def start_fetch(block, buf, buf_sem):
        buf[...] = jnp.zeros_like(buf)
        fetch(block, buf, buf_sem, lambda copy: copy.start())

    def step(buf, buf_sem, other, other_sem):
        @pl.when(blk == 0)
        def _():
            start_fetch(blk, buf, buf_sem)

        @pl.when(blk + 1 < pl.num_programs(0))
        def _():
            start_fetch(blk + 1, other, other_sem)

        fetch(blk, buf, buf_sem, lambda copy: copy.wait())
        out_ref[...] = x_ref[...] + _dot(sel, buf[...])

    @pl.when(blk % 2 == 0)
    def _():
        step(local_ref.at[0], sems.at[0], local_ref.at[1], sems.at[1])

    @pl.when(blk % 2 == 1)
    def _():
        step(local_ref.at[1], sems.at[1], local_ref.at[0], sems.at[0])


def _combine(runs, x2, lpos, ys):
    tok = lambda n: pl.BlockSpec((RANK_TILE, n), lambda b, *_: (b, 0))
    return pl.pallas_call(
        _combine_kernel,
        out_shape=jax.ShapeDtypeStruct(x2.shape, F32),
        grid_spec=pltpu.PrefetchScalarGridSpec(
            num_scalar_prefetch=3,
            grid=(x2.shape[0] // RANK_TILE,),
            in_specs=[tok(D_MODEL), tok(LANE), pl.BlockSpec(memory_space=pl.ANY)],
            out_specs=tok(D_MODEL),
            scratch_shapes=[pltpu.VMEM((2, LOCAL_ROWS, D_MODEL), BF16), pltpu.SemaphoreType.DMA((2,))],
        ),
        compiler_params=_cparams(("arbitrary",)),
        name="moe_combine",
    )(*runs, x2, lpos, ys)


def _moe_ffn(x2, g, w_router, w_gate_up, w_down):
    t = x2.shape[0]
    nblk = t // RANK_TILE
    hn, idx, gates, counts = _router(x2, g, w_router)
    lpos, lpos_rows, tile_info, run_tab = _rank(idx, counts)
    runs = tuple(run_tab.reshape(3, nblk, LANE)[:, :, :N_EXPERTS].reshape(3, -1))
    rows = t * TOP_K + nblk * N_EXPERTS * (RUN_ALIGN - 1) + N_EXPERTS * (MOE_ROW_TILE - 1)
    rows = -(-rows // MOE_ROW_TILE) * MOE_ROW_TILE
    assert rows // MOE_ROW_TILE <= LANE
    assert MOE_ROW_TILE % RANK_TILE == 0 and rows % RANK_TILE == 0
    xs = _dispatch(runs, (tile_info[:, 2], tile_info[:, 3], tile_info[:1, 1]), hn, gates, lpos_rows, rows)
    ys = _moe_experts(tile_info[:, 0], tile_info[:1, 1], tile_info[:, 4], xs,
                      w_gate_up.astype(BF16), w_down.astype(BF16))
    return _combine(runs, x2, lpos, ys)


def _pad_heads(w, heads, dim, pad):
    w = w.reshape(w.shape[:-1] + (heads, dim))
    return _pad_lanes(w, pad).reshape(w.shape[:-2] + (heads * pad,))


def kernel(x, mem, positions, a_norm, a_w_in, a_gate_bias, a_head_norm, a_w_out, b_norm, b_w_in, b_q_a_norm, b_w_uq, b_q_head_norm, b_w_out, kv_norm, w_dkv, kv_a_norm, w_ukv, k_head_norm, mem_norm, mem_w_kv, mem_q_norm, mem_k_norm, ffn_norm, dense_w_gate_up, dense_w_down, moe_router, moe_w_gate_up, moe_w_down):
    nb, seq, _ = x.shape
    t = nb * seq
    x2 = x.reshape(t, D_MODEL)
    gmat = jnp.kron(jnp.eye(M_HEADS, dtype=F32), jnp.full((M_HEAD_DIM, M_HEAD_DIM), 1.0 / M_HEAD_DIM, F32)).astype(BF16)

    kbd0, vbd0 = _memory_kv(mem, mem_norm[0], mem_w_kv[0], mem_k_norm[0], gmat)
    w_in = a_w_in[0]
    qk_w, v_w = A_HEADS * A_QK_DIM, A_HEADS * A_V_DIM
    o0, o1, o2, o3, o4 = qk_w, 2 * qk_w, 2 * qk_w + v_w, 2 * qk_w + 2 * v_w, 2 * qk_w + 2 * v_w + 2 * A_HEADS
    w_main = jnp.concatenate([
        _pad_heads(w_in[:, :o0], A_HEADS, A_QK_DIM, A_QK_PAD),
        _pad_heads(w_in[:, o0:o1], A_HEADS, A_QK_DIM, A_QK_PAD),
        w_in[:, o4:]], axis=1).astype(BF16)
    w_vo_t = jnp.concatenate([
        _pad_heads(w_in[:, o1:o2], A_HEADS, A_V_DIM, A_V_ROWS),
        _pad_heads(w_in[:, o2:o3], A_HEADS, A_V_DIM, A_V_ROWS)], axis=1).T.astype(BF16)
    q, k, vt, ot, mq, gc, gr = _a_projection(x2, a_norm[0], w_main, w_vo_t, w_in[:, o3:o4], a_gate_bias[0])
    three = lambda a: a.reshape(nb, seq, a.shape[-1])
    head_g = _pad_heads(a_head_norm[0].reshape(1, -1), A_HEADS, A_V_DIM, A_V_ROWS)
    hm = _mlstm(three(q), three(k), vt, ot, gc, gr,
                jnp.broadcast_to(head_g.reshape(-1, 1), (A_HEADS * A_V_ROWS, MLSTM_CHUNK)))
    w_out = a_w_out[0]
    w_out_h = jnp.pad(w_out[:v_w].reshape(A_HEADS, A_V_DIM, D_MODEL), ((0, 0), (0, A_V_PAD - A_V_DIM), (0, 0)))
    w_out_h = w_out_h.reshape(A_HEADS * A_V_PAD, D_MODEL).astype(BF16)
    x2 = _mix_out(x2, hm.reshape(t, -1), mq, kbd0, vbd0, gmat, mem_q_norm[0],
                  w_out_h, w_out[v_w:].astype(BF16), seq)
    x2 = _dense_ffn(x2, ffn_norm[0], dense_w_gate_up[0].astype(BF16), dense_w_down[0].astype(BF16))

    cs, sn = _rope_tables(positions)
    k_sh, vt_sh, qh, mq1 = _mla_projection(x2, kv_norm, w_dkv, kv_a_norm, w_ukv, k_head_norm,
                                           b_norm[0], b_w_in[0], b_q_a_norm[0], b_w_uq[0], b_q_head_norm[0], cs, sn)

    kbd1, vbd1 = _memory_kv(mem, mem_norm[1], mem_w_kv[1], mem_k_norm[1], gmat)
    att = _causal_attention(three(qh), three(k_sh), vt_sh, seq)
    w_out = b_w_out[0]
    n_att = B_HEADS * V_HEAD
    x2 = _mix_out(x2, att.reshape(t, -1), mq1, kbd1, vbd1, gmat, mem_q_norm[1],
                  w_out[:n_att].astype(BF16), w_out[n_att:].astype(BF16), seq)
    x2 = _moe_ffn(x2, ffn_norm[1], moe_router[0], moe_w_gate_up[0], moe_w_down[0])
    return x2.reshape(nb, seq, D_MODEL)
```
